```python
import math
import jax, jax.numpy as jnp
from jax import lax
import numpy as np

D_MODEL = 2048
BATCH = 8
SEQ = 8192
DEPTH = 4

MIX_WIDTH = D_MODEL
N_MIXERS = 4
GROUP_WIDTH = MIX_WIDTH // N_MIXERS
HEAD_DIM = 128
N_HEADS = GROUP_WIDTH // HEAD_DIM
LRU_BLOCKS = 8
LRU_BLOCK_DIM = GROUP_WIDTH // LRU_BLOCKS
LRU_C = 8.0
SHORT_CONV = 4
FFN_CONV = 3
D_FF = ((8 * D_MODEL // 3 + 255) // 256) * 256
FOX_BLOCK = 128
GDN_CHUNK = 64
DILATED_PAIRS = ((128, 1), (512, 4), (2048, 16))
EPS = 1e-6
NEG_INF = -1e30

IN_SIZES = (GROUP_WIDTH, GROUP_WIDTH,
            3 * GROUP_WIDTH, N_HEADS,
            3 * GROUP_WIDTH, GROUP_WIDTH, N_HEADS, N_HEADS,
            3 * GROUP_WIDTH)
IN_COLS = sum(IN_SIZES)

kernel_name = "hybrid_parallel_heads_rglru_fox_gdn_dilated"


def rmsnorm(x, gain):
    xf = x.astype(jnp.float32)
    y = xf * lax.rsqrt(jnp.mean(xf * xf, axis=-1, keepdims=True) + EPS)
    return (y * gain).astype(x.dtype)


def group_rmsnorm(y, gain, group):
    B, T, W = y.shape
    yg = y.astype(jnp.float32).reshape(B, T, W // group, group)
    yg = yg * lax.rsqrt(jnp.mean(yg * yg, axis=-1, keepdims=True) + EPS)
    return yg.reshape(B, T, W) * gain


def causal_dwconv(x, w, b=None):
    K = w.shape[0]
    T = x.shape[1]
    xp = jnp.pad(x, ((0, 0), (K - 1, 0), (0, 0)))
    y = sum(xp[:, k:k + T, :] * w[k] for k in range(K))
    return y if b is None else y + b


def split_cols(z, sizes):
    idx = np.cumsum(sizes)[:-1].tolist()
    return jnp.split(z, idx, axis=-1)


def to_heads(z):
    B, T, W = z.shape
    return z.reshape(B, T, W // HEAD_DIM, HEAD_DIM).transpose(0, 2, 1, 3)


def from_heads(z):
    B, H, T, d = z.shape
    return z.transpose(0, 2, 1, 3).reshape(B, T, H * d)


def l2norm(t):
    return t * lax.rsqrt(jnp.sum(t * t, axis=-1, keepdims=True) + EPS)


def rg_lru(xa, conv_w, conv_b, wa, ba, wx, bx, lam):
    B, T, W = xa.shape
    xc = causal_dwconv(xa, conv_w, conv_b).astype(jnp.float32)
    xb = xc.reshape(B, T, LRU_BLOCKS, LRU_BLOCK_DIM)
    r = jax.nn.sigmoid(jnp.einsum('btnc,ncd->btnd', xb, wa).reshape(B, T, W) + ba)
    i = jax.nn.sigmoid(jnp.einsum('btnc,ncd->btnd', xb, wx).reshape(B, T, W) + bx)
    log_a = -LRU_C * r * jax.nn.softplus(-lam)
    a = jnp.exp(log_a)
    u = jnp.sqrt(-jnp.expm1(2.0 * log_a)) * (i * xc)

    def combine(left, right):
        a_l, h_l = left
        a_r, h_r = right
        return a_l * a_r, a_r * h_l + h_r

    _, h = lax.associative_scan(combine, (a, u), axis=1)
    return h


def forgetting_attention(q, k, v, log_f):
    B, H, T, hd = q.shape
    nb = T // FOX_BLOCK
    c = jnp.cumsum(log_f, axis=-1)
    q = q * hd ** -0.5
    qb = jnp.moveaxis(q.reshape(B, H, nb, FOX_BLOCK, hd), 2, 0)
    cb = jnp.moveaxis(c.reshape(B, H, nb, FOX_BLOCK), 2, 0)
    starts = jnp.arange(nb) * FOX_BLOCK
    kpos = jnp.arange(T)

    def block(args):
        q_blk, c_blk, start = args
        s = jnp.einsum('bhqd,bhkd->bhqk', q_blk, k).astype(jnp.float32)
        s = s + c_blk[..., :, None] - c[..., None, :]
        qpos = start + jnp.arange(FOX_BLOCK)
        s = jnp.where(kpos[None, :] <= qpos[:, None], s, NEG_INF)
        p = jax.nn.softmax(s, axis=-1)
        return jnp.einsum('bhqk,bhkd->bhqd', p.astype(v.dtype), v)

    o = lax.map(block, (qb, cb, starts))
    return jnp.moveaxis(o, 0, 2).reshape(B, H, T, hd)


def gated_delta_rule(q, k, v, g, beta):
    B, H, T, dk = q.shape
    dv = v.shape[-1]
    C = GDN_CHUNK
    N = T // C
    q = q * dk ** -0.5
    qc = q.reshape(B, H, N, C, dk)
    kc = k.reshape(B, H, N, C, dk)
    vc = v.reshape(B, H, N, C, dv)
    bc = beta.reshape(B, H, N, C)
    gc = jnp.cumsum(g.reshape(B, H, N, C), axis=-1)
    tril = jnp.tril(jnp.ones((C, C), bool))
    strict = jnp.tril(jnp.ones((C, C), bool), -1)
    diff = gc[..., :, None] - gc[..., None, :]
    decay = jnp.where(tril, jnp.exp(jnp.where(tril, diff, 0.0)), 0.0)
    kbeta = kc * bc[..., None]
    vbeta = vc * bc[..., None]
    kk = jnp.einsum('bhnid,bhnjd->bhnij', kbeta, kc) * decay
    a_mat = jnp.where(strict, kk, 0.0) + jnp.eye(C, dtype=jnp.float32)
    rhs = jnp.concatenate([vbeta, kbeta * jnp.exp(gc)[..., None]], axis=-1)
    sol = lax.linalg.triangular_solve(a_mat, rhs, left_side=True, lower=True, unit_diagonal=True)
    u, w = sol[..., :dv], sol[..., dv:]
    qk = jnp.where(tril, jnp.einsum('bhnid,bhnjd->bhnij', qc, kc) * decay, 0.0)
    xs = tuple(jnp.moveaxis(t, 2, 0) for t in (qc, kc, u, w, qk, gc))

    def step(S, inp):
        q_i, k_i, u_i, w_i, qk_i, g_i = inp
        v_new = u_i - jnp.einsum('bhcd,bhde->bhce', w_i, S)
        o_inter = jnp.einsum('bhcd,bhde->bhce', q_i * jnp.exp(g_i)[..., None], S)
        o = o_inter + jnp.einsum('bhij,bhje->bhie', qk_i, v_new)
        g_last = g_i[..., -1:]
        S = S * jnp.exp(g_last)[..., None] + jnp.einsum(
            'bhcd,bhce->bhde', k_i * jnp.exp(g_last - g_i)[..., None], v_new)
        return S, o

    S0 = jnp.zeros((B, H, dk, dv), jnp.float32)
    _, o = lax.scan(step, S0, xs)
    return jnp.moveaxis(o, 0, 2).reshape(B, H, T, dv)


def gated_deltanet(qkv, z, beta_logit, alpha_logit, conv_w, a_log, dt_bias, norm_g):
    qkv = jax.nn.silu(causal_dwconv(qkv, conv_w)).astype(jnp.float32)
    q, k, v = [to_heads(t) for t in jnp.split(qkv, 3, axis=-1)]
    q, k = l2norm(q), l2norm(k)
    beta = jax.nn.sigmoid(beta_logit.astype(jnp.float32)).transpose(0, 2, 1)
    g = (-jnp.exp(a_log) * jax.nn.softplus(alpha_logit.astype(jnp.float32) + dt_bias)).transpose(0, 2, 1)
    o = gated_delta_rule(q, k, v, g, beta)
    o = o * lax.rsqrt(jnp.mean(o * o, axis=-1, keepdims=True) + EPS) * norm_g
    o = o * jax.nn.silu(to_heads(z).astype(jnp.float32))
    return from_heads(o)


def window_attention_lse(q, k, v, span):
    *lead, L, hd = q.shape
    nl = len(lead)
    n = -(-L // span)
    padw = [(0, 0)] * nl + [(0, n * span - L), (0, 0)]

    def blocks(t):
        return jnp.pad(t, padw).reshape(*lead, n, span, hd)

    def with_prev(t):
        prev = jnp.pad(t, [(0, 0)] * nl + [(1, 0), (0, 0), (0, 0)])[..., :-1, :, :]
        return jnp.concatenate([prev, t], axis=-2)

    qb = blocks(q * hd ** -0.5)
    kk = with_prev(blocks(k))
    vv = with_prev(blocks(v))
    s = jnp.einsum('...nqd,...nkd->...nqk', qb, kk).astype(jnp.float32)
    i = jnp.arange(span)[:, None]
    j = jnp.arange(2 * span)[None, :]
    dist = i + span - j
    key_pos = jnp.arange(n)[:, None, None] * span - span + j
    mask = (dist >= 0) & (dist <= span) & (key_pos >= 0)
    s = jnp.where(mask, s, NEG_INF)
    m = jnp.max(s, axis=-1, keepdims=True)
    p = jnp.exp(s - m)
    den = jnp.sum(p, axis=-1, keepdims=True)
    out = jnp.einsum('...nqk,...nkd->...nqd', p.astype(v.dtype), vv) / den.astype(v.dtype)
    lse = (m + jnp.log(den))[..., 0]
    out = out.reshape(*lead, n * span, hd)[..., :L, :]
    lse = lse.reshape(*lead, n * span)[..., :L]
    return out, lse


def dilated_branch(q, k, v, window, dil):
    B, H, T, hd = q.shape
    Td = T // dil

    def to_res(t):
        return t.reshape(B, H, Td, dil, hd).swapaxes(2, 3)

    o, lse = window_attention_lse(to_res(q), to_res(k), to_res(v), window // dil)
    return o.swapaxes(2, 3).reshape(B, H, T, hd), lse.swapaxes(2, 3).reshape(B, H, T)


def dilated_attention(q, k, v):
    outs, lses = [], []
    for window, dil in DILATED_PAIRS:
        o, lse = dilated_branch(q, k, v, window, dil)
        outs.append(o)
        lses.append(lse)
    wts = jax.nn.softmax(jnp.stack(lses), axis=0)
    return jnp.einsum('gbht,gbhtd->bhtd', wts.astype(q.dtype), jnp.stack(outs))


def conv_ffn(h, w_up, conv_w, conv_b, w_down):
    u = causal_dwconv(h @ w_up, conv_w, conv_b)
    up, gate = jnp.split(u, 2, axis=-1)
    return (jax.nn.silu(gate) * up) @ w_down


def _fwd_setup_inputs(seed: int = 0) -> dict:
    key = jax.random.key(seed)
    ks = iter(jax.random.split(key, 32))
    f32 = jnp.float32

    def nrm(shape, scale):
        return scale * jax.random.normal(next(ks), shape, f32)

    def gain(shape):
        return 1.0 + 0.02 * jax.random.normal(next(ks), shape, f32)

    res_scale = (2 * DEPTH) ** -0.5
    x = jax.random.normal(next(ks), (BATCH, SEQ, D_MODEL), f32)
    norm_mix = gain((DEPTH, D_MODEL))
    w_in = nrm((DEPTH, D_MODEL, IN_COLS), D_MODEL ** -0.5)
    lru_conv_w = nrm((DEPTH, SHORT_CONV, GROUP_WIDTH), SHORT_CONV ** -0.5)
    lru_conv_b = nrm((DEPTH, GROUP_WIDTH), 0.01)
    lru_wa = nrm((DEPTH, LRU_BLOCKS, LRU_BLOCK_DIM, LRU_BLOCK_DIM), LRU_BLOCK_DIM ** -0.5)
    lru_ba = nrm((DEPTH, GROUP_WIDTH), 0.01)
    lru_wx = nrm((DEPTH, LRU_BLOCKS, LRU_BLOCK_DIM, LRU_BLOCK_DIM), LRU_BLOCK_DIM ** -0.5)
    lru_bx = nrm((DEPTH, GROUP_WIDTH), 0.01)
    a_c = jax.random.uniform(next(ks), (DEPTH, GROUP_WIDTH), f32, 0.9, 0.999)
    a_base = a_c ** (1.0 / LRU_C)
    lru_lambda = jnp.log(a_base) - jnp.log1p(-a_base)
    fox_f_bias = 3.0 + nrm((DEPTH, N_HEADS), 0.5)
    gdn_conv_w = nrm((DEPTH, SHORT_CONV, 3 * GROUP_WIDTH), SHORT_CONV ** -0.5)
    gdn_a_log = jnp.log(jax.random.uniform(next(ks), (DEPTH, N_HEADS), f32, 1.0, 16.0))
    dt = jnp.exp(jax.random.uniform(next(ks), (DEPTH, N_HEADS), f32, math.log(1e-3), math.log(1e-1)))
    gdn_dt_bias = dt + jnp.log(-jnp.expm1(-dt))
    gdn_norm = gain((DEPTH, HEAD_DIM))
    norm_a = gain((DEPTH, GROUP_WIDTH))
    norm_b = gain((DEPTH, GROUP_WIDTH))
    norm_d = gain((DEPTH, GROUP_WIDTH))
    w_out = nrm((DEPTH, MIX_WIDTH, D_MODEL), MIX_WIDTH ** -0.5 * res_scale)
    norm_ffn = gain((DEPTH, D_MODEL))
    ffn_w_up = nrm((DEPTH, D_MODEL, 2 * D_FF), D_MODEL ** -0.5)
    ffn_conv_w = nrm((DEPTH, FFN_CONV, 2 * D_FF), FFN_CONV ** -0.5)
    ffn_conv_b = nrm((DEPTH, 2 * D_FF), 0.01)
    ffn_w_down = nrm((DEPTH, D_FF, D_MODEL), D_FF ** -0.5 * res_scale)
    norm_final = gain((D_MODEL,))
    return {"x": x, "norm_mix": norm_mix, "w_in": w_in,
            "lru_conv_w": lru_conv_w, "lru_conv_b": lru_conv_b,
            "lru_wa": lru_wa, "lru_ba": lru_ba, "lru_wx": lru_wx, "lru_bx": lru_bx,
            "lru_lambda": lru_lambda, "fox_f_bias": fox_f_bias,
            "gdn_conv_w": gdn_conv_w, "gdn_a_log": gdn_a_log, "gdn_dt_bias": gdn_dt_bias,
            "gdn_norm": gdn_norm, "norm_a": norm_a, "norm_b": norm_b, "norm_d": norm_d,
            "w_out": w_out, "norm_ffn": norm_ffn, "ffn_w_up": ffn_w_up,
            "ffn_conv_w": ffn_conv_w, "ffn_conv_b": ffn_conv_b, "ffn_w_down": ffn_w_down,
            "norm_final": norm_final}


def _fwd_reference(x, norm_mix, w_in, lru_conv_w, lru_conv_b, lru_wa, lru_ba, lru_wx, lru_bx,
              lru_lambda, fox_f_bias, gdn_conv_w, gdn_a_log, gdn_dt_bias, gdn_norm,
              norm_a, norm_b, norm_d, w_out, norm_ffn, ffn_w_up, ffn_conv_w, ffn_conv_b,
              ffn_w_down, norm_final):
    for l in range(DEPTH):
        h = rmsnorm(x, norm_mix[l])
        z = h @ w_in[l]
        a_x, a_gate, b_qkv, b_f, c_qkv, c_z, c_beta, c_alpha, d_qkv = split_cols(z, IN_SIZES)

        h_a = rg_lru(a_x, lru_conv_w[l], lru_conv_b[l], lru_wa[l], lru_ba[l],
                     lru_wx[l], lru_bx[l], lru_lambda[l])
        y_a = group_rmsnorm(h_a, norm_a[l], LRU_BLOCK_DIM) * jax.nn.gelu(a_gate.astype(jnp.float32))

        bq, bk, bv = [to_heads(t) for t in jnp.split(b_qkv, 3, axis=-1)]
        log_f = jax.nn.log_sigmoid(b_f.astype(jnp.float32) + fox_f_bias[l]).transpose(0, 2, 1)
        y_b = group_rmsnorm(from_heads(forgetting_attention(bq, bk, bv, log_f)), norm_b[l], HEAD_DIM)

        y_c = gated_deltanet(c_qkv, c_z, c_beta, c_alpha, gdn_conv_w[l], gdn_a_log[l],
                             gdn_dt_bias[l], gdn_norm[l])

        dq, dk, dv = [to_heads(t) for t in jnp.split(d_qkv, 3, axis=-1)]
        y_d = group_rmsnorm(from_heads(dilated_attention(dq, dk, dv)), norm_d[l], HEAD_DIM)

        y = jnp.concatenate([y_a, y_b, y_c, y_d], axis=-1).astype(x.dtype)
        x = x + y @ w_out[l]

        h = rmsnorm(x, norm_ffn[l])
        x = x + conv_ffn(h, ffn_w_up[l], ffn_conv_w[l], ffn_conv_b[l], ffn_w_down[l])
    return rmsnorm(x, norm_final)


import jax as _jax
import jax.numpy as _jnp

TWIN_FORMAT = 'train_step'
FWD_PARAMS = ['x', 'norm_mix', 'w_in', 'lru_conv_w', 'lru_conv_b', 'lru_wa', 'lru_ba', 'lru_wx', 'lru_bx', 'lru_lambda', 'fox_f_bias', 'gdn_conv_w', 'gdn_a_log', 'gdn_dt_bias', 'gdn_norm', 'norm_a', 'norm_b', 'norm_d', 'w_out', 'norm_ffn', 'ffn_w_up', 'ffn_conv_w', 'ffn_conv_b', 'ffn_w_down', 'norm_final']
TWIN_WEIGHTS = ['norm_mix', 'w_in', 'lru_conv_w', 'lru_conv_b', 'lru_wa', 'lru_ba', 'lru_wx', 'lru_bx', 'lru_lambda', 'fox_f_bias', 'gdn_conv_w', 'gdn_a_log', 'gdn_dt_bias', 'gdn_norm', 'norm_a', 'norm_b', 'norm_d', 'w_out', 'norm_ffn', 'ffn_w_up', 'ffn_conv_w', 'ffn_conv_b', 'ffn_w_down', 'norm_final']
TWIN_DIFF_INPUT = 'x'
TWIN_INPUTS = ['x', 'norm_mix', 'w_in', 'lru_conv_w', 'lru_conv_b', 'lru_wa', 'lru_ba', 'lru_wx', 'lru_bx', 'lru_lambda', 'fox_f_bias', 'gdn_conv_w', 'gdn_a_log', 'gdn_dt_bias', 'gdn_norm', 'norm_a', 'norm_b', 'norm_d', 'w_out', 'norm_ffn', 'ffn_w_up', 'ffn_conv_w', 'ffn_conv_b', 'ffn_w_down', 'norm_final', 'loss_target', 'm_norm_mix', 'm_w_in', 'm_lru_conv_w', 'm_lru_conv_b', 'm_lru_wa', 'm_lru_ba', 'm_lru_wx', 'm_lru_bx', 'm_lru_lambda', 'm_fox_f_bias', 'm_gdn_conv_w', 'm_gdn_a_log', 'm_gdn_dt_bias', 'm_gdn_norm', 'm_norm_a', 'm_norm_b', 'm_norm_d', 'm_w_out', 'm_norm_ffn', 'm_ffn_w_up', 'm_ffn_conv_w', 'm_ffn_conv_b', 'm_ffn_w_down', 'm_norm_final', 'v_norm_mix', 'v_w_in', 'v_lru_conv_w', 'v_lru_conv_b', 'v_lru_wa', 'v_lru_ba', 'v_lru_wx', 'v_lru_bx', 'v_lru_lambda', 'v_fox_f_bias', 'v_gdn_conv_w', 'v_gdn_a_log', 'v_gdn_dt_bias', 'v_gdn_norm', 'v_norm_a', 'v_norm_b', 'v_norm_d', 'v_w_out', 'v_norm_ffn', 'v_ffn_w_up', 'v_ffn_conv_w', 'v_ffn_conv_b', 'v_ffn_w_down', 'v_norm_final']
TWIN_OUTPUTS = ['loss', 'grad_x', 'grad_norm_mix', 'grad_w_in', 'grad_lru_conv_w', 'grad_lru_conv_b', 'grad_lru_wa', 'grad_lru_ba', 'grad_lru_wx', 'grad_lru_bx', 'grad_lru_lambda', 'grad_fox_f_bias', 'grad_gdn_conv_w', 'grad_gdn_a_log', 'grad_gdn_dt_bias', 'grad_gdn_norm', 'grad_norm_a', 'grad_norm_b', 'grad_norm_d', 'grad_w_out', 'grad_norm_ffn', 'grad_ffn_w_up', 'grad_ffn_conv_w', 'grad_ffn_conv_b', 'grad_ffn_w_down', 'grad_norm_final', 'delta_norm_mix', 'delta_w_in', 'delta_lru_conv_w', 'delta_lru_conv_b', 'delta_lru_wa', 'delta_lru_ba', 'delta_lru_wx', 'delta_lru_bx', 'delta_lru_lambda', 'delta_fox_f_bias', 'delta_gdn_conv_w', 'delta_gdn_a_log', 'delta_gdn_dt_bias', 'delta_gdn_norm', 'delta_norm_a', 'delta_norm_b', 'delta_norm_d', 'delta_w_out', 'delta_norm_ffn', 'delta_ffn_w_up', 'delta_ffn_conv_w', 'delta_ffn_conv_b', 'delta_ffn_w_down', 'delta_norm_final', 'new_m_norm_mix', 'new_m_w_in', 'new_m_lru_conv_w', 'new_m_lru_conv_b', 'new_m_lru_wa', 'new_m_lru_ba', 'new_m_lru_wx', 'new_m_lru_bx', 'new_m_lru_lambda', 'new_m_fox_f_bias', 'new_m_gdn_conv_w', 'new_m_gdn_a_log', 'new_m_gdn_dt_bias', 'new_m_gdn_norm', 'new_m_norm_a', 'new_m_norm_b', 'new_m_norm_d', 'new_m_w_out', 'new_m_norm_ffn', 'new_m_ffn_w_up', 'new_m_ffn_conv_w', 'new_m_ffn_conv_b', 'new_m_ffn_w_down', 'new_m_norm_final', 'new_v_norm_mix', 'new_v_w_in', 'new_v_lru_conv_w', 'new_v_lru_conv_b', 'new_v_lru_wa', 'new_v_lru_ba', 'new_v_lru_wx', 'new_v_lru_bx', 'new_v_lru_lambda', 'new_v_fox_f_bias', 'new_v_gdn_conv_w', 'new_v_gdn_a_log', 'new_v_gdn_dt_bias', 'new_v_gdn_norm', 'new_v_norm_a', 'new_v_norm_b', 'new_v_norm_d', 'new_v_w_out', 'new_v_norm_ffn', 'new_v_ffn_w_up', 'new_v_ffn_conv_w', 'new_v_ffn_conv_b', 'new_v_ffn_w_down', 'new_v_norm_final']
TWIN_LEAF_KINDS = {'loss': 'loss', 'grad_x': 'grad_x', 'grad_norm_mix': 'grad_w', 'grad_w_in': 'grad_w', 'grad_lru_conv_w': 'grad_w', 'grad_lru_conv_b': 'grad_w', 'grad_lru_wa': 'grad_w', 'grad_lru_ba': 'grad_w', 'grad_lru_wx': 'grad_w', 'grad_lru_bx': 'grad_w', 'grad_lru_lambda': 'grad_w', 'grad_fox_f_bias': 'grad_w', 'grad_gdn_conv_w': 'grad_w', 'grad_gdn_a_log': 'grad_w', 'grad_gdn_dt_bias': 'grad_w', 'grad_gdn_norm': 'grad_w', 'grad_norm_a': 'grad_w', 'grad_norm_b': 'grad_w', 'grad_norm_d': 'grad_w', 'grad_w_out': 'grad_w', 'grad_norm_ffn': 'grad_w', 'grad_ffn_w_up': 'grad_w', 'grad_ffn_conv_w': 'grad_w', 'grad_ffn_conv_b': 'grad_w', 'grad_ffn_w_down': 'grad_w', 'grad_norm_final': 'grad_w', 'delta_norm_mix': 'delta_w', 'delta_w_in': 'delta_w', 'delta_lru_conv_w': 'delta_w', 'delta_lru_conv_b': 'delta_w', 'delta_lru_wa': 'delta_w', 'delta_lru_ba': 'delta_w', 'delta_lru_wx': 'delta_w', 'delta_lru_bx': 'delta_w', 'delta_lru_lambda': 'delta_w', 'delta_fox_f_bias': 'delta_w', 'delta_gdn_conv_w': 'delta_w', 'delta_gdn_a_log': 'delta_w', 'delta_gdn_dt_bias': 'delta_w', 'delta_gdn_norm': 'delta_w', 'delta_norm_a': 'delta_w', 'delta_norm_b': 'delta_w', 'delta_norm_d': 'delta_w', 'delta_w_out': 'delta_w', 'delta_norm_ffn': 'delta_w', 'delta_ffn_w_up': 'delta_w', 'delta_ffn_conv_w': 'delta_w', 'delta_ffn_conv_b': 'delta_w', 'delta_ffn_w_down': 'delta_w', 'delta_norm_final': 'delta_w', 'new_m_norm_mix': 'new_m', 'new_m_w_in': 'new_m', 'new_m_lru_conv_w': 'new_m', 'new_m_lru_conv_b': 'new_m', 'new_m_lru_wa': 'new_m', 'new_m_lru_ba': 'new_m', 'new_m_lru_wx': 'new_m', 'new_m_lru_bx': 'new_m', 'new_m_lru_lambda': 'new_m', 'new_m_fox_f_bias': 'new_m', 'new_m_gdn_conv_w': 'new_m', 'new_m_gdn_a_log': 'new_m', 'new_m_gdn_dt_bias': 'new_m', 'new_m_gdn_norm': 'new_m', 'new_m_norm_a': 'new_m', 'new_m_norm_b': 'new_m', 'new_m_norm_d': 'new_m', 'new_m_w_out': 'new_m', 'new_m_norm_ffn': 'new_m', 'new_m_ffn_w_up': 'new_m', 'new_m_ffn_conv_w': 'new_m', 'new_m_ffn_conv_b': 'new_m', 'new_m_ffn_w_down': 'new_m', 'new_m_norm_final': 'new_m', 'new_v_norm_mix': 'new_v', 'new_v_w_in': 'new_v', 'new_v_lru_conv_w': 'new_v', 'new_v_lru_conv_b': 'new_v', 'new_v_lru_wa': 'new_v', 'new_v_lru_ba': 'new_v', 'new_v_lru_wx': 'new_v', 'new_v_lru_bx': 'new_v', 'new_v_lru_lambda': 'new_v', 'new_v_fox_f_bias': 'new_v', 'new_v_gdn_conv_w': 'new_v', 'new_v_gdn_a_log': 'new_v', 'new_v_gdn_dt_bias': 'new_v', 'new_v_gdn_norm': 'new_v', 'new_v_norm_a': 'new_v', 'new_v_norm_b': 'new_v', 'new_v_norm_d': 'new_v', 'new_v_w_out': 'new_v', 'new_v_norm_ffn': 'new_v', 'new_v_ffn_w_up': 'new_v', 'new_v_ffn_conv_w': 'new_v', 'new_v_ffn_conv_b': 'new_v', 'new_v_ffn_w_down': 'new_v', 'new_v_norm_final': 'new_v'}


def _forward(args):
    return _fwd_reference(*[args[k] for k in FWD_PARAMS])


def _output_shape():
    def fwd():
        inp = _fwd_setup_inputs(0)
        return _fwd_reference(*[inp[k] for k in FWD_PARAMS])
    out = _jax.eval_shape(fwd)
    return out.shape, out.dtype

N_MICROBATCH = 1
ADAM_LR = 0.001
ADAM_B1 = 0.9
ADAM_B2 = 0.999
ADAM_EPS = 1e-08
ADAM_WD = 0.01
ADAM_STEP = 10
PER_EXAMPLE_BATCH_AXIS = {'x': 0, 'loss_target': 0}
SHARED_INPUTS = []
_WEIGHT_DTYPES = {'norm_mix': _jnp.float32, 'w_in': _jnp.float32, 'lru_conv_w': _jnp.float32, 'lru_conv_b': _jnp.float32, 'lru_wa': _jnp.float32, 'lru_ba': _jnp.float32, 'lru_wx': _jnp.float32, 'lru_bx': _jnp.float32, 'lru_lambda': _jnp.float32, 'fox_f_bias': _jnp.float32, 'gdn_conv_w': _jnp.float32, 'gdn_a_log': _jnp.float32, 'gdn_dt_bias': _jnp.float32, 'gdn_norm': _jnp.float32, 'norm_a': _jnp.float32, 'norm_b': _jnp.float32, 'norm_d': _jnp.float32, 'w_out': _jnp.float32, 'norm_ffn': _jnp.float32, 'ffn_w_up': _jnp.float32, 'ffn_conv_w': _jnp.float32, 'ffn_conv_b': _jnp.float32, 'ffn_w_down': _jnp.float32, 'norm_final': _jnp.float32}
MOMENT_SCALE = {'norm_mix': 5.805331e-02, 'w_in': 3.328673e-02, 'lru_conv_w': 3.295998e-02, 'lru_conv_b': 4.072160e-01, 'lru_wa': 1.207825e-02, 'lru_ba': 8.849184e-03, 'lru_wx': 2.253059e-02, 'lru_bx': 1.080206e-02, 'lru_lambda': 1.686250e-02, 'fox_f_bias': 4.420701e-01, 'gdn_conv_w': 1.922922e-02, 'gdn_a_log': 1.806943e-01, 'gdn_dt_bias': 1.709120e-01, 'gdn_norm': 5.292457e-02, 'norm_a': 3.499441e-02, 'norm_b': 5.054030e-02, 'norm_d': 6.191296e-02, 'w_out': 1.225502e-01, 'norm_ffn': 3.673471e-02, 'ffn_w_up': 1.540096e-02, 'ffn_conv_w': 1.539370e-02, 'ffn_conv_b': 1.868372e-02, 'ffn_w_down': 7.120778e-02, 'norm_final': 3.198529e+01}


def _to_microbatches(a, axis):
    t = _jnp.moveaxis(a, axis, 0)
    t = t.reshape((N_MICROBATCH, t.shape[0] // N_MICROBATCH) + t.shape[1:])
    return _jnp.moveaxis(t, 1, axis + 1)


def setup_inputs(seed: int = 0) -> dict:
    inp = _fwd_setup_inputs(seed)
    key = _jax.random.fold_in(_jax.random.key(seed), 7919)
    shape, _ = _output_shape()
    out = dict(inp)
    out["loss_target"] = _jax.random.normal(_jax.random.fold_in(key, 0), shape, _jnp.float32)
    for i, name in enumerate(TWIN_WEIGHTS):
        w = inp[name].astype(_jnp.float32)
        if MOMENT_SCALE is None:
            s = _jnp.sqrt(_jnp.mean(_jnp.square(w)) + 1e-30)
        else:
            s = MOMENT_SCALE[name]
        km, kv = _jax.random.split(_jax.random.fold_in(key, i + 1))
        out[name] = w
        out["m_" + name] = s * _jax.random.normal(km, w.shape, _jnp.float32)
        out["v_" + name] = (s * s) * _jax.random.uniform(kv, w.shape, _jnp.float32, 0.5, 1.5)
    if N_MICROBATCH > 1:
        for name, axis in PER_EXAMPLE_BATCH_AXIS.items():
            out[name] = _to_microbatches(out[name], axis)
    return {'x': out['x'], 'norm_mix': out['norm_mix'], 'w_in': out['w_in'], 'lru_conv_w': out['lru_conv_w'], 'lru_conv_b': out['lru_conv_b'], 'lru_wa': out['lru_wa'], 'lru_ba': out['lru_ba'], 'lru_wx': out['lru_wx'], 'lru_bx': out['lru_bx'], 'lru_lambda': out['lru_lambda'], 'fox_f_bias': out['fox_f_bias'], 'gdn_conv_w': out['gdn_conv_w'], 'gdn_a_log': out['gdn_a_log'], 'gdn_dt_bias': out['gdn_dt_bias'], 'gdn_norm': out['gdn_norm'], 'norm_a': out['norm_a'], 'norm_b': out['norm_b'], 'norm_d': out['norm_d'], 'w_out': out['w_out'], 'norm_ffn': out['norm_ffn'], 'ffn_w_up': out['ffn_w_up'], 'ffn_conv_w': out['ffn_conv_w'], 'ffn_conv_b': out['ffn_conv_b'], 'ffn_w_down': out['ffn_w_down'], 'norm_final': out['norm_final'], 'loss_target': out['loss_target'], 'm_norm_mix': out['m_norm_mix'], 'm_w_in': out['m_w_in'], 'm_lru_conv_w': out['m_lru_conv_w'], 'm_lru_conv_b': out['m_lru_conv_b'], 'm_lru_wa': out['m_lru_wa'], 'm_lru_ba': out['m_lru_ba'], 'm_lru_wx': out['m_lru_wx'], 'm_lru_bx': out['m_lru_bx'], 'm_lru_lambda': out['m_lru_lambda'], 'm_fox_f_bias': out['m_fox_f_bias'], 'm_gdn_conv_w': out['m_gdn_conv_w'], 'm_gdn_a_log': out['m_gdn_a_log'], 'm_gdn_dt_bias': out['m_gdn_dt_bias'], 'm_gdn_norm': out['m_gdn_norm'], 'm_norm_a': out['m_norm_a'], 'm_norm_b': out['m_norm_b'], 'm_norm_d': out['m_norm_d'], 'm_w_out': out['m_w_out'], 'm_norm_ffn': out['m_norm_ffn'], 'm_ffn_w_up': out['m_ffn_w_up'], 'm_ffn_conv_w': out['m_ffn_conv_w'], 'm_ffn_conv_b': out['m_ffn_conv_b'], 'm_ffn_w_down': out['m_ffn_w_down'], 'm_norm_final': out['m_norm_final'], 'v_norm_mix': out['v_norm_mix'], 'v_w_in': out['v_w_in'], 'v_lru_conv_w': out['v_lru_conv_w'], 'v_lru_conv_b': out['v_lru_conv_b'], 'v_lru_wa': out['v_lru_wa'], 'v_lru_ba': out['v_lru_ba'], 'v_lru_wx': out['v_lru_wx'], 'v_lru_bx': out['v_lru_bx'], 'v_lru_lambda': out['v_lru_lambda'], 'v_fox_f_bias': out['v_fox_f_bias'], 'v_gdn_conv_w': out['v_gdn_conv_w'], 'v_gdn_a_log': out['v_gdn_a_log'], 'v_gdn_dt_bias': out['v_gdn_dt_bias'], 'v_gdn_norm': out['v_gdn_norm'], 'v_norm_a': out['v_norm_a'], 'v_norm_b': out['v_norm_b'], 'v_norm_d': out['v_norm_d'], 'v_w_out': out['v_w_out'], 'v_norm_ffn': out['v_norm_ffn'], 'v_ffn_w_up': out['v_ffn_w_up'], 'v_ffn_conv_w': out['v_ffn_conv_w'], 'v_ffn_conv_b': out['v_ffn_conv_b'], 'v_ffn_w_down': out['v_ffn_w_down'], 'v_norm_final': out['v_norm_final']}


def _loss(weights, diff, rest, loss_target):
    with _jax.named_scope("forward"):
        args = {**rest, TWIN_DIFF_INPUT: diff, **{k: w.astype(_WEIGHT_DTYPES[k]) for k, w in weights.items()}}
        y = _forward(args)
    with _jax.named_scope("loss_head"):
        err = _jnp.square(y.astype(_jnp.float32) - loss_target)
        return 0.5 * _jnp.sum(_jnp.mean(err, axis=-1)) if err.ndim else 0.5 * err


def _adamw(w, g, m, v):
    m = ADAM_B1 * m + (1.0 - ADAM_B1) * g
    v = ADAM_B2 * v + (1.0 - ADAM_B2) * _jnp.square(g)
    m_hat = m / (1.0 - ADAM_B1 ** ADAM_STEP)
    v_hat = v / (1.0 - ADAM_B2 ** ADAM_STEP)
    delta = -ADAM_LR * (m_hat / (_jnp.sqrt(v_hat) + ADAM_EPS) + ADAM_WD * w)
    return delta, m, v


def reference(x, norm_mix, w_in, lru_conv_w, lru_conv_b, lru_wa, lru_ba, lru_wx, lru_bx, lru_lambda, fox_f_bias, gdn_conv_w, gdn_a_log, gdn_dt_bias, gdn_norm, norm_a, norm_b, norm_d, w_out, norm_ffn, ffn_w_up, ffn_conv_w, ffn_conv_b, ffn_w_down, norm_final, loss_target, m_norm_mix, m_w_in, m_lru_conv_w, m_lru_conv_b, m_lru_wa, m_lru_ba, m_lru_wx, m_lru_bx, m_lru_lambda, m_fox_f_bias, m_gdn_conv_w, m_gdn_a_log, m_gdn_dt_bias, m_gdn_norm, m_norm_a, m_norm_b, m_norm_d, m_w_out, m_norm_ffn, m_ffn_w_up, m_ffn_conv_w, m_ffn_conv_b, m_ffn_w_down, m_norm_final, v_norm_mix, v_w_in, v_lru_conv_w, v_lru_conv_b, v_lru_wa, v_lru_ba, v_lru_wx, v_lru_bx, v_lru_lambda, v_fox_f_bias, v_gdn_conv_w, v_gdn_a_log, v_gdn_dt_bias, v_gdn_norm, v_norm_a, v_norm_b, v_norm_d, v_w_out, v_norm_ffn, v_ffn_w_up, v_ffn_conv_w, v_ffn_conv_b, v_ffn_w_down, v_norm_final):
    given = dict(x=x, norm_mix=norm_mix, w_in=w_in, lru_conv_w=lru_conv_w, lru_conv_b=lru_conv_b, lru_wa=lru_wa, lru_ba=lru_ba, lru_wx=lru_wx, lru_bx=lru_bx, lru_lambda=lru_lambda, fox_f_bias=fox_f_bias, gdn_conv_w=gdn_conv_w, gdn_a_log=gdn_a_log, gdn_dt_bias=gdn_dt_bias, gdn_norm=gdn_norm, norm_a=norm_a, norm_b=norm_b, norm_d=norm_d, w_out=w_out, norm_ffn=norm_ffn, ffn_w_up=ffn_w_up, ffn_conv_w=ffn_conv_w, ffn_conv_b=ffn_conv_b, ffn_w_down=ffn_w_down, norm_final=norm_final, loss_target=loss_target, m_norm_mix=m_norm_mix, m_w_in=m_w_in, m_lru_conv_w=m_lru_conv_w, m_lru_conv_b=m_lru_conv_b, m_lru_wa=m_lru_wa, m_lru_ba=m_lru_ba, m_lru_wx=m_lru_wx, m_lru_bx=m_lru_bx, m_lru_lambda=m_lru_lambda, m_fox_f_bias=m_fox_f_bias, m_gdn_conv_w=m_gdn_conv_w, m_gdn_a_log=m_gdn_a_log, m_gdn_dt_bias=m_gdn_dt_bias, m_gdn_norm=m_gdn_norm, m_norm_a=m_norm_a, m_norm_b=m_norm_b, m_norm_d=m_norm_d, m_w_out=m_w_out, m_norm_ffn=m_norm_ffn, m_ffn_w_up=m_ffn_w_up, m_ffn_conv_w=m_ffn_conv_w, m_ffn_conv_b=m_ffn_conv_b, m_ffn_w_down=m_ffn_w_down, m_norm_final=m_norm_final, v_norm_mix=v_norm_mix, v_w_in=v_w_in, v_lru_conv_w=v_lru_conv_w, v_lru_conv_b=v_lru_conv_b, v_lru_wa=v_lru_wa, v_lru_ba=v_lru_ba, v_lru_wx=v_lru_wx, v_lru_bx=v_lru_bx, v_lru_lambda=v_lru_lambda, v_fox_f_bias=v_fox_f_bias, v_gdn_conv_w=v_gdn_conv_w, v_gdn_a_log=v_gdn_a_log, v_gdn_dt_bias=v_gdn_dt_bias, v_gdn_norm=v_gdn_norm, v_norm_a=v_norm_a, v_norm_b=v_norm_b, v_norm_d=v_norm_d, v_w_out=v_w_out, v_norm_ffn=v_norm_ffn, v_ffn_w_up=v_ffn_w_up, v_ffn_conv_w=v_ffn_conv_w, v_ffn_conv_b=v_ffn_conv_b, v_ffn_w_down=v_ffn_w_down, v_norm_final=v_norm_final)
    weights = {n: given[n] for n in TWIN_WEIGHTS}
    shared = {n: given[n] for n in SHARED_INPUTS}
    per_example = {n: given[n] for n in ['x']}
    grad_fn = _jax.value_and_grad(_loss, argnums=(0, 1))

    def one_microbatch(ex, loss_target):
        ex = dict(ex)
        diff = ex.pop(TWIN_DIFF_INPUT)
        return grad_fn(weights, diff, {**shared, **ex}, loss_target)

    if N_MICROBATCH == 1:
        loss, (grad_w, grad_x) = one_microbatch(per_example, given["loss_target"])
    else:
        def body(carry, xs):
            loss_sum, grad_sum = carry
            l_k, (gw_k, gx_k) = one_microbatch(xs[0], xs[1])
            with _jax.named_scope("update"):
                return (loss_sum + l_k, _jax.tree.map(_jnp.add, grad_sum, gw_k)), gx_k

        init = (_jnp.zeros((), _jnp.float32), _jax.tree.map(_jnp.zeros_like, weights))
        (loss, grad_w), grad_x = _jax.lax.scan(body, init, (per_example, given["loss_target"]))
    with _jax.named_scope("update"):
        delta_w, new_m, new_v = {}, {}, {}
        for n in TWIN_WEIGHTS:
            delta_w[n], new_m[n], new_v[n] = _adamw(weights[n], grad_w[n], given["m_" + n], given["v_" + n])
    return (loss, grad_x, *[grad_w[n] for n in TWIN_WEIGHTS], *[delta_w[n] for n in TWIN_WEIGHTS],
            *[new_m[n] for n in TWIN_WEIGHTS], *[new_v[n] for n in TWIN_WEIGHTS])
```

```python
import functools

import numpy as np
import jax
import jax.numpy as jnp
from jax import lax
from jax.experimental import pallas as pl
from jax.experimental.pallas import tpu as pltpu

F32 = jnp.float32
MM_DTYPE = jnp.bfloat16
VMEM_LIMIT_BYTES = 56 * 1024 * 1024
LANES = 128
SUBLANES = 8

GROUP = 512
HEAD = 128
N_HEADS = GROUP // HEAD
LRU_BLOCKS = 8
LRU_BD = GROUP // LRU_BLOCKS
LRU_C = 8.0
GDN_CHUNK = 64
DILATED_PAIRS = ((128, 1), (512, 4), (2048, 16))
EPS = 1e-6
NEG = -1e30
ATT_SCALE = HEAD ** -0.5

ADAM_LR, ADAM_B1, ADAM_B2, ADAM_EPS, ADAM_WD, ADAM_STEP = 0.001, 0.9, 0.999, 1e-08, 0.01, 10

Z_AX, Z_AG, Z_BQ, Z_BK, Z_BV = 0, 512, 1024, 1536, 2048
Z_CQKV, Z_CZ, Z_DQ, Z_DK, Z_DV, Z_GATES, Z_COLS = 2560, 4096, 4608, 5120, 5632, 6144, 6272
IN_COLS = 6156
G_F, G_BETA, G_ALPHA = 0, 4, 8

MESH_ID = pl.DeviceIdType.MESH
ANY = pl.BlockSpec(memory_space=pl.ANY)


def _tile(n, target, align):
    t = min(n, target) // align * align
    while t >= align:
        if n % t == 0:
            return t
        t -= align
    return n


def _params(*sem):
    return pltpu.CompilerParams(dimension_semantics=sem, vmem_limit_bytes=VMEM_LIMIT_BYTES)


def _sds(shape, dtype=F32):
    return jax.ShapeDtypeStruct(tuple(shape), dtype)


def _dg(a, b, ca, cb, hi):
    dims = (((ca,), (cb,)), ((), ()))
    if hi:
        return lax.dot_general(a.astype(F32), b.astype(F32), dims, precision=lax.Precision.HIGHEST,
                               preferred_element_type=F32)
    return lax.dot_general(a.astype(MM_DTYPE), b.astype(MM_DTYPE), dims, preferred_element_type=F32)


_FORMS = {"nn": (1, 0), "nt": (1, 1), "tn": (0, 0)}


@functools.partial(jax.custom_vjp, nondiff_argnums=(2, 3))
def _mmf(a, b, form, hi):
    ca, cb = _FORMS[form]
    return _dg(a, b, ca, cb, hi)


def _mmf_fwd(a, b, form, hi):
    return _mmf(a, b, form, hi), (a, b)


def _mmf_bwd(form, hi, res, g):
    a, b = res
    if form == "nn":
        return _dg(g, b, 1, 1, hi), _dg(a, g, 0, 0, hi)
    if form == "nt":
        return _dg(g, b, 1, 0, hi), _dg(g, a, 0, 0, hi)
    return _dg(b, g, 1, 1, hi), _dg(a, g, 1, 0, hi)


_mmf.defvjp(_mmf_fwd, _mmf_bwd)


def _mm(a, b):
    return _mmf(a, b, "nn", False)


def _mm_nt(a, b):
    return _mmf(a, b, "nt", False)


def _mm_tn(a, b):
    return _mmf(a, b, "tn", False)


def _mmh(a, b):
    return _mmf(a, b, "nn", True)


def _matmul(a, b, *, form="nn", add=None, out_dtype=F32, name):
    ca, cb = _FORMS[form]
    m, k = (a.shape[1], a.shape[0]) if form == "tn" else a.shape
    n = b.shape[0] if form == "nt" else b.shape[1]
    tm, tn, tk = _tile(m, 512, LANES), _tile(n, 1024, LANES), _tile(k, 1024, LANES)
    nk = k // tk
    a_spec = (pl.BlockSpec((tk, tm), lambda i, j, kk: (kk, i)) if form == "tn"
              else pl.BlockSpec((tm, tk), lambda i, j, kk: (i, kk)))
    b_spec = (pl.BlockSpec((tn, tk), lambda i, j, kk: (j, kk)) if form == "nt"
              else pl.BlockSpec((tk, tn), lambda i, j, kk: (kk, j)))
    o_spec = pl.BlockSpec((tm, tn), lambda i, j, kk: (i, j))

    def body(*refs):
        if add is None:
            a_ref, b_ref, o_ref, acc_ref = refs
        else:
            a_ref, b_ref, add_ref, o_ref, acc_ref = refs
        kk = pl.program_id(2)

        @pl.when(kk == 0)
        def _():
            acc_ref[...] = jnp.zeros_like(acc_ref)

        acc_ref[...] += _dg(a_ref[...], b_ref[...], ca, cb, False)

        @pl.when(kk == nk - 1)
        def _():
            r = acc_ref[...]
            if add is not None:
                r = r + add_ref[...]
            o_ref[...] = r.astype(out_dtype)

    args = (a, b) if add is None else (a, b, add)
    specs = [a_spec, b_spec] if add is None else [a_spec, b_spec, o_spec]
    return pl.pallas_call(
        body, name=name, grid=(m // tm, n // tn, nk), in_specs=specs, out_specs=o_spec,
        out_shape=_sds((m, n), out_dtype), scratch_shapes=[pltpu.VMEM((tm, tn), F32)],
        compiler_params=_params("parallel", "parallel", "arbitrary"))(*args)


def _rms(x, g):
    return x * lax.rsqrt(jnp.mean(x * x, axis=-1, keepdims=True) + EPS) * g


def _rmsnorm_fwd(x, g, *, name):
    t, d = x.shape
    tb = _tile(t, 512, SUBLANES)

    def body(x_ref, g_ref, o_ref):
        o_ref[...] = _rms(x_ref[...], g_ref[...]).astype(o_ref.dtype)

    row = pl.BlockSpec((tb, d), lambda i: (i, 0))
    vec = pl.BlockSpec((1, d), lambda i: (0, 0))
    return pl.pallas_call(body, name=name, grid=(t // tb,), in_specs=[row, vec], out_specs=row,
                          out_shape=_sds((t, d), MM_DTYPE), compiler_params=_params("parallel"))(x, g.reshape(1, d))


def _rmsnorm_bwd(x, g, dh, res, *, name):
    t, d = x.shape
    tb = _tile(t, 256, SUBLANES)

    def body(x_ref, g_ref, dh_ref, res_ref, dx_ref, dg_ref):
        _, vjp = jax.vjp(_rms, x_ref[...], g_ref[...])
        dx, dg = vjp(dh_ref[...])
        dx_ref[...] = dx + res_ref[...]

        @pl.when(pl.program_id(0) == 0)
        def _():
            dg_ref[...] = jnp.zeros_like(dg_ref)

        dg_ref[...] += dg

    row = pl.BlockSpec((tb, d), lambda i: (i, 0))
    vec = pl.BlockSpec((1, d), lambda i: (0, 0))
    dx, dg = pl.pallas_call(body, name=name, grid=(t // tb,), in_specs=[row, vec, row, row], out_specs=[row, vec],
                            out_shape=[_sds((t, d)), _sds((1, d))], compiler_params=_params("arbitrary"))(
                                x, g.reshape(1, d), dh, res)
    return dx, dg.reshape(d)


def _loss_head(x, g, tgt, *, name):
    t, d = x.shape
    tb = _tile(t, 256, SUBLANES)

    def body(x_ref, g_ref, t_ref, loss_ref, dx_ref, dg_ref):
        def f(xv, gv):
            e = _rms(xv, gv) - t_ref[...]
            return jnp.sum(jnp.sum(e * e, axis=-1, keepdims=True), axis=0, keepdims=True) * (0.5 / d)

        l, vjp = jax.vjp(f, x_ref[...], g_ref[...])
        dx, dg = vjp(jnp.ones((1, 1), F32))
        dx_ref[...] = dx

        @pl.when(pl.program_id(0) == 0)
        def _():
            dg_ref[...] = jnp.zeros_like(dg_ref)
            loss_ref[...] = jnp.zeros_like(loss_ref)

        dg_ref[...] += dg
        loss_ref[...] += jnp.zeros(loss_ref.shape, F32) + l

    row = pl.BlockSpec((tb, d), lambda i: (i, 0))
    vec = pl.BlockSpec((1, d), lambda i: (0, 0))
    lspec = pl.BlockSpec((SUBLANES, LANES), lambda i: (0, 0))
    loss, dx, dg = pl.pallas_call(
        body, name=name, grid=(t // tb,), in_specs=[row, vec, row], out_specs=[lspec, row, vec],
        out_shape=[_sds((SUBLANES, LANES)), _sds((t, d)), _sds((1, d))], compiler_params=_params("arbitrary"))(
            x, g.reshape(1, d), tgt)
    return loss[0, 0], dx, dg.reshape(d)


def _delayed(x, prev, j):
    if j == 0:
        return x
    sh = pltpu.roll(x, j, axis=0)
    row = lax.broadcasted_iota(jnp.int32, prev.shape, 0)
    top = jnp.where(row < j, pltpu.roll(prev, j, axis=0), sh[0:SUBLANES])
    return jnp.concatenate([top, sh[SUBLANES:]], axis=0)


def _advanced(x, nxt, j):
    if j == 0:
        return x
    tb = x.shape[0]
    sh = pltpu.roll(x, tb - j, axis=0)
    row = lax.broadcasted_iota(jnp.int32, nxt.shape, 0)
    bot = jnp.where(row + j < SUBLANES, sh[tb - SUBLANES:], pltpu.roll(nxt, SUBLANES - j, axis=0))
    return jnp.concatenate([sh[:tb - SUBLANES], bot], axis=0)


def _conv_tiles(t, ncols, coff):
    tc = _tile(ncols, 512, LANES)
    assert coff % tc == 0
    tb = _tile(t, 512, SUBLANES)
    return tc, tb, coff // tc


def _conv_fwd(x, w, b, *, ncols, coff=0, name):
    t = x.shape[0]
    kw = w.shape[0]
    tc, tb, cb = _conv_tiles(t, ncols, coff)
    n8 = tb // SUBLANES

    def body(x_ref, p_ref, w_ref, b_ref, o_ref):
        xv = x_ref[...]
        prev = jnp.where(pl.program_id(1) > 0, p_ref[...], 0.0)
        acc = jnp.zeros_like(xv) + b_ref[...]
        for k in range(kw):
            acc = acc + w_ref[k:k + 1, :] * _delayed(xv, prev, kw - 1 - k)
        o_ref[...] = acc

    in_specs = [
        pl.BlockSpec((tb, tc), lambda c, i: (i, c + cb)),
        pl.BlockSpec((SUBLANES, tc), lambda c, i: (jnp.maximum(i * n8 - 1, 0), c + cb)),
        pl.BlockSpec((kw, tc), lambda c, i: (0, c)),
        pl.BlockSpec((1, tc), lambda c, i: (0, c)),
    ]
    return pl.pallas_call(
        body, name=name, grid=(ncols // tc, t // tb), in_specs=in_specs,
        out_specs=pl.BlockSpec((tb, tc), lambda c, i: (i, c)), out_shape=_sds((t, ncols)),
        compiler_params=_params("parallel", "parallel"))(x, x, w, b.reshape(1, ncols))


def _conv_bwd(x, w, dy, *, ncols, coff=0, name):
    t = x.shape[0]
    kw = w.shape[0]
    tc, tb, cb = _conv_tiles(t, ncols, coff)
    n8 = tb // SUBLANES
    nt = t // tb

    def body(x_ref, p_ref, dy_ref, n_ref, w_ref, dx_ref, dw_ref, db_ref):
        i = pl.program_id(1)
        xv, dyv = x_ref[...], dy_ref[...]
        prev = jnp.where(i > 0, p_ref[...], 0.0)
        nxt = jnp.where(i < nt - 1, n_ref[...], 0.0)

        @pl.when(i == 0)
        def _():
            dw_ref[...] = jnp.zeros_like(dw_ref)
            db_ref[...] = jnp.zeros_like(db_ref)

        dx = jnp.zeros_like(dyv)
        for k in range(kw):
            j = kw - 1 - k
            dx = dx + w_ref[k:k + 1, :] * _advanced(dyv, nxt, j)
            dw_ref[k:k + 1, :] += jnp.sum(dyv * _delayed(xv, prev, j), axis=0, keepdims=True)
        dx_ref[...] = dx
        db_ref[...] += jnp.sum(dyv, axis=0, keepdims=True)

    in_specs = [
        pl.BlockSpec((tb, tc), lambda c, i: (i, c + cb)),
        pl.BlockSpec((SUBLANES, tc), lambda c, i: (jnp.maximum(i * n8 - 1, 0), c + cb)),
        pl.BlockSpec((tb, tc), lambda c, i: (i, c)),
        pl.BlockSpec((SUBLANES, tc), lambda c, i: (jnp.minimum((i + 1) * n8, nt * n8 - 1), c)),
        pl.BlockSpec((kw, tc), lambda c, i: (0, c)),
    ]
    out_specs = [
        pl.BlockSpec((tb, tc), lambda c, i: (i, c)),
        pl.BlockSpec((kw, tc), lambda c, i: (0, c)),
        pl.BlockSpec((1, tc), lambda c, i: (0, c)),
    ]
    dx, dw, db = pl.pallas_call(
        body, name=name, grid=(ncols // tc, nt), in_specs=in_specs, out_specs=out_specs,
        out_shape=[_sds((t, ncols)), _sds((kw, ncols)), _sds((1, ncols))],
        compiler_params=_params("parallel", "arbitrary"))(x, x, dy, dy, w)
    return dx, dw, db.reshape(ncols)


def _scan(a, u, *, name):
    t, c = u.shape
    tc = _tile(c, 512, LANES)
    tb = _tile(t, 256, SUBLANES)

    def body(*refs):
        if a is None:
            u_ref, o_ref, carry_ref = refs
        else:
            a_ref, u_ref, o_ref, carry_ref = refs

        @pl.when(pl.program_id(1) == 0)
        def _():
            carry_ref[...] = jnp.zeros_like(carry_ref)

        hv = u_ref[...]
        av = None if a is None else a_ref[...]
        row = lax.broadcasted_iota(jnp.int32, hv.shape, 0)
        s = 1
        while s < tb:
            live = row >= s
            h_sh = jnp.where(live, pltpu.roll(hv, s, axis=0), 0.0)
            if av is None:
                hv = hv + h_sh
            else:
                hv = av * h_sh + hv
                av = av * jnp.where(live, pltpu.roll(av, s, axis=0), 1.0)
            s *= 2
        hv = hv + carry_ref[...] if av is None else hv + av * carry_ref[...]
        o_ref[...] = hv
        carry_ref[...] = o_ref[pl.ds(tb - 1, 1), :]

    spec = pl.BlockSpec((tb, tc), lambda cc, i: (i, cc))
    args, specs = ((u,), [spec]) if a is None else ((a, u), [spec, spec])
    return pl.pallas_call(
        body, name=name, grid=(c // tc, t // tb), in_specs=specs, out_specs=spec, out_shape=_sds((t, c)),
        scratch_shapes=[pltpu.VMEM((1, tc), F32)], compiler_params=_params("parallel", "arbitrary"))(*args)


def _shift_down(x):
    return lax.pad(x, jnp.zeros((), x.dtype), ((1, -1, 0), (0, 0, 0)))


def _shift_up(x):
    return lax.pad(x, jnp.zeros((), x.dtype), ((-1, 1, 0), (0, 0, 0)))


def _scan_reverse(a, u, *, name):
    ur = jnp.flip(u, axis=0)
    ar = None if a is None else jnp.flip(_shift_up(a), axis=0)
    return jnp.flip(_scan(ar, ur, name=name), axis=0)


def _neg_expm1(y):
    small = -(y * (1.0 + y * (0.5 + y * (1.0 / 6.0 + y * (1.0 / 24.0 + y * (1.0 / 120.0))))))
    return jnp.where(y > -0.05, small, 1.0 - jnp.exp(y))


def _lru_gates(xc, wa, ba, wx, bx, lam):
    r = jax.nn.sigmoid(_mm(xc, wa) + ba)
    i = jax.nn.sigmoid(_mm(xc, wx) + bx)
    log_a = -LRU_C * r * jax.nn.softplus(-lam)
    a = jnp.exp(log_a)
    u = jnp.sqrt(_neg_expm1(2.0 * log_a)) * (i * xc)
    return a, u


def _lru_specs(t):
    tb = _tile(t, 256, SUBLANES)
    row = pl.BlockSpec((tb, GROUP), lambda i: (i, 0))
    mat = pl.BlockSpec((GROUP, GROUP), lambda i: (0, 0))
    vec = pl.BlockSpec((1, GROUP), lambda i: (0, 0))
    return tb, row, mat, vec


def _lru_gates_fwd(xc, wa, ba, wx, bx, lam, *, name):
    t = xc.shape[0]
    tb, row, mat, vec = _lru_specs(t)

    def body(xc_ref, wa_ref, ba_ref, wx_ref, bx_ref, lam_ref, a_ref, u_ref):
        a, u = _lru_gates(xc_ref[...], wa_ref[...], ba_ref[...], wx_ref[...], bx_ref[...], lam_ref[...])
        a_ref[...] = a
        u_ref[...] = u

    return pl.pallas_call(
        body, name=name, grid=(t // tb,), in_specs=[row, mat, vec, mat, vec, vec], out_specs=[row, row],
        out_shape=[_sds((t, GROUP)), _sds((t, GROUP))], compiler_params=_params("parallel"))(xc, wa, ba, wx, bx, lam)


def _lru_gates_bwd(xc, wa, ba, wx, bx, lam, g, h_prev, *, name):
    t = xc.shape[0]
    tb, row, mat, vec = _lru_specs(t)

    def body(xc_ref, wa_ref, ba_ref, wx_ref, bx_ref, lam_ref, g_ref, hp_ref,
             dxc_ref, dwa_ref, dba_ref, dwx_ref, dbx_ref, dlam_ref):
        _, vjp = jax.vjp(_lru_gates, xc_ref[...], wa_ref[...], ba_ref[...], wx_ref[...], bx_ref[...], lam_ref[...])
        gv = g_ref[...]
        dxc, dwa, dba, dwx, dbx, dlam = vjp((gv * hp_ref[...], gv))
        dxc_ref[...] = dxc
        accs = (dwa_ref, dba_ref, dwx_ref, dbx_ref, dlam_ref)

        @pl.when(pl.program_id(0) == 0)
        def _():
            for r in accs:
                r[...] = jnp.zeros_like(r)

        for r, v in zip(accs, (dwa, dba, dwx, dbx, dlam)):
            r[...] += v

    return pl.pallas_call(
        body, name=name, grid=(t // tb,), in_specs=[row, mat, vec, mat, vec, vec, row, row],
        out_specs=[row, mat, vec, mat, vec, vec],
        out_shape=[_sds((t, GROUP)), _sds((GROUP, GROUP)), _sds((1, GROUP)), _sds((GROUP, GROUP)), _sds((1, GROUP)),
                   _sds((1, GROUP))],
        compiler_params=_params("arbitrary"))(xc, wa, ba, wx, bx, lam, g, h_prev)


def _block_diag(w):
    eye = jnp.eye(LRU_BLOCKS, dtype=w.dtype)
    return (eye[:, None, :, None] * w[:, :, None, :]).reshape(GROUP, GROUP)


def _diag_blocks(m):
    m4 = m.reshape(LRU_BLOCKS, LRU_BD, LRU_BLOCKS, LRU_BD)
    return jnp.stack([m4[n, :, n, :] for n in range(LRU_BLOCKS)])


def _post(hg, gain, gate):
    y = hg * lax.rsqrt(jnp.mean(hg * hg, axis=-1, keepdims=True) + EPS) * gain
    return y if gate is None else y * jax.nn.gelu(gate)


def _post_specs(rows, g):
    rb = _tile(rows, 2048, SUBLANES)
    return rb, pl.BlockSpec((rb, g), lambda i: (i, 0)), pl.BlockSpec((rb, g), lambda i: (0, 0))


def _post_fwd(hg, gain8, gate, *, name):
    rows, g = hg.shape
    rb, row, fixed = _post_specs(rows, g)
    gain_t = jnp.tile(gain8, (rb // SUBLANES, 1))

    def body(*refs):
        if gate is None:
            h_ref, gn_ref, o_ref = refs
            o_ref[...] = _post(h_ref[...], gn_ref[...], None).astype(o_ref.dtype)
        else:
            h_ref, gn_ref, gt_ref, o_ref = refs
            o_ref[...] = _post(h_ref[...], gn_ref[...], gt_ref[...]).astype(o_ref.dtype)

    args, specs = ((hg, gain_t), [row, fixed]) if gate is None else ((hg, gain_t, gate), [row, fixed, row])
    return pl.pallas_call(body, name=name, grid=(rows // rb,), in_specs=specs, out_specs=row,
                          out_shape=_sds((rows, g), MM_DTYPE), compiler_params=_params("parallel"))(*args)


def _post_bwd(hg, gain8, gate, dy, *, name):
    rows, g = hg.shape
    rb, row, fixed = _post_specs(rows, g)
    gain_t = jnp.tile(gain8, (rb // SUBLANES, 1))
    g8 = pl.BlockSpec((SUBLANES, g), lambda i: (0, 0))

    def body(*refs):
        if gate is None:
            h_ref, gn_ref, dy_ref, dh_ref, dgn_ref = refs
            _, vjp = jax.vjp(lambda h, gn: _post(h, gn, None), h_ref[...], gn_ref[...])
            dh, dgn = vjp(dy_ref[...])
        else:
            h_ref, gn_ref, gt_ref, dy_ref, dh_ref, dgn_ref, dgt_ref = refs
            _, vjp = jax.vjp(_post, h_ref[...], gn_ref[...], gt_ref[...])
            dh, dgn, dgt = vjp(dy_ref[...])
            dgt_ref[...] = dgt
        dh_ref[...] = dh

        @pl.when(pl.program_id(0) == 0)
        def _():
            dgn_ref[...] = jnp.zeros_like(dgn_ref)

        dgn_ref[...] += dgn.reshape(rb // SUBLANES, SUBLANES, g).sum(axis=0)

    if gate is None:
        dh, dgn = pl.pallas_call(
            body, name=name, grid=(rows // rb,), in_specs=[row, fixed, row], out_specs=[row, g8],
            out_shape=[_sds((rows, g)), _sds((SUBLANES, g))], compiler_params=_params("arbitrary"))(hg, gain_t, dy)
        return dh, dgn, None
    dh, dgn, dgt = pl.pallas_call(
        body, name=name, grid=(rows // rb,), in_specs=[row, fixed, row, row], out_specs=[row, g8, row],
        out_shape=[_sds((rows, g)), _sds((SUBLANES, g)), _sds((rows, g))],
        compiler_params=_params("arbitrary"))(hg, gain_t, gate, dy)
    return dh, dgn, dgt


def _gates(zg, fb, alog, dtb):
    lane = lax.broadcasted_iota(jnp.int32, zg.shape, 1)
    logf = jax.nn.log_sigmoid(zg + fb)
    beta = jax.nn.sigmoid(zg)
    gdec = -jnp.exp(alog) * jax.nn.softplus(zg + dtb)
    return jnp.where(lane < G_BETA, logf, jnp.where(lane < G_ALPHA, beta, jnp.where(lane < G_ALPHA + 4, gdec, 0.0)))


def _lane_row(v, off):
    return jnp.pad(v.reshape(1, N_HEADS), ((0, 0), (off, LANES - N_HEADS - off)))


def _gates_fwd(z, fb, alog, dtb, *, name):
    t = z.shape[0]
    tb = _tile(t, 1024, SUBLANES)
    zspec = pl.BlockSpec((tb, LANES), lambda i: (i, Z_GATES // LANES))
    row = pl.BlockSpec((tb, LANES), lambda i: (i, 0))
    vec = pl.BlockSpec((1, LANES), lambda i: (0, 0))

    def body(z_ref, fb_ref, al_ref, dt_ref, o_ref):
        o_ref[...] = _gates(z_ref[...], fb_ref[...], al_ref[...], dt_ref[...])

    return pl.pallas_call(body, name=name, grid=(t // tb,), in_specs=[zspec, vec, vec, vec], out_specs=row,
                          out_shape=_sds((t, LANES)), compiler_params=_params("parallel"))(z, fb, alog, dtb)


def _gates_bwd(z, fb, alog, dtb, dg, *, name):
    t = z.shape[0]
    tb = _tile(t, 1024, SUBLANES)
    zspec = pl.BlockSpec((tb, LANES), lambda i: (i, Z_GATES // LANES))
    row = pl.BlockSpec((tb, LANES), lambda i: (i, 0))
    vec = pl.BlockSpec((1, LANES), lambda i: (0, 0))

    def body(z_ref, fb_ref, al_ref, dt_ref, dg_ref, dz_ref, dfb_ref, dal_ref, ddt_ref):
        _, vjp = jax.vjp(_gates, z_ref[...], fb_ref[...], al_ref[...], dt_ref[...])
        dz, dfb, dal, ddt = vjp(dg_ref[...])
        dz_ref[...] = dz
        accs = (dfb_ref, dal_ref, ddt_ref)

        @pl.when(pl.program_id(0) == 0)
        def _():
            for r in accs:
                r[...] = jnp.zeros_like(r)

        for r, v in zip(accs, (dfb, dal, ddt)):
            r[...] += v

    return pl.pallas_call(
        body, name=name, grid=(t // tb,), in_specs=[zspec, vec, vec, vec, row], out_specs=[row, vec, vec, vec],
        out_shape=[_sds((t, LANES)), _sds((1, LANES)), _sds((1, LANES)), _sds((1, LANES))],
        compiler_params=_params("arbitrary"))(z, fb, alog, dtb, dg)


def _pair_weight(mode, d):
    if mode == "fox":
        return (d >= 0).astype(F32)
    w = jnp.zeros(d.shape, F32)
    for win, dil in DILATED_PAIRS:
        w = w + ((d >= 0) & (d <= win) & ((d & (dil - 1)) == 0)).astype(F32)
    return w


def _att_geometry(t, mode):
    tq = _tile(t, 256, LANES)
    nq = t // tq
    band = nq - 1 if mode == "fox" else min(DILATED_PAIRS[-1][0] // tq, nq - 1)
    return tq, nq, band


def _att_scores(mode, q_ref, k_ref, cq_ref, ck_ref, qi, kj, tq):
    qs = (q_ref[...] * ATT_SCALE).astype(MM_DTYPE)
    s = _dg(qs, k_ref[...], 1, 1, False)
    if mode == "fox":
        s = s + (cq_ref[0] - ck_ref[0])
    qpos = qi * tq + lax.broadcasted_iota(jnp.int32, s.shape, 0)
    kpos = kj * tq + lax.broadcasted_iota(jnp.int32, s.shape, 1)
    w = _pair_weight(mode, qpos - kpos)
    return qs, jnp.where(w > 0.0, s, NEG), w


def _att_fwd(z, cq, ck, *, mode, qoff, koff, voff, name):
    t = z.shape[0]
    tq, nq, band = _att_geometry(t, mode)
    qb, kb, vb = qoff // HEAD, koff // HEAD, voff // HEAD

    def kidx(i, j):
        return i - band + j

    def body(q_ref, k_ref, v_ref, cq_ref, ck_ref, o_ref, lse_ref, m_ref, l_ref, acc_ref):
        i, j = pl.program_id(1), pl.program_id(2)

        @pl.when(j == 0)
        def _():
            m_ref[...] = jnp.full_like(m_ref, NEG)
            l_ref[...] = jnp.zeros_like(l_ref)
            acc_ref[...] = jnp.zeros_like(acc_ref)

        @pl.when(kidx(i, j) >= 0)
        def _():
            _, s, w = _att_scores(mode, q_ref, k_ref, cq_ref, ck_ref, i, kidx(i, j), tq)
            m_old = m_ref[...]
            m_new = jnp.maximum(m_old, jnp.max(s, axis=-1, keepdims=True))
            alpha = jnp.exp(m_old - m_new)
            p = w * jnp.exp(s - m_new)
            l_ref[...] = alpha * l_ref[...] + jnp.sum(p, axis=-1, keepdims=True)
            acc_ref[...] = alpha * acc_ref[...] + _dg(p, v_ref[...], 1, 0, False)
            m_ref[...] = m_new

        @pl.when(j == band)
        def _():
            o_ref[...] = acc_ref[...] / l_ref[...]
            lse_ref[0] = m_ref[...] + jnp.log(l_ref[...])

    kmap = lambda h, i, j: (jnp.maximum(kidx(i, j), 0), kb + h)
    in_specs = [
        pl.BlockSpec((tq, HEAD), lambda h, i, j: (i, qb + h)),
        pl.BlockSpec((tq, HEAD), kmap),
        pl.BlockSpec((tq, HEAD), lambda h, i, j: (jnp.maximum(kidx(i, j), 0), vb + h)),
        pl.BlockSpec((1, tq, 1), lambda h, i, j: (h, i, 0)),
        pl.BlockSpec((1, 1, tq), lambda h, i, j: (h, 0, jnp.maximum(kidx(i, j), 0))),
    ]
    out_specs = [pl.BlockSpec((tq, HEAD), lambda h, i, j: (i, h)), pl.BlockSpec((1, tq, 1), lambda h, i, j: (h, i, 0))]
    return pl.pallas_call(
        body, name=name, grid=(N_HEADS, nq, band + 1), in_specs=in_specs, out_specs=out_specs,
        out_shape=[_sds((t, GROUP)), _sds((N_HEADS, t, 1))],
        scratch_shapes=[pltpu.VMEM((tq, 1), F32), pltpu.VMEM((tq, 1), F32), pltpu.VMEM((tq, HEAD), F32)],
        compiler_params=_params("parallel", "parallel", "arbitrary"))(z, z, z, cq, ck)


def _att_bwd_terms(mode, q_ref, k_ref, v_ref, cq_ref, ck_ref, do_ref, o_ref, lse_ref, qi, kj, tq):
    qs, s, w = _att_scores(mode, q_ref, k_ref, cq_ref, ck_ref, qi, kj, tq)
    p = w * jnp.exp(s - lse_ref[0])
    dov = do_ref[...]
    delta = jnp.sum(dov * o_ref[...], axis=-1, keepdims=True)
    dp = _dg(dov, v_ref[...], 1, 1, False)
    ds = p * (dp - delta)
    return qs, p, ds, dov


def _att_bwd_q(z, cq, ck, do, o, lse, *, mode, qoff, koff, voff, name):
    t = z.shape[0]
    tq, nq, band = _att_geometry(t, mode)
    qb, kb, vb = qoff // HEAD, koff // HEAD, voff // HEAD

    def kidx(i, j):
        return i - band + j

    def body(q_ref, k_ref, v_ref, cq_ref, ck_ref, do_ref, o_ref, lse_ref, dq_ref, dcq_ref):
        i, j = pl.program_id(1), pl.program_id(2)

        @pl.when(j == 0)
        def _():
            dq_ref[...] = jnp.zeros_like(dq_ref)
            dcq_ref[...] = jnp.zeros_like(dcq_ref)

        @pl.when(kidx(i, j) >= 0)
        def _():
            _, _, ds, _ = _att_bwd_terms(mode, q_ref, k_ref, v_ref, cq_ref, ck_ref, do_ref, o_ref, lse_ref,
                                         i, kidx(i, j), tq)
            dq_ref[...] += _dg(ds, k_ref[...], 1, 0, False) * ATT_SCALE
            dcq_ref[0] += jnp.sum(ds, axis=-1, keepdims=True)

    qrow = lambda off: pl.BlockSpec((tq, HEAD), lambda h, i, j: (i, off + h))
    krow = lambda off: pl.BlockSpec((tq, HEAD), lambda h, i, j: (jnp.maximum(kidx(i, j), 0), off + h))
    in_specs = [
        qrow(qb), krow(kb), krow(vb),
        pl.BlockSpec((1, tq, 1), lambda h, i, j: (h, i, 0)),
        pl.BlockSpec((1, 1, tq), lambda h, i, j: (h, 0, jnp.maximum(kidx(i, j), 0))),
        qrow(0), qrow(0),
        pl.BlockSpec((1, tq, 1), lambda h, i, j: (h, i, 0)),
    ]
    return pl.pallas_call(
        body, name=name, grid=(N_HEADS, nq, band + 1), in_specs=in_specs,
        out_specs=[qrow(0), pl.BlockSpec((1, tq, 1), lambda h, i, j: (h, i, 0))],
        out_shape=[_sds((t, GROUP)), _sds((N_HEADS, t, 1))],
        compiler_params=_params("parallel", "parallel", "arbitrary"))(z, z, z, cq, ck, do, o, lse)


def _att_bwd_kv(z, cq, ck, do, o, lse, *, mode, qoff, koff, voff, name):
    t = z.shape[0]
    tq, nq, band = _att_geometry(t, mode)
    qb, kb, vb = qoff // HEAD, koff // HEAD, voff // HEAD

    def body(q_ref, k_ref, v_ref, cq_ref, ck_ref, do_ref, o_ref, lse_ref, dk_ref, dv_ref, dck_ref):
        jk, r = pl.program_id(1), pl.program_id(2)
        qi = jk + r

        @pl.when(r == 0)
        def _():
            dk_ref[...] = jnp.zeros_like(dk_ref)
            dv_ref[...] = jnp.zeros_like(dv_ref)
            dck_ref[...] = jnp.zeros_like(dck_ref)

        @pl.when(qi < nq)
        def _():
            qs, p, ds, dov = _att_bwd_terms(mode, q_ref, k_ref, v_ref, cq_ref, ck_ref, do_ref, o_ref, lse_ref,
                                            qi, jk, tq)
            dv_ref[...] += _dg(p, dov, 0, 0, False)
            dk_ref[...] += _dg(ds, qs, 0, 0, False)
            dck_ref[0] += -jnp.sum(ds, axis=0, keepdims=True)

    qrow = lambda off: pl.BlockSpec((tq, HEAD), lambda h, jk, r: (jnp.minimum(jk + r, nq - 1), off + h))
    krow = lambda off: pl.BlockSpec((tq, HEAD), lambda h, jk, r: (jk, off + h))
    in_specs = [
        qrow(qb), krow(kb), krow(vb),
        pl.BlockSpec((1, tq, 1), lambda h, jk, r: (h, jnp.minimum(jk + r, nq - 1), 0)),
        pl.BlockSpec((1, 1, tq), lambda h, jk, r: (h, 0, jk)),
        qrow(0), qrow(0),
        pl.BlockSpec((1, tq, 1), lambda h, jk, r: (h, jnp.minimum(jk + r, nq - 1), 0)),
    ]
    out_specs = [krow(0), krow(0), pl.BlockSpec((1, 1, tq), lambda h, jk, r: (h, 0, jk))]
    return pl.pallas_call(
        body, name=name, grid=(N_HEADS, nq, band + 1), in_specs=in_specs, out_specs=out_specs,
        out_shape=[_sds((t, GROUP)), _sds((t, GROUP)), _sds((N_HEADS, 1, t))],
        compiler_params=_params("parallel", "parallel", "arbitrary"))(z, z, z, cq, ck, do, o, lse)


def _silu(x):
    return x * jax.nn.sigmoid(x)


def _l2n(x):
    return x * lax.rsqrt(jnp.sum(x * x, axis=-1, keepdims=True) + EPS)


def _gdn_chunk(states, xqkv, zg, gts, ng):
    c = GDN_CHUNK
    ri = lax.broadcasted_iota(jnp.int32, (c, c), 0)
    ci = lax.broadcasted_iota(jnp.int32, (c, c), 1)
    tril, strict, eye = ri >= ci, ri > ci, ri == ci
    eyef = eye.astype(F32)
    gcs = _mmh(tril.astype(F32), gts)
    lane = lax.broadcasted_iota(jnp.int32, gts.shape, 1)
    last = lax.broadcasted_iota(jnp.int32, (c, 1), 0) == c - 1
    ys, new_states = [], []
    for h in range(N_HEADS):
        q = _l2n(_silu(xqkv[:, h * HEAD:(h + 1) * HEAD])) * ATT_SCALE
        k = _l2n(_silu(xqkv[:, GROUP + h * HEAD:GROUP + (h + 1) * HEAD]))
        v = _silu(xqkv[:, 2 * GROUP + h * HEAD:2 * GROUP + (h + 1) * HEAD])
        beta = jnp.sum(jnp.where(lane == G_BETA + h, gts, 0.0), axis=-1, keepdims=True)
        gc = jnp.sum(jnp.where(lane == G_ALPHA + h, gcs, 0.0), axis=-1, keepdims=True)
        gr = jnp.sum(jnp.where(eye, jnp.broadcast_to(gc, (c, c)), 0.0), axis=0, keepdims=True)
        decay = jnp.where(tril, jnp.exp(jnp.where(tril, gc - gr, 0.0)), 0.0)
        kbeta, vbeta = k * beta, v * beta
        low = jnp.where(strict, _mm_nt(kbeta, k) * decay, 0.0)
        inv, pw = eyef - low, _mmh(low, low)
        for step in range(5):
            inv = inv + _mmh(inv, pw)
            if step < 4:
                pw = _mmh(pw, pw)
        eg = jnp.exp(gc)
        u = _mmh(inv, vbeta)
        w = _mmh(inv, kbeta * eg)
        qk = jnp.where(tril, _mm_nt(q, k) * decay, 0.0)
        s_in = states[h]
        v_new = u - _mm(w, s_in)
        o = _mm(q * eg, s_in) + _mm(qk, v_new)
        g_last = jnp.sum(jnp.where(last, gc, 0.0), axis=0, keepdims=True)
        new_states.append(s_in * jnp.exp(g_last) + _mm_tn(k * jnp.exp(g_last - gc), v_new))
        o = o * lax.rsqrt(jnp.mean(o * o, axis=-1, keepdims=True) + EPS) * ng
        ys.append(o * _silu(zg[:, h * HEAD:(h + 1) * HEAD]))
    return jnp.concatenate(ys, axis=1), tuple(new_states)


def _gdn_fwd(xqkv, z, gts, ng, *, name):
    t = xqkv.shape[0]
    n = t // GDN_CHUNK
    c = GDN_CHUNK

    def body(x_ref, z_ref, g_ref, ng_ref, y_ref, sv_ref, s_ref):
        @pl.when(pl.program_id(0) == 0)
        def _():
            s_ref[...] = jnp.zeros_like(s_ref)

        sv_ref[0] = s_ref[...]
        y, new = _gdn_chunk(tuple(s_ref[h] for h in range(N_HEADS)), x_ref[...], z_ref[...], g_ref[...], ng_ref[...])
        y_ref[...] = y.astype(y_ref.dtype)
        for h in range(N_HEADS):
            s_ref[h] = new[h]

    in_specs = [
        pl.BlockSpec((c, 3 * GROUP), lambda i: (i, 0)),
        pl.BlockSpec((c, GROUP), lambda i: (i, Z_CZ // GROUP)),
        pl.BlockSpec((c, LANES), lambda i: (i, 0)),
        pl.BlockSpec((1, HEAD), lambda i: (0, 0)),
    ]
    out_specs = [pl.BlockSpec((c, GROUP), lambda i: (i, 0)),
                 pl.BlockSpec((1, N_HEADS, HEAD, HEAD), lambda i: (i, 0, 0, 0))]
    return pl.pallas_call(
        body, name=name, grid=(n,), in_specs=in_specs, out_specs=out_specs,
        out_shape=[_sds((t, GROUP), MM_DTYPE), _sds((n, N_HEADS, HEAD, HEAD))],
        scratch_shapes=[pltpu.VMEM((N_HEADS, HEAD, HEAD), F32)], compiler_params=_params("arbitrary"))(xqkv, z, gts, ng)


def _gdn_bwd(xqkv, z, gts, ng, states, dy, *, name):
    t = xqkv.shape[0]
    n = t // GDN_CHUNK
    c = GDN_CHUNK

    def body(x_ref, z_ref, g_ref, ng_ref, sv_ref, dy_ref, dx_ref, dz_ref, dg_ref, dng_ref, ds_ref):
        @pl.when(pl.program_id(0) == 0)
        def _():
            ds_ref[...] = jnp.zeros_like(ds_ref)
            dng_ref[...] = jnp.zeros_like(dng_ref)

        s_in = tuple(sv_ref[0, h] for h in range(N_HEADS))
        _, vjp = jax.vjp(_gdn_chunk, s_in, x_ref[...], z_ref[...], g_ref[...], ng_ref[...])
        ds_in, dx, dz, dg, dng = vjp((dy_ref[...], tuple(ds_ref[h] for h in range(N_HEADS))))
        dx_ref[...] = dx
        dz_ref[...] = dz
        dg_ref[...] = dg
        dng_ref[...] += dng
        for h in range(N_HEADS):
            ds_ref[h] = ds_in[h]

    rev = lambda i: n - 1 - i
    in_specs = [
        pl.BlockSpec((c, 3 * GROUP), lambda i: (rev(i), 0)),
        pl.BlockSpec((c, GROUP), lambda i: (rev(i), Z_CZ // GROUP)),
        pl.BlockSpec((c, LANES), lambda i: (rev(i), 0)),
        pl.BlockSpec((1, HEAD), lambda i: (0, 0)),
        pl.BlockSpec((1, N_HEADS, HEAD, HEAD), lambda i: (rev(i), 0, 0, 0)),
        pl.BlockSpec((c, GROUP), lambda i: (rev(i), 0)),
    ]
    out_specs = [
        pl.BlockSpec((c, 3 * GROUP), lambda i: (rev(i), 0)),
        pl.BlockSpec((c, GROUP), lambda i: (rev(i), 0)),
        pl.BlockSpec((c, LANES), lambda i: (rev(i), 0)),
        pl.BlockSpec((1, HEAD), lambda i: (0, 0)),
    ]
    dx, dz, dg, dng = pl.pallas_call(
        body, name=name, grid=(n,), in_specs=in_specs, out_specs=out_specs,
        out_shape=[_sds((t, 3 * GROUP)), _sds((t, GROUP)), _sds((t, LANES)), _sds((1, HEAD))],
        scratch_shapes=[pltpu.VMEM((N_HEADS, HEAD, HEAD), F32)],
        compiler_params=_params("arbitrary"))(xqkv, z, gts, ng, states, dy)
    return dx, dz, dg, dng.reshape(HEAD)


def _swiglu(up, gate):
    return _silu(gate) * up


def _swiglu_fwd(u, *, name):
    t, two_f = u.shape
    dff = two_f // 2
    tb, tc = _tile(t, 512, SUBLANES), _tile(dff, 512, LANES)
    nh = dff // tc

    def body(up_ref, gate_ref, o_ref):
        o_ref[...] = _swiglu(up_ref[...], gate_ref[...]).astype(o_ref.dtype)

    in_specs = [pl.BlockSpec((tb, tc), lambda i, j: (i, j)), pl.BlockSpec((tb, tc), lambda i, j: (i, j + nh))]
    return pl.pallas_call(body, name=name, grid=(t // tb, nh), in_specs=in_specs,
                          out_specs=pl.BlockSpec((tb, tc), lambda i, j: (i, j)), out_shape=_sds((t, dff), MM_DTYPE),
                          compiler_params=_params("parallel", "parallel"))(u, u)


def _swiglu_bwd(u, dact, *, name):
    t, two_f = u.shape
    dff = two_f // 2
    tb, tc = _tile(t, 512, SUBLANES), _tile(dff, 512, LANES)
    nh = dff // tc

    def body(up_ref, gate_ref, da_ref, o_ref):
        _, vjp = jax.vjp(_swiglu, up_ref[...], gate_ref[...])
        dup, dgate = vjp(da_ref[...])
        o_ref[...] = jnp.where(pl.program_id(1) < nh, dup, dgate)

    in_specs = [pl.BlockSpec((tb, tc), lambda i, j: (i, j % nh)), pl.BlockSpec((tb, tc), lambda i, j: (i, j % nh + nh)),
                pl.BlockSpec((tb, tc), lambda i, j: (i, j % nh))]
    return pl.pallas_call(body, name=name, grid=(t // tb, 2 * nh), in_specs=in_specs,
                          out_specs=pl.BlockSpec((tb, tc), lambda i, j: (i, j)), out_shape=_sds((t, two_f)),
                          compiler_params=_params("parallel", "parallel"))(u, u, dact)


def _sum_slots(parts, *, name):
    if not isinstance(parts, (list, tuple)):
        parts = [parts[s] for s in range(parts.shape[0])]
    r = parts[0].shape[0]
    rb = _tile(r, 2048, SUBLANES)
    spec = pl.BlockSpec((rb, LANES), lambda i: (i, 0))

    def body(*refs):
        acc = refs[0][...]
        for ref in refs[1:-1]:
            acc = acc + ref[...]
        refs[-1][...] = acc

    return pl.pallas_call(body, name=name, grid=(r // rb,), in_specs=[spec] * len(parts), out_specs=spec,
                          out_shape=_sds((r, LANES)), compiler_params=_params("parallel"))(*parts)


def _adamw(w, g, m, v, *, name):
    r, c = w.shape
    rb = _tile(r, 128, SUBLANES)
    spec = pl.BlockSpec((rb, c), lambda i: (i, 0))

    def body(w_ref, g_ref, m_ref, v_ref, d_ref, nm_ref, nv_ref):
        gv = g_ref[...]
        mn = ADAM_B1 * m_ref[...] + (1.0 - ADAM_B1) * gv
        vn = ADAM_B2 * v_ref[...] + (1.0 - ADAM_B2) * (gv * gv)
        m_hat = mn / (1.0 - ADAM_B1 ** ADAM_STEP)
        v_hat = vn / (1.0 - ADAM_B2 ** ADAM_STEP)
        d_ref[...] = -ADAM_LR * (m_hat / (jnp.sqrt(v_hat) + ADAM_EPS) + ADAM_WD * w_ref[...])
        nm_ref[...] = mn
        nv_ref[...] = vn

    return pl.pallas_call(body, name=name, grid=(r // rb,), in_specs=[spec] * 4, out_specs=[spec] * 3,
                          out_shape=[_sds((r, c))] * 3, compiler_params=_params("parallel"))(w, g, m, v)


def _position():
    return lax.axis_index("x"), lax.axis_index("y"), lax.axis_index("c")


def _chip_gather(shards, *, name):
    n = len(shards)

    def body(*refs):
        ins, outs = refs[:n], refs[n:2 * n]
        send_sems, recv_sems, local_sems = refs[2 * n:]
        x, y, c = _position()
        mine = 2 * x + y
        chips = [(1 - x, y), (x, 1 - y), (1 - x, 1 - y)]
        local = [pltpu.make_async_copy(ins[a], outs[a].at[mine], local_sems.at[a]) for a in range(n)]
        for cp in local:
            cp.start()
        sends = []
        for a in range(n):
            for r, (px, py) in enumerate(chips):
                sends.append(pltpu.make_async_remote_copy(
                    src_ref=ins[a], dst_ref=outs[a].at[mine], send_sem=send_sems.at[3 * a + r],
                    recv_sem=recv_sems.at[3 * a + r], device_id=(px, py, c), device_id_type=MESH_ID))
        for cp in sends:
            cp.start()
        for a in range(n):
            for r, (px, py) in enumerate(chips):
                pltpu.make_async_remote_copy(
                    src_ref=ins[a], dst_ref=outs[a].at[2 * px + py], send_sem=send_sems.at[3 * a + r],
                    recv_sem=recv_sems.at[3 * a + r], device_id=(px, py, c), device_id_type=MESH_ID).wait_recv()
        for cp in sends:
            cp.wait_send()
        for cp in local:
            cp.wait()

    return pl.pallas_call(
        body, name=name, in_specs=[ANY] * n, out_specs=[ANY] * n,
        out_shape=[_sds((4,) + s.shape, s.dtype) for s in shards],
        scratch_shapes=[pltpu.SemaphoreType.DMA((3 * n,)), pltpu.SemaphoreType.DMA((3 * n,)),
                        pltpu.SemaphoreType.DMA((n,))],
        compiler_params=pltpu.CompilerParams(has_side_effects=True))(*shards)


def _sibling_send(v, *, name):
    def body(v_ref, got_ref, send_sem, recv_sem):
        x, y, c = _position()
        cp = pltpu.make_async_remote_copy(src_ref=v_ref, dst_ref=got_ref, send_sem=send_sem, recv_sem=recv_sem,
                                          device_id=(x, y, 1 - c), device_id_type=MESH_ID)
        cp.start()
        cp.wait()

    return pl.pallas_call(
        body, name=name, in_specs=[ANY], out_specs=ANY, out_shape=_sds(v.shape, v.dtype),
        scratch_shapes=[pltpu.SemaphoreType.DMA, pltpu.SemaphoreType.DMA],
        compiler_params=pltpu.CompilerParams(has_side_effects=True))(v)


def _chip_exchange(v, *, name):
    def body(v_ref, got_ref, send_sems, recv_sems, local_sem):
        x, y, c = _position()
        mine = 2 * x + y
        chips = [(1 - x, y), (x, 1 - y), (1 - x, 1 - y)]
        local = pltpu.make_async_copy(v_ref.at[mine], got_ref.at[mine], local_sem)
        local.start()
        sends = [pltpu.make_async_remote_copy(
            src_ref=v_ref.at[2 * px + py], dst_ref=got_ref.at[mine], send_sem=send_sems.at[r],
            recv_sem=recv_sems.at[r], device_id=(px, py, c), device_id_type=MESH_ID)
            for r, (px, py) in enumerate(chips)]
        for cp in sends:
            cp.start()
        for r, (px, py) in enumerate(chips):
            pltpu.make_async_remote_copy(
                src_ref=v_ref.at[mine], dst_ref=got_ref.at[2 * px + py], send_sem=send_sems.at[r],
                recv_sem=recv_sems.at[r], device_id=(px, py, c), device_id_type=MESH_ID).wait_recv()
        for cp in sends:
            cp.wait_send()
        local.wait()

    return pl.pallas_call(
        body, name=name, in_specs=[ANY], out_specs=ANY, out_shape=_sds(v.shape, v.dtype),
        scratch_shapes=[pltpu.SemaphoreType.DMA((3,)), pltpu.SemaphoreType.DMA((3,)), pltpu.SemaphoreType.DMA],
        compiler_params=pltpu.CompilerParams(has_side_effects=True))(v)


def _sibling_pair(v, *, name):
    def body(v_ref, out_ref, send_sem, recv_sem, local_sem):
        x, y, c = _position()
        local = pltpu.make_async_copy(v_ref, out_ref.at[c], local_sem)
        local.start()
        cp = pltpu.make_async_remote_copy(src_ref=v_ref, dst_ref=out_ref.at[c], send_sem=send_sem, recv_sem=recv_sem,
                                          device_id=(x, y, 1 - c), device_id_type=MESH_ID)
        cp.start()
        pltpu.make_async_remote_copy(src_ref=v_ref, dst_ref=out_ref.at[1 - c], send_sem=send_sem, recv_sem=recv_sem,
                                     device_id=(x, y, 1 - c), device_id_type=MESH_ID).wait_recv()
        cp.wait_send()
        local.wait()

    return pl.pallas_call(
        body, name=name, in_specs=[ANY], out_specs=ANY, out_shape=_sds((2,) + v.shape, v.dtype),
        scratch_shapes=[pltpu.SemaphoreType.DMA, pltpu.SemaphoreType.DMA, pltpu.SemaphoreType.DMA],
        compiler_params=pltpu.CompilerParams(has_side_effects=True))(v)


def _all_sum(v, *, name):
    r = v.shape[0]
    masks = [(mx, my, mc) for mx in (0, 1) for my in (0, 1) for mc in (0, 1)][1:]

    def body(v_ref, out_ref, slots, send_sems, recv_sems, local_sem):
        x, y, c = _position()
        me = 4 * x + 2 * y + c

        def peer(mask):
            return tuple(1 - p if bit else p for p, bit in zip((x, y, c), mask))

        local = pltpu.make_async_copy(v_ref, slots.at[me], local_sem)
        local.start()
        sends = [pltpu.make_async_remote_copy(
            src_ref=v_ref, dst_ref=slots.at[me], send_sem=send_sems.at[k], recv_sem=recv_sems.at[k],
            device_id=peer(mask), device_id_type=MESH_ID) for k, mask in enumerate(masks)]
        for cp in sends:
            cp.start()
        for k, mask in enumerate(masks):
            px, py, pc = peer(mask)
            pltpu.make_async_remote_copy(
                src_ref=v_ref, dst_ref=slots.at[4 * px + 2 * py + pc], send_sem=send_sems.at[k],
                recv_sem=recv_sems.at[k], device_id=(px, py, pc), device_id_type=MESH_ID).wait_recv()
        for cp in sends:
            cp.wait_send()
        local.wait()
        acc = slots[0]
        for s in range(1, 8):
            acc = acc + slots[s]
        out_ref[...] = acc

    vm = pl.BlockSpec(memory_space=pltpu.VMEM)
    return pl.pallas_call(
        body, name=name, in_specs=[vm], out_specs=vm, out_shape=_sds((r, LANES)),
        scratch_shapes=[pltpu.VMEM((8, r, LANES), F32), pltpu.SemaphoreType.DMA((7,)), pltpu.SemaphoreType.DMA((7,)),
                        pltpu.SemaphoreType.DMA],
        compiler_params=pltpu.CompilerParams(vmem_limit_bytes=VMEM_LIMIT_BYTES, has_side_effects=True))(v)


def _pack_rows(arrays, align=SUBLANES * LANES):
    flat = jnp.concatenate([a.reshape(-1) for a in arrays])
    n = flat.shape[0]
    pad = (-n) % align
    if pad:
        flat = jnp.concatenate([flat, jnp.zeros((pad,), flat.dtype)])
    return flat.reshape(-1, LANES), [a.shape for a in arrays]


def _unpack_rows(rows, shapes):
    flat = rows.reshape(-1)
    out, off = [], 0
    for s in shapes:
        n = int(np.prod(s))
        out.append(flat[off:off + n].reshape(s))
        off += n
    return out


def _pad_w_in(w):
    d = w.shape[0]
    return jnp.concatenate([w[:, 0:2560], w[:, 2564:4612], w[:, 4620:6156], w[:, 2560:2564], w[:, 4612:4620],
                            jnp.zeros((d, Z_COLS - IN_COLS), w.dtype)], axis=1)


def _unpad_w_in(g):
    return jnp.concatenate([g[:, 0:2560], g[:, 6144:6148], g[:, 2560:4608], g[:, 6148:6156], g[:, 4608:6144]], axis=1)


def _gate_rows(p):
    return (_lane_row(p["fox_f_bias"], G_F), _lane_row(p["gdn_a_log"], G_ALPHA), _lane_row(p["gdn_dt_bias"], G_ALPHA))


def _head_cols(c_rows, t):
    ct = c_rows[:, :N_HEADS].T
    return ct.reshape(N_HEADS, t, 1), ct.reshape(N_HEADS, 1, t)


def _layer_fwd(x, p):
    t = x.shape[0]
    s = {"x": x}
    s["h"] = _rmsnorm_fwd(x, p["norm_mix"], name="mix_norm")
    z = s["z"] = _matmul(s["h"], p["w_in"], name="in_proj")
    s["xc"] = _conv_fwd(z, p["lru_conv_w"], p["lru_conv_b"], ncols=GROUP, coff=Z_AX, name="lru_conv")
    lru = s["lru"] = (_block_diag(p["lru_wa"]).astype(MM_DTYPE), p["lru_ba"].reshape(1, GROUP),
                      _block_diag(p["lru_wx"]).astype(MM_DTYPE), p["lru_bx"].reshape(1, GROUP),
                      p["lru_lambda"].reshape(1, GROUP))
    s["a"], u = _lru_gates_fwd(s["xc"], *lru, name="lru_gates")
    s["ha"] = _scan(s["a"], u, name="lru_scan")
    s["gate_a"] = z[:, Z_AG:Z_AG + GROUP].reshape(t * LRU_BLOCKS, LRU_BD)
    y_a = _post_fwd(s["ha"].reshape(t * LRU_BLOCKS, LRU_BD), p["norm_a"].reshape(LRU_BLOCKS, LRU_BD), s["gate_a"],
                    name="lru_post")
    s["gts"] = _gates_fwd(z, *_gate_rows(p), name="gates")
    s["cq"], s["ck"] = _head_cols(_scan(None, s["gts"], name="fox_cumsum"), t)
    s["ob"], s["lse_b"] = _att_fwd(z, s["cq"], s["ck"], mode="fox", qoff=Z_BQ, koff=Z_BK, voff=Z_BV, name="fox_att")
    gain_b = jnp.tile(p["norm_b"].reshape(N_HEADS, HEAD), (2, 1))
    y_b = _post_fwd(s["ob"].reshape(t * N_HEADS, HEAD), gain_b, None, name="fox_post")
    s["cconv"] = _conv_fwd(z, p["gdn_conv_w"], jnp.zeros((3 * GROUP,), F32), ncols=3 * GROUP, coff=Z_CQKV,
                           name="gdn_conv")
    y_c, s["gdn_states"] = _gdn_fwd(s["cconv"], z, s["gts"], p["gdn_norm"].reshape(1, HEAD), name="gdn_chunks")
    s["od"], s["lse_d"] = _att_fwd(z, s["cq"], s["ck"], mode="dil", qoff=Z_DQ, koff=Z_DK, voff=Z_DV, name="dil_att")
    gain_d = jnp.tile(p["norm_d"].reshape(N_HEADS, HEAD), (2, 1))
    y_d = _post_fwd(s["od"].reshape(t * N_HEADS, HEAD), gain_d, None, name="dil_post")
    y = s["y"] = jnp.concatenate([y_a.reshape(t, GROUP), y_b.reshape(t, GROUP), y_c, y_d.reshape(t, GROUP)], axis=1)
    x1 = s["x1"] = _matmul(y, p["w_out"], add=x, name="out_proj")
    s["h2"] = _rmsnorm_fwd(x1, p["norm_ffn"], name="ffn_norm")
    s["uu"] = _matmul(s["h2"], p["ffn_w_up"], name="ffn_up")
    s["u"] = _conv_fwd(s["uu"], p["ffn_conv_w"], p["ffn_conv_b"], ncols=s["uu"].shape[1], name="ffn_conv")
    s["act"] = _swiglu_fwd(s["u"], name="ffn_swiglu")
    return _matmul(s["act"], p["ffn_w_down"], add=x1, name="ffn_down"), s


def _layer_bwd(dx2, p, s):
    t = dx2.shape[0]
    g = {}
    dact = _matmul(dx2, p["ffn_w_down"], form="nt", name="ffn_down_dx")
    g["ffn_w_down"] = _matmul(s["act"], dx2, form="tn", name="ffn_down_dw")
    du = _swiglu_bwd(s["u"], dact, name="ffn_swiglu_bwd")
    duu, g["ffn_conv_w"], g["ffn_conv_b"] = _conv_bwd(s["uu"], p["ffn_conv_w"], du, ncols=du.shape[1],
                                                      name="ffn_conv_bwd")
    dh2 = _matmul(duu, p["ffn_w_up"], form="nt", name="ffn_up_dx")
    g["ffn_w_up"] = _matmul(s["h2"], duu, form="tn", name="ffn_up_dw")
    dx1, g["norm_ffn"] = _rmsnorm_bwd(s["x1"], p["norm_ffn"], dh2, dx2, name="ffn_norm_bwd")
    dy = _matmul(dx1, p["w_out"], form="nt", name="out_proj_dx")
    g["w_out"] = _matmul(s["y"], dx1, form="tn", name="out_proj_dw")
    z = s["z"]
    dha, dgn, dgate_a = _post_bwd(s["ha"].reshape(t * LRU_BLOCKS, LRU_BD), p["norm_a"].reshape(LRU_BLOCKS, LRU_BD),
                                  s["gate_a"], dy[:, 0:GROUP].reshape(t * LRU_BLOCKS, LRU_BD), name="lru_post_bwd")
    g["norm_a"] = dgn.reshape(GROUP)
    gsc = _scan_reverse(s["a"], dha.reshape(t, GROUP), name="lru_scan_bwd")
    dxc, dwa, dba, dwx, dbx, dlam = _lru_gates_bwd(s["xc"], *s["lru"], gsc, _shift_down(s["ha"]), name="lru_gates_bwd")
    g["lru_wa"], g["lru_wx"] = _diag_blocks(dwa), _diag_blocks(dwx)
    g["lru_ba"], g["lru_bx"], g["lru_lambda"] = dba.reshape(GROUP), dbx.reshape(GROUP), dlam.reshape(GROUP)
    dax, g["lru_conv_w"], g["lru_conv_b"] = _conv_bwd(z, p["lru_conv_w"], dxc, ncols=GROUP, coff=Z_AX,
                                                      name="lru_conv_bwd")
    gain_b = jnp.tile(p["norm_b"].reshape(N_HEADS, HEAD), (2, 1))
    dob, dgn, _ = _post_bwd(s["ob"].reshape(t * N_HEADS, HEAD), gain_b, None,
                            dy[:, GROUP:2 * GROUP].reshape(t * N_HEADS, HEAD), name="fox_post_bwd")
    g["norm_b"] = (dgn[:N_HEADS] + dgn[N_HEADS:]).reshape(GROUP)
    dob = dob.reshape(t, GROUP)
    fox = dict(mode="fox", qoff=Z_BQ, koff=Z_BK, voff=Z_BV)
    dbq, dcq = _att_bwd_q(z, s["cq"], s["ck"], dob, s["ob"], s["lse_b"], name="fox_att_dq", **fox)
    dbk, dbv, dck = _att_bwd_kv(z, s["cq"], s["ck"], dob, s["ob"], s["lse_b"], name="fox_att_dkv", **fox)
    pad_lanes = ((0, 0), (0, LANES - N_HEADS))
    dc_rows = _sum_slots([jnp.pad(dcq.reshape(N_HEADS, t).T, pad_lanes), jnp.pad(dck.reshape(N_HEADS, t).T, pad_lanes)],
                         name="fox_dc_sum")
    dgts_fox = _scan_reverse(None, dc_rows, name="fox_cumsum_bwd")
    gain_d = jnp.tile(p["norm_d"].reshape(N_HEADS, HEAD), (2, 1))
    dod, dgn, _ = _post_bwd(s["od"].reshape(t * N_HEADS, HEAD), gain_d, None,
                            dy[:, 3 * GROUP:4 * GROUP].reshape(t * N_HEADS, HEAD), name="dil_post_bwd")
    g["norm_d"] = (dgn[:N_HEADS] + dgn[N_HEADS:]).reshape(GROUP)
    dod = dod.reshape(t, GROUP)
    dil = dict(mode="dil", qoff=Z_DQ, koff=Z_DK, voff=Z_DV)
    ddq, _ = _att_bwd_q(z, s["cq"], s["ck"], dod, s["od"], s["lse_d"], name="dil_att_dq", **dil)
    ddk, ddv, _ = _att_bwd_kv(z, s["cq"], s["ck"], dod, s["od"], s["lse_d"], name="dil_att_dkv", **dil)
    dcconv, dcz, dgts_gdn, g["gdn_norm"] = _gdn_bwd(s["cconv"], z, s["gts"], p["gdn_norm"].reshape(1, HEAD),
                                                    s["gdn_states"], dy[:, 2 * GROUP:3 * GROUP], name="gdn_chunks_bwd")
    dcqkv, g["gdn_conv_w"], _ = _conv_bwd(z, p["gdn_conv_w"], dcconv, ncols=3 * GROUP, coff=Z_CQKV,
                                          name="gdn_conv_bwd")
    dgts = _sum_slots([dgts_fox, dgts_gdn], name="gates_dsum")
    dzg, dfb, dal, ddt = _gates_bwd(z, *_gate_rows(p), dgts, name="gates_bwd")
    g["fox_f_bias"] = dfb[0, G_F:G_F + N_HEADS]
    g["gdn_a_log"] = dal[0, G_ALPHA:G_ALPHA + N_HEADS]
    g["gdn_dt_bias"] = ddt[0, G_ALPHA:G_ALPHA + N_HEADS]
    dz = jnp.concatenate([dax, dgate_a.reshape(t, GROUP), dbq, dbk, dbv, dcqkv, dcz, ddq, ddk, ddv, dzg], axis=1)
    dh = _matmul(dz, p["w_in"], form="nt", name="in_proj_dx")
    g["w_in"] = _matmul(s["h"], dz, form="tn", name="in_proj_dw")
    dx, g["norm_mix"] = _rmsnorm_bwd(s["x"], p["norm_mix"], dh, dx1, name="mix_norm_bwd")
    return dx, g


def _local_step(x, tgt, layers, norm_final):
    saved = []
    for p in layers:
        x, s = _layer_fwd(x, p)
        saved.append(s)
    loss, dx, dnf = _loss_head(x, norm_final, tgt, name="loss_head")
    grads = []
    for p, s in zip(reversed(layers), reversed(saved)):
        dx, g = _layer_bwd(dx, p, s)
        grads.append(g)
    return loss, dx, grads[::-1], dnf


BIG = ("w_in", "w_out", "ffn_w_up", "ffn_w_down")
SHARDED_SMALL = ("lru_conv_w", "gdn_conv_w", "ffn_conv_w")
NAMES = ("norm_mix", "w_in", "lru_conv_w", "lru_conv_b", "lru_wa", "lru_ba", "lru_wx", "lru_bx", "lru_lambda",
         "fox_f_bias", "gdn_conv_w", "gdn_a_log", "gdn_dt_bias", "gdn_norm", "norm_a", "norm_b", "norm_d", "w_out",
         "norm_ffn", "ffn_w_up", "ffn_conv_w", "ffn_conv_b", "ffn_w_down", "norm_final")
SMALL = tuple(n for n in NAMES if n not in BIG)


def _big_pieces(g, k_axis_cols):
    if k_axis_cols:
        d, n = g.shape
        return g.reshape(d, 4, n // 4).transpose(1, 0, 2).reshape(4, -1, LANES)
    return g.reshape(4, -1, LANES)


def _reduce_big(gl, *, c):
    whole = {"w_in": _unpad_w_in(gl["w_in"]), "w_out": gl["w_out"], "ffn_w_up": gl["ffn_w_up"],
             "ffn_w_down": gl["ffn_w_down"]}
    cols = {"w_in": True, "w_out": False, "ffn_w_up": True, "ffn_w_down": False}
    pieces = [_big_pieces(whole[n], cols[n]) for n in BIG]
    rows = [q.shape[1] for q in pieces]
    packed = jnp.concatenate(pieces, axis=1)
    half = packed.shape[1] // 2
    halves = packed.reshape(4, 2, half, LANES)
    mine = lax.dynamic_index_in_dim(halves, c, axis=1, keepdims=False).reshape(4 * half, LANES)
    other = lax.dynamic_index_in_dim(halves, 1 - c, axis=1, keepdims=False).reshape(4 * half, LANES)
    got = _sibling_send(other, name="grad_sibling_send")
    chip_sum = _sum_slots([mine, got], name="grad_sibling_sum").reshape(4, half, LANES)
    from_chips = _chip_exchange(chip_sum, name="grad_chip_exchange")
    total_half = _sum_slots(from_chips, name="grad_chip_sum")
    total = _sibling_pair(total_half, name="grad_sibling_pair").reshape(2 * half, LANES)
    out, off = {}, 0
    for n, r in zip(BIG, rows):
        out[n] = total[off:off + r]
        off += r
    return out


def kernel(x, norm_mix, w_in, lru_conv_w, lru_conv_b, lru_wa, lru_ba, lru_wx, lru_bx, lru_lambda, fox_f_bias, gdn_conv_w, gdn_a_log, gdn_dt_bias, gdn_norm, norm_a, norm_b, norm_d, w_out, norm_ffn, ffn_w_up, ffn_conv_w, ffn_conv_b, ffn_w_down, norm_final, loss_target, m_norm_mix, m_w_in, m_lru_conv_w, m_lru_conv_b, m_lru_wa, m_lru_ba, m_lru_wx, m_lru_bx, m_lru_lambda, m_fox_f_bias, m_gdn_conv_w, m_gdn_a_log, m_gdn_dt_bias, m_gdn_norm, m_norm_a, m_norm_b, m_norm_d, m_w_out, m_norm_ffn, m_ffn_w_up, m_ffn_conv_w, m_ffn_conv_b, m_ffn_w_down, m_norm_final, v_norm_mix, v_w_in, v_lru_conv_w, v_lru_conv_b, v_lru_wa, v_lru_ba, v_lru_wx, v_lru_bx, v_lru_lambda, v_fox_f_bias, v_gdn_conv_w, v_gdn_a_log, v_gdn_dt_bias, v_gdn_norm, v_norm_a, v_norm_b, v_norm_d, v_w_out, v_norm_ffn, v_ffn_w_up, v_ffn_conv_w, v_ffn_conv_b, v_ffn_w_down, v_norm_final):
    w = dict(zip(NAMES, (norm_mix, w_in, lru_conv_w, lru_conv_b, lru_wa, lru_ba, lru_wx, lru_bx, lru_lambda, fox_f_bias,
                         gdn_conv_w, gdn_a_log, gdn_dt_bias, gdn_norm, norm_a, norm_b, norm_d, w_out, norm_ffn, ffn_w_up,
                         ffn_conv_w, ffn_conv_b, ffn_w_down, norm_final)))
    m = dict(zip(NAMES, (m_norm_mix, m_w_in, m_lru_conv_w, m_lru_conv_b, m_lru_wa, m_lru_ba, m_lru_wx, m_lru_bx,
                         m_lru_lambda, m_fox_f_bias, m_gdn_conv_w, m_gdn_a_log, m_gdn_dt_bias, m_gdn_norm, m_norm_a,
                         m_norm_b, m_norm_d, m_w_out, m_norm_ffn, m_ffn_w_up, m_ffn_conv_w, m_ffn_conv_b, m_ffn_w_down,
                         m_norm_final)))
    v = dict(zip(NAMES, (v_norm_mix, v_w_in, v_lru_conv_w, v_lru_conv_b, v_lru_wa, v_lru_ba, v_lru_wx, v_lru_bx,
                         v_lru_lambda, v_fox_f_bias, v_gdn_conv_w, v_gdn_a_log, v_gdn_dt_bias, v_gdn_norm, v_norm_a,
                         v_norm_b, v_norm_d, v_w_out, v_norm_ffn, v_ffn_w_up, v_ffn_conv_w, v_ffn_conv_b, v_ffn_w_down,
                         v_norm_final)))
    depth = w_in.shape[0]
    xi, yi, ci = _position()
    chip = 2 * xi + yi

    conv_rows, conv_shapes = _pack_rows([w[n] for n in SHARDED_SMALL])
    (conv_all,) = _chip_gather([conv_rows], name="conv_taps_gather")
    conv_full = {}
    per_chip = [_unpack_rows(conv_all[k], conv_shapes) for k in range(4)]
    for i, n in enumerate(SHARDED_SMALL):
        conv_full[n] = jnp.concatenate([per_chip[k][i] for k in range(4)], axis=-1)
    layers = []
    for l in range(depth):
        g_in, g_out, g_up, g_dn = _chip_gather([w[n][l].astype(MM_DTYPE) for n in BIG], name="weights_gather")
        p = {n: w[n][l] for n in SMALL if n != "norm_final" and n not in SHARDED_SMALL}
        for n in SHARDED_SMALL:
            p[n] = conv_full[n][l]
        p["w_in"] = _pad_w_in(jnp.concatenate([g_in[k] for k in range(4)], axis=1))
        p["w_out"] = g_out.reshape(-1, g_out.shape[-1])
        p["ffn_w_up"] = jnp.concatenate([g_up[k] for k in range(4)], axis=1)
        p["ffn_w_down"] = g_dn.reshape(-1, g_dn.shape[-1])
        layers.append(p)

    loss, grad_x, grads, g_norm_final = _local_step(x[0], loss_target[0], layers, norm_final)
    loss = lax.psum(loss, ("x", "y", "c"))

    big = [_reduce_big(grads[l], c=ci) for l in range(depth)]
    small_names = [n for n in SMALL if n != "norm_final"]
    small_rows, small_shapes = _pack_rows([jnp.stack([grads[l][n] for l in range(depth)]) for n in small_names]
                                          + [g_norm_final])
    small_sum = _unpack_rows(_all_sum(small_rows, name="small_grads_sum"), small_shapes)
    gsum = dict(zip(small_names + ["norm_final"], small_sum))
    for n in SHARDED_SMALL:
        width = w[n].shape[-1]
        gsum[n] = lax.dynamic_slice_in_dim(gsum[n], chip * width, width, axis=-1)
    for n in BIG:
        gsum[n] = jnp.stack([big[l][n].reshape(w[n].shape[1:]) for l in range(depth)])

    delta, new_m, new_v = {}, {}, {}
    for n in BIG:
        cols = w[n].shape[-1]
        d_, m_, v_ = _adamw(w[n].reshape(-1, cols), gsum[n].reshape(-1, cols), m[n].reshape(-1, cols),
                            v[n].reshape(-1, cols), name="adamw_" + n)
        delta[n], new_m[n], new_v[n] = (a.reshape(w[n].shape) for a in (d_, m_, v_))
    packs = [_pack_rows([src[n] for n in SMALL]) for src in (w, gsum, m, v)]
    outs = _adamw(*[pk[0] for pk in packs], name="adamw_small")
    for dst, rows_ in zip((delta, new_m, new_v), outs):
        dst.update(zip(SMALL, _unpack_rows(rows_, packs[0][1])))
    return (loss, grad_x[None], *[gsum[n] for n in NAMES], *[delta[n] for n in NAMES], *[new_m[n] for n in NAMES],
            *[new_v[n] for n in NAMES])
```

```python
import functools

import numpy as np
import jax
import jax.numpy as jnp
from jax import lax
from jax.experimental import pallas as pl
from jax.experimental.pallas import tpu as pltpu

F32 = jnp.float32
MM_DTYPE = jnp.bfloat16
VMEM_LIMIT_BYTES = 56 * 1024 * 1024
LANES = 128
SUBLANES = 8

GROUP = 512
HEAD = 128
N_HEADS = GROUP // HEAD
LRU_BLOCKS = 8
LRU_BD = GROUP // LRU_BLOCKS
LRU_C = 8.0
GDN_CHUNK = 64
GDN_BLOCK = 2 * GDN_CHUNK
DILATED_PAIRS = ((128, 1), (512, 4), (2048, 16))
EPS = 1e-6
NEG = -1e30
ATT_SCALE = HEAD ** -0.5
ATT_TILE = 512

ADAM_LR, ADAM_B1, ADAM_B2, ADAM_EPS, ADAM_WD, ADAM_STEP = 0.001, 0.9, 0.999, 1e-08, 0.01, 10

Z_AX, Z_AG, Z_BQ, Z_BK, Z_BV = 0, 512, 1024, 1536, 2048
Z_CQKV, Z_CZ, Z_DQ, Z_DK, Z_DV, Z_GATES, Z_COLS = 2560, 4096, 4608, 5120, 5632, 6144, 6272
IN_COLS = 6156
G_F, G_BETA, G_ALPHA = 0, 4, 8

MESH_ID = pl.DeviceIdType.MESH
ANY = pl.BlockSpec(memory_space=pl.ANY)


def _tile(n, target, align):
    t = min(n, target) // align * align
    while t >= align:
        if n % t == 0:
            return t
        t -= align
    return n


def _params(*sem):
    return pltpu.CompilerParams(dimension_semantics=sem, vmem_limit_bytes=VMEM_LIMIT_BYTES)


def _sds(shape, dtype=F32):
    return jax.ShapeDtypeStruct(tuple(shape), dtype)


def _dg(a, b, ca, cb, hi):
    dims = (((ca,), (cb,)), ((), ()))
    dot = lambda p, q: lax.dot_general(p, q, dims, preferred_element_type=F32)
    if hi:
        a_hi, b_hi = a.astype(MM_DTYPE), b.astype(MM_DTYPE)
        a_lo = (a - a_hi.astype(F32)).astype(MM_DTYPE)
        b_lo = (b - b_hi.astype(F32)).astype(MM_DTYPE)
        return dot(a_hi, b_hi) + (dot(a_hi, b_lo) + dot(a_lo, b_hi))
    return dot(a.astype(MM_DTYPE), b.astype(MM_DTYPE))


_FORMS = {"nn": (1, 0), "nt": (1, 1), "tn": (0, 0)}


@functools.partial(jax.custom_vjp, nondiff_argnums=(2, 3))
def _mmf(a, b, form, hi):
    ca, cb = _FORMS[form]
    return _dg(a, b, ca, cb, hi)


def _mmf_fwd(a, b, form, hi):
    return _mmf(a, b, form, hi), (a, b)


def _mmf_bwd(form, hi, res, g):
    a, b = res
    if form == "nn":
        return _dg(g, b, 1, 1, hi), _dg(a, g, 0, 0, hi)
    if form == "nt":
        return _dg(g, b, 1, 0, hi), _dg(g, a, 0, 0, hi)
    return _dg(b, g, 1, 1, hi), _dg(a, g, 1, 0, hi)


_mmf.defvjp(_mmf_fwd, _mmf_bwd)


def _mm(a, b):
    return _mmf(a, b, "nn", False)


def _mm_nt(a, b):
    return _mmf(a, b, "nt", False)


def _mm_tn(a, b):
    return _mmf(a, b, "tn", False)


def _mmh(a, b):
    return _mmf(a, b, "nn", True)


MATMUL_VMEM_BUDGET = 40 * 1024 * 1024


def _matmul_tiles(m, n, k, a_bytes, b_bytes, o_bytes, has_add):
    divisors = lambda d: [t for t in range(d, 0, -LANES) if d % t == 0 and t % LANES == 0] or [d]
    for tk in divisors(k):
        for tm_t, tn_t in ((512, 1024), (1024, 512), (512, 512), (512, 256), (256, 512), (256, 256), (256, 128),
                           (128, 128)):
            tm, tn = _tile(m, tm_t, LANES), _tile(n, tn_t, LANES)
            need = 2 * (tm * tk * a_bytes + tk * tn * b_bytes) + 2 * tm * tn * o_bytes
            need += (2 * tm * tn * 4 if has_add else 0) + (tm * tn * 4 if tk < k else 0)
            if need <= MATMUL_VMEM_BUDGET:
                return tm, tn, tk
    return _tile(m, 128, LANES), _tile(n, 128, LANES), _tile(k, 128, LANES)


def _matmul(a, b, *, form="nn", add=None, out_dtype=F32, tiles=None, name):
    ca, cb = _FORMS[form]
    m, k = (a.shape[1], a.shape[0]) if form == "tn" else a.shape
    n = b.shape[0] if form == "nt" else b.shape[1]
    tm, tn, tk = tiles or _matmul_tiles(m, n, k, a.dtype.itemsize, b.dtype.itemsize, jnp.dtype(out_dtype).itemsize,
                                        add is not None)
    nk = k // tk
    a_spec = (pl.BlockSpec((tk, tm), lambda i, j, kk: (kk, i)) if form == "tn"
              else pl.BlockSpec((tm, tk), lambda i, j, kk: (i, kk)))
    b_spec = (pl.BlockSpec((tn, tk), lambda i, j, kk: (j, kk)) if form == "nt"
              else pl.BlockSpec((tk, tn), lambda i, j, kk: (kk, j)))
    o_spec = pl.BlockSpec((tm, tn), lambda i, j, kk: (i, j))

    def body(*refs):
        a_ref, b_ref = refs[:2]
        add_ref = None if add is None else refs[2]
        o_ref = refs[2 + (add is not None)]
        part = _dg(a_ref[...], b_ref[...], ca, cb, False)
        if nk == 1:
            o_ref[...] = (part if add is None else part + add_ref[...]).astype(out_dtype)
            return
        acc_ref = refs[-1]
        kk = pl.program_id(2)

        @pl.when(kk == 0)
        def _():
            acc_ref[...] = jnp.zeros_like(acc_ref)

        acc_ref[...] += part

        @pl.when(kk == nk - 1)
        def _():
            r = acc_ref[...]
            if add is not None:
                r = r + add_ref[...]
            o_ref[...] = r.astype(out_dtype)

    args = (a, b) if add is None else (a, b, add)
    specs = [a_spec, b_spec] if add is None else [a_spec, b_spec, o_spec]
    return pl.pallas_call(
        body, name=name, grid=(m // tm, n // tn, nk), in_specs=specs, out_specs=o_spec,
        out_shape=_sds((m, n), out_dtype), scratch_shapes=[] if nk == 1 else [pltpu.VMEM((tm, tn), F32)],
        compiler_params=_params("parallel", "parallel", "arbitrary"))(*args)


def _rms(x, g):
    return x * lax.rsqrt(jnp.mean(x * x, axis=-1, keepdims=True) + EPS) * g


def _rmsnorm_fwd(x, g, *, name):
    t, d = x.shape
    tb = _tile(t, 512, SUBLANES)

    def body(x_ref, g_ref, o_ref):
        o_ref[...] = _rms(x_ref[...], g_ref[...]).astype(o_ref.dtype)

    row = pl.BlockSpec((tb, d), lambda i: (i, 0))
    vec = pl.BlockSpec((1, d), lambda i: (0, 0))
    return pl.pallas_call(body, name=name, grid=(t // tb,), in_specs=[row, vec], out_specs=row,
                          out_shape=_sds((t, d), MM_DTYPE), compiler_params=_params("parallel"))(x, g.reshape(1, d))


def _rmsnorm_bwd(x, g, dh, res, *, name):
    t, d = x.shape
    tb = _tile(t, 256, SUBLANES)

    def body(x_ref, g_ref, dh_ref, res_ref, dx_ref, dg_ref):
        _, vjp = jax.vjp(_rms, x_ref[...], g_ref[...])
        dx, dg = vjp(dh_ref[...])
        dx_ref[...] = dx + res_ref[...]

        @pl.when(pl.program_id(0) == 0)
        def _():
            dg_ref[...] = jnp.zeros_like(dg_ref)

        dg_ref[...] += dg

    row = pl.BlockSpec((tb, d), lambda i: (i, 0))
    vec = pl.BlockSpec((1, d), lambda i: (0, 0))
    dx, dg = pl.pallas_call(body, name=name, grid=(t // tb,), in_specs=[row, vec, row, row], out_specs=[row, vec],
                            out_shape=[_sds((t, d)), _sds((1, d))], compiler_params=_params("arbitrary"))(
                                x, g.reshape(1, d), dh, res)
    return dx, dg.reshape(d)


def _loss_head(x, g, tgt, *, name):
    t, d = x.shape
    tb = _tile(t, 256, SUBLANES)

    def body(x_ref, g_ref, t_ref, loss_ref, dx_ref, dg_ref):
        def f(xv, gv):
            e = _rms(xv, gv) - t_ref[...]
            return jnp.sum(jnp.sum(e * e, axis=-1, keepdims=True), axis=0, keepdims=True) * (0.5 / d)

        l, vjp = jax.vjp(f, x_ref[...], g_ref[...])
        dx, dg = vjp(jnp.ones((1, 1), F32))
        dx_ref[...] = dx

        @pl.when(pl.program_id(0) == 0)
        def _():
            dg_ref[...] = jnp.zeros_like(dg_ref)
            loss_ref[...] = jnp.zeros_like(loss_ref)

        dg_ref[...] += dg
        loss_ref[...] += jnp.zeros(loss_ref.shape, F32) + l

    row = pl.BlockSpec((tb, d), lambda i: (i, 0))
    vec = pl.BlockSpec((1, d), lambda i: (0, 0))
    lspec = pl.BlockSpec((SUBLANES, LANES), lambda i: (0, 0))
    loss, dx, dg = pl.pallas_call(
        body, name=name, grid=(t // tb,), in_specs=[row, vec, row], out_specs=[lspec, row, vec],
        out_shape=[_sds((SUBLANES, LANES)), _sds((t, d)), _sds((1, d))], compiler_params=_params("arbitrary"))(
            x, g.reshape(1, d), tgt)
    return loss[0, 0], dx, dg.reshape(d)


def _delayed(x, prev, j):
    if j == 0:
        return x
    sh = pltpu.roll(x, j, axis=0)
    row = lax.broadcasted_iota(jnp.int32, prev.shape, 0)
    top = jnp.where(row < j, pltpu.roll(prev, j, axis=0), sh[0:SUBLANES])
    return jnp.concatenate([top, sh[SUBLANES:]], axis=0)


def _advanced(x, nxt, j):
    if j == 0:
        return x
    tb = x.shape[0]
    sh = pltpu.roll(x, tb - j, axis=0)
    row = lax.broadcasted_iota(jnp.int32, nxt.shape, 0)
    bot = jnp.where(row + j < SUBLANES, sh[tb - SUBLANES:], pltpu.roll(nxt, SUBLANES - j, axis=0))
    return jnp.concatenate([sh[:tb - SUBLANES], bot], axis=0)


def _conv_tiles(t, ncols, coff):
    tc = _tile(ncols, 512, LANES)
    assert coff % tc == 0
    tb = _tile(t, 512, SUBLANES)
    return tc, tb, coff // tc


def _conv_fwd(x, w, b, *, ncols, coff=0, name):
    t = x.shape[0]
    kw = w.shape[0]
    tc, tb, cb = _conv_tiles(t, ncols, coff)
    n8 = tb // SUBLANES

    def body(x_ref, p_ref, w_ref, b_ref, o_ref):
        xv = x_ref[...]
        prev = jnp.where(pl.program_id(1) > 0, p_ref[...], 0.0)
        acc = jnp.zeros_like(xv) + b_ref[...]
        for k in range(kw):
            acc = acc + w_ref[k:k + 1, :] * _delayed(xv, prev, kw - 1 - k)
        o_ref[...] = acc

    in_specs = [
        pl.BlockSpec((tb, tc), lambda c, i: (i, c + cb)),
        pl.BlockSpec((SUBLANES, tc), lambda c, i: (jnp.maximum(i * n8 - 1, 0), c + cb)),
        pl.BlockSpec((kw, tc), lambda c, i: (0, c)),
        pl.BlockSpec((1, tc), lambda c, i: (0, c)),
    ]
    return pl.pallas_call(
        body, name=name, grid=(ncols // tc, t // tb), in_specs=in_specs,
        out_specs=pl.BlockSpec((tb, tc), lambda c, i: (i, c)), out_shape=_sds((t, ncols)),
        compiler_params=_params("parallel", "parallel"))(x, x, w, b.reshape(1, ncols))


def _conv_bwd(x, w, dy, *, ncols, coff=0, name):
    t = x.shape[0]
    kw = w.shape[0]
    tc, tb, cb = _conv_tiles(t, ncols, coff)
    n8 = tb // SUBLANES
    nt = t // tb

    def body(x_ref, p_ref, dy_ref, n_ref, w_ref, dx_ref, dw_ref, db_ref):
        i = pl.program_id(1)
        xv, dyv = x_ref[...], dy_ref[...]
        prev = jnp.where(i > 0, p_ref[...], 0.0)
        nxt = jnp.where(i < nt - 1, n_ref[...], 0.0)

        @pl.when(i == 0)
        def _():
            dw_ref[...] = jnp.zeros_like(dw_ref)
            db_ref[...] = jnp.zeros_like(db_ref)

        dx = jnp.zeros_like(dyv)
        for k in range(kw):
            j = kw - 1 - k
            dx = dx + w_ref[k:k + 1, :] * _advanced(dyv, nxt, j)
            dw_ref[k:k + 1, :] += jnp.sum(dyv * _delayed(xv, prev, j), axis=0, keepdims=True)
        dx_ref[...] = dx.astype(dx_ref.dtype)
        db_ref[...] += jnp.sum(dyv, axis=0, keepdims=True)

    in_specs = [
        pl.BlockSpec((tb, tc), lambda c, i: (i, c + cb)),
        pl.BlockSpec((SUBLANES, tc), lambda c, i: (jnp.maximum(i * n8 - 1, 0), c + cb)),
        pl.BlockSpec((tb, tc), lambda c, i: (i, c)),
        pl.BlockSpec((SUBLANES, tc), lambda c, i: (jnp.minimum((i + 1) * n8, nt * n8 - 1), c)),
        pl.BlockSpec((kw, tc), lambda c, i: (0, c)),
    ]
    out_specs = [
        pl.BlockSpec((tb, tc), lambda c, i: (i, c)),
        pl.BlockSpec((kw, tc), lambda c, i: (0, c)),
        pl.BlockSpec((1, tc), lambda c, i: (0, c)),
    ]
    dx, dw, db = pl.pallas_call(
        body, name=name, grid=(ncols // tc, nt), in_specs=in_specs, out_specs=out_specs,
        out_shape=[_sds((t, ncols), MM_DTYPE), _sds((kw, ncols)), _sds((1, ncols))],
        compiler_params=_params("parallel", "arbitrary"))(x, x, dy, dy, w)
    return dx, dw, db.reshape(ncols)


def _scan(a, u, *, reverse=False, name):
    t, c = u.shape
    tc = _tile(c, 512, LANES)
    tb = _tile(t, 256, SUBLANES)
    nt = t // tb

    def body(*refs):
        if a is None:
            u_ref, o_ref, carry_ref = refs
        else:
            a_ref, u_ref, o_ref, carry_ref = refs

        @pl.when(pl.program_id(1) == 0)
        def _():
            carry_ref[...] = jnp.zeros_like(carry_ref)

        hv = u_ref[...]
        av = None if a is None else a_ref[...]
        row = lax.broadcasted_iota(jnp.int32, hv.shape, 0)
        s = 1
        while s < tb:
            live = row < tb - s if reverse else row >= s
            shift = tb - s if reverse else s
            h_sh = jnp.where(live, pltpu.roll(hv, shift, axis=0), 0.0)
            if av is None:
                hv = hv + h_sh
            else:
                hv = av * h_sh + hv
                av = av * jnp.where(live, pltpu.roll(av, shift, axis=0), 1.0)
            s *= 2
        hv = hv + carry_ref[...] if av is None else hv + av * carry_ref[...]
        o_ref[...] = hv
        carry_ref[...] = o_ref[pl.ds(0 if reverse else tb - 1, 1), :]

    spec = pl.BlockSpec((tb, tc), (lambda cc, i: (nt - 1 - i, cc)) if reverse else (lambda cc, i: (i, cc)))
    args, specs = ((u,), [spec]) if a is None else ((a, u), [spec, spec])
    return pl.pallas_call(
        body, name=name, grid=(c // tc, t // tb), in_specs=specs, out_specs=spec, out_shape=_sds((t, c)),
        scratch_shapes=[pltpu.VMEM((1, tc), F32)], compiler_params=_params("parallel", "arbitrary"))(*args)


def _shift_down(x):
    return lax.pad(x, jnp.zeros((), x.dtype), ((1, -1, 0), (0, 0, 0)))


def _shift_up(x):
    return lax.pad(x, jnp.zeros((), x.dtype), ((-1, 1, 0), (0, 0, 0)))


def _neg_expm1(y):
    small = -(y * (1.0 + y * (0.5 + y * (1.0 / 6.0 + y * (1.0 / 24.0 + y * (1.0 / 120.0))))))
    return jnp.where(y > -0.05, small, 1.0 - jnp.exp(y))


def _lru_gates(xc, wa, ba, wx, bx, lam):
    r = jax.nn.sigmoid(_mm(xc, wa) + ba)
    i = jax.nn.sigmoid(_mm(xc, wx) + bx)
    log_a = -LRU_C * r * jax.nn.softplus(-lam)
    a = jnp.exp(log_a)
    u = jnp.sqrt(_neg_expm1(2.0 * log_a)) * (i * xc)
    return a, u


def _lru_specs(t):
    tb = _tile(t, 256, SUBLANES)
    row = pl.BlockSpec((tb, GROUP), lambda i: (i, 0))
    mat = pl.BlockSpec((GROUP, GROUP), lambda i: (0, 0))
    vec = pl.BlockSpec((1, GROUP), lambda i: (0, 0))
    return tb, row, mat, vec


def _lru_gates_fwd(xc, wa, ba, wx, bx, lam, *, name):
    t = xc.shape[0]
    tb, row, mat, vec = _lru_specs(t)

    def body(xc_ref, wa_ref, ba_ref, wx_ref, bx_ref, lam_ref, a_ref, u_ref):
        a, u = _lru_gates(xc_ref[...], wa_ref[...], ba_ref[...], wx_ref[...], bx_ref[...], lam_ref[...])
        a_ref[...] = a
        u_ref[...] = u

    return pl.pallas_call(
        body, name=name, grid=(t // tb,), in_specs=[row, mat, vec, mat, vec, vec], out_specs=[row, row],
        out_shape=[_sds((t, GROUP)), _sds((t, GROUP))], compiler_params=_params("parallel"))(xc, wa, ba, wx, bx, lam)


def _lru_gates_bwd(xc, wa, ba, wx, bx, lam, g, h_prev, *, name):
    t = xc.shape[0]
    tb, row, mat, vec = _lru_specs(t)

    def body(xc_ref, wa_ref, ba_ref, wx_ref, bx_ref, lam_ref, g_ref, hp_ref,
             dxc_ref, dwa_ref, dba_ref, dwx_ref, dbx_ref, dlam_ref):
        _, vjp = jax.vjp(_lru_gates, xc_ref[...], wa_ref[...], ba_ref[...], wx_ref[...], bx_ref[...], lam_ref[...])
        gv = g_ref[...]
        dxc, dwa, dba, dwx, dbx, dlam = vjp((gv * hp_ref[...], gv))
        dxc_ref[...] = dxc
        accs = (dwa_ref, dba_ref, dwx_ref, dbx_ref, dlam_ref)

        @pl.when(pl.program_id(0) == 0)
        def _():
            for r in accs:
                r[...] = jnp.zeros_like(r)

        for r, v in zip(accs, (dwa, dba, dwx, dbx, dlam)):
            r[...] += v

    return pl.pallas_call(
        body, name=name, grid=(t // tb,), in_specs=[row, mat, vec, mat, vec, vec, row, row],
        out_specs=[row, mat, vec, mat, vec, vec],
        out_shape=[_sds((t, GROUP)), _sds((GROUP, GROUP)), _sds((1, GROUP)), _sds((GROUP, GROUP)), _sds((1, GROUP)),
                   _sds((1, GROUP))],
        compiler_params=_params("arbitrary"))(xc, wa, ba, wx, bx, lam, g, h_prev)


def _block_diag(w):
    eye = jnp.eye(LRU_BLOCKS, dtype=w.dtype)
    return (eye[:, None, :, None] * w[:, :, None, :]).reshape(GROUP, GROUP)


def _diag_blocks(m):
    m4 = m.reshape(LRU_BLOCKS, LRU_BD, LRU_BLOCKS, LRU_BD)
    return jnp.stack([m4[n, :, n, :] for n in range(LRU_BLOCKS)])


def _post(hg, gain, gate):
    y = hg * lax.rsqrt(jnp.mean(hg * hg, axis=-1, keepdims=True) + EPS) * gain
    return y if gate is None else y * jax.nn.gelu(gate)


def _post_specs(rows, g):
    rb = _tile(rows, 2048, SUBLANES)
    return rb, pl.BlockSpec((rb, g), lambda i: (i, 0)), pl.BlockSpec((rb, g), lambda i: (0, 0))


def _post_fwd(hg, gain8, gate, *, name):
    rows, g = hg.shape
    rb, row, fixed = _post_specs(rows, g)
    gain_t = jnp.tile(gain8, (rb // SUBLANES, 1))

    def body(*refs):
        if gate is None:
            h_ref, gn_ref, o_ref = refs
            o_ref[...] = _post(h_ref[...], gn_ref[...], None).astype(o_ref.dtype)
        else:
            h_ref, gn_ref, gt_ref, o_ref = refs
            o_ref[...] = _post(h_ref[...], gn_ref[...], gt_ref[...]).astype(o_ref.dtype)

    args, specs = ((hg, gain_t), [row, fixed]) if gate is None else ((hg, gain_t, gate), [row, fixed, row])
    return pl.pallas_call(body, name=name, grid=(rows // rb,), in_specs=specs, out_specs=row,
                          out_shape=_sds((rows, g), MM_DTYPE), compiler_params=_params("parallel"))(*args)


def _post_bwd(hg, gain8, gate, dy, *, name):
    rows, g = hg.shape
    rb, row, fixed = _post_specs(rows, g)
    gain_t = jnp.tile(gain8, (rb // SUBLANES, 1))
    g8 = pl.BlockSpec((SUBLANES, g), lambda i: (0, 0))

    def body(*refs):
        if gate is None:
            h_ref, gn_ref, dy_ref, dh_ref, dgn_ref = refs
            _, vjp = jax.vjp(lambda h, gn: _post(h, gn, None), h_ref[...], gn_ref[...])
            dh, dgn = vjp(dy_ref[...])
        else:
            h_ref, gn_ref, gt_ref, dy_ref, dh_ref, dgn_ref, dgt_ref = refs
            _, vjp = jax.vjp(_post, h_ref[...], gn_ref[...], gt_ref[...])
            dh, dgn, dgt = vjp(dy_ref[...])
            dgt_ref[...] = dgt.astype(dgt_ref.dtype)
        dh_ref[...] = dh

        @pl.when(pl.program_id(0) == 0)
        def _():
            dgn_ref[...] = jnp.zeros_like(dgn_ref)

        dgn_ref[...] += dgn.reshape(rb // SUBLANES, SUBLANES, g).sum(axis=0)

    if gate is None:
        dh, dgn = pl.pallas_call(
            body, name=name, grid=(rows // rb,), in_specs=[row, fixed, row], out_specs=[row, g8],
            out_shape=[_sds((rows, g)), _sds((SUBLANES, g))], compiler_params=_params("arbitrary"))(hg, gain_t, dy)
        return dh, dgn, None
    dh, dgn, dgt = pl.pallas_call(
        body, name=name, grid=(rows // rb,), in_specs=[row, fixed, row, row], out_specs=[row, g8, row],
        out_shape=[_sds((rows, g)), _sds((SUBLANES, g)), _sds((rows, g), MM_DTYPE)],
        compiler_params=_params("arbitrary"))(hg, gain_t, gate, dy)
    return dh, dgn, dgt


def _gates(zg, fb, alog, dtb):
    lane = lax.broadcasted_iota(jnp.int32, zg.shape, 1)
    logf = jax.nn.log_sigmoid(zg + fb)
    beta = jax.nn.sigmoid(zg)
    gdec = -jnp.exp(alog) * jax.nn.softplus(zg + dtb)
    return jnp.where(lane < G_BETA, logf, jnp.where(lane < G_ALPHA, beta, jnp.where(lane < G_ALPHA + 4, gdec, 0.0)))


def _lane_row(v, off):
    return jnp.pad(v.reshape(1, N_HEADS), ((0, 0), (off, LANES - N_HEADS - off)))


def _gates_fwd(z, fb, alog, dtb, *, name):
    t = z.shape[0]
    tb = _tile(t, 1024, SUBLANES)
    zspec = pl.BlockSpec((tb, LANES), lambda i: (i, Z_GATES // LANES))
    row = pl.BlockSpec((tb, LANES), lambda i: (i, 0))
    vec = pl.BlockSpec((1, LANES), lambda i: (0, 0))

    def body(z_ref, fb_ref, al_ref, dt_ref, o_ref):
        o_ref[...] = _gates(z_ref[...], fb_ref[...], al_ref[...], dt_ref[...])

    return pl.pallas_call(body, name=name, grid=(t // tb,), in_specs=[zspec, vec, vec, vec], out_specs=row,
                          out_shape=_sds((t, LANES)), compiler_params=_params("parallel"))(z, fb, alog, dtb)


def _gates_bwd(z, fb, alog, dtb, dg, *, name):
    t = z.shape[0]
    tb = _tile(t, 1024, SUBLANES)
    zspec = pl.BlockSpec((tb, LANES), lambda i: (i, Z_GATES // LANES))
    row = pl.BlockSpec((tb, LANES), lambda i: (i, 0))
    vec = pl.BlockSpec((1, LANES), lambda i: (0, 0))

    def body(z_ref, fb_ref, al_ref, dt_ref, dg_ref, dz_ref, dfb_ref, dal_ref, ddt_ref):
        _, vjp = jax.vjp(_gates, z_ref[...], fb_ref[...], al_ref[...], dt_ref[...])
        dz, dfb, dal, ddt = vjp(dg_ref[...])
        dz_ref[...] = dz.astype(dz_ref.dtype)
        accs = (dfb_ref, dal_ref, ddt_ref)

        @pl.when(pl.program_id(0) == 0)
        def _():
            for r in accs:
                r[...] = jnp.zeros_like(r)

        for r, v in zip(accs, (dfb, dal, ddt)):
            r[...] += v

    return pl.pallas_call(
        body, name=name, grid=(t // tb,), in_specs=[zspec, vec, vec, vec, row], out_specs=[row, vec, vec, vec],
        out_shape=[_sds((t, LANES), MM_DTYPE), _sds((1, LANES)), _sds((1, LANES)), _sds((1, LANES))],
        compiler_params=_params("arbitrary"))(z, fb, alog, dtb, dg)


def _pair_weight(d):
    w = jnp.zeros(d.shape, F32)
    for win, dil in DILATED_PAIRS:
        w = w + ((d >= 0) & (d <= win) & ((d & (dil - 1)) == 0)).astype(F32)
    return w


def _att_geometry(t, mode):
    tq = _tile(t, ATT_TILE, LANES)
    nq = t // tq
    band = nq - 1 if mode == "fox" else min(DILATED_PAIRS[-1][0] // tq, nq - 1)
    return tq, nq, band


def _qkv_prep(z, *, off, name):
    t = z.shape[0]
    tb = _tile(t, 512, SUBLANES)
    cb = off // GROUP

    def body(z_ref, o_ref):
        scale = jnp.where(pl.program_id(1) == 0, ATT_SCALE, 1.0)
        o_ref[...] = (z_ref[...] * scale).astype(o_ref.dtype)

    return pl.pallas_call(
        body, name=name, grid=(t // tb, 3), in_specs=[pl.BlockSpec((tb, GROUP), lambda i, j: (i, j + cb))],
        out_specs=pl.BlockSpec((tb, GROUP), lambda i, j: (i, j)), out_shape=_sds((t, 3 * GROUP), MM_DTYPE),
        compiler_params=_params("parallel", "parallel"))(z)


def _block(ref, j, tq):
    return ref[pl.ds(pl.multiple_of(j * tq, tq), tq), :]


def _lane_block(ref, j, tq):
    return ref[0, :, pl.ds(pl.multiple_of(j * tq, tq), tq)]


def _att_tile(mode, q, kj, cq, ckj, dist, masked):
    s = _dg(q, kj, 1, 1, False)
    if mode == "fox":
        s = s + (cq - ckj)
        if not masked:
            return s, None
        return jnp.where(dist >= 0, s, NEG), None
    w = _pair_weight(dist)
    return jnp.where(w > 0.0, s, NEG), w


def _att_fwd(qkv, cq, ck, *, mode, name):
    t = qkv.shape[0]
    tq, nq, band = _att_geometry(t, mode)

    def body(q_ref, k_ref, v_ref, cq_ref, ck_ref, o_ref, lse_ref):
        i = pl.program_id(1)
        q, cqv = q_ref[...], cq_ref[0]
        base = (lax.broadcasted_iota(jnp.int32, (tq, tq), 0) - lax.broadcasted_iota(jnp.int32, (tq, tq), 1))

        def step(j, carry, masked):
            m_old, l_old, acc = carry
            s, w = _att_tile(mode, q, _block(k_ref, j, tq), cqv, _lane_block(ck_ref, j, tq), base + (i - j) * tq, masked)
            m_new = jnp.maximum(m_old, jnp.max(s, axis=-1, keepdims=True))
            alpha = jnp.exp(m_old - m_new)
            p = jnp.exp(s - m_new)
            if w is not None:
                p = p * w
            l_new = alpha * l_old + jnp.sum(p, axis=-1, keepdims=True)
            return m_new, l_new, alpha * acc + _dg(p, _block(v_ref, j, tq), 1, 0, False)

        carry = (jnp.full((tq, 1), NEG, F32), jnp.zeros((tq, 1), F32), jnp.zeros((tq, HEAD), F32))
        if mode == "fox":
            carry = lax.fori_loop(0, i, lambda j, c: step(j, c, False), carry)
            carry = step(i, carry, True)
        else:
            carry = lax.fori_loop(jnp.maximum(i - band, 0), i + 1, lambda j, c: step(j, c, True), carry)
        m_fin, l_fin, acc = carry
        o_ref[...] = acc / l_fin
        lse_ref[0] = m_fin + jnp.log(l_fin)

    col = pl.BlockSpec((1, tq, 1), lambda h, i: (h, i, 0))
    in_specs = [
        pl.BlockSpec((tq, HEAD), lambda h, i: (i, h)),
        pl.BlockSpec((t, HEAD), lambda h, i: (0, N_HEADS + h)),
        pl.BlockSpec((t, HEAD), lambda h, i: (0, 2 * N_HEADS + h)),
        col,
        pl.BlockSpec((1, 1, t), lambda h, i: (h, 0, 0)),
    ]
    return pl.pallas_call(
        body, name=name, grid=(N_HEADS, nq), in_specs=in_specs,
        out_specs=[pl.BlockSpec((tq, HEAD), lambda h, i: (i, h)), col],
        out_shape=[_sds((t, GROUP)), _sds((N_HEADS, t, 1))],
        compiler_params=_params("parallel", "parallel"))(qkv, qkv, qkv, cq, ck)


def _att_bwd_q(qkv, cq, ck, do, o, lse, *, mode, name):
    t = qkv.shape[0]
    tq, nq, band = _att_geometry(t, mode)

    def body(q_ref, k_ref, v_ref, cq_ref, ck_ref, do_ref, o_ref, lse_ref, dq_ref, dcq_ref, delta_ref):
        i = pl.program_id(1)
        q, cqv, lse = q_ref[...], cq_ref[0], lse_ref[0]
        dov = do_ref[...]
        delta = jnp.sum(dov * o_ref[...], axis=-1, keepdims=True)
        do16 = dov.astype(MM_DTYPE)
        base = (lax.broadcasted_iota(jnp.int32, (tq, tq), 0) - lax.broadcasted_iota(jnp.int32, (tq, tq), 1))

        def step(j, carry, masked):
            dq, dcq = carry
            kj = _block(k_ref, j, tq)
            s, w = _att_tile(mode, q, kj, cqv, _lane_block(ck_ref, j, tq), base + (i - j) * tq, masked)
            p = jnp.exp(s - lse)
            if w is not None:
                p = p * w
            ds = p * (_dg(do16, _block(v_ref, j, tq), 1, 1, False) - delta)
            return dq + _dg(ds, kj, 1, 0, False), dcq + jnp.sum(ds, axis=-1, keepdims=True)

        carry = (jnp.zeros((tq, HEAD), F32), jnp.zeros((tq, 1), F32))
        if mode == "fox":
            carry = lax.fori_loop(0, i, lambda j, c: step(j, c, False), carry)
            carry = step(i, carry, True)
        else:
            carry = lax.fori_loop(jnp.maximum(i - band, 0), i + 1, lambda j, c: step(j, c, True), carry)
        dq_ref[...] = (carry[0] * ATT_SCALE).astype(dq_ref.dtype)
        dcq_ref[0] = carry[1]
        delta_ref[0] = delta

    col = pl.BlockSpec((1, tq, 1), lambda h, i: (h, i, 0))
    row = pl.BlockSpec((tq, HEAD), lambda h, i: (i, h))
    in_specs = [
        row,
        pl.BlockSpec((t, HEAD), lambda h, i: (0, N_HEADS + h)),
        pl.BlockSpec((t, HEAD), lambda h, i: (0, 2 * N_HEADS + h)),
        col,
        pl.BlockSpec((1, 1, t), lambda h, i: (h, 0, 0)),
        row, row, col,
    ]
    return pl.pallas_call(
        body, name=name, grid=(N_HEADS, nq), in_specs=in_specs, out_specs=[row, col, col],
        out_shape=[_sds((t, GROUP), MM_DTYPE), _sds((N_HEADS, t, 1)), _sds((N_HEADS, t, 1))],
        compiler_params=_params("parallel", "parallel"))(qkv, qkv, qkv, cq, ck, do, o, lse)


def _att_bwd_kv(qkv, cq, ck, do, lse_row, delta_row, *, mode, name):
    t = qkv.shape[0]
    tq, nq, band = _att_geometry(t, mode)

    def body(q_ref, k_ref, v_ref, cq_ref, ck_ref, do_ref, lse_ref, delta_ref, dk_ref, dv_ref, dck_ref):
        jk = pl.program_id(1)
        kj, vj, ckv = k_ref[...], v_ref[...], cq_ref[0]
        base = (lax.broadcasted_iota(jnp.int32, (tq, tq), 1) - lax.broadcasted_iota(jnp.int32, (tq, tq), 0))

        def step(qi, carry, masked):
            dk, dv, dck = carry
            qb = _block(q_ref, qi, tq)
            s = _dg(kj, qb, 1, 1, False)
            dist = base + (qi - jk) * tq
            w = None
            if mode == "fox":
                s = s + (_lane_block(ck_ref, qi, tq) - ckv)
                if masked:
                    s = jnp.where(dist >= 0, s, NEG)
            else:
                w = _pair_weight(dist)
                s = jnp.where(w > 0.0, s, NEG)
            p = jnp.exp(s - _lane_block(lse_ref, qi, tq))
            if w is not None:
                p = p * w
            do16 = _block(do_ref, qi, tq).astype(MM_DTYPE)
            ds = p * (_dg(vj, do16, 1, 1, False) - _lane_block(delta_ref, qi, tq))
            return (dk + _dg(ds, qb, 1, 0, False), dv + _dg(p, do16, 1, 0, False),
                    dck - jnp.sum(ds, axis=-1, keepdims=True))

        carry = (jnp.zeros((tq, HEAD), F32), jnp.zeros((tq, HEAD), F32), jnp.zeros((tq, 1), F32))
        if mode == "fox":
            carry = step(jk, carry, True)
            carry = lax.fori_loop(jk + 1, nq, lambda qi, c: step(qi, c, False), carry)
        else:
            carry = lax.fori_loop(jk, jnp.minimum(jk + band, nq - 1) + 1, lambda qi, c: step(qi, c, True), carry)
        dk_ref[...] = carry[0].astype(dk_ref.dtype)
        dv_ref[...] = carry[1].astype(dv_ref.dtype)
        dck_ref[0] = carry[2]

    col = pl.BlockSpec((1, tq, 1), lambda h, j: (h, j, 0))
    lanes = pl.BlockSpec((1, 1, t), lambda h, j: (h, 0, 0))
    in_specs = [
        pl.BlockSpec((t, HEAD), lambda h, j: (0, h)),
        pl.BlockSpec((tq, HEAD), lambda h, j: (j, N_HEADS + h)),
        pl.BlockSpec((tq, HEAD), lambda h, j: (j, 2 * N_HEADS + h)),
        col, lanes,
        pl.BlockSpec((t, HEAD), lambda h, j: (0, h)),
        lanes, lanes,
    ]
    out = pl.BlockSpec((tq, HEAD), lambda h, j: (j, h))
    return pl.pallas_call(
        body, name=name, grid=(N_HEADS, nq), in_specs=in_specs, out_specs=[out, out, col],
        out_shape=[_sds((t, GROUP), MM_DTYPE), _sds((t, GROUP), MM_DTYPE), _sds((N_HEADS, t, 1))],
        compiler_params=_params("parallel", "parallel"))(qkv, qkv, qkv, cq, ck, do, lse_row, delta_row)


def _silu(x):
    return x * jax.nn.sigmoid(x)


def _l2n(x):
    return x * lax.rsqrt(jnp.sum(x * x, axis=-1, keepdims=True) + EPS)


def _gdn_chunk(states, xqkv, zg, gts, ng):
    c = GDN_CHUNK
    ri = lax.broadcasted_iota(jnp.int32, (c, c), 0)
    ci = lax.broadcasted_iota(jnp.int32, (c, c), 1)
    tril, strict, eye = ri >= ci, ri > ci, ri == ci
    eyef = eye.astype(F32)
    gcs = _mmh(tril.astype(F32), gts)
    lane = lax.broadcasted_iota(jnp.int32, gts.shape, 1)
    last = lax.broadcasted_iota(jnp.int32, (c, 1), 0) == c - 1
    ys, new_states = [], []
    for h in range(N_HEADS):
        q = _l2n(_silu(xqkv[:, h * HEAD:(h + 1) * HEAD])) * ATT_SCALE
        k = _l2n(_silu(xqkv[:, GROUP + h * HEAD:GROUP + (h + 1) * HEAD]))
        v = _silu(xqkv[:, 2 * GROUP + h * HEAD:2 * GROUP + (h + 1) * HEAD])
        beta = jnp.sum(jnp.where(lane == G_BETA + h, gts, 0.0), axis=-1, keepdims=True)
        gc = jnp.sum(jnp.where(lane == G_ALPHA + h, gcs, 0.0), axis=-1, keepdims=True)
        gr = jnp.sum(jnp.where(eye, jnp.broadcast_to(gc, (c, c)), 0.0), axis=0, keepdims=True)
        decay = jnp.where(tril, jnp.exp(jnp.where(tril, gc - gr, 0.0)), 0.0)
        kbeta, vbeta = k * beta, v * beta
        low = jnp.where(strict, _mm_nt(kbeta, k) * decay, 0.0)
        inv, pw = eyef - low, _mmh(low, low)
        for step in range(5):
            inv = inv + _mmh(inv, pw)
            if step < 4:
                pw = _mmh(pw, pw)
        eg = jnp.exp(gc)
        u = _mmh(inv, vbeta)
        w = _mmh(inv, kbeta * eg)
        qk = jnp.where(tril, _mm_nt(q, k) * decay, 0.0)
        s_in = states[h]
        v_new = u - _mm(w, s_in)
        o = _mm(q * eg, s_in) + _mm(qk, v_new)
        g_last = jnp.sum(jnp.where(last, gc, 0.0), axis=0, keepdims=True)
        new_states.append(s_in * jnp.exp(g_last) + _mm_tn(k * jnp.exp(g_last - gc), v_new))
        o = o * lax.rsqrt(jnp.mean(o * o, axis=-1, keepdims=True) + EPS) * ng
        ys.append(o * _silu(zg[:, h * HEAD:(h + 1) * HEAD]))
    return jnp.concatenate(ys, axis=1), tuple(new_states)


def _gdn_block(states, xqkv, zg, gts, ng):
    ys = []
    for r in range(0, xqkv.shape[0], GDN_CHUNK):
        y, states = _gdn_chunk(states, xqkv[r:r + GDN_CHUNK], zg[r:r + GDN_CHUNK], gts[r:r + GDN_CHUNK], ng)
        ys.append(y)
    return jnp.concatenate(ys, axis=0), states


def _gdn_fwd(xqkv, z, gts, ng, *, name):
    t = xqkv.shape[0]
    c = _tile(t, GDN_BLOCK, GDN_CHUNK)
    n = t // c

    def body(x_ref, z_ref, g_ref, ng_ref, y_ref, sv_ref, s_ref):
        @pl.when(pl.program_id(0) == 0)
        def _():
            s_ref[...] = jnp.zeros_like(s_ref)

        sv_ref[0] = s_ref[...]
        y, new = _gdn_block(tuple(s_ref[h] for h in range(N_HEADS)), x_ref[...], z_ref[...], g_ref[...], ng_ref[...])
        y_ref[...] = y.astype(y_ref.dtype)
        for h in range(N_HEADS):
            s_ref[h] = new[h]

    in_specs = [
        pl.BlockSpec((c, 3 * GROUP), lambda i: (i, 0)),
        pl.BlockSpec((c, GROUP), lambda i: (i, Z_CZ // GROUP)),
        pl.BlockSpec((c, LANES), lambda i: (i, 0)),
        pl.BlockSpec((1, HEAD), lambda i: (0, 0)),
    ]
    out_specs = [pl.BlockSpec((c, GROUP), lambda i: (i, 0)),
                 pl.BlockSpec((1, N_HEADS, HEAD, HEAD), lambda i: (i, 0, 0, 0))]
    return pl.pallas_call(
        body, name=name, grid=(n,), in_specs=in_specs, out_specs=out_specs,
        out_shape=[_sds((t, GROUP), MM_DTYPE), _sds((n, N_HEADS, HEAD, HEAD))],
        scratch_shapes=[pltpu.VMEM((N_HEADS, HEAD, HEAD), F32)], compiler_params=_params("arbitrary"))(xqkv, z, gts, ng)


def _gdn_bwd(xqkv, z, gts, ng, states, dy, *, name):
    t = xqkv.shape[0]
    c = _tile(t, GDN_BLOCK, GDN_CHUNK)
    n = t // c

    def body(x_ref, z_ref, g_ref, ng_ref, sv_ref, dy_ref, dx_ref, dz_ref, dg_ref, dng_ref, ds_ref):
        @pl.when(pl.program_id(0) == 0)
        def _():
            ds_ref[...] = jnp.zeros_like(ds_ref)
            dng_ref[...] = jnp.zeros_like(dng_ref)

        s_in = tuple(sv_ref[0, h] for h in range(N_HEADS))
        _, vjp = jax.vjp(_gdn_block, s_in, x_ref[...], z_ref[...], g_ref[...], ng_ref[...])
        ds_in, dx, dz, dg, dng = vjp((dy_ref[...], tuple(ds_ref[h] for h in range(N_HEADS))))
        dx_ref[...] = dx
        dz_ref[...] = dz.astype(dz_ref.dtype)
        dg_ref[...] = dg
        dng_ref[...] += dng
        for h in range(N_HEADS):
            ds_ref[h] = ds_in[h]

    rev = lambda i: n - 1 - i
    in_specs = [
        pl.BlockSpec((c, 3 * GROUP), lambda i: (rev(i), 0)),
        pl.BlockSpec((c, GROUP), lambda i: (rev(i), Z_CZ // GROUP)),
        pl.BlockSpec((c, LANES), lambda i: (rev(i), 0)),
        pl.BlockSpec((1, HEAD), lambda i: (0, 0)),
        pl.BlockSpec((1, N_HEADS, HEAD, HEAD), lambda i: (rev(i), 0, 0, 0)),
        pl.BlockSpec((c, GROUP), lambda i: (rev(i), 0)),
    ]
    out_specs = [
        pl.BlockSpec((c, 3 * GROUP), lambda i: (rev(i), 0)),
        pl.BlockSpec((c, GROUP), lambda i: (rev(i), 0)),
        pl.BlockSpec((c, LANES), lambda i: (rev(i), 0)),
        pl.BlockSpec((1, HEAD), lambda i: (0, 0)),
    ]
    dx, dz, dg, dng = pl.pallas_call(
        body, name=name, grid=(n,), in_specs=in_specs, out_specs=out_specs,
        out_shape=[_sds((t, 3 * GROUP)), _sds((t, GROUP), MM_DTYPE), _sds((t, LANES)), _sds((1, HEAD))],
        scratch_shapes=[pltpu.VMEM((N_HEADS, HEAD, HEAD), F32)],
        compiler_params=_params("arbitrary"))(xqkv, z, gts, ng, states, dy)
    return dx, dz, dg, dng.reshape(HEAD)


def _swiglu(up, gate):
    return _silu(gate) * up


def _swiglu_fwd(u, *, name):
    t, two_f = u.shape
    dff = two_f // 2
    tb, tc = _tile(t, 512, SUBLANES), _tile(dff, 512, LANES)
    nh = dff // tc

    def body(up_ref, gate_ref, o_ref):
        o_ref[...] = _swiglu(up_ref[...], gate_ref[...]).astype(o_ref.dtype)

    in_specs = [pl.BlockSpec((tb, tc), lambda i, j: (i, j)), pl.BlockSpec((tb, tc), lambda i, j: (i, j + nh))]
    return pl.pallas_call(body, name=name, grid=(t // tb, nh), in_specs=in_specs,
                          out_specs=pl.BlockSpec((tb, tc), lambda i, j: (i, j)), out_shape=_sds((t, dff), MM_DTYPE),
                          compiler_params=_params("parallel", "parallel"))(u, u)


def _swiglu_bwd(u, dact, *, name):
    t, two_f = u.shape
    dff = two_f // 2
    tb, tc = _tile(t, 512, SUBLANES), _tile(dff, 512, LANES)
    nh = dff // tc

    def body(up_ref, gate_ref, da_ref, o_ref):
        _, vjp = jax.vjp(_swiglu, up_ref[...], gate_ref[...])
        dup, dgate = vjp(da_ref[...])
        o_ref[...] = jnp.where(pl.program_id(1) < nh, dup, dgate)

    in_specs = [pl.BlockSpec((tb, tc), lambda i, j: (i, j % nh)), pl.BlockSpec((tb, tc), lambda i, j: (i, j % nh + nh)),
                pl.BlockSpec((tb, tc), lambda i, j: (i, j % nh))]
    return pl.pallas_call(body, name=name, grid=(t // tb, 2 * nh), in_specs=in_specs,
                          out_specs=pl.BlockSpec((tb, tc), lambda i, j: (i, j)), out_shape=_sds((t, two_f)),
                          compiler_params=_params("parallel", "parallel"))(u, u, dact)


def _sum_slots(parts, *, out_dtype=F32, name):
    if not isinstance(parts, (list, tuple)):
        parts = [parts[s] for s in range(parts.shape[0])]
    r = parts[0].shape[0]
    rb = _tile(r, 2048, 2 * SUBLANES)
    spec = pl.BlockSpec((rb, LANES), lambda i: (i, 0))

    def body(*refs):
        acc = refs[0][...].astype(F32)
        for ref in refs[1:-1]:
            acc = acc + ref[...].astype(F32)
        refs[-1][...] = acc.astype(out_dtype)

    return pl.pallas_call(body, name=name, grid=(r // rb,), in_specs=[spec] * len(parts), out_specs=spec,
                          out_shape=_sds((r, LANES), out_dtype), compiler_params=_params("parallel"))(*parts)


def _adamw(w, g, m, v, *, name):
    r, c = w.shape
    rb = _tile(r, 128, SUBLANES)
    spec = pl.BlockSpec((rb, c), lambda i: (i, 0))

    def body(w_ref, g_ref, m_ref, v_ref, d_ref, nm_ref, nv_ref):
        gv = g_ref[...]
        mn = ADAM_B1 * m_ref[...] + (1.0 - ADAM_B1) * gv
        vn = ADAM_B2 * v_ref[...] + (1.0 - ADAM_B2) * (gv * gv)
        m_hat = mn / (1.0 - ADAM_B1 ** ADAM_STEP)
        v_hat = vn / (1.0 - ADAM_B2 ** ADAM_STEP)
        d_ref[...] = -ADAM_LR * (m_hat / (jnp.sqrt(v_hat) + ADAM_EPS) + ADAM_WD * w_ref[...])
        nm_ref[...] = mn
        nv_ref[...] = vn

    return pl.pallas_call(body, name=name, grid=(r // rb,), in_specs=[spec] * 4, out_specs=[spec] * 3,
                          out_shape=[_sds((r, c))] * 3, compiler_params=_params("parallel"))(w, g, m, v)


def _position():
    return lax.axis_index("x"), lax.axis_index("y"), lax.axis_index("c")


def _chip_gather(shards, *, name):
    n = len(shards)

    def body(*refs):
        ins, outs = refs[:n], refs[n:2 * n]
        send_sems, recv_sems, local_sems = refs[2 * n:]
        x, y, c = _position()
        mine = 2 * x + y
        chips = [(1 - x, y), (x, 1 - y), (1 - x, 1 - y)]
        local = [pltpu.make_async_copy(ins[a], outs[a].at[mine], local_sems.at[a]) for a in range(n)]
        for cp in local:
            cp.start()
        sends = []
        for a in range(n):
            for r, (px, py) in enumerate(chips):
                sends.append(pltpu.make_async_remote_copy(
                    src_ref=ins[a], dst_ref=outs[a].at[mine], send_sem=send_sems.at[3 * a + r],
                    recv_sem=recv_sems.at[3 * a + r], device_id=(px, py, c), device_id_type=MESH_ID))
        for cp in sends:
            cp.start()
        for a in range(n):
            for r, (px, py) in enumerate(chips):
                pltpu.make_async_remote_copy(
                    src_ref=ins[a], dst_ref=outs[a].at[2 * px + py], send_sem=send_sems.at[3 * a + r],
                    recv_sem=recv_sems.at[3 * a + r], device_id=(px, py, c), device_id_type=MESH_ID).wait_recv()
        for cp in sends:
            cp.wait_send()
        for cp in local:
            cp.wait()

    return pl.pallas_call(
        body, name=name, in_specs=[ANY] * n, out_specs=[ANY] * n,
        out_shape=[_sds((4,) + s.shape, s.dtype) for s in shards],
        scratch_shapes=[pltpu.SemaphoreType.DMA((3 * n,)), pltpu.SemaphoreType.DMA((3 * n,)),
                        pltpu.SemaphoreType.DMA((n,))],
        compiler_params=pltpu.CompilerParams(has_side_effects=True))(*shards)


def _chip_gather_halves(shards, *, name):
    n = len(shards)

    def body(*refs):
        ins, outs = refs[:n], refs[n:2 * n]
        send_sems, recv_sems, local_sems = refs[2 * n:]
        x, y, c = _position()
        mine = 2 * x + y
        chips = [(1 - x, y), (x, 1 - y), (1 - x, 1 - y)]
        local = [pltpu.make_async_copy(ins[a], outs[a].at[mine], local_sems.at[a]) for a in range(n)]
        for cp in local:
            cp.start()

        def copy(a, r, chip_of_block, half, to, second):
            k = (3 * n if second else 0) + 3 * a + r
            px, py = chip_of_block
            src = outs[a].at[2 * px + py, half] if second else ins[a].at[half]
            return pltpu.make_async_remote_copy(
                src_ref=src, dst_ref=outs[a].at[2 * px + py, half], send_sem=send_sems.at[k], recv_sem=recv_sems.at[k],
                device_id=to, device_id_type=MESH_ID)

        first = [copy(a, r, (x, y), c, (px, py, c), False) for a in range(n) for r, (px, py) in enumerate(chips)]
        for cp in first:
            cp.start()
        passed = []
        for a in range(n):
            for r, chip in enumerate(chips):
                copy(a, r, chip, c, (x, y, c), False).wait_recv()
                passed.append(copy(a, r, chip, c, (x, y, 1 - c), True))
                passed[-1].start()
        for a in range(n):
            for r, chip in enumerate(chips):
                copy(a, r, chip, 1 - c, (x, y, c), True).wait_recv()
        for cp in first + passed:
            cp.wait_send()
        for cp in local:
            cp.wait()

    return pl.pallas_call(
        body, name=name, in_specs=[ANY] * n, out_specs=[ANY] * n,
        out_shape=[_sds((4,) + s.shape, s.dtype) for s in shards],
        scratch_shapes=[pltpu.SemaphoreType.DMA((6 * n,)), pltpu.SemaphoreType.DMA((6 * n,)),
                        pltpu.SemaphoreType.DMA((n,))],
        compiler_params=pltpu.CompilerParams(has_side_effects=True))(*shards)


def _sibling_send(v, *, name):
    def body(v_ref, got_ref, send_sem, recv_sem):
        x, y, c = _position()
        cp = pltpu.make_async_remote_copy(src_ref=v_ref, dst_ref=got_ref, send_sem=send_sem, recv_sem=recv_sem,
                                          device_id=(x, y, 1 - c), device_id_type=MESH_ID)
        cp.start()
        cp.wait()

    return pl.pallas_call(
        body, name=name, in_specs=[ANY], out_specs=ANY, out_shape=_sds(v.shape, v.dtype),
        scratch_shapes=[pltpu.SemaphoreType.DMA, pltpu.SemaphoreType.DMA],
        compiler_params=pltpu.CompilerParams(has_side_effects=True))(v)


def _chip_exchange(v, *, name):
    def body(v_ref, got_ref, send_sems, recv_sems, local_sem):
        x, y, c = _position()
        mine = 2 * x + y
        chips = [(1 - x, y), (x, 1 - y), (1 - x, 1 - y)]
        local = pltpu.make_async_copy(v_ref.at[mine], got_ref.at[mine], local_sem)
        local.start()
        sends = [pltpu.make_async_remote_copy(
            src_ref=v_ref.at[2 * px + py], dst_ref=got_ref.at[mine], send_sem=send_sems.at[r],
            recv_sem=recv_sems.at[r], device_id=(px, py, c), device_id_type=MESH_ID)
            for r, (px, py) in enumerate(chips)]
        for cp in sends:
            cp.start()
        for r, (px, py) in enumerate(chips):
            pltpu.make_async_remote_copy(
                src_ref=v_ref.at[mine], dst_ref=got_ref.at[2 * px + py], send_sem=send_sems.at[r],
                recv_sem=recv_sems.at[r], device_id=(px, py, c), device_id_type=MESH_ID).wait_recv()
        for cp in sends:
            cp.wait_send()
        local.wait()

    return pl.pallas_call(
        body, name=name, in_specs=[ANY], out_specs=ANY, out_shape=_sds(v.shape, v.dtype),
        scratch_shapes=[pltpu.SemaphoreType.DMA((3,)), pltpu.SemaphoreType.DMA((3,)), pltpu.SemaphoreType.DMA],
        compiler_params=pltpu.CompilerParams(has_side_effects=True))(v)


def _sibling_pair(v, *, name):
    def body(v_ref, out_ref, send_sem, recv_sem, local_sem):
        x, y, c = _position()
        local = pltpu.make_async_copy(v_ref, out_ref.at[c], local_sem)
        local.start()
        cp = pltpu.make_async_remote_copy(src_ref=v_ref, dst_ref=out_ref.at[c], send_sem=send_sem, recv_sem=recv_sem,
                                          device_id=(x, y, 1 - c), device_id_type=MESH_ID)
        cp.start()
        pltpu.make_async_remote_copy(src_ref=v_ref, dst_ref=out_ref.at[1 - c], send_sem=send_sem, recv_sem=recv_sem,
                                     device_id=(x, y, 1 - c), device_id_type=MESH_ID).wait_recv()
        cp.wait_send()
        local.wait()

    return pl.pallas_call(
        body, name=name, in_specs=[ANY], out_specs=ANY, out_shape=_sds((2,) + v.shape, v.dtype),
        scratch_shapes=[pltpu.SemaphoreType.DMA, pltpu.SemaphoreType.DMA, pltpu.SemaphoreType.DMA],
        compiler_params=pltpu.CompilerParams(has_side_effects=True))(v)


def _all_sum(v, *, name):
    r = v.shape[0]
    masks = [(mx, my, mc) for mx in (0, 1) for my in (0, 1) for mc in (0, 1)][1:]

    def body(v_ref, out_ref, slots, send_sems, recv_sems, local_sem):
        x, y, c = _position()
        me = 4 * x + 2 * y + c

        def peer(mask):
            return tuple(1 - p if bit else p for p, bit in zip((x, y, c), mask))

        local = pltpu.make_async_copy(v_ref, slots.at[me], local_sem)
        local.start()
        sends = [pltpu.make_async_remote_copy(
            src_ref=v_ref, dst_ref=slots.at[me], send_sem=send_sems.at[k], recv_sem=recv_sems.at[k],
            device_id=peer(mask), device_id_type=MESH_ID) for k, mask in enumerate(masks)]
        for cp in sends:
            cp.start()
        for k, mask in enumerate(masks):
            px, py, pc = peer(mask)
            pltpu.make_async_remote_copy(
                src_ref=v_ref, dst_ref=slots.at[4 * px + 2 * py + pc], send_sem=send_sems.at[k],
                recv_sem=recv_sems.at[k], device_id=(px, py, pc), device_id_type=MESH_ID).wait_recv()
        for cp in sends:
            cp.wait_send()
        local.wait()
        acc = slots[0]
        for s in range(1, 8):
            acc = acc + slots[s]
        out_ref[...] = acc

    vm = pl.BlockSpec(memory_space=pltpu.VMEM)
    return pl.pallas_call(
        body, name=name, in_specs=[vm], out_specs=vm, out_shape=_sds((r, LANES)),
        scratch_shapes=[pltpu.VMEM((8, r, LANES), F32), pltpu.SemaphoreType.DMA((7,)), pltpu.SemaphoreType.DMA((7,)),
                        pltpu.SemaphoreType.DMA],
        compiler_params=pltpu.CompilerParams(vmem_limit_bytes=VMEM_LIMIT_BYTES, has_side_effects=True))(v)


def _pack_rows(arrays, align=SUBLANES * LANES):
    flat = jnp.concatenate([a.reshape(-1) for a in arrays])
    n = flat.shape[0]
    pad = (-n) % align
    if pad:
        flat = jnp.concatenate([flat, jnp.zeros((pad,), flat.dtype)])
    return flat.reshape(-1, LANES), [a.shape for a in arrays]


def _unpack_rows(rows, shapes):
    flat = rows.reshape(-1)
    out, off = [], 0
    for s in shapes:
        n = int(np.prod(s))
        out.append(flat[off:off + n].reshape(s))
        off += n
    return out


def _pad_w_in(w):
    d = w.shape[0]
    return jnp.concatenate([w[:, 0:2560], w[:, 2564:4612], w[:, 4620:6156], w[:, 2560:2564], w[:, 4612:4620],
                            jnp.zeros((d, Z_COLS - IN_COLS), w.dtype)], axis=1)


def _unpad_w_in(g):
    return jnp.concatenate([g[:, 0:2560], g[:, 6144:6148], g[:, 2560:4608], g[:, 6148:6156], g[:, 4608:6144]], axis=1)


def _gate_rows(p):
    return (_lane_row(p["fox_f_bias"], G_F), _lane_row(p["gdn_a_log"], G_ALPHA), _lane_row(p["gdn_dt_bias"], G_ALPHA))


def _head_cols(c_rows, t):
    ct = c_rows[:, :N_HEADS].T
    return ct.reshape(N_HEADS, t, 1), ct.reshape(N_HEADS, 1, t)


def _layer_fwd(x, p):
    t = x.shape[0]
    s = {"x": x}
    s["h"] = _rmsnorm_fwd(x, p["norm_mix"], name="mix_norm")
    z = s["z"] = _matmul(s["h"], p["w_in"], name="in_proj")
    s["xc"] = _conv_fwd(z, p["lru_conv_w"], p["lru_conv_b"], ncols=GROUP, coff=Z_AX, name="lru_conv")
    lru = s["lru"] = (_block_diag(p["lru_wa"]).astype(MM_DTYPE), p["lru_ba"].reshape(1, GROUP),
                      _block_diag(p["lru_wx"]).astype(MM_DTYPE), p["lru_bx"].reshape(1, GROUP),
                      p["lru_lambda"].reshape(1, GROUP))
    s["a"], u = _lru_gates_fwd(s["xc"], *lru, name="lru_gates")
    s["ha"] = _scan(s["a"], u, name="lru_scan")
    s["gate_a"] = z[:, Z_AG:Z_AG + GROUP].reshape(t * LRU_BLOCKS, LRU_BD)
    y_a = _post_fwd(s["ha"].reshape(t * LRU_BLOCKS, LRU_BD), p["norm_a"].reshape(LRU_BLOCKS, LRU_BD), s["gate_a"],
                    name="lru_post")
    s["gts"] = _gates_fwd(z, *_gate_rows(p), name="gates")
    s["cq"], s["ck"] = _head_cols(_scan(None, s["gts"], name="fox_cumsum"), t)
    s["qkv_b"] = _qkv_prep(z, off=Z_BQ, name="fox_prep")
    s["ob"], s["lse_b"] = _att_fwd(s["qkv_b"], s["cq"], s["ck"], mode="fox", name="fox_att")
    gain_b = jnp.tile(p["norm_b"].reshape(N_HEADS, HEAD), (2, 1))
    y_b = _post_fwd(s["ob"].reshape(t * N_HEADS, HEAD), gain_b, None, name="fox_post")
    s["cconv"] = _conv_fwd(z, p["gdn_conv_w"], jnp.zeros((3 * GROUP,), F32), ncols=3 * GROUP, coff=Z_CQKV,
                           name="gdn_conv")
    y_c, s["gdn_states"] = _gdn_fwd(s["cconv"], z, s["gts"], p["gdn_norm"].reshape(1, HEAD), name="gdn_chunks")
    s["qkv_d"] = _qkv_prep(z, off=Z_DQ, name="dil_prep")
    s["od"], s["lse_d"] = _att_fwd(s["qkv_d"], s["cq"], s["ck"], mode="dil", name="dil_att")
    gain_d = jnp.tile(p["norm_d"].reshape(N_HEADS, HEAD), (2, 1))
    y_d = _post_fwd(s["od"].reshape(t * N_HEADS, HEAD), gain_d, None, name="dil_post")
    y = s["y"] = jnp.concatenate([y_a.reshape(t, GROUP), y_b.reshape(t, GROUP), y_c, y_d.reshape(t, GROUP)], axis=1)
    x1 = s["x1"] = _matmul(y, p["w_out"], add=x, name="out_proj")
    s["h2"] = _rmsnorm_fwd(x1, p["norm_ffn"], name="ffn_norm")
    s["uu"] = _matmul(s["h2"], p["ffn_w_up"], name="ffn_up")
    s["u"] = _conv_fwd(s["uu"], p["ffn_conv_w"], p["ffn_conv_b"], ncols=s["uu"].shape[1], name="ffn_conv")
    s["act"] = _swiglu_fwd(s["u"], name="ffn_swiglu")
    return _matmul(s["act"], p["ffn_w_down"], add=x1, name="ffn_down"), s


def _layer_bwd(dx2, p, s):
    t = dx2.shape[0]
    g = {}
    dact = _matmul(dx2, p["ffn_w_down"], form="nt", name="ffn_down_dx")
    g["ffn_w_down"] = _matmul(s["act"], dx2, form="tn", name="ffn_down_dw")
    du = _swiglu_bwd(s["u"], dact, name="ffn_swiglu_bwd")
    duu, g["ffn_conv_w"], g["ffn_conv_b"] = _conv_bwd(s["uu"], p["ffn_conv_w"], du, ncols=du.shape[1],
                                                      name="ffn_conv_bwd")
    dh2 = _matmul(duu, p["ffn_w_up"], form="nt", name="ffn_up_dx")
    g["ffn_w_up"] = _matmul(s["h2"], duu, form="tn", name="ffn_up_dw")
    dx1, g["norm_ffn"] = _rmsnorm_bwd(s["x1"], p["norm_ffn"], dh2, dx2, name="ffn_norm_bwd")
    dy = _matmul(dx1, p["w_out"], form="nt", name="out_proj_dx")
    g["w_out"] = _matmul(s["y"], dx1, form="tn", name="out_proj_dw")
    z = s["z"]
    dha, dgn, dgate_a = _post_bwd(s["ha"].reshape(t * LRU_BLOCKS, LRU_BD), p["norm_a"].reshape(LRU_BLOCKS, LRU_BD),
                                  s["gate_a"], dy[:, 0:GROUP].reshape(t * LRU_BLOCKS, LRU_BD), name="lru_post_bwd")
    g["norm_a"] = dgn.reshape(GROUP)
    gsc = _scan(_shift_up(s["a"]), dha.reshape(t, GROUP), reverse=True, name="lru_scan_bwd")
    dxc, dwa, dba, dwx, dbx, dlam = _lru_gates_bwd(s["xc"], *s["lru"], gsc, _shift_down(s["ha"]), name="lru_gates_bwd")
    g["lru_wa"], g["lru_wx"] = _diag_blocks(dwa), _diag_blocks(dwx)
    g["lru_ba"], g["lru_bx"], g["lru_lambda"] = dba.reshape(GROUP), dbx.reshape(GROUP), dlam.reshape(GROUP)
    dax, g["lru_conv_w"], g["lru_conv_b"] = _conv_bwd(z, p["lru_conv_w"], dxc, ncols=GROUP, coff=Z_AX,
                                                      name="lru_conv_bwd")
    gain_b = jnp.tile(p["norm_b"].reshape(N_HEADS, HEAD), (2, 1))
    dob, dgn, _ = _post_bwd(s["ob"].reshape(t * N_HEADS, HEAD), gain_b, None,
                            dy[:, GROUP:2 * GROUP].reshape(t * N_HEADS, HEAD), name="fox_post_bwd")
    g["norm_b"] = (dgn[:N_HEADS] + dgn[N_HEADS:]).reshape(GROUP)
    dob = dob.reshape(t, GROUP)
    dbq, dcq, delta = _att_bwd_q(s["qkv_b"], s["cq"], s["ck"], dob, s["ob"], s["lse_b"], mode="fox", name="fox_att_dq")
    dbk, dbv, dck = _att_bwd_kv(s["qkv_b"], s["cq"], s["ck"], dob, s["lse_b"].reshape(N_HEADS, 1, t),
                                delta.reshape(N_HEADS, 1, t), mode="fox", name="fox_att_dkv")
    pad_lanes = ((0, 0), (0, LANES - N_HEADS))
    dc_rows = _sum_slots([jnp.pad(dcq.reshape(N_HEADS, t).T, pad_lanes), jnp.pad(dck.reshape(N_HEADS, t).T, pad_lanes)],
                         name="fox_dc_sum")
    dgts_fox = _scan(None, dc_rows, reverse=True, name="fox_cumsum_bwd")
    gain_d = jnp.tile(p["norm_d"].reshape(N_HEADS, HEAD), (2, 1))
    dod, dgn, _ = _post_bwd(s["od"].reshape(t * N_HEADS, HEAD), gain_d, None,
                            dy[:, 3 * GROUP:4 * GROUP].reshape(t * N_HEADS, HEAD), name="dil_post_bwd")
    g["norm_d"] = (dgn[:N_HEADS] + dgn[N_HEADS:]).reshape(GROUP)
    dod = dod.reshape(t, GROUP)
    ddq, _, delta = _att_bwd_q(s["qkv_d"], s["cq"], s["ck"], dod, s["od"], s["lse_d"], mode="dil", name="dil_att_dq")
    ddk, ddv, _ = _att_bwd_kv(s["qkv_d"], s["cq"], s["ck"], dod, s["lse_d"].reshape(N_HEADS, 1, t),
                              delta.reshape(N_HEADS, 1, t), mode="dil", name="dil_att_dkv")
    dcconv, dcz, dgts_gdn, g["gdn_norm"] = _gdn_bwd(s["cconv"], z, s["gts"], p["gdn_norm"].reshape(1, HEAD),
                                                    s["gdn_states"], dy[:, 2 * GROUP:3 * GROUP], name="gdn_chunks_bwd")
    dcqkv, g["gdn_conv_w"], _ = _conv_bwd(z, p["gdn_conv_w"], dcconv, ncols=3 * GROUP, coff=Z_CQKV,
                                          name="gdn_conv_bwd")
    dgts = _sum_slots([dgts_fox, dgts_gdn], name="gates_dsum")
    dzg, dfb, dal, ddt = _gates_bwd(z, *_gate_rows(p), dgts, name="gates_bwd")
    g["fox_f_bias"] = dfb[0, G_F:G_F + N_HEADS]
    g["gdn_a_log"] = dal[0, G_ALPHA:G_ALPHA + N_HEADS]
    g["gdn_dt_bias"] = ddt[0, G_ALPHA:G_ALPHA + N_HEADS]
    dz = jnp.concatenate([dax, dgate_a.reshape(t, GROUP), dbq, dbk, dbv, dcqkv, dcz, ddq, ddk, ddv, dzg], axis=1)
    dh = _matmul(dz, p["w_in"], form="nt", name="in_proj_dx")
    g["w_in"] = _matmul(s["h"], dz, form="tn", name="in_proj_dw")
    dx, g["norm_mix"] = _rmsnorm_bwd(s["x"], p["norm_mix"], dh, dx1, name="mix_norm_bwd")
    return dx, g


def _local_step(x, tgt, layers, norm_final):
    saved = []
    for p in layers:
        x, s = _layer_fwd(x, p)
        saved.append(s)
    loss, dx, dnf = _loss_head(x, norm_final, tgt, name="loss_head")
    grads = []
    for p, s in zip(reversed(layers), reversed(saved)):
        dx, g = _layer_bwd(dx, p, s)
        grads.append(g)
    return loss, dx, grads[::-1], dnf


BIG = ("w_in", "w_out", "ffn_w_up", "ffn_w_down")
PACK_ROWS = 4096
SHARDED_SMALL = ("lru_conv_w", "gdn_conv_w", "ffn_conv_w")
NAMES = ("norm_mix", "w_in", "lru_conv_w", "lru_conv_b", "lru_wa", "lru_ba", "lru_wx", "lru_bx", "lru_lambda",
         "fox_f_bias", "gdn_conv_w", "gdn_a_log", "gdn_dt_bias", "gdn_norm", "norm_a", "norm_b", "norm_d", "w_out",
         "norm_ffn", "ffn_w_up", "ffn_conv_w", "ffn_conv_b", "ffn_w_down", "norm_final")
SMALL = tuple(n for n in NAMES if n not in BIG)


def _big_pieces(g, k_axis_cols):
    if k_axis_cols:
        d, n = g.shape
        return g.reshape(d, 4, n // 4).transpose(1, 0, 2).reshape(4, -1, LANES)
    return g.reshape(4, -1, LANES)


def _reduce_big(gl, *, c):
    whole = {"w_in": _unpad_w_in(gl["w_in"]), "w_out": gl["w_out"], "ffn_w_up": gl["ffn_w_up"],
             "ffn_w_down": gl["ffn_w_down"]}
    cols = {"w_in": True, "w_out": False, "ffn_w_up": True, "ffn_w_down": False}
    pieces = [_big_pieces(whole[n], cols[n]) for n in BIG]
    rows = [q.shape[1] for q in pieces]
    pad = (-sum(rows)) % PACK_ROWS
    packed = jnp.concatenate(pieces + [jnp.zeros((4, pad, LANES), F32)], axis=1)
    half = packed.shape[1] // 2
    halves = packed.reshape(4, 2, half, LANES)
    mine = lax.dynamic_index_in_dim(halves, c, axis=1, keepdims=False).reshape(4 * half, LANES)
    other = lax.dynamic_index_in_dim(halves, 1 - c, axis=1, keepdims=False).reshape(4 * half, LANES)
    got = _sibling_send(other, name="grad_sibling_send")
    chip_sum = _sum_slots([mine, got], out_dtype=MM_DTYPE, name="grad_sibling_sum").reshape(4, half, LANES)
    from_chips = _chip_exchange(chip_sum, name="grad_chip_exchange")
    total_half = _sum_slots(from_chips, name="grad_chip_sum")
    total = _sibling_pair(total_half, name="grad_sibling_pair").reshape(2 * half, LANES)
    out, off = {}, 0
    for n, r in zip(BIG, rows):
        out[n] = total[off:off + r]
        off += r
    return out


def kernel(x, norm_mix, w_in, lru_conv_w, lru_conv_b, lru_wa, lru_ba, lru_wx, lru_bx, lru_lambda, fox_f_bias, gdn_conv_w, gdn_a_log, gdn_dt_bias, gdn_norm, norm_a, norm_b, norm_d, w_out, norm_ffn, ffn_w_up, ffn_conv_w, ffn_conv_b, ffn_w_down, norm_final, loss_target, m_norm_mix, m_w_in, m_lru_conv_w, m_lru_conv_b, m_lru_wa, m_lru_ba, m_lru_wx, m_lru_bx, m_lru_lambda, m_fox_f_bias, m_gdn_conv_w, m_gdn_a_log, m_gdn_dt_bias, m_gdn_norm, m_norm_a, m_norm_b, m_norm_d, m_w_out, m_norm_ffn, m_ffn_w_up, m_ffn_conv_w, m_ffn_conv_b, m_ffn_w_down, m_norm_final, v_norm_mix, v_w_in, v_lru_conv_w, v_lru_conv_b, v_lru_wa, v_lru_ba, v_lru_wx, v_lru_bx, v_lru_lambda, v_fox_f_bias, v_gdn_conv_w, v_gdn_a_log, v_gdn_dt_bias, v_gdn_norm, v_norm_a, v_norm_b, v_norm_d, v_w_out, v_norm_ffn, v_ffn_w_up, v_ffn_conv_w, v_ffn_conv_b, v_ffn_w_down, v_norm_final):
    w = dict(zip(NAMES, (norm_mix, w_in, lru_conv_w, lru_conv_b, lru_wa, lru_ba, lru_wx, lru_bx, lru_lambda, fox_f_bias,
                         gdn_conv_w, gdn_a_log, gdn_dt_bias, gdn_norm, norm_a, norm_b, norm_d, w_out, norm_ffn, ffn_w_up,
                         ffn_conv_w, ffn_conv_b, ffn_w_down, norm_final)))
    m = dict(zip(NAMES, (m_norm_mix, m_w_in, m_lru_conv_w, m_lru_conv_b, m_lru_wa, m_lru_ba, m_lru_wx, m_lru_bx,
                         m_lru_lambda, m_fox_f_bias, m_gdn_conv_w, m_gdn_a_log, m_gdn_dt_bias, m_gdn_norm, m_norm_a,
                         m_norm_b, m_norm_d, m_w_out, m_norm_ffn, m_ffn_w_up, m_ffn_conv_w, m_ffn_conv_b, m_ffn_w_down,
                         m_norm_final)))
    v = dict(zip(NAMES, (v_norm_mix, v_w_in, v_lru_conv_w, v_lru_conv_b, v_lru_wa, v_lru_ba, v_lru_wx, v_lru_bx,
                         v_lru_lambda, v_fox_f_bias, v_gdn_conv_w, v_gdn_a_log, v_gdn_dt_bias, v_gdn_norm, v_norm_a,
                         v_norm_b, v_norm_d, v_w_out, v_norm_ffn, v_ffn_w_up, v_ffn_conv_w, v_ffn_conv_b, v_ffn_w_down,
                         v_norm_final)))
    depth = w_in.shape[0]
    xi, yi, ci = _position()
    chip = 2 * xi + yi

    conv_rows, conv_shapes = _pack_rows([w[n] for n in SHARDED_SMALL])
    (conv_all,) = _chip_gather([conv_rows], name="conv_taps_gather")
    conv_full = {}
    per_chip = [_unpack_rows(conv_all[k], conv_shapes) for k in range(4)]
    for i, n in enumerate(SHARDED_SMALL):
        conv_full[n] = jnp.concatenate([per_chip[k][i] for k in range(4)], axis=-1)
    layers = []
    for l in range(depth):
        halves = [w[n][l].astype(MM_DTYPE).reshape(2, w[n].shape[1] // 2, w[n].shape[2]) for n in BIG]
        g_in, g_out, g_up, g_dn = (g.reshape((4,) + w[n].shape[1:]) for n, g in zip(
            BIG, _chip_gather_halves(halves, name="weights_gather")))
        p = {n: w[n][l] for n in SMALL if n != "norm_final" and n not in SHARDED_SMALL}
        for n in SHARDED_SMALL:
            p[n] = conv_full[n][l]
        p["w_in"] = _pad_w_in(jnp.concatenate([g_in[k] for k in range(4)], axis=1))
        p["w_out"] = g_out.reshape(-1, g_out.shape[-1])
        p["ffn_w_up"] = jnp.concatenate([g_up[k] for k in range(4)], axis=1)
        p["ffn_w_down"] = g_dn.reshape(-1, g_dn.shape[-1])
        layers.append(p)

    loss, grad_x, grads, g_norm_final = _local_step(x[0], loss_target[0], layers, norm_final)
    loss = lax.psum(loss, ("x", "y", "c"))

    big = [_reduce_big(grads[l], c=ci) for l in range(depth)]
    small_names = [n for n in SMALL if n != "norm_final"]
    small_rows, small_shapes = _pack_rows([jnp.stack([grads[l][n] for l in range(depth)]) for n in small_names]
                                          + [g_norm_final])
    small_sum = _unpack_rows(_all_sum(small_rows, name="small_grads_sum"), small_shapes)
    gsum = dict(zip(small_names + ["norm_final"], small_sum))
    for n in SHARDED_SMALL:
        width = w[n].shape[-1]
        gsum[n] = lax.dynamic_slice_in_dim(gsum[n], chip * width, width, axis=-1)
    for n in BIG:
        gsum[n] = jnp.stack([big[l][n].reshape(w[n].shape[1:]) for l in range(depth)])

    delta, new_m, new_v = {}, {}, {}
    for n in BIG:
        cols = w[n].shape[-1]
        d_, m_, v_ = _adamw(w[n].reshape(-1, cols), gsum[n].reshape(-1, cols), m[n].reshape(-1, cols),
                            v[n].reshape(-1, cols), name="adamw_" + n)
        delta[n], new_m[n], new_v[n] = (a.reshape(w[n].shape) for a in (d_, m_, v_))
    packs = [_pack_rows([src[n] for n in SMALL]) for src in (w, gsum, m, v)]
    outs = _adamw(*[pk[0] for pk in packs], name="adamw_small")
    for dst, rows_ in zip((delta, new_m, new_v), outs):
        dst.update(zip(SMALL, _unpack_rows(rows_, packs[0][1])))
    return (loss, grad_x[None], *[gsum[n] for n in NAMES], *[delta[n] for n in NAMES], *[new_m[n] for n in NAMES],
            *[new_v[n] for n in NAMES])
```

```python
import functools

import numpy as np
import jax
import jax.numpy as jnp
from jax import lax
from jax.experimental import pallas as pl
from jax.experimental.pallas import tpu as pltpu

F32 = jnp.float32
MM_DTYPE = jnp.bfloat16
VMEM_LIMIT_BYTES = 56 * 1024 * 1024
LANES = 128
SUBLANES = 8

GROUP = 512
HEAD = 128
N_HEADS = GROUP // HEAD
LRU_BLOCKS = 8
LRU_BD = GROUP // LRU_BLOCKS
LRU_C = 8.0
GDN_CHUNK = 64
DILATED_PAIRS = ((128, 1), (512, 4), (2048, 16))
EPS = 1e-6
NEG = -1e30
ATT_SCALE = HEAD ** -0.5
ATT_TILE = 512

ADAM_LR, ADAM_B1, ADAM_B2, ADAM_EPS, ADAM_WD, ADAM_STEP = 0.001, 0.9, 0.999, 1e-08, 0.01, 10

Z_AX, Z_AG, Z_BQ, Z_BK, Z_BV = 0, 512, 1024, 1536, 2048
Z_CQKV, Z_CZ, Z_DQ, Z_DK, Z_DV, Z_GATES, Z_COLS = 2560, 4096, 4608, 5120, 5632, 6144, 6272
IN_COLS = 6156
G_F, G_BETA, G_ALPHA = 0, 4, 8

MESH_ID = pl.DeviceIdType.MESH
ANY = pl.BlockSpec(memory_space=pl.ANY)


def _tile(n, target, align):
    t = min(n, target) // align * align
    while t >= align:
        if n % t == 0:
            return t
        t -= align
    return n


def _params(*sem):
    return pltpu.CompilerParams(dimension_semantics=sem, vmem_limit_bytes=VMEM_LIMIT_BYTES)


def _sds(shape, dtype=F32):
    return jax.ShapeDtypeStruct(tuple(shape), dtype)


def _dg(a, b, ca, cb, hi, batched=False):
    off = int(batched)
    dims = (((ca + off,), (cb + off,)), (((0,), (0,)) if batched else ((), ())))
    dot = lambda p, q: lax.dot_general(p, q, dims, preferred_element_type=F32)
    if hi:
        a_hi, b_hi = a.astype(MM_DTYPE), b.astype(MM_DTYPE)
        a_lo = (a - a_hi.astype(F32)).astype(MM_DTYPE)
        b_lo = (b - b_hi.astype(F32)).astype(MM_DTYPE)
        return dot(a_hi, b_hi) + (dot(a_hi, b_lo) + dot(a_lo, b_hi))
    return dot(a.astype(MM_DTYPE), b.astype(MM_DTYPE))


_FORMS = {"nn": (1, 0), "nt": (1, 1), "tn": (0, 0)}


@functools.partial(jax.custom_vjp, nondiff_argnums=(2, 3, 4))
def _mmf(a, b, form, hi, batched=False):
    ca, cb = _FORMS[form]
    return _dg(a, b, ca, cb, hi, batched)


def _mmf_fwd(a, b, form, hi, batched):
    return _mmf(a, b, form, hi, batched), (a, b)


def _mmf_bwd(form, hi, batched, res, g):
    a, b = res
    if form == "nn":
        return _dg(g, b, 1, 1, hi, batched), _dg(a, g, 0, 0, hi, batched)
    if form == "nt":
        return _dg(g, b, 1, 0, hi, batched), _dg(g, a, 0, 0, hi, batched)
    return _dg(b, g, 1, 1, hi, batched), _dg(a, g, 1, 0, hi, batched)


_mmf.defvjp(_mmf_fwd, _mmf_bwd)


def _mm(a, b):
    return _mmf(a, b, "nn", False)


def _mm_nt(a, b):
    return _mmf(a, b, "nt", False)


def _mm_tn(a, b):
    return _mmf(a, b, "tn", False)


MXU_DIM = 256
MATMUL_VMEM_BUDGET = 40 * 1024 * 1024


def _matmul_tiles(m, n, k, a_bytes, b_bytes, o_bytes, has_add):
    divisors = lambda d: [t for t in range(d, 0, -LANES) if d % t == 0 and t % LANES == 0] or [d]
    for tk in divisors(k):
        fits = []
        for tm in divisors(m):
            for tn in divisors(n):
                need = 2 * (tm * tk * a_bytes + tk * tn * b_bytes) + 2 * tm * tn * o_bytes
                need += (2 * tm * tn * 4 if has_add else 0) + (tm * tn * 4 if tk < k else 0)
                if tm <= 1024 and tn <= 1024 and need <= MATMUL_VMEM_BUDGET:
                    fits.append((min(tm, MXU_DIM) * min(tn, MXU_DIM), tm * tn, tm, tn))
        if fits:
            return max(fits)[2:] + (tk,)
    return _tile(m, 128, LANES), _tile(n, 128, LANES), _tile(k, 128, LANES)


def _matmul(a, b, *, form="nn", add=None, out_dtype=F32, tiles=None, name):
    ca, cb = _FORMS[form]
    m, k = (a.shape[1], a.shape[0]) if form == "tn" else a.shape
    n = b.shape[0] if form == "nt" else b.shape[1]
    tm, tn, tk = tiles or _matmul_tiles(m, n, k, a.dtype.itemsize, b.dtype.itemsize, jnp.dtype(out_dtype).itemsize,
                                        add is not None)
    nk = k // tk
    a_spec = (pl.BlockSpec((tk, tm), lambda i, j, kk: (kk, i)) if form == "tn"
              else pl.BlockSpec((tm, tk), lambda i, j, kk: (i, kk)))
    b_spec = (pl.BlockSpec((tn, tk), lambda i, j, kk: (j, kk)) if form == "nt"
              else pl.BlockSpec((tk, tn), lambda i, j, kk: (kk, j)))
    o_spec = pl.BlockSpec((tm, tn), lambda i, j, kk: (i, j))

    def body(*refs):
        a_ref, b_ref = refs[:2]
        add_ref = None if add is None else refs[2]
        o_ref = refs[2 + (add is not None)]
        part = _dg(a_ref[...], b_ref[...], ca, cb, False)
        if nk == 1:
            o_ref[...] = (part if add is None else part + add_ref[...]).astype(out_dtype)
            return
        acc_ref = refs[-1]
        kk = pl.program_id(2)

        @pl.when(kk == 0)
        def _():
            acc_ref[...] = jnp.zeros_like(acc_ref)

        acc_ref[...] += part

        @pl.when(kk == nk - 1)
        def _():
            r = acc_ref[...]
            if add is not None:
                r = r + add_ref[...]
            o_ref[...] = r.astype(out_dtype)

    args = (a, b) if add is None else (a, b, add)
    specs = [a_spec, b_spec] if add is None else [a_spec, b_spec, o_spec]
    return pl.pallas_call(
        body, name=name, grid=(m // tm, n // tn, nk), in_specs=specs, out_specs=o_spec,
        out_shape=_sds((m, n), out_dtype), scratch_shapes=[] if nk == 1 else [pltpu.VMEM((tm, tn), F32)],
        compiler_params=_params("parallel", "parallel", "arbitrary"))(*args)


def _rms(x, g):
    return x * lax.rsqrt(jnp.mean(x * x, axis=-1, keepdims=True) + EPS) * g


def _rmsnorm_fwd(x, g, *, name):
    t, d = x.shape
    tb = _tile(t, 512, SUBLANES)

    def body(x_ref, g_ref, o_ref):
        o_ref[...] = _rms(x_ref[...], g_ref[...]).astype(o_ref.dtype)

    row = pl.BlockSpec((tb, d), lambda i: (i, 0))
    vec = pl.BlockSpec((1, d), lambda i: (0, 0))
    return pl.pallas_call(body, name=name, grid=(t // tb,), in_specs=[row, vec], out_specs=row,
                          out_shape=_sds((t, d), MM_DTYPE), compiler_params=_params("parallel"))(x, g.reshape(1, d))


def _rmsnorm_bwd(x, g, dh, res, *, name):
    t, d = x.shape
    tb = _tile(t, 256, SUBLANES)

    def body(x_ref, g_ref, dh_ref, res_ref, dx_ref, dg_ref):
        _, vjp = jax.vjp(_rms, x_ref[...], g_ref[...])
        dx, dg = vjp(dh_ref[...])
        dx_ref[...] = dx + res_ref[...]

        @pl.when(pl.program_id(0) == 0)
        def _():
            dg_ref[...] = jnp.zeros_like(dg_ref)

        dg_ref[...] += dg

    row = pl.BlockSpec((tb, d), lambda i: (i, 0))
    vec = pl.BlockSpec((1, d), lambda i: (0, 0))
    dx, dg = pl.pallas_call(body, name=name, grid=(t // tb,), in_specs=[row, vec, row, row], out_specs=[row, vec],
                            out_shape=[_sds((t, d)), _sds((1, d))], compiler_params=_params("arbitrary"))(
                                x, g.reshape(1, d), dh, res)
    return dx, dg.reshape(d)


def _loss_head(x, g, tgt, *, name):
    t, d = x.shape
    tb = _tile(t, 256, SUBLANES)

    def body(x_ref, g_ref, t_ref, loss_ref, dx_ref, dg_ref):
        def f(xv, gv):
            e = _rms(xv, gv) - t_ref[...]
            return jnp.sum(jnp.sum(e * e, axis=-1, keepdims=True), axis=0, keepdims=True) * (0.5 / d)

        l, vjp = jax.vjp(f, x_ref[...], g_ref[...])
        dx, dg = vjp(jnp.ones((1, 1), F32))
        dx_ref[...] = dx

        @pl.when(pl.program_id(0) == 0)
        def _():
            dg_ref[...] = jnp.zeros_like(dg_ref)
            loss_ref[...] = jnp.zeros_like(loss_ref)

        dg_ref[...] += dg
        loss_ref[...] += jnp.zeros(loss_ref.shape, F32) + l

    row = pl.BlockSpec((tb, d), lambda i: (i, 0))
    vec = pl.BlockSpec((1, d), lambda i: (0, 0))
    lspec = pl.BlockSpec((SUBLANES, LANES), lambda i: (0, 0))
    loss, dx, dg = pl.pallas_call(
        body, name=name, grid=(t // tb,), in_specs=[row, vec, row], out_specs=[lspec, row, vec],
        out_shape=[_sds((SUBLANES, LANES)), _sds((t, d)), _sds((1, d))], compiler_params=_params("arbitrary"))(
            x, g.reshape(1, d), tgt)
    return loss[0, 0], dx, dg.reshape(d)


def _delayed(x, prev, j):
    if j == 0:
        return x
    sh = pltpu.roll(x, j, axis=0)
    row = lax.broadcasted_iota(jnp.int32, prev.shape, 0)
    top = jnp.where(row < j, pltpu.roll(prev, j, axis=0), sh[0:SUBLANES])
    return jnp.concatenate([top, sh[SUBLANES:]], axis=0)


def _advanced(x, nxt, j):
    if j == 0:
        return x
    tb = x.shape[0]
    sh = pltpu.roll(x, tb - j, axis=0)
    row = lax.broadcasted_iota(jnp.int32, nxt.shape, 0)
    bot = jnp.where(row + j < SUBLANES, sh[tb - SUBLANES:], pltpu.roll(nxt, SUBLANES - j, axis=0))
    return jnp.concatenate([sh[:tb - SUBLANES], bot], axis=0)


def _conv_tiles(t, ncols, coff):
    tc = _tile(ncols, 512, LANES)
    assert coff % tc == 0
    tb = _tile(t, 512, SUBLANES)
    return tc, tb, coff // tc


def _conv_fwd(x, w, b, *, ncols, coff=0, name):
    t = x.shape[0]
    kw = w.shape[0]
    tc, tb, cb = _conv_tiles(t, ncols, coff)
    n8 = tb // SUBLANES

    def body(x_ref, p_ref, w_ref, b_ref, o_ref):
        xv = x_ref[...]
        prev = jnp.where(pl.program_id(1) > 0, p_ref[...], 0.0)
        acc = jnp.zeros_like(xv) + b_ref[...]
        for k in range(kw):
            acc = acc + w_ref[k:k + 1, :] * _delayed(xv, prev, kw - 1 - k)
        o_ref[...] = acc

    in_specs = [
        pl.BlockSpec((tb, tc), lambda c, i: (i, c + cb)),
        pl.BlockSpec((SUBLANES, tc), lambda c, i: (jnp.maximum(i * n8 - 1, 0), c + cb)),
        pl.BlockSpec((kw, tc), lambda c, i: (0, c)),
        pl.BlockSpec((1, tc), lambda c, i: (0, c)),
    ]
    return pl.pallas_call(
        body, name=name, grid=(ncols // tc, t // tb), in_specs=in_specs,
        out_specs=pl.BlockSpec((tb, tc), lambda c, i: (i, c)), out_shape=_sds((t, ncols)),
        compiler_params=_params("parallel", "parallel"))(x, x, w, b.reshape(1, ncols))


def _conv_bwd(x, w, dy, *, ncols, coff=0, name):
    t = x.shape[0]
    kw = w.shape[0]
    tc, tb, cb = _conv_tiles(t, ncols, coff)
    n8 = tb // SUBLANES
    nt = t // tb

    def body(x_ref, p_ref, dy_ref, n_ref, w_ref, dx_ref, dw_ref, db_ref):
        i = pl.program_id(1)
        xv, dyv = x_ref[...], dy_ref[...]
        prev = jnp.where(i > 0, p_ref[...], 0.0)
        nxt = jnp.where(i < nt - 1, n_ref[...], 0.0)

        @pl.when(i == 0)
        def _():
            dw_ref[...] = jnp.zeros_like(dw_ref)
            db_ref[...] = jnp.zeros_like(db_ref)

        dx = jnp.zeros_like(dyv)
        for k in range(kw):
            j = kw - 1 - k
            dx = dx + w_ref[k:k + 1, :] * _advanced(dyv, nxt, j)
            dw_ref[k:k + 1, :] += jnp.sum(dyv * _delayed(xv, prev, j), axis=0, keepdims=True)
        dx_ref[...] = dx.astype(dx_ref.dtype)
        db_ref[...] += jnp.sum(dyv, axis=0, keepdims=True)

    in_specs = [
        pl.BlockSpec((tb, tc), lambda c, i: (i, c + cb)),
        pl.BlockSpec((SUBLANES, tc), lambda c, i: (jnp.maximum(i * n8 - 1, 0), c + cb)),
        pl.BlockSpec((tb, tc), lambda c, i: (i, c)),
        pl.BlockSpec((SUBLANES, tc), lambda c, i: (jnp.minimum((i + 1) * n8, nt * n8 - 1), c)),
        pl.BlockSpec((kw, tc), lambda c, i: (0, c)),
    ]
    out_specs = [
        pl.BlockSpec((tb, tc), lambda c, i: (i, c)),
        pl.BlockSpec((kw, tc), lambda c, i: (0, c)),
        pl.BlockSpec((1, tc), lambda c, i: (0, c)),
    ]
    dx, dw, db = pl.pallas_call(
        body, name=name, grid=(ncols // tc, nt), in_specs=in_specs, out_specs=out_specs,
        out_shape=[_sds((t, ncols), MM_DTYPE), _sds((kw, ncols)), _sds((1, ncols))],
        compiler_params=_params("parallel", "arbitrary"))(x, x, dy, dy, w)
    return dx, dw, db.reshape(ncols)


def _scan(a, u, *, reverse=False, name):
    t, c = u.shape
    tc = _tile(c, 512, LANES)
    tb = _tile(t, 256, SUBLANES)
    nt = t // tb

    def body(*refs):
        if a is None:
            u_ref, o_ref, carry_ref = refs
        else:
            a_ref, u_ref, o_ref, carry_ref = refs

        @pl.when(pl.program_id(1) == 0)
        def _():
            carry_ref[...] = jnp.zeros_like(carry_ref)

        hv = u_ref[...]
        av = None if a is None else a_ref[...]
        row = lax.broadcasted_iota(jnp.int32, hv.shape, 0)
        s = 1
        while s < tb:
            live = row < tb - s if reverse else row >= s
            shift = tb - s if reverse else s
            h_sh = jnp.where(live, pltpu.roll(hv, shift, axis=0), 0.0)
            if av is None:
                hv = hv + h_sh
            else:
                hv = av * h_sh + hv
                av = av * jnp.where(live, pltpu.roll(av, shift, axis=0), 1.0)
            s *= 2
        hv = hv + carry_ref[...] if av is None else hv + av * carry_ref[...]
        o_ref[...] = hv
        carry_ref[...] = o_ref[pl.ds(0 if reverse else tb - 1, 1), :]

    spec = pl.BlockSpec((tb, tc), (lambda cc, i: (nt - 1 - i, cc)) if reverse else (lambda cc, i: (i, cc)))
    args, specs = ((u,), [spec]) if a is None else ((a, u), [spec, spec])
    return pl.pallas_call(
        body, name=name, grid=(c // tc, t // tb), in_specs=specs, out_specs=spec, out_shape=_sds((t, c)),
        scratch_shapes=[pltpu.VMEM((1, tc), F32)], compiler_params=_params("parallel", "arbitrary"))(*args)


def _shift_down(x):
    return lax.pad(x, jnp.zeros((), x.dtype), ((1, -1, 0), (0, 0, 0)))


def _shift_up(x):
    return lax.pad(x, jnp.zeros((), x.dtype), ((-1, 1, 0), (0, 0, 0)))


def _neg_expm1(y):
    small = -(y * (1.0 + y * (0.5 + y * (1.0 / 6.0 + y * (1.0 / 24.0 + y * (1.0 / 120.0))))))
    return jnp.where(y > -0.05, small, 1.0 - jnp.exp(y))


def _lru_gates(xc, wa, ba, wx, bx, lam):
    r = jax.nn.sigmoid(_mm(xc, wa) + ba)
    i = jax.nn.sigmoid(_mm(xc, wx) + bx)
    log_a = -LRU_C * r * jax.nn.softplus(-lam)
    a = jnp.exp(log_a)
    u = jnp.sqrt(_neg_expm1(2.0 * log_a)) * (i * xc)
    return a, u


def _lru_specs(t):
    tb = _tile(t, 256, SUBLANES)
    row = pl.BlockSpec((tb, GROUP), lambda i: (i, 0))
    mat = pl.BlockSpec((GROUP, GROUP), lambda i: (0, 0))
    vec = pl.BlockSpec((1, GROUP), lambda i: (0, 0))
    return tb, row, mat, vec


def _lru_gates_fwd(xc, wa, ba, wx, bx, lam, *, name):
    t = xc.shape[0]
    tb, row, mat, vec = _lru_specs(t)

    def body(xc_ref, wa_ref, ba_ref, wx_ref, bx_ref, lam_ref, a_ref, u_ref):
        a, u = _lru_gates(xc_ref[...], wa_ref[...], ba_ref[...], wx_ref[...], bx_ref[...], lam_ref[...])
        a_ref[...] = a
        u_ref[...] = u

    return pl.pallas_call(
        body, name=name, grid=(t // tb,), in_specs=[row, mat, vec, mat, vec, vec], out_specs=[row, row],
        out_shape=[_sds((t, GROUP)), _sds((t, GROUP))], compiler_params=_params("parallel"))(xc, wa, ba, wx, bx, lam)


def _lru_gates_bwd(xc, wa, ba, wx, bx, lam, g, h_prev, *, name):
    t = xc.shape[0]
    tb, row, mat, vec = _lru_specs(t)

    def body(xc_ref, wa_ref, ba_ref, wx_ref, bx_ref, lam_ref, g_ref, hp_ref,
             dxc_ref, dwa_ref, dba_ref, dwx_ref, dbx_ref, dlam_ref):
        _, vjp = jax.vjp(_lru_gates, xc_ref[...], wa_ref[...], ba_ref[...], wx_ref[...], bx_ref[...], lam_ref[...])
        gv = g_ref[...]
        dxc, dwa, dba, dwx, dbx, dlam = vjp((gv * hp_ref[...], gv))
        dxc_ref[...] = dxc
        accs = (dwa_ref, dba_ref, dwx_ref, dbx_ref, dlam_ref)

        @pl.when(pl.program_id(0) == 0)
        def _():
            for r in accs:
                r[...] = jnp.zeros_like(r)

        for r, v in zip(accs, (dwa, dba, dwx, dbx, dlam)):
            r[...] += v

    return pl.pallas_call(
        body, name=name, grid=(t // tb,), in_specs=[row, mat, vec, mat, vec, vec, row, row],
        out_specs=[row, mat, vec, mat, vec, vec],
        out_shape=[_sds((t, GROUP)), _sds((GROUP, GROUP)), _sds((1, GROUP)), _sds((GROUP, GROUP)), _sds((1, GROUP)),
                   _sds((1, GROUP))],
        compiler_params=_params("arbitrary"))(xc, wa, ba, wx, bx, lam, g, h_prev)


def _block_diag(w):
    eye = jnp.eye(LRU_BLOCKS, dtype=w.dtype)
    return (eye[:, None, :, None] * w[:, :, None, :]).reshape(GROUP, GROUP)


def _diag_blocks(m):
    m4 = m.reshape(LRU_BLOCKS, LRU_BD, LRU_BLOCKS, LRU_BD)
    return jnp.stack([m4[n, :, n, :] for n in range(LRU_BLOCKS)])


def _post(hg, gain, gate):
    y = hg * lax.rsqrt(jnp.mean(hg * hg, axis=-1, keepdims=True) + EPS) * gain
    return y if gate is None else y * jax.nn.gelu(gate)


def _post_specs(rows, g):
    rb = _tile(rows, 2048, SUBLANES)
    return rb, pl.BlockSpec((rb, g), lambda i: (i, 0)), pl.BlockSpec((rb, g), lambda i: (0, 0))


def _post_fwd(hg, gain8, gate, *, name):
    rows, g = hg.shape
    rb, row, fixed = _post_specs(rows, g)
    gain_t = jnp.tile(gain8, (rb // SUBLANES, 1))

    def body(*refs):
        if gate is None:
            h_ref, gn_ref, o_ref = refs
            o_ref[...] = _post(h_ref[...], gn_ref[...], None).astype(o_ref.dtype)
        else:
            h_ref, gn_ref, gt_ref, o_ref = refs
            o_ref[...] = _post(h_ref[...], gn_ref[...], gt_ref[...]).astype(o_ref.dtype)

    args, specs = ((hg, gain_t), [row, fixed]) if gate is None else ((hg, gain_t, gate), [row, fixed, row])
    return pl.pallas_call(body, name=name, grid=(rows // rb,), in_specs=specs, out_specs=row,
                          out_shape=_sds((rows, g), MM_DTYPE), compiler_params=_params("parallel"))(*args)


def _post_bwd(hg, gain8, gate, dy, *, name):
    rows, g = hg.shape
    rb, row, fixed = _post_specs(rows, g)
    gain_t = jnp.tile(gain8, (rb // SUBLANES, 1))
    g8 = pl.BlockSpec((SUBLANES, g), lambda i: (0, 0))

    def body(*refs):
        if gate is None:
            h_ref, gn_ref, dy_ref, dh_ref, dgn_ref = refs
            _, vjp = jax.vjp(lambda h, gn: _post(h, gn, None), h_ref[...], gn_ref[...])
            dh, dgn = vjp(dy_ref[...])
        else:
            h_ref, gn_ref, gt_ref, dy_ref, dh_ref, dgn_ref, dgt_ref = refs
            _, vjp = jax.vjp(_post, h_ref[...], gn_ref[...], gt_ref[...])
            dh, dgn, dgt = vjp(dy_ref[...])
            dgt_ref[...] = dgt.astype(dgt_ref.dtype)
        dh_ref[...] = dh

        @pl.when(pl.program_id(0) == 0)
        def _():
            dgn_ref[...] = jnp.zeros_like(dgn_ref)

        dgn_ref[...] += dgn.reshape(rb // SUBLANES, SUBLANES, g).sum(axis=0)

    if gate is None:
        dh, dgn = pl.pallas_call(
            body, name=name, grid=(rows // rb,), in_specs=[row, fixed, row], out_specs=[row, g8],
            out_shape=[_sds((rows, g)), _sds((SUBLANES, g))], compiler_params=_params("arbitrary"))(hg, gain_t, dy)
        return dh, dgn, None
    dh, dgn, dgt = pl.pallas_call(
        body, name=name, grid=(rows // rb,), in_specs=[row, fixed, row, row], out_specs=[row, g8, row],
        out_shape=[_sds((rows, g)), _sds((SUBLANES, g)), _sds((rows, g), MM_DTYPE)],
        compiler_params=_params("arbitrary"))(hg, gain_t, gate, dy)
    return dh, dgn, dgt


def _gates(zg, fb, alog, dtb):
    lane = lax.broadcasted_iota(jnp.int32, zg.shape, 1)
    logf = jax.nn.log_sigmoid(zg + fb)
    beta = jax.nn.sigmoid(zg)
    gdec = -jnp.exp(alog) * jax.nn.softplus(zg + dtb)
    return jnp.where(lane < G_BETA, logf, jnp.where(lane < G_ALPHA, beta, jnp.where(lane < G_ALPHA + 4, gdec, 0.0)))


def _lane_row(v, off):
    return jnp.pad(v.reshape(1, N_HEADS), ((0, 0), (off, LANES - N_HEADS - off)))


def _gates_fwd(z, fb, alog, dtb, *, name):
    t = z.shape[0]
    tb = _tile(t, 1024, SUBLANES)
    zspec = pl.BlockSpec((tb, LANES), lambda i: (i, Z_GATES // LANES))
    row = pl.BlockSpec((tb, LANES), lambda i: (i, 0))
    vec = pl.BlockSpec((1, LANES), lambda i: (0, 0))

    def body(z_ref, fb_ref, al_ref, dt_ref, o_ref):
        o_ref[...] = _gates(z_ref[...], fb_ref[...], al_ref[...], dt_ref[...])

    return pl.pallas_call(body, name=name, grid=(t // tb,), in_specs=[zspec, vec, vec, vec], out_specs=row,
                          out_shape=_sds((t, LANES)), compiler_params=_params("parallel"))(z, fb, alog, dtb)


def _gates_bwd(z, fb, alog, dtb, dg, *, name):
    t = z.shape[0]
    tb = _tile(t, 1024, SUBLANES)
    zspec = pl.BlockSpec((tb, LANES), lambda i: (i, Z_GATES // LANES))
    row = pl.BlockSpec((tb, LANES), lambda i: (i, 0))
    vec = pl.BlockSpec((1, LANES), lambda i: (0, 0))

    def body(z_ref, fb_ref, al_ref, dt_ref, dg_ref, dz_ref, dfb_ref, dal_ref, ddt_ref):
        _, vjp = jax.vjp(_gates, z_ref[...], fb_ref[...], al_ref[...], dt_ref[...])
        dz, dfb, dal, ddt = vjp(dg_ref[...])
        dz_ref[...] = dz.astype(dz_ref.dtype)
        accs = (dfb_ref, dal_ref, ddt_ref)

        @pl.when(pl.program_id(0) == 0)
        def _():
            for r in accs:
                r[...] = jnp.zeros_like(r)

        for r, v in zip(accs, (dfb, dal, ddt)):
            r[...] += v

    return pl.pallas_call(
        body, name=name, grid=(t // tb,), in_specs=[zspec, vec, vec, vec, row], out_specs=[row, vec, vec, vec],
        out_shape=[_sds((t, LANES), MM_DTYPE), _sds((1, LANES)), _sds((1, LANES)), _sds((1, LANES))],
        compiler_params=_params("arbitrary"))(z, fb, alog, dtb, dg)


def _pair_weights(tq, band, transposed):
    d = np.arange(tq)[:, None] - np.arange(tq)[None, :]
    d = (d.T if transposed else d)[None] + (np.arange(band + 1) * tq)[:, None, None]
    w = sum(((d >= 0) & (d <= win) & (d % dil == 0)).astype(np.float32) for win, dil in DILATED_PAIRS)
    return jnp.asarray(w, F32)


def _att_geometry(t, mode):
    tq = _tile(t, ATT_TILE, LANES)
    nq = t // tq
    band = nq - 1 if mode == "fox" else min(DILATED_PAIRS[-1][0] // tq, nq - 1)
    return tq, nq, band


def _qkv_prep(z, *, off, name):
    t = z.shape[0]
    tb = _tile(t, 512, SUBLANES)
    cb = off // GROUP

    def body(z_ref, o_ref):
        scale = jnp.where(pl.program_id(1) == 0, ATT_SCALE, 1.0)
        o_ref[...] = (z_ref[...] * scale).astype(o_ref.dtype)

    return pl.pallas_call(
        body, name=name, grid=(t // tb, 3), in_specs=[pl.BlockSpec((tb, GROUP), lambda i, j: (i, j + cb))],
        out_specs=pl.BlockSpec((tb, GROUP), lambda i, j: (i, j)), out_shape=_sds((t, 3 * GROUP), MM_DTYPE),
        compiler_params=_params("parallel", "parallel"))(z)


def _block(ref, j, tq):
    return ref[pl.ds(pl.multiple_of(j * tq, tq), tq), :]


def _lane_block(ref, j, tq):
    return ref[0, :, pl.ds(pl.multiple_of(j * tq, tq), tq)]


def _att_tile(mode, q, kj, cq, ckj, causal, w):
    s = _dg(q, kj, 1, 1, False)
    if mode == "fox":
        s = s + (cq - ckj)
        return s if causal is None else jnp.where(causal, s, NEG)
    return jnp.where(w > 0.0, s, NEG)


def _att_fwd(qkv, cq, ck, *, mode, name):
    t = qkv.shape[0]
    tq, nq, band = _att_geometry(t, mode)
    weights = _pair_weights(tq, band if mode == "dil" else 0, False)

    def body(q_ref, k_ref, v_ref, cq_ref, ck_ref, w_ref, o_ref, lse_ref):
        i = pl.program_id(1)
        q, cqv = q_ref[...], cq_ref[0]
        causal = (lax.broadcasted_iota(jnp.int32, (tq, tq), 0) >= lax.broadcasted_iota(jnp.int32, (tq, tq), 1))

        def step(j, carry, masked):
            m_old, l_old, acc = carry
            w = w_ref[i - j] if mode == "dil" else None
            s = _att_tile(mode, q, _block(k_ref, j, tq), cqv, _lane_block(ck_ref, j, tq), causal if masked else None, w)
            m_new = jnp.maximum(m_old, jnp.max(s, axis=-1, keepdims=True))
            alpha = jnp.exp(m_old - m_new)
            p = jnp.exp(s - m_new)
            if w is not None:
                p = p * w
            l_new = alpha * l_old + jnp.sum(p, axis=-1, keepdims=True)
            return m_new, l_new, alpha * acc + _dg(p, _block(v_ref, j, tq), 1, 0, False)

        carry = (jnp.full((tq, 1), NEG, F32), jnp.zeros((tq, 1), F32), jnp.zeros((tq, HEAD), F32))
        if mode == "fox":
            carry = lax.fori_loop(0, i, lambda j, c: step(j, c, False), carry)
            carry = step(i, carry, True)
        else:
            carry = lax.fori_loop(jnp.maximum(i - band, 0), i + 1, lambda j, c: step(j, c, True), carry)
        m_fin, l_fin, acc = carry
        o_ref[...] = acc / l_fin
        lse_ref[0] = m_fin + jnp.log(l_fin)

    col = pl.BlockSpec((1, tq, 1), lambda h, i: (h, i, 0))
    in_specs = [
        pl.BlockSpec((tq, HEAD), lambda h, i: (i, h)),
        pl.BlockSpec((t, HEAD), lambda h, i: (0, N_HEADS + h)),
        pl.BlockSpec((t, HEAD), lambda h, i: (0, 2 * N_HEADS + h)),
        col,
        pl.BlockSpec((1, 1, t), lambda h, i: (h, 0, 0)),
        pl.BlockSpec(weights.shape, lambda h, i: (0, 0, 0)),
    ]
    return pl.pallas_call(
        body, name=name, grid=(N_HEADS, nq), in_specs=in_specs,
        out_specs=[pl.BlockSpec((tq, HEAD), lambda h, i: (i, h)), col],
        out_shape=[_sds((t, GROUP)), _sds((N_HEADS, t, 1))],
        compiler_params=_params("parallel", "parallel"))(qkv, qkv, qkv, cq, ck, weights)


def _att_bwd_q(qkv, cq, ck, do, o, lse, *, mode, name):
    t = qkv.shape[0]
    tq, nq, band = _att_geometry(t, mode)
    weights = _pair_weights(tq, band if mode == "dil" else 0, False)

    def body(q_ref, k_ref, v_ref, cq_ref, ck_ref, w_ref, do_ref, o_ref, lse_ref, dq_ref, dcq_ref, delta_ref):
        i = pl.program_id(1)
        q, cqv, lse = q_ref[...], cq_ref[0], lse_ref[0]
        dov = do_ref[...]
        delta = jnp.sum(dov * o_ref[...], axis=-1, keepdims=True)
        do16 = dov.astype(MM_DTYPE)
        causal = (lax.broadcasted_iota(jnp.int32, (tq, tq), 0) >= lax.broadcasted_iota(jnp.int32, (tq, tq), 1))

        def step(j, carry, masked):
            dq, dcq = carry
            kj = _block(k_ref, j, tq)
            w = w_ref[i - j] if mode == "dil" else None
            s = _att_tile(mode, q, kj, cqv, _lane_block(ck_ref, j, tq), causal if masked else None, w)
            p = jnp.exp(s - lse)
            if w is not None:
                p = p * w
            ds = p * (_dg(do16, _block(v_ref, j, tq), 1, 1, False) - delta)
            return dq + _dg(ds, kj, 1, 0, False), dcq + jnp.sum(ds, axis=-1, keepdims=True)

        carry = (jnp.zeros((tq, HEAD), F32), jnp.zeros((tq, 1), F32))
        if mode == "fox":
            carry = lax.fori_loop(0, i, lambda j, c: step(j, c, False), carry)
            carry = step(i, carry, True)
        else:
            carry = lax.fori_loop(jnp.maximum(i - band, 0), i + 1, lambda j, c: step(j, c, True), carry)
        dq_ref[...] = (carry[0] * ATT_SCALE).astype(dq_ref.dtype)
        dcq_ref[0] = carry[1]
        delta_ref[0] = delta

    col = pl.BlockSpec((1, tq, 1), lambda h, i: (h, i, 0))
    row = pl.BlockSpec((tq, HEAD), lambda h, i: (i, h))
    in_specs = [
        row,
        pl.BlockSpec((t, HEAD), lambda h, i: (0, N_HEADS + h)),
        pl.BlockSpec((t, HEAD), lambda h, i: (0, 2 * N_HEADS + h)),
        col,
        pl.BlockSpec((1, 1, t), lambda h, i: (h, 0, 0)),
        pl.BlockSpec(weights.shape, lambda h, i: (0, 0, 0)),
        row, row, col,
    ]
    return pl.pallas_call(
        body, name=name, grid=(N_HEADS, nq), in_specs=in_specs, out_specs=[row, col, col],
        out_shape=[_sds((t, GROUP), MM_DTYPE), _sds((N_HEADS, t, 1)), _sds((N_HEADS, t, 1))],
        compiler_params=_params("parallel", "parallel"))(qkv, qkv, qkv, cq, ck, weights, do, o, lse)


def _att_bwd_kv(qkv, cq, ck, do, lse_row, delta_row, *, mode, name):
    t = qkv.shape[0]
    tq, nq, band = _att_geometry(t, mode)
    weights = _pair_weights(tq, band if mode == "dil" else 0, True)

    def body(q_ref, k_ref, v_ref, cq_ref, ck_ref, w_ref, do_ref, lse_ref, delta_ref, dk_ref, dv_ref, dck_ref):
        jk = pl.program_id(1)
        kj, vj, ckv = k_ref[...], v_ref[...], cq_ref[0]
        causal = (lax.broadcasted_iota(jnp.int32, (tq, tq), 1) >= lax.broadcasted_iota(jnp.int32, (tq, tq), 0))

        def step(qi, carry, masked):
            dk, dv, dck = carry
            qb = _block(q_ref, qi, tq)
            s = _dg(kj, qb, 1, 1, False)
            w = None
            if mode == "fox":
                s = s + (_lane_block(ck_ref, qi, tq) - ckv)
                if masked:
                    s = jnp.where(causal, s, NEG)
            else:
                w = w_ref[qi - jk]
                s = jnp.where(w > 0.0, s, NEG)
            p = jnp.exp(s - _lane_block(lse_ref, qi, tq))
            if w is not None:
                p = p * w
            do16 = _block(do_ref, qi, tq).astype(MM_DTYPE)
            ds = p * (_dg(vj, do16, 1, 1, False) - _lane_block(delta_ref, qi, tq))
            return (dk + _dg(ds, qb, 1, 0, False), dv + _dg(p, do16, 1, 0, False),
                    dck - jnp.sum(ds, axis=-1, keepdims=True))

        carry = (jnp.zeros((tq, HEAD), F32), jnp.zeros((tq, HEAD), F32), jnp.zeros((tq, 1), F32))
        if mode == "fox":
            carry = step(jk, carry, True)
            carry = lax.fori_loop(jk + 1, nq, lambda qi, c: step(qi, c, False), carry)
        else:
            carry = lax.fori_loop(jk, jnp.minimum(jk + band, nq - 1) + 1, lambda qi, c: step(qi, c, True), carry)
        dk_ref[...] = carry[0].astype(dk_ref.dtype)
        dv_ref[...] = carry[1].astype(dv_ref.dtype)
        dck_ref[0] = carry[2]

    col = pl.BlockSpec((1, tq, 1), lambda h, j: (h, j, 0))
    lanes = pl.BlockSpec((1, 1, t), lambda h, j: (h, 0, 0))
    in_specs = [
        pl.BlockSpec((t, HEAD), lambda h, j: (0, h)),
        pl.BlockSpec((tq, HEAD), lambda h, j: (j, N_HEADS + h)),
        pl.BlockSpec((tq, HEAD), lambda h, j: (j, 2 * N_HEADS + h)),
        col, lanes,
        pl.BlockSpec(weights.shape, lambda h, j: (0, 0, 0)),
        pl.BlockSpec((t, HEAD), lambda h, j: (0, h)),
        lanes, lanes,
    ]
    out = pl.BlockSpec((tq, HEAD), lambda h, j: (j, h))
    return pl.pallas_call(
        body, name=name, grid=(N_HEADS, nq), in_specs=in_specs, out_specs=[out, out, col],
        out_shape=[_sds((t, GROUP), MM_DTYPE), _sds((t, GROUP), MM_DTYPE), _sds((N_HEADS, t, 1))],
        compiler_params=_params("parallel", "parallel"))(qkv, qkv, qkv, cq, ck, weights, do, lse_row, delta_row)


def _silu(x):
    return x * jax.nn.sigmoid(x)


def _l2n(x):
    return x * lax.rsqrt(jnp.sum(x * x, axis=-1, keepdims=True) + EPS)


def _gdn_pre(xqkv, gts):
    rows, c = xqkv.shape[0], GDN_CHUNK
    nb = rows // c
    ri = lax.broadcasted_iota(jnp.int32, (1, c, c), 1)
    ci = lax.broadcasted_iota(jnp.int32, (1, c, c), 2)
    tril, strict, eye = ri >= ci, ri > ci, ri == ci
    eyef = eye.astype(F32)
    last = lax.broadcasted_iota(jnp.int32, (1, c, 1), 1) == c - 1
    lane = lax.broadcasted_iota(jnp.int32, gts.shape, 1)
    to3 = lambda a: a.reshape(nb, c, a.shape[-1])
    to2 = lambda a: a.reshape(rows, a.shape[-1])
    bmm = lambda a, b: _mmf(a, b, "nn", False, True)
    bmm_nt = lambda a, b: _mmf(a, b, "nt", False, True)
    gcs = to2(_mmf(jnp.broadcast_to(tril.astype(F32), (nb, c, c)), to3(gts), "nn", True, True))
    us, ws, qgs, kds, qks, egls = [], [], [], [], [], []
    for h in range(N_HEADS):
        q = to3(_l2n(_silu(xqkv[:, h * HEAD:(h + 1) * HEAD])) * ATT_SCALE)
        k = to3(_l2n(_silu(xqkv[:, GROUP + h * HEAD:GROUP + (h + 1) * HEAD])))
        v = to3(_silu(xqkv[:, 2 * GROUP + h * HEAD:2 * GROUP + (h + 1) * HEAD]))
        beta = to3(jnp.sum(jnp.where(lane == G_BETA + h, gts, 0.0), axis=-1, keepdims=True))
        gc = to3(jnp.sum(jnp.where(lane == G_ALPHA + h, gcs, 0.0), axis=-1, keepdims=True))
        gr = jnp.sum(jnp.where(eye, jnp.broadcast_to(gc, (nb, c, c)), 0.0), axis=1, keepdims=True)
        decay = jnp.where(tril, jnp.exp(jnp.where(tril, gc - gr, 0.0)), 0.0)
        kbeta, vbeta = k * beta, v * beta
        low = jnp.where(strict, bmm_nt(kbeta, k) * decay, 0.0)
        inv, pw = eyef - low, bmm(low, low)
        for step in range(5):
            inv = inv + bmm(inv, pw)
            if step < 4:
                pw = bmm(pw, pw)
        eg = jnp.exp(gc)
        g_last = jnp.sum(jnp.where(last, gc, 0.0), axis=1, keepdims=True)
        us.append(to2(bmm(inv, vbeta)))
        ws.append(to2(bmm(inv, kbeta * eg)))
        qgs.append(to2(q * eg))
        kds.append(to2(k * jnp.exp(g_last - gc)))
        qks.append(to2(jnp.where(tril, bmm_nt(q, k) * decay, 0.0)))
        egls.append(jnp.broadcast_to(jnp.exp(g_last), (nb, 1, LANES)))
    cat = lambda parts: jnp.concatenate(parts, axis=1)
    return cat(us), cat(ws), cat(qgs), cat(kds), jnp.stack(qks, axis=0), jnp.stack(egls, axis=1)


def _gdn_seq(states, u, w, qg, kd, qk, egl):
    states = list(states)
    outs = []
    for r in range(u.shape[0] // GDN_CHUNK):
        rs = slice(r * GDN_CHUNK, (r + 1) * GDN_CHUNK)
        heads = []
        for h in range(N_HEADS):
            hs = slice(h * HEAD, (h + 1) * HEAD)
            s_in = states[h]
            v_new = u[rs, hs] - _mm(w[rs, hs], s_in)
            heads.append(_mm(qg[rs, hs], s_in) + _mm(qk[h, rs], v_new))
            states[h] = s_in * egl[r, h] + _mm_tn(kd[rs, hs], v_new)
        outs.append(jnp.concatenate(heads, axis=1))
    return jnp.concatenate(outs, axis=0), tuple(states)


def _gdn_out(o, zg, ng):
    ys = []
    for h in range(N_HEADS):
        oh = o[:, h * HEAD:(h + 1) * HEAD]
        ys.append(oh * lax.rsqrt(jnp.mean(oh * oh, axis=-1, keepdims=True) + EPS) * ng * _silu(zg[:, h * HEAD:(h + 1) * HEAD]))
    return jnp.concatenate(ys, axis=1)


GDN_PRE_ROWS = 8 * GDN_CHUNK
GDN_SEQ_ROWS = 4 * GDN_CHUNK


def _gdn_pre_specs(t):
    rows = _tile(t, GDN_PRE_ROWS, GDN_CHUNK)
    nb = rows // GDN_CHUNK
    wide = pl.BlockSpec((rows, GROUP), lambda i: (i, 0))
    ins = [pl.BlockSpec((rows, 3 * GROUP), lambda i: (i, 0)), pl.BlockSpec((rows, LANES), lambda i: (i, 0))]
    mids = [wide, wide, wide, wide, pl.BlockSpec((N_HEADS, rows, GDN_CHUNK), lambda i: (0, i, 0)),
            pl.BlockSpec((nb, N_HEADS, 1, LANES), lambda i: (i, 0, 0, 0))]
    shapes = [_sds((t, GROUP))] * 4 + [_sds((N_HEADS, t, GDN_CHUNK)), _sds((t // GDN_CHUNK, N_HEADS, 1, LANES))]
    return rows, ins, mids, shapes


def _gdn_pre_fwd(xqkv, gts, *, name):
    t = xqkv.shape[0]
    rows, ins, mids, shapes = _gdn_pre_specs(t)

    def body(x_ref, g_ref, *out_refs):
        for ref, val in zip(out_refs, _gdn_pre(x_ref[...], g_ref[...])):
            ref[...] = val

    return pl.pallas_call(body, name=name, grid=(t // rows,), in_specs=ins, out_specs=mids, out_shape=shapes,
                          compiler_params=_params("parallel"))(xqkv, gts)


def _gdn_pre_bwd(xqkv, gts, dmids, *, name):
    t = xqkv.shape[0]
    rows, ins, mids, _ = _gdn_pre_specs(t)

    def body(x_ref, g_ref, *refs):
        _, vjp = jax.vjp(_gdn_pre, x_ref[...], g_ref[...])
        dx, dg = vjp(tuple(r[...] for r in refs[:6]))
        refs[6][...] = dx
        refs[7][...] = dg

    return pl.pallas_call(body, name=name, grid=(t // rows,), in_specs=ins + mids, out_specs=ins,
                          out_shape=[_sds((t, 3 * GROUP)), _sds((t, LANES))],
                          compiler_params=_params("parallel"))(xqkv, gts, *dmids)


def _gdn_seq_specs(t, rev):
    rows = _tile(t, GDN_SEQ_ROWS, GDN_CHUNK)
    nb, n = rows // GDN_CHUNK, t // rows
    at = (lambda i: n - 1 - i) if rev else (lambda i: i)
    wide = pl.BlockSpec((rows, GROUP), lambda i: (at(i), 0))
    mids = [wide, wide, wide, wide, pl.BlockSpec((N_HEADS, rows, GDN_CHUNK), lambda i: (0, at(i), 0)),
            pl.BlockSpec((nb, N_HEADS, 1, LANES), lambda i: (at(i), 0, 0, 0))]
    state = pl.BlockSpec((1, N_HEADS, HEAD, HEAD), lambda i: (at(i), 0, 0, 0))
    return rows, n, wide, mids, state


def _gdn_seq_fwd(mids_in, *, name):
    t = mids_in[0].shape[0]
    rows, n, wide, mids, state = _gdn_seq_specs(t, False)

    def body(u_ref, w_ref, qg_ref, kd_ref, qk_ref, egl_ref, o_ref, sv_ref, s_ref):
        @pl.when(pl.program_id(0) == 0)
        def _():
            s_ref[...] = jnp.zeros_like(s_ref)

        sv_ref[0] = s_ref[...]
        o, new = _gdn_seq(tuple(s_ref[h] for h in range(N_HEADS)), u_ref[...], w_ref[...], qg_ref[...], kd_ref[...],
                          qk_ref[...], egl_ref[...])
        o_ref[...] = o
        for h in range(N_HEADS):
            s_ref[h] = new[h]

    return pl.pallas_call(
        body, name=name, grid=(n,), in_specs=mids, out_specs=[wide, state],
        out_shape=[_sds((t, GROUP)), _sds((n, N_HEADS, HEAD, HEAD))],
        scratch_shapes=[pltpu.VMEM((N_HEADS, HEAD, HEAD), F32)], compiler_params=_params("arbitrary"))(*mids_in)


def _gdn_seq_bwd(mids_in, states, do, *, name):
    t = mids_in[0].shape[0]
    rows, n, wide, mids, state = _gdn_seq_specs(t, True)

    def body(u_ref, w_ref, qg_ref, kd_ref, qk_ref, egl_ref, sv_ref, do_ref, *refs):
        d_refs, ds_ref = refs[:6], refs[6]

        @pl.when(pl.program_id(0) == 0)
        def _():
            ds_ref[...] = jnp.zeros_like(ds_ref)

        s_in = tuple(sv_ref[0, h] for h in range(N_HEADS))
        _, vjp = jax.vjp(_gdn_seq, s_in, u_ref[...], w_ref[...], qg_ref[...], kd_ref[...], qk_ref[...], egl_ref[...])
        grads = vjp((do_ref[...], tuple(ds_ref[h] for h in range(N_HEADS))))
        for ref, val in zip(d_refs, grads[1:]):
            ref[...] = val
        for h in range(N_HEADS):
            ds_ref[h] = grads[0][h]

    shapes = [_sds(m.shape) for m in mids_in]
    return pl.pallas_call(
        body, name=name, grid=(n,), in_specs=mids + [state, wide], out_specs=mids, out_shape=shapes,
        scratch_shapes=[pltpu.VMEM((N_HEADS, HEAD, HEAD), F32)],
        compiler_params=_params("arbitrary"))(*mids_in, states, do)


def _gdn_out_specs(t):
    rows = _tile(t, 512, SUBLANES)
    wide = pl.BlockSpec((rows, GROUP), lambda i: (i, 0))
    return rows, wide, pl.BlockSpec((rows, GROUP), lambda i: (i, Z_CZ // GROUP)), pl.BlockSpec((1, HEAD), lambda i: (0, 0))


def _gdn_out_fwd(o, z, ng, *, name):
    t = o.shape[0]
    rows, wide, zspec, vec = _gdn_out_specs(t)

    def body(o_ref, z_ref, ng_ref, y_ref):
        y_ref[...] = _gdn_out(o_ref[...], z_ref[...], ng_ref[...]).astype(y_ref.dtype)

    return pl.pallas_call(body, name=name, grid=(t // rows,), in_specs=[wide, zspec, vec], out_specs=wide,
                          out_shape=_sds((t, GROUP), MM_DTYPE), compiler_params=_params("parallel"))(o, z, ng)


def _gdn_out_bwd(o, z, ng, dy, *, name):
    t = o.shape[0]
    rows, wide, zspec, vec = _gdn_out_specs(t)

    def body(o_ref, z_ref, ng_ref, dy_ref, do_ref, dz_ref, dng_ref):
        _, vjp = jax.vjp(_gdn_out, o_ref[...], z_ref[...], ng_ref[...])
        do, dz, dng = vjp(dy_ref[...])
        do_ref[...] = do
        dz_ref[...] = dz.astype(dz_ref.dtype)

        @pl.when(pl.program_id(0) == 0)
        def _():
            dng_ref[...] = jnp.zeros_like(dng_ref)

        dng_ref[...] += dng

    do, dz, dng = pl.pallas_call(
        body, name=name, grid=(t // rows,), in_specs=[wide, zspec, vec, wide], out_specs=[wide, wide, vec],
        out_shape=[_sds((t, GROUP)), _sds((t, GROUP), MM_DTYPE), _sds((1, HEAD))],
        compiler_params=_params("arbitrary"))(o, z, ng, dy)
    return do, dz, dng.reshape(HEAD)


def _swiglu(up, gate):
    return _silu(gate) * up


def _conv_value(x_ref, p_ref, w_ref, b_ref, first):
    xv = x_ref[...]
    prev = jnp.where(first, 0.0, p_ref[...])
    acc = jnp.zeros_like(xv) + b_ref[...]
    kw = w_ref.shape[0]
    for k in range(kw):
        acc = acc + w_ref[k:k + 1, :] * _delayed(xv, prev, kw - 1 - k)
    return acc


def _ffn_act_specs(t, two_f, kw, col_of):
    dff = two_f // 2
    tb, tc = _tile(t, 512, SUBLANES), _tile(dff, 512, LANES)
    nh, n8 = dff // tc, tb // SUBLANES
    specs = []
    for half in (0, 1):
        col = lambda j, half=half: col_of(j, nh) + half * nh
        specs += [pl.BlockSpec((tb, tc), lambda j, i, col=col: (i, col(j))),
                  pl.BlockSpec((SUBLANES, tc), lambda j, i, col=col: (jnp.maximum(i * n8 - 1, 0), col(j))),
                  pl.BlockSpec((kw, tc), lambda j, i, col=col: (0, col(j))),
                  pl.BlockSpec((1, tc), lambda j, i, col=col: (0, col(j)))]
    return tb, tc, nh, specs


def _ffn_act_fwd(uu, w, b, *, name):
    t, two_f = uu.shape
    tb, tc, nh, specs = _ffn_act_specs(t, two_f, w.shape[0], lambda j, nh: j)

    def body(xu, pu, wu, bu, xg, pg, wg, bg, o_ref):
        first = pl.program_id(1) == 0
        o_ref[...] = _swiglu(_conv_value(xu, pu, wu, bu, first), _conv_value(xg, pg, wg, bg, first)).astype(o_ref.dtype)

    b2 = b.reshape(1, two_f)
    return pl.pallas_call(body, name=name, grid=(nh, t // tb), in_specs=specs,
                          out_specs=pl.BlockSpec((tb, tc), lambda j, i: (i, j)), out_shape=_sds((t, two_f // 2), MM_DTYPE),
                          compiler_params=_params("parallel", "parallel"))(uu, uu, w, b2, uu, uu, w, b2)


def _ffn_act_bwd(uu, w, b, dact, *, name):
    t, two_f = uu.shape
    tb, tc, nh, specs = _ffn_act_specs(t, two_f, w.shape[0], lambda j, nh: j % nh)

    def body(xu, pu, wu, bu, xg, pg, wg, bg, da_ref, o_ref):
        first = pl.program_id(1) == 0
        _, vjp = jax.vjp(_swiglu, _conv_value(xu, pu, wu, bu, first), _conv_value(xg, pg, wg, bg, first))
        dup, dgate = vjp(da_ref[...])
        o_ref[...] = jnp.where(pl.program_id(0) < nh, dup, dgate)

    b2 = b.reshape(1, two_f)
    return pl.pallas_call(body, name=name, grid=(2 * nh, t // tb),
                          in_specs=specs + [pl.BlockSpec((tb, tc), lambda j, i: (i, j % nh))],
                          out_specs=pl.BlockSpec((tb, tc), lambda j, i: (i, j)), out_shape=_sds((t, two_f)),
                          compiler_params=_params("parallel", "parallel"))(uu, uu, w, b2, uu, uu, w, b2, dact)


def _sum_slots(parts, *, out_dtype=F32, name):
    if not isinstance(parts, (list, tuple)):
        parts = [parts[s] for s in range(parts.shape[0])]
    r = parts[0].shape[0]
    rb = _tile(r, 2048, 2 * SUBLANES)
    spec = pl.BlockSpec((rb, LANES), lambda i: (i, 0))

    def body(*refs):
        acc = refs[0][...].astype(F32)
        for ref in refs[1:-1]:
            acc = acc + ref[...].astype(F32)
        refs[-1][...] = acc.astype(out_dtype)

    return pl.pallas_call(body, name=name, grid=(r // rb,), in_specs=[spec] * len(parts), out_specs=spec,
                          out_shape=_sds((r, LANES), out_dtype), compiler_params=_params("parallel"))(*parts)


def _adamw(w, g, m, v, *, name):
    r, c = w.shape
    rb = _tile(r, 128, SUBLANES)
    spec = pl.BlockSpec((rb, c), lambda i: (i, 0))

    def body(w_ref, g_ref, m_ref, v_ref, d_ref, nm_ref, nv_ref):
        gv = g_ref[...]
        mn = ADAM_B1 * m_ref[...] + (1.0 - ADAM_B1) * gv
        vn = ADAM_B2 * v_ref[...] + (1.0 - ADAM_B2) * (gv * gv)
        m_hat = mn / (1.0 - ADAM_B1 ** ADAM_STEP)
        v_hat = vn / (1.0 - ADAM_B2 ** ADAM_STEP)
        d_ref[...] = -ADAM_LR * (m_hat / (jnp.sqrt(v_hat) + ADAM_EPS) + ADAM_WD * w_ref[...])
        nm_ref[...] = mn
        nv_ref[...] = vn

    return pl.pallas_call(body, name=name, grid=(r // rb,), in_specs=[spec] * 4, out_specs=[spec] * 3,
                          out_shape=[_sds((r, c))] * 3, compiler_params=_params("parallel"))(w, g, m, v)


def _position():
    return lax.axis_index("x"), lax.axis_index("y"), lax.axis_index("c")


def _chip_gather(shards, *, name):
    n = len(shards)

    def body(*refs):
        ins, outs = refs[:n], refs[n:2 * n]
        send_sems, recv_sems, local_sems = refs[2 * n:]
        x, y, c = _position()
        mine = 2 * x + y
        chips = [(1 - x, y), (x, 1 - y), (1 - x, 1 - y)]
        local = [pltpu.make_async_copy(ins[a], outs[a].at[mine], local_sems.at[a]) for a in range(n)]
        for cp in local:
            cp.start()
        sends = []
        for a in range(n):
            for r, (px, py) in enumerate(chips):
                sends.append(pltpu.make_async_remote_copy(
                    src_ref=ins[a], dst_ref=outs[a].at[mine], send_sem=send_sems.at[3 * a + r],
                    recv_sem=recv_sems.at[3 * a + r], device_id=(px, py, c), device_id_type=MESH_ID))
        for cp in sends:
            cp.start()
        for a in range(n):
            for r, (px, py) in enumerate(chips):
                pltpu.make_async_remote_copy(
                    src_ref=ins[a], dst_ref=outs[a].at[2 * px + py], send_sem=send_sems.at[3 * a + r],
                    recv_sem=recv_sems.at[3 * a + r], device_id=(px, py, c), device_id_type=MESH_ID).wait_recv()
        for cp in sends:
            cp.wait_send()
        for cp in local:
            cp.wait()

    return pl.pallas_call(
        body, name=name, in_specs=[ANY] * n, out_specs=[ANY] * n,
        out_shape=[_sds((4,) + s.shape, s.dtype) for s in shards],
        scratch_shapes=[pltpu.SemaphoreType.DMA((3 * n,)), pltpu.SemaphoreType.DMA((3 * n,)),
                        pltpu.SemaphoreType.DMA((n,))],
        compiler_params=pltpu.CompilerParams(has_side_effects=True))(*shards)


def _chip_gather_halves(shards, *, name):
    n = len(shards)

    def body(*refs):
        ins, outs = refs[:n], refs[n:2 * n]
        send_sems, recv_sems = refs[2 * n:]
        x, y, c = _position()
        chips = [(1 - x, y), (x, 1 - y), (1 - x, 1 - y)]

        def copy(a, r, chip_of_block, half, to, second):
            k = (3 * n if second else 0) + 3 * a + r
            px, py = chip_of_block
            src = outs[a].at[2 * px + py, half] if second else ins[a].at[half]
            return pltpu.make_async_remote_copy(
                src_ref=src, dst_ref=outs[a].at[2 * px + py, half], send_sem=send_sems.at[k], recv_sem=recv_sems.at[k],
                device_id=to, device_id_type=MESH_ID)

        first = [copy(a, r, (x, y), c, (px, py, c), False) for a in range(n) for r, (px, py) in enumerate(chips)]
        for cp in first:
            cp.start()
        passed = []
        for a in range(n):
            for r, chip in enumerate(chips):
                copy(a, r, chip, c, (x, y, c), False).wait_recv()
                passed.append(copy(a, r, chip, c, (x, y, 1 - c), True))
                passed[-1].start()
        for a in range(n):
            for r, chip in enumerate(chips):
                copy(a, r, chip, 1 - c, (x, y, c), True).wait_recv()
        for cp in first + passed:
            cp.wait_send()

    outs = pl.pallas_call(
        body, name=name, in_specs=[ANY] * n, out_specs=[ANY] * n,
        out_shape=[_sds((4,) + s.shape, s.dtype) for s in shards],
        scratch_shapes=[pltpu.SemaphoreType.DMA((6 * n,)), pltpu.SemaphoreType.DMA((6 * n,))],
        compiler_params=pltpu.CompilerParams(has_side_effects=True))(*shards)
    x, y, _ = _position()
    return [lax.dynamic_update_index_in_dim(o, s, 2 * x + y, 0) for o, s in zip(outs, shards)]


def _sibling_send(v, *, name):
    def body(v_ref, got_ref, send_sem, recv_sem):
        x, y, c = _position()
        cp = pltpu.make_async_remote_copy(src_ref=v_ref, dst_ref=got_ref, send_sem=send_sem, recv_sem=recv_sem,
                                          device_id=(x, y, 1 - c), device_id_type=MESH_ID)
        cp.start()
        cp.wait()

    return pl.pallas_call(
        body, name=name, in_specs=[ANY], out_specs=ANY, out_shape=_sds(v.shape, v.dtype),
        scratch_shapes=[pltpu.SemaphoreType.DMA, pltpu.SemaphoreType.DMA],
        compiler_params=pltpu.CompilerParams(has_side_effects=True))(v)


def _chip_exchange(v, *, name):
    def body(v_ref, got_ref, send_sems, recv_sems):
        x, y, c = _position()
        mine = 2 * x + y
        chips = [(1 - x, y), (x, 1 - y), (1 - x, 1 - y)]
        sends = [pltpu.make_async_remote_copy(
            src_ref=v_ref.at[2 * px + py], dst_ref=got_ref.at[mine], send_sem=send_sems.at[r],
            recv_sem=recv_sems.at[r], device_id=(px, py, c), device_id_type=MESH_ID)
            for r, (px, py) in enumerate(chips)]
        for cp in sends:
            cp.start()
        for r, (px, py) in enumerate(chips):
            pltpu.make_async_remote_copy(
                src_ref=v_ref.at[mine], dst_ref=got_ref.at[2 * px + py], send_sem=send_sems.at[r],
                recv_sem=recv_sems.at[r], device_id=(px, py, c), device_id_type=MESH_ID).wait_recv()
        for cp in sends:
            cp.wait_send()

    got = pl.pallas_call(
        body, name=name, in_specs=[ANY], out_specs=ANY, out_shape=_sds(v.shape, v.dtype),
        scratch_shapes=[pltpu.SemaphoreType.DMA((3,)), pltpu.SemaphoreType.DMA((3,))],
        compiler_params=pltpu.CompilerParams(has_side_effects=True))(v)
    x, y, _ = _position()
    mine = 2 * x + y
    return lax.dynamic_update_index_in_dim(got, lax.dynamic_index_in_dim(v, mine, 0, keepdims=False), mine, 0)


def _all_sum(v, *, name):
    r = v.shape[0]
    masks = [(mx, my, mc) for mx in (0, 1) for my in (0, 1) for mc in (0, 1)][1:]

    def body(v_ref, out_ref, slots, send_sems, recv_sems, local_sem):
        x, y, c = _position()
        me = 4 * x + 2 * y + c

        def peer(mask):
            return tuple(1 - p if bit else p for p, bit in zip((x, y, c), mask))

        local = pltpu.make_async_copy(v_ref, slots.at[me], local_sem)
        local.start()
        sends = [pltpu.make_async_remote_copy(
            src_ref=v_ref, dst_ref=slots.at[me], send_sem=send_sems.at[k], recv_sem=recv_sems.at[k],
            device_id=peer(mask), device_id_type=MESH_ID) for k, mask in enumerate(masks)]
        for cp in sends:
            cp.start()
        for k, mask in enumerate(masks):
            px, py, pc = peer(mask)
            pltpu.make_async_remote_copy(
                src_ref=v_ref, dst_ref=slots.at[4 * px + 2 * py + pc], send_sem=send_sems.at[k],
                recv_sem=recv_sems.at[k], device_id=(px, py, pc), device_id_type=MESH_ID).wait_recv()
        for cp in sends:
            cp.wait_send()
        local.wait()
        acc = slots[0]
        for s in range(1, 8):
            acc = acc + slots[s]
        out_ref[...] = acc

    vm = pl.BlockSpec(memory_space=pltpu.VMEM)
    return pl.pallas_call(
        body, name=name, in_specs=[vm], out_specs=vm, out_shape=_sds((r, LANES)),
        scratch_shapes=[pltpu.VMEM((8, r, LANES), F32), pltpu.SemaphoreType.DMA((7,)), pltpu.SemaphoreType.DMA((7,)),
                        pltpu.SemaphoreType.DMA],
        compiler_params=pltpu.CompilerParams(vmem_limit_bytes=VMEM_LIMIT_BYTES, has_side_effects=True))(v)


def _pack_rows(arrays, align=SUBLANES * LANES):
    flat = jnp.concatenate([a.reshape(-1) for a in arrays])
    n = flat.shape[0]
    pad = (-n) % align
    if pad:
        flat = jnp.concatenate([flat, jnp.zeros((pad,), flat.dtype)])
    return flat.reshape(-1, LANES), [a.shape for a in arrays]


def _unpack_rows(rows, shapes):
    flat = rows.reshape(-1)
    out, off = [], 0
    for s in shapes:
        n = int(np.prod(s))
        out.append(flat[off:off + n].reshape(s))
        off += n
    return out


def _pad_w_in(w):
    d = w.shape[0]
    return jnp.concatenate([w[:, 0:2560], w[:, 2564:4612], w[:, 4620:6156], w[:, 2560:2564], w[:, 4612:4620],
                            jnp.zeros((d, Z_COLS - IN_COLS), w.dtype)], axis=1)


def _unpad_w_in(g):
    return jnp.concatenate([g[:, 0:2560], g[:, 6144:6148], g[:, 2560:4608], g[:, 6148:6156], g[:, 4608:6144]], axis=1)


def _gate_rows(p):
    return (_lane_row(p["fox_f_bias"], G_F), _lane_row(p["gdn_a_log"], G_ALPHA), _lane_row(p["gdn_dt_bias"], G_ALPHA))


def _head_cols(c_rows, t):
    ct = c_rows[:, :N_HEADS].T
    return ct.reshape(N_HEADS, t, 1), ct.reshape(N_HEADS, 1, t)


def _layer_fwd(x, p):
    t = x.shape[0]
    s = {"x": x}
    s["h"] = _rmsnorm_fwd(x, p["norm_mix"], name="mix_norm")
    z = s["z"] = _matmul(s["h"], p["w_in"], name="in_proj")
    s["xc"] = _conv_fwd(z, p["lru_conv_w"], p["lru_conv_b"], ncols=GROUP, coff=Z_AX, name="lru_conv")
    lru = s["lru"] = (_block_diag(p["lru_wa"]).astype(MM_DTYPE), p["lru_ba"].reshape(1, GROUP),
                      _block_diag(p["lru_wx"]).astype(MM_DTYPE), p["lru_bx"].reshape(1, GROUP),
                      p["lru_lambda"].reshape(1, GROUP))
    s["a"], u = _lru_gates_fwd(s["xc"], *lru, name="lru_gates")
    s["ha"] = _scan(s["a"], u, name="lru_scan")
    s["gate_a"] = z[:, Z_AG:Z_AG + GROUP].reshape(t * LRU_BLOCKS, LRU_BD)
    y_a = _post_fwd(s["ha"].reshape(t * LRU_BLOCKS, LRU_BD), p["norm_a"].reshape(LRU_BLOCKS, LRU_BD), s["gate_a"],
                    name="lru_post")
    s["gts"] = _gates_fwd(z, *_gate_rows(p), name="gates")
    s["cq"], s["ck"] = _head_cols(_scan(None, s["gts"], name="fox_cumsum"), t)
    s["qkv_b"] = _qkv_prep(z, off=Z_BQ, name="fox_prep")
    s["ob"], s["lse_b"] = _att_fwd(s["qkv_b"], s["cq"], s["ck"], mode="fox", name="fox_att")
    gain_b = jnp.tile(p["norm_b"].reshape(N_HEADS, HEAD), (2, 1))
    y_b = _post_fwd(s["ob"].reshape(t * N_HEADS, HEAD), gain_b, None, name="fox_post")
    s["cconv"] = _conv_fwd(z, p["gdn_conv_w"], jnp.zeros((3 * GROUP,), F32), ncols=3 * GROUP, coff=Z_CQKV,
                           name="gdn_conv")
    s["gdn_mids"] = _gdn_pre_fwd(s["cconv"], s["gts"], name="gdn_pre")
    s["oc"], s["gdn_states"] = _gdn_seq_fwd(s["gdn_mids"], name="gdn_seq")
    y_c = _gdn_out_fwd(s["oc"], z, p["gdn_norm"].reshape(1, HEAD), name="gdn_out")
    s["qkv_d"] = _qkv_prep(z, off=Z_DQ, name="dil_prep")
    s["od"], s["lse_d"] = _att_fwd(s["qkv_d"], s["cq"], s["ck"], mode="dil", name="dil_att")
    gain_d = jnp.tile(p["norm_d"].reshape(N_HEADS, HEAD), (2, 1))
    y_d = _post_fwd(s["od"].reshape(t * N_HEADS, HEAD), gain_d, None, name="dil_post")
    y = s["y"] = jnp.concatenate([y_a.reshape(t, GROUP), y_b.reshape(t, GROUP), y_c, y_d.reshape(t, GROUP)], axis=1)
    x1 = s["x1"] = _matmul(y, p["w_out"], add=x, name="out_proj")
    s["h2"] = _rmsnorm_fwd(x1, p["norm_ffn"], name="ffn_norm")
    s["uu"] = _matmul(s["h2"], p["ffn_w_up"], name="ffn_up")
    s["act"] = _ffn_act_fwd(s["uu"], p["ffn_conv_w"], p["ffn_conv_b"], name="ffn_conv_swiglu")
    return _matmul(s["act"], p["ffn_w_down"], add=x1, name="ffn_down"), s


def _layer_bwd(dx2, p, s):
    t = dx2.shape[0]
    g = {}
    dact = _matmul(dx2, p["ffn_w_down"], form="nt", name="ffn_down_dx")
    g["ffn_w_down"] = _matmul(s["act"], dx2, form="tn", name="ffn_down_dw")
    du = _ffn_act_bwd(s["uu"], p["ffn_conv_w"], p["ffn_conv_b"], dact, name="ffn_conv_swiglu_bwd")
    duu, g["ffn_conv_w"], g["ffn_conv_b"] = _conv_bwd(s["uu"], p["ffn_conv_w"], du, ncols=du.shape[1],
                                                      name="ffn_conv_bwd")
    dh2 = _matmul(duu, p["ffn_w_up"], form="nt", name="ffn_up_dx")
    g["ffn_w_up"] = _matmul(s["h2"], duu, form="tn", name="ffn_up_dw")
    dx1, g["norm_ffn"] = _rmsnorm_bwd(s["x1"], p["norm_ffn"], dh2, dx2, name="ffn_norm_bwd")
    dy = _matmul(dx1, p["w_out"], form="nt", name="out_proj_dx")
    g["w_out"] = _matmul(s["y"], dx1, form="tn", name="out_proj_dw")
    z = s["z"]
    dha, dgn, dgate_a = _post_bwd(s["ha"].reshape(t * LRU_BLOCKS, LRU_BD), p["norm_a"].reshape(LRU_BLOCKS, LRU_BD),
                                  s["gate_a"], dy[:, 0:GROUP].reshape(t * LRU_BLOCKS, LRU_BD), name="lru_post_bwd")
    g["norm_a"] = dgn.reshape(GROUP)
    gsc = _scan(_shift_up(s["a"]), dha.reshape(t, GROUP), reverse=True, name="lru_scan_bwd")
    dxc, dwa, dba, dwx, dbx, dlam = _lru_gates_bwd(s["xc"], *s["lru"], gsc, _shift_down(s["ha"]), name="lru_gates_bwd")
    g["lru_wa"], g["lru_wx"] = _diag_blocks(dwa), _diag_blocks(dwx)
    g["lru_ba"], g["lru_bx"], g["lru_lambda"] = dba.reshape(GROUP), dbx.reshape(GROUP), dlam.reshape(GROUP)
    dax, g["lru_conv_w"], g["lru_conv_b"] = _conv_bwd(z, p["lru_conv_w"], dxc, ncols=GROUP, coff=Z_AX,
                                                      name="lru_conv_bwd")
    gain_b = jnp.tile(p["norm_b"].reshape(N_HEADS, HEAD), (2, 1))
    dob, dgn, _ = _post_bwd(s["ob"].reshape(t * N_HEADS, HEAD), gain_b, None,
                            dy[:, GROUP:2 * GROUP].reshape(t * N_HEADS, HEAD), name="fox_post_bwd")
    g["norm_b"] = (dgn[:N_HEADS] + dgn[N_HEADS:]).reshape(GROUP)
    dob = dob.reshape(t, GROUP)
    dbq, dcq, delta = _att_bwd_q(s["qkv_b"], s["cq"], s["ck"], dob, s["ob"], s["lse_b"], mode="fox", name="fox_att_dq")
    dbk, dbv, dck = _att_bwd_kv(s["qkv_b"], s["cq"], s["ck"], dob, s["lse_b"].reshape(N_HEADS, 1, t),
                                delta.reshape(N_HEADS, 1, t), mode="fox", name="fox_att_dkv")
    pad_lanes = ((0, 0), (0, LANES - N_HEADS))
    dc_rows = _sum_slots([jnp.pad(dcq.reshape(N_HEADS, t).T, pad_lanes), jnp.pad(dck.reshape(N_HEADS, t).T, pad_lanes)],
                         name="fox_dc_sum")
    dgts_fox = _scan(None, dc_rows, reverse=True, name="fox_cumsum_bwd")
    gain_d = jnp.tile(p["norm_d"].reshape(N_HEADS, HEAD), (2, 1))
    dod, dgn, _ = _post_bwd(s["od"].reshape(t * N_HEADS, HEAD), gain_d, None,
                            dy[:, 3 * GROUP:4 * GROUP].reshape(t * N_HEADS, HEAD), name="dil_post_bwd")
    g["norm_d"] = (dgn[:N_HEADS] + dgn[N_HEADS:]).reshape(GROUP)
    dod = dod.reshape(t, GROUP)
    ddq, _, delta = _att_bwd_q(s["qkv_d"], s["cq"], s["ck"], dod, s["od"], s["lse_d"], mode="dil", name="dil_att_dq")
    ddk, ddv, _ = _att_bwd_kv(s["qkv_d"], s["cq"], s["ck"], dod, s["lse_d"].reshape(N_HEADS, 1, t),
                              delta.reshape(N_HEADS, 1, t), mode="dil", name="dil_att_dkv")
    doc, dcz, g["gdn_norm"] = _gdn_out_bwd(s["oc"], z, p["gdn_norm"].reshape(1, HEAD), dy[:, 2 * GROUP:3 * GROUP],
                                           name="gdn_out_bwd")
    dmids = _gdn_seq_bwd(s["gdn_mids"], s["gdn_states"], doc, name="gdn_seq_bwd")
    dcconv, dgts_gdn = _gdn_pre_bwd(s["cconv"], s["gts"], dmids, name="gdn_pre_bwd")
    dcqkv, g["gdn_conv_w"], _ = _conv_bwd(z, p["gdn_conv_w"], dcconv, ncols=3 * GROUP, coff=Z_CQKV,
                                          name="gdn_conv_bwd")
    dgts = _sum_slots([dgts_fox, dgts_gdn], name="gates_dsum")
    dzg, dfb, dal, ddt = _gates_bwd(z, *_gate_rows(p), dgts, name="gates_bwd")
    g["fox_f_bias"] = dfb[0, G_F:G_F + N_HEADS]
    g["gdn_a_log"] = dal[0, G_ALPHA:G_ALPHA + N_HEADS]
    g["gdn_dt_bias"] = ddt[0, G_ALPHA:G_ALPHA + N_HEADS]
    dz = jnp.concatenate([dax, dgate_a.reshape(t, GROUP), dbq, dbk, dbv, dcqkv, dcz, ddq, ddk, ddv, dzg], axis=1)
    dh = _matmul(dz, p["w_in"], form="nt", name="in_proj_dx")
    g["w_in"] = _matmul(s["h"], dz, form="tn", name="in_proj_dw")
    dx, g["norm_mix"] = _rmsnorm_bwd(s["x"], p["norm_mix"], dh, dx1, name="mix_norm_bwd")
    return dx, g


def _local_step(x, tgt, layers, norm_final):
    saved = []
    for p in layers:
        x, s = _layer_fwd(x, p)
        saved.append(s)
    loss, dx, dnf = _loss_head(x, norm_final, tgt, name="loss_head")
    grads = []
    for p, s in zip(reversed(layers), reversed(saved)):
        dx, g = _layer_bwd(dx, p, s)
        grads.append(g)
    return loss, dx, grads[::-1], dnf


BIG = ("w_in", "w_out", "ffn_w_up", "ffn_w_down")
PACK_ROWS = 4096
SHARDED_SMALL = ("lru_conv_w", "gdn_conv_w", "ffn_conv_w")
NAMES = ("norm_mix", "w_in", "lru_conv_w", "lru_conv_b", "lru_wa", "lru_ba", "lru_wx", "lru_bx", "lru_lambda",
         "fox_f_bias", "gdn_conv_w", "gdn_a_log", "gdn_dt_bias", "gdn_norm", "norm_a", "norm_b", "norm_d", "w_out",
         "norm_ffn", "ffn_w_up", "ffn_conv_w", "ffn_conv_b", "ffn_w_down", "norm_final")
SMALL = tuple(n for n in NAMES if n not in BIG)


def _big_pieces(g, k_axis_cols):
    if k_axis_cols:
        d, n = g.shape
        return g.reshape(d, 4, n // 4).transpose(1, 0, 2).reshape(4, -1, LANES)
    return g.reshape(4, -1, LANES)


def _reduce_big(gl, *, c):
    whole = {"w_in": _unpad_w_in(gl["w_in"]), "w_out": gl["w_out"], "ffn_w_up": gl["ffn_w_up"],
             "ffn_w_down": gl["ffn_w_down"]}
    cols = {"w_in": True, "w_out": False, "ffn_w_up": True, "ffn_w_down": False}
    pieces = [_big_pieces(whole[n], cols[n]) for n in BIG]
    rows = [q.shape[1] for q in pieces]
    pad = (-sum(rows)) % PACK_ROWS
    packed = jnp.concatenate(pieces + [jnp.zeros((4, pad, LANES), F32)], axis=1)
    half = packed.shape[1] // 2
    halves = packed.reshape(4, 2, half, LANES)
    mine = lax.dynamic_index_in_dim(halves, c, axis=1, keepdims=False).reshape(4 * half, LANES)
    other = lax.dynamic_index_in_dim(halves, 1 - c, axis=1, keepdims=False).reshape(4 * half, LANES)
    got = _sibling_send(other, name="grad_sibling_send")
    chip_sum = _sum_slots([mine, got], out_dtype=MM_DTYPE, name="grad_sibling_sum").reshape(4, half, LANES)
    from_chips = _chip_exchange(chip_sum, name="grad_chip_exchange")
    total_half = _sum_slots(from_chips, name="grad_chip_sum")
    other_half = _sibling_send(total_half, name="grad_sibling_swap")
    total = jnp.where(c == 0, jnp.concatenate([total_half, other_half]), jnp.concatenate([other_half, total_half]))
    out, off = {}, 0
    for n, r in zip(BIG, rows):
        out[n] = total[off:off + r]
        off += r
    return out


def kernel(x, norm_mix, w_in, lru_conv_w, lru_conv_b, lru_wa, lru_ba, lru_wx, lru_bx, lru_lambda, fox_f_bias, gdn_conv_w, gdn_a_log, gdn_dt_bias, gdn_norm, norm_a, norm_b, norm_d, w_out, norm_ffn, ffn_w_up, ffn_conv_w, ffn_conv_b, ffn_w_down, norm_final, loss_target, m_norm_mix, m_w_in, m_lru_conv_w, m_lru_conv_b, m_lru_wa, m_lru_ba, m_lru_wx, m_lru_bx, m_lru_lambda, m_fox_f_bias, m_gdn_conv_w, m_gdn_a_log, m_gdn_dt_bias, m_gdn_norm, m_norm_a, m_norm_b, m_norm_d, m_w_out, m_norm_ffn, m_ffn_w_up, m_ffn_conv_w, m_ffn_conv_b, m_ffn_w_down, m_norm_final, v_norm_mix, v_w_in, v_lru_conv_w, v_lru_conv_b, v_lru_wa, v_lru_ba, v_lru_wx, v_lru_bx, v_lru_lambda, v_fox_f_bias, v_gdn_conv_w, v_gdn_a_log, v_gdn_dt_bias, v_gdn_norm, v_norm_a, v_norm_b, v_norm_d, v_w_out, v_norm_ffn, v_ffn_w_up, v_ffn_conv_w, v_ffn_conv_b, v_ffn_w_down, v_norm_final):
    w = dict(zip(NAMES, (norm_mix, w_in, lru_conv_w, lru_conv_b, lru_wa, lru_ba, lru_wx, lru_bx, lru_lambda, fox_f_bias,
                         gdn_conv_w, gdn_a_log, gdn_dt_bias, gdn_norm, norm_a, norm_b, norm_d, w_out, norm_ffn, ffn_w_up,
                         ffn_conv_w, ffn_conv_b, ffn_w_down, norm_final)))
    m = dict(zip(NAMES, (m_norm_mix, m_w_in, m_lru_conv_w, m_lru_conv_b, m_lru_wa, m_lru_ba, m_lru_wx, m_lru_bx,
                         m_lru_lambda, m_fox_f_bias, m_gdn_conv_w, m_gdn_a_log, m_gdn_dt_bias, m_gdn_norm, m_norm_a,
                         m_norm_b, m_norm_d, m_w_out, m_norm_ffn, m_ffn_w_up, m_ffn_conv_w, m_ffn_conv_b, m_ffn_w_down,
                         m_norm_final)))
    v = dict(zip(NAMES, (v_norm_mix, v_w_in, v_lru_conv_w, v_lru_conv_b, v_lru_wa, v_lru_ba, v_lru_wx, v_lru_bx,
                         v_lru_lambda, v_fox_f_bias, v_gdn_conv_w, v_gdn_a_log, v_gdn_dt_bias, v_gdn_norm, v_norm_a,
                         v_norm_b, v_norm_d, v_w_out, v_norm_ffn, v_ffn_w_up, v_ffn_conv_w, v_ffn_conv_b, v_ffn_w_down,
                         v_norm_final)))
    depth = w_in.shape[0]
    xi, yi, ci = _position()
    chip = 2 * xi + yi

    conv_rows, conv_shapes = _pack_rows([w[n] for n in SHARDED_SMALL])
    (conv_all,) = _chip_gather([conv_rows], name="conv_taps_gather")
    conv_full = {}
    per_chip = [_unpack_rows(conv_all[k], conv_shapes) for k in range(4)]
    for i, n in enumerate(SHARDED_SMALL):
        conv_full[n] = jnp.concatenate([per_chip[k][i] for k in range(4)], axis=-1)
    layers = []
    for l in range(depth):
        halves = [w[n][l].astype(MM_DTYPE).reshape(2, w[n].shape[1] // 2, w[n].shape[2]) for n in BIG]
        g_in, g_out, g_up, g_dn = (g.reshape((4,) + w[n].shape[1:]) for n, g in zip(
            BIG, _chip_gather_halves(halves, name="weights_gather")))
        p = {n: w[n][l] for n in SMALL if n != "norm_final" and n not in SHARDED_SMALL}
        for n in SHARDED_SMALL:
            p[n] = conv_full[n][l]
        p["w_in"] = _pad_w_in(jnp.concatenate([g_in[k] for k in range(4)], axis=1))
        p["w_out"] = g_out.reshape(-1, g_out.shape[-1])
        p["ffn_w_up"] = jnp.concatenate([g_up[k] for k in range(4)], axis=1)
        p["ffn_w_down"] = g_dn.reshape(-1, g_dn.shape[-1])
        layers.append(p)

    loss, grad_x, grads, g_norm_final = _local_step(x[0], loss_target[0], layers, norm_final)
    loss = lax.psum(loss, ("x", "y", "c"))

    big = [_reduce_big(grads[l], c=ci) for l in range(depth)]
    small_names = [n for n in SMALL if n != "norm_final"]
    small_rows, small_shapes = _pack_rows([jnp.stack([grads[l][n] for l in range(depth)]) for n in small_names]
                                          + [g_norm_final])
    small_sum = _unpack_rows(_all_sum(small_rows, name="small_grads_sum"), small_shapes)
    gsum = dict(zip(small_names + ["norm_final"], small_sum))
    for n in SHARDED_SMALL:
        width = w[n].shape[-1]
        gsum[n] = lax.dynamic_slice_in_dim(gsum[n], chip * width, width, axis=-1)
    for n in BIG:
        gsum[n] = jnp.stack([big[l][n].reshape(w[n].shape[1:]) for l in range(depth)])

    delta, new_m, new_v = {}, {}, {}
    for n in BIG:
        cols = w[n].shape[-1]
        d_, m_, v_ = _adamw(w[n].reshape(-1, cols), gsum[n].reshape(-1, cols), m[n].reshape(-1, cols),
                            v[n].reshape(-1, cols), name="adamw_" + n)
        delta[n], new_m[n], new_v[n] = (a.reshape(w[n].shape) for a in (d_, m_, v_))
    packs = [_pack_rows([src[n] for n in SMALL]) for src in (w, gsum, m, v)]
    outs = _adamw(*[pk[0] for pk in packs], name="adamw_small")
    for dst, rows_ in zip((delta, new_m, new_v), outs):
        dst.update(zip(SMALL, _unpack_rows(rows_, packs[0][1])))
    return (loss, grad_x[None], *[gsum[n] for n in NAMES], *[delta[n] for n in NAMES], *[new_m[n] for n in NAMES],
            *[new_v[n] for n in NAMES])
```

```python
import functools

import numpy as np
import jax
import jax.numpy as jnp
from jax import lax
from jax.experimental import pallas as pl
from jax.experimental.pallas import tpu as pltpu

F32 = jnp.float32
MM_DTYPE = jnp.bfloat16
VMEM_LIMIT_BYTES = 56 * 1024 * 1024
LANES = 128
SUBLANES = 8

GROUP = 512
HEAD = 128
N_HEADS = GROUP // HEAD
LRU_BLOCKS = 8
LRU_BD = GROUP // LRU_BLOCKS
LRU_C = 8.0
GDN_CHUNK = 64
DILATED_PAIRS = ((128, 1), (512, 4), (2048, 16))
EPS = 1e-6
NEG = -1e30
ATT_SCALE = HEAD ** -0.5
ATT_TILE = 512

ADAM_LR, ADAM_B1, ADAM_B2, ADAM_EPS, ADAM_WD, ADAM_STEP = 0.001, 0.9, 0.999, 1e-08, 0.01, 10

Z_AX, Z_AG, Z_BQ, Z_BK, Z_BV = 0, 512, 1024, 1536, 2048
Z_CQKV, Z_CZ, Z_DQ, Z_DK, Z_DV, Z_GATES, Z_COLS = 2560, 4096, 4608, 5120, 5632, 6144, 6272
IN_COLS = 6156
G_F, G_BETA, G_ALPHA = 0, 4, 8

MESH_ID = pl.DeviceIdType.MESH
ANY = pl.BlockSpec(memory_space=pl.ANY)


def _tile(n, target, align):
    t = min(n, target) // align * align
    while t >= align:
        if n % t == 0:
            return t
        t -= align
    return n


def _params(*sem):
    return pltpu.CompilerParams(dimension_semantics=sem, vmem_limit_bytes=VMEM_LIMIT_BYTES)


def _sds(shape, dtype=F32):
    return jax.ShapeDtypeStruct(tuple(shape), dtype)


def _dg(a, b, ca, cb, hi, batched=False):
    off = int(batched)
    dims = (((ca + off,), (cb + off,)), (((0,), (0,)) if batched else ((), ())))
    dot = lambda p, q: lax.dot_general(p, q, dims, preferred_element_type=F32)
    if hi:
        a_hi, b_hi = a.astype(MM_DTYPE), b.astype(MM_DTYPE)
        a_lo = (a - a_hi.astype(F32)).astype(MM_DTYPE)
        b_lo = (b - b_hi.astype(F32)).astype(MM_DTYPE)
        return dot(a_hi, b_hi) + (dot(a_hi, b_lo) + dot(a_lo, b_hi))
    return dot(a.astype(MM_DTYPE), b.astype(MM_DTYPE))


_FORMS = {"nn": (1, 0), "nt": (1, 1), "tn": (0, 0)}


@functools.partial(jax.custom_vjp, nondiff_argnums=(2, 3, 4))
def _mmf(a, b, form, hi, batched=False):
    ca, cb = _FORMS[form]
    return _dg(a, b, ca, cb, hi, batched)


def _mmf_fwd(a, b, form, hi, batched):
    return _mmf(a, b, form, hi, batched), (a, b)


def _mmf_bwd(form, hi, batched, res, g):
    a, b = res
    if form == "nn":
        return _dg(g, b, 1, 1, hi, batched), _dg(a, g, 0, 0, hi, batched)
    if form == "nt":
        return _dg(g, b, 1, 0, hi, batched), _dg(g, a, 0, 0, hi, batched)
    return _dg(b, g, 1, 1, hi, batched), _dg(a, g, 1, 0, hi, batched)


_mmf.defvjp(_mmf_fwd, _mmf_bwd)


def _mm(a, b):
    return _mmf(a, b, "nn", False)


def _mm_nt(a, b):
    return _mmf(a, b, "nt", False)


def _mm_tn(a, b):
    return _mmf(a, b, "tn", False)


MXU_DIM = 256
MATMUL_VMEM_BUDGET = 40 * 1024 * 1024


def _matmul_tiles(m, n, k, a_bytes, b_bytes, o_bytes, has_add):
    divisors = lambda d: [t for t in range(d, 0, -LANES) if d % t == 0 and t % LANES == 0] or [d]
    for tk in divisors(k):
        fits = []
        for tm in divisors(m):
            for tn in divisors(n):
                need = 2 * (tm * tk * a_bytes + tk * tn * b_bytes) + 2 * tm * tn * o_bytes
                need += (2 * tm * tn * 4 if has_add else 0) + (tm * tn * 4 if tk < k else 0)
                if tm <= 1024 and tn <= 1024 and need <= MATMUL_VMEM_BUDGET:
                    fits.append((min(tm, MXU_DIM) * min(tn, MXU_DIM), tm * tn, tm, tn))
        if fits:
            return max(fits)[2:] + (tk,)
    return _tile(m, 128, LANES), _tile(n, 128, LANES), _tile(k, 128, LANES)


def _matmul(a, b, *, form="nn", add=None, out_dtype=F32, tiles=None, name):
    ca, cb = _FORMS[form]
    m, k = (a.shape[1], a.shape[0]) if form == "tn" else a.shape
    n = b.shape[0] if form == "nt" else b.shape[1]
    tm, tn, tk = tiles or _matmul_tiles(m, n, k, a.dtype.itemsize, b.dtype.itemsize, jnp.dtype(out_dtype).itemsize,
                                        add is not None)
    nk = k // tk
    a_spec = (pl.BlockSpec((tk, tm), lambda i, j, kk: (kk, i)) if form == "tn"
              else pl.BlockSpec((tm, tk), lambda i, j, kk: (i, kk)))
    b_spec = (pl.BlockSpec((tn, tk), lambda i, j, kk: (j, kk)) if form == "nt"
              else pl.BlockSpec((tk, tn), lambda i, j, kk: (kk, j)))
    o_spec = pl.BlockSpec((tm, tn), lambda i, j, kk: (i, j))

    def body(*refs):
        a_ref, b_ref = refs[:2]
        add_ref = None if add is None else refs[2]
        o_ref = refs[2 + (add is not None)]
        part = _dg(a_ref[...], b_ref[...], ca, cb, False)
        if nk == 1:
            o_ref[...] = (part if add is None else part + add_ref[...]).astype(out_dtype)
            return
        acc_ref = refs[-1]
        kk = pl.program_id(2)

        @pl.when(kk == 0)
        def _():
            acc_ref[...] = jnp.zeros_like(acc_ref)

        acc_ref[...] += part

        @pl.when(kk == nk - 1)
        def _():
            r = acc_ref[...]
            if add is not None:
                r = r + add_ref[...]
            o_ref[...] = r.astype(out_dtype)

    args = (a, b) if add is None else (a, b, add)
    specs = [a_spec, b_spec] if add is None else [a_spec, b_spec, o_spec]
    return pl.pallas_call(
        body, name=name, grid=(m // tm, n // tn, nk), in_specs=specs, out_specs=o_spec,
        out_shape=_sds((m, n), out_dtype), scratch_shapes=[] if nk == 1 else [pltpu.VMEM((tm, tn), F32)],
        compiler_params=_params("parallel", "parallel", "arbitrary"))(*args)


def _rms(x, g):
    return x * lax.rsqrt(jnp.mean(x * x, axis=-1, keepdims=True) + EPS) * g


def _rmsnorm_fwd(x, g, *, name):
    t, d = x.shape
    tb = _tile(t, 512, SUBLANES)

    def body(x_ref, g_ref, o_ref):
        o_ref[...] = _rms(x_ref[...], g_ref[...]).astype(o_ref.dtype)

    row = pl.BlockSpec((tb, d), lambda i: (i, 0))
    vec = pl.BlockSpec((1, d), lambda i: (0, 0))
    return pl.pallas_call(body, name=name, grid=(t // tb,), in_specs=[row, vec], out_specs=row,
                          out_shape=_sds((t, d), MM_DTYPE), compiler_params=_params("parallel"))(x, g.reshape(1, d))


def _rmsnorm_bwd(x, g, dh, res, *, name):
    t, d = x.shape
    tb = _tile(t, 256, SUBLANES)

    def body(x_ref, g_ref, dh_ref, res_ref, dx_ref, dg_ref):
        _, vjp = jax.vjp(_rms, x_ref[...], g_ref[...])
        dx, dg = vjp(dh_ref[...])
        dx_ref[...] = dx + res_ref[...]

        @pl.when(pl.program_id(0) == 0)
        def _():
            dg_ref[...] = jnp.zeros_like(dg_ref)

        dg_ref[...] += dg

    row = pl.BlockSpec((tb, d), lambda i: (i, 0))
    vec = pl.BlockSpec((1, d), lambda i: (0, 0))
    dx, dg = pl.pallas_call(body, name=name, grid=(t // tb,), in_specs=[row, vec, row, row], out_specs=[row, vec],
                            out_shape=[_sds((t, d)), _sds((1, d))], compiler_params=_params("arbitrary"))(
                                x, g.reshape(1, d), dh, res)
    return dx, dg.reshape(d)


def _loss_head(x, g, tgt, *, name):
    t, d = x.shape
    tb = _tile(t, 256, SUBLANES)

    def body(x_ref, g_ref, t_ref, loss_ref, dx_ref, dg_ref):
        def f(xv, gv):
            e = _rms(xv, gv) - t_ref[...]
            return jnp.sum(jnp.sum(e * e, axis=-1, keepdims=True), axis=0, keepdims=True) * (0.5 / d)

        l, vjp = jax.vjp(f, x_ref[...], g_ref[...])
        dx, dg = vjp(jnp.ones((1, 1), F32))
        dx_ref[...] = dx

        @pl.when(pl.program_id(0) == 0)
        def _():
            dg_ref[...] = jnp.zeros_like(dg_ref)
            loss_ref[...] = jnp.zeros_like(loss_ref)

        dg_ref[...] += dg
        loss_ref[...] += jnp.zeros(loss_ref.shape, F32) + l

    row = pl.BlockSpec((tb, d), lambda i: (i, 0))
    vec = pl.BlockSpec((1, d), lambda i: (0, 0))
    lspec = pl.BlockSpec((SUBLANES, LANES), lambda i: (0, 0))
    loss, dx, dg = pl.pallas_call(
        body, name=name, grid=(t // tb,), in_specs=[row, vec, row], out_specs=[lspec, row, vec],
        out_shape=[_sds((SUBLANES, LANES)), _sds((t, d)), _sds((1, d))], compiler_params=_params("arbitrary"))(
            x, g.reshape(1, d), tgt)
    return loss[0, 0], dx, dg.reshape(d)


def _delayed(x, prev, j):
    if j == 0:
        return x
    sh = pltpu.roll(x, j, axis=0)
    row = lax.broadcasted_iota(jnp.int32, prev.shape, 0)
    top = jnp.where(row < j, pltpu.roll(prev, j, axis=0), sh[0:SUBLANES])
    return jnp.concatenate([top, sh[SUBLANES:]], axis=0)


def _advanced(x, nxt, j):
    if j == 0:
        return x
    tb = x.shape[0]
    sh = pltpu.roll(x, tb - j, axis=0)
    row = lax.broadcasted_iota(jnp.int32, nxt.shape, 0)
    bot = jnp.where(row + j < SUBLANES, sh[tb - SUBLANES:], pltpu.roll(nxt, SUBLANES - j, axis=0))
    return jnp.concatenate([sh[:tb - SUBLANES], bot], axis=0)


def _conv_tiles(t, ncols, coff):
    tc = _tile(ncols, 512, LANES)
    assert coff % tc == 0
    tb = _tile(t, 512, SUBLANES)
    return tc, tb, coff // tc


def _conv_fwd(x, w, b, *, ncols, coff=0, name):
    t = x.shape[0]
    kw = w.shape[0]
    tc, tb, cb = _conv_tiles(t, ncols, coff)
    n8 = tb // SUBLANES

    def body(x_ref, p_ref, w_ref, b_ref, o_ref):
        xv = x_ref[...]
        prev = jnp.where(pl.program_id(1) > 0, p_ref[...], 0.0)
        acc = jnp.zeros_like(xv) + b_ref[...]
        for k in range(kw):
            acc = acc + w_ref[k:k + 1, :] * _delayed(xv, prev, kw - 1 - k)
        o_ref[...] = acc

    in_specs = [
        pl.BlockSpec((tb, tc), lambda c, i: (i, c + cb)),
        pl.BlockSpec((SUBLANES, tc), lambda c, i: (jnp.maximum(i * n8 - 1, 0), c + cb)),
        pl.BlockSpec((kw, tc), lambda c, i: (0, c)),
        pl.BlockSpec((1, tc), lambda c, i: (0, c)),
    ]
    return pl.pallas_call(
        body, name=name, grid=(ncols // tc, t // tb), in_specs=in_specs,
        out_specs=pl.BlockSpec((tb, tc), lambda c, i: (i, c)), out_shape=_sds((t, ncols)),
        compiler_params=_params("parallel", "parallel"))(x, x, w, b.reshape(1, ncols))


def _conv_bwd(x, w, dy, *, ncols, coff=0, name):
    t = x.shape[0]
    kw = w.shape[0]
    tc, tb, cb = _conv_tiles(t, ncols, coff)
    n8 = tb // SUBLANES
    nt = t // tb

    def body(x_ref, p_ref, dy_ref, n_ref, w_ref, dx_ref, dw_ref, db_ref):
        i = pl.program_id(1)
        xv, dyv = x_ref[...], dy_ref[...]
        prev = jnp.where(i > 0, p_ref[...], 0.0)
        nxt = jnp.where(i < nt - 1, n_ref[...], 0.0)

        @pl.when(i == 0)
        def _():
            dw_ref[...] = jnp.zeros_like(dw_ref)
            db_ref[...] = jnp.zeros_like(db_ref)

        dx = jnp.zeros_like(dyv)
        for k in range(kw):
            j = kw - 1 - k
            dx = dx + w_ref[k:k + 1, :] * _advanced(dyv, nxt, j)
            dw_ref[k:k + 1, :] += jnp.sum(dyv * _delayed(xv, prev, j), axis=0, keepdims=True)
        dx_ref[...] = dx.astype(dx_ref.dtype)
        db_ref[...] += jnp.sum(dyv, axis=0, keepdims=True)

    in_specs = [
        pl.BlockSpec((tb, tc), lambda c, i: (i, c + cb)),
        pl.BlockSpec((SUBLANES, tc), lambda c, i: (jnp.maximum(i * n8 - 1, 0), c + cb)),
        pl.BlockSpec((tb, tc), lambda c, i: (i, c)),
        pl.BlockSpec((SUBLANES, tc), lambda c, i: (jnp.minimum((i + 1) * n8, nt * n8 - 1), c)),
        pl.BlockSpec((kw, tc), lambda c, i: (0, c)),
    ]
    out_specs = [
        pl.BlockSpec((tb, tc), lambda c, i: (i, c)),
        pl.BlockSpec((kw, tc), lambda c, i: (0, c)),
        pl.BlockSpec((1, tc), lambda c, i: (0, c)),
    ]
    dx, dw, db = pl.pallas_call(
        body, name=name, grid=(ncols // tc, nt), in_specs=in_specs, out_specs=out_specs,
        out_shape=[_sds((t, ncols), MM_DTYPE), _sds((kw, ncols)), _sds((1, ncols))],
        compiler_params=_params("parallel", "arbitrary"))(x, x, dy, dy, w)
    return dx, dw, db.reshape(ncols)


def _scan(a, u, *, reverse=False, name):
    t, c = u.shape
    tc = _tile(c, 512, LANES)
    tb = _tile(t, 256, SUBLANES)
    nt = t // tb

    def body(*refs):
        if a is None:
            u_ref, o_ref, carry_ref = refs
        else:
            a_ref, u_ref, o_ref, carry_ref = refs

        @pl.when(pl.program_id(1) == 0)
        def _():
            carry_ref[...] = jnp.zeros_like(carry_ref)

        hv = u_ref[...]
        av = None if a is None else a_ref[...]
        row = lax.broadcasted_iota(jnp.int32, hv.shape, 0)
        s = 1
        while s < tb:
            live = row < tb - s if reverse else row >= s
            shift = tb - s if reverse else s
            h_sh = jnp.where(live, pltpu.roll(hv, shift, axis=0), 0.0)
            if av is None:
                hv = hv + h_sh
            else:
                hv = av * h_sh + hv
                av = av * jnp.where(live, pltpu.roll(av, shift, axis=0), 1.0)
            s *= 2
        hv = hv + carry_ref[...] if av is None else hv + av * carry_ref[...]
        o_ref[...] = hv
        carry_ref[...] = o_ref[pl.ds(0 if reverse else tb - 1, 1), :]

    spec = pl.BlockSpec((tb, tc), (lambda cc, i: (nt - 1 - i, cc)) if reverse else (lambda cc, i: (i, cc)))
    args, specs = ((u,), [spec]) if a is None else ((a, u), [spec, spec])
    return pl.pallas_call(
        body, name=name, grid=(c // tc, t // tb), in_specs=specs, out_specs=spec, out_shape=_sds((t, c)),
        scratch_shapes=[pltpu.VMEM((1, tc), F32)], compiler_params=_params("parallel", "arbitrary"))(*args)


def _shift_down(x):
    return lax.pad(x, jnp.zeros((), x.dtype), ((1, -1, 0), (0, 0, 0)))


def _shift_up(x):
    return lax.pad(x, jnp.zeros((), x.dtype), ((-1, 1, 0), (0, 0, 0)))


def _neg_expm1(y):
    small = -(y * (1.0 + y * (0.5 + y * (1.0 / 6.0 + y * (1.0 / 24.0 + y * (1.0 / 120.0))))))
    return jnp.where(y > -0.05, small, 1.0 - jnp.exp(y))


def _lru_gates(xc, wa, ba, wx, bx, lam):
    r = jax.nn.sigmoid(_mm(xc, wa) + ba)
    i = jax.nn.sigmoid(_mm(xc, wx) + bx)
    log_a = -LRU_C * r * jax.nn.softplus(-lam)
    a = jnp.exp(log_a)
    u = jnp.sqrt(_neg_expm1(2.0 * log_a)) * (i * xc)
    return a, u


def _lru_specs(t):
    tb = _tile(t, 256, SUBLANES)
    row = pl.BlockSpec((tb, GROUP), lambda i: (i, 0))
    mat = pl.BlockSpec((GROUP, GROUP), lambda i: (0, 0))
    vec = pl.BlockSpec((1, GROUP), lambda i: (0, 0))
    return tb, row, mat, vec


def _lru_gates_fwd(xc, wa, ba, wx, bx, lam, *, name):
    t = xc.shape[0]
    tb, row, mat, vec = _lru_specs(t)

    def body(xc_ref, wa_ref, ba_ref, wx_ref, bx_ref, lam_ref, a_ref, u_ref):
        a, u = _lru_gates(xc_ref[...], wa_ref[...], ba_ref[...], wx_ref[...], bx_ref[...], lam_ref[...])
        a_ref[...] = a
        u_ref[...] = u

    return pl.pallas_call(
        body, name=name, grid=(t // tb,), in_specs=[row, mat, vec, mat, vec, vec], out_specs=[row, row],
        out_shape=[_sds((t, GROUP)), _sds((t, GROUP))], compiler_params=_params("parallel"))(xc, wa, ba, wx, bx, lam)


def _lru_gates_bwd(xc, wa, ba, wx, bx, lam, g, h_prev, *, name):
    t = xc.shape[0]
    tb, row, mat, vec = _lru_specs(t)

    def body(xc_ref, wa_ref, ba_ref, wx_ref, bx_ref, lam_ref, g_ref, hp_ref,
             dxc_ref, dwa_ref, dba_ref, dwx_ref, dbx_ref, dlam_ref):
        _, vjp = jax.vjp(_lru_gates, xc_ref[...], wa_ref[...], ba_ref[...], wx_ref[...], bx_ref[...], lam_ref[...])
        gv = g_ref[...]
        dxc, dwa, dba, dwx, dbx, dlam = vjp((gv * hp_ref[...], gv))
        dxc_ref[...] = dxc
        accs = (dwa_ref, dba_ref, dwx_ref, dbx_ref, dlam_ref)

        @pl.when(pl.program_id(0) == 0)
        def _():
            for r in accs:
                r[...] = jnp.zeros_like(r)

        for r, v in zip(accs, (dwa, dba, dwx, dbx, dlam)):
            r[...] += v

    return pl.pallas_call(
        body, name=name, grid=(t // tb,), in_specs=[row, mat, vec, mat, vec, vec, row, row],
        out_specs=[row, mat, vec, mat, vec, vec],
        out_shape=[_sds((t, GROUP)), _sds((GROUP, GROUP)), _sds((1, GROUP)), _sds((GROUP, GROUP)), _sds((1, GROUP)),
                   _sds((1, GROUP))],
        compiler_params=_params("arbitrary"))(xc, wa, ba, wx, bx, lam, g, h_prev)


def _block_diag(w):
    eye = jnp.eye(LRU_BLOCKS, dtype=w.dtype)
    return (eye[:, None, :, None] * w[:, :, None, :]).reshape(GROUP, GROUP)


def _diag_blocks(m):
    m4 = m.reshape(LRU_BLOCKS, LRU_BD, LRU_BLOCKS, LRU_BD)
    return jnp.stack([m4[n, :, n, :] for n in range(LRU_BLOCKS)])


def _post(hg, gain, gate):
    y = hg * lax.rsqrt(jnp.mean(hg * hg, axis=-1, keepdims=True) + EPS) * gain
    return y if gate is None else y * jax.nn.gelu(gate)


def _post_specs(rows, g):
    rb = _tile(rows, 2048, SUBLANES)
    return rb, pl.BlockSpec((rb, g), lambda i: (i, 0)), pl.BlockSpec((rb, g), lambda i: (0, 0))


def _post_fwd(hg, gain8, gate, *, name):
    rows, g = hg.shape
    rb, row, fixed = _post_specs(rows, g)
    gain_t = jnp.tile(gain8, (rb // SUBLANES, 1))

    def body(*refs):
        if gate is None:
            h_ref, gn_ref, o_ref = refs
            o_ref[...] = _post(h_ref[...], gn_ref[...], None).astype(o_ref.dtype)
        else:
            h_ref, gn_ref, gt_ref, o_ref = refs
            o_ref[...] = _post(h_ref[...], gn_ref[...], gt_ref[...]).astype(o_ref.dtype)

    args, specs = ((hg, gain_t), [row, fixed]) if gate is None else ((hg, gain_t, gate), [row, fixed, row])
    return pl.pallas_call(body, name=name, grid=(rows // rb,), in_specs=specs, out_specs=row,
                          out_shape=_sds((rows, g), MM_DTYPE), compiler_params=_params("parallel"))(*args)


def _post_bwd(hg, gain8, gate, dy, *, name):
    rows, g = hg.shape
    rb, row, fixed = _post_specs(rows, g)
    gain_t = jnp.tile(gain8, (rb // SUBLANES, 1))
    g8 = pl.BlockSpec((SUBLANES, g), lambda i: (0, 0))

    def body(*refs):
        if gate is None:
            h_ref, gn_ref, dy_ref, dh_ref, dgn_ref = refs
            _, vjp = jax.vjp(lambda h, gn: _post(h, gn, None), h_ref[...], gn_ref[...])
            dh, dgn = vjp(dy_ref[...])
        else:
            h_ref, gn_ref, gt_ref, dy_ref, dh_ref, dgn_ref, dgt_ref = refs
            _, vjp = jax.vjp(_post, h_ref[...], gn_ref[...], gt_ref[...])
            dh, dgn, dgt = vjp(dy_ref[...])
            dgt_ref[...] = dgt.astype(dgt_ref.dtype)
        dh_ref[...] = dh

        @pl.when(pl.program_id(0) == 0)
        def _():
            dgn_ref[...] = jnp.zeros_like(dgn_ref)

        dgn_ref[...] += dgn.reshape(rb // SUBLANES, SUBLANES, g).sum(axis=0)

    if gate is None:
        dh, dgn = pl.pallas_call(
            body, name=name, grid=(rows // rb,), in_specs=[row, fixed, row], out_specs=[row, g8],
            out_shape=[_sds((rows, g)), _sds((SUBLANES, g))], compiler_params=_params("arbitrary"))(hg, gain_t, dy)
        return dh, dgn, None
    dh, dgn, dgt = pl.pallas_call(
        body, name=name, grid=(rows // rb,), in_specs=[row, fixed, row, row], out_specs=[row, g8, row],
        out_shape=[_sds((rows, g)), _sds((SUBLANES, g)), _sds((rows, g), MM_DTYPE)],
        compiler_params=_params("arbitrary"))(hg, gain_t, gate, dy)
    return dh, dgn, dgt


def _gates(zg, fb, alog, dtb):
    lane = lax.broadcasted_iota(jnp.int32, zg.shape, 1)
    logf = jax.nn.log_sigmoid(zg + fb)
    beta = jax.nn.sigmoid(zg)
    gdec = -jnp.exp(alog) * jax.nn.softplus(zg + dtb)
    return jnp.where(lane < G_BETA, logf, jnp.where(lane < G_ALPHA, beta, jnp.where(lane < G_ALPHA + 4, gdec, 0.0)))


def _lane_row(v, off):
    return jnp.pad(v.reshape(1, N_HEADS), ((0, 0), (off, LANES - N_HEADS - off)))


def _gates_fwd(z, fb, alog, dtb, *, name):
    t = z.shape[0]
    tb = _tile(t, 1024, SUBLANES)
    zspec = pl.BlockSpec((tb, LANES), lambda i: (i, Z_GATES // LANES))
    row = pl.BlockSpec((tb, LANES), lambda i: (i, 0))
    vec = pl.BlockSpec((1, LANES), lambda i: (0, 0))

    def body(z_ref, fb_ref, al_ref, dt_ref, o_ref):
        o_ref[...] = _gates(z_ref[...], fb_ref[...], al_ref[...], dt_ref[...])

    return pl.pallas_call(body, name=name, grid=(t // tb,), in_specs=[zspec, vec, vec, vec], out_specs=row,
                          out_shape=_sds((t, LANES)), compiler_params=_params("parallel"))(z, fb, alog, dtb)


def _gates_bwd(z, fb, alog, dtb, dg, *, name):
    t = z.shape[0]
    tb = _tile(t, 1024, SUBLANES)
    zspec = pl.BlockSpec((tb, LANES), lambda i: (i, Z_GATES // LANES))
    row = pl.BlockSpec((tb, LANES), lambda i: (i, 0))
    vec = pl.BlockSpec((1, LANES), lambda i: (0, 0))

    def body(z_ref, fb_ref, al_ref, dt_ref, dg_ref, dz_ref, dfb_ref, dal_ref, ddt_ref):
        _, vjp = jax.vjp(_gates, z_ref[...], fb_ref[...], al_ref[...], dt_ref[...])
        dz, dfb, dal, ddt = vjp(dg_ref[...])
        dz_ref[...] = dz.astype(dz_ref.dtype)
        accs = (dfb_ref, dal_ref, ddt_ref)

        @pl.when(pl.program_id(0) == 0)
        def _():
            for r in accs:
                r[...] = jnp.zeros_like(r)

        for r, v in zip(accs, (dfb, dal, ddt)):
            r[...] += v

    return pl.pallas_call(
        body, name=name, grid=(t // tb,), in_specs=[zspec, vec, vec, vec, row], out_specs=[row, vec, vec, vec],
        out_shape=[_sds((t, LANES), MM_DTYPE), _sds((1, LANES)), _sds((1, LANES)), _sds((1, LANES))],
        compiler_params=_params("arbitrary"))(z, fb, alog, dtb, dg)


class _Hosted:
    def __init__(self, args, out_shapes, sems, start, finish):
        self.args, self.out_shapes, self.sems, self.start, self.finish = list(args), list(out_shapes), list(sems), start, finish


def _grid_call(body, *, name, grid, in_specs, out_specs, out_shape, args, hosted=None):
    if hosted is None:
        outs = pl.pallas_call(body, name=name, grid=grid, in_specs=in_specs, out_specs=out_specs, out_shape=out_shape,
                              compiler_params=_params(*(["parallel"] * len(grid))))(*args)
        return outs, None
    n_in, n_out, h_in, h_out = len(in_specs), len(out_specs), len(hosted.args), len(hosted.out_shapes)

    def wrapped(*refs):
        core_in, host_in = refs[:n_in], refs[n_in:n_in + h_in]
        rest = refs[n_in + h_in:]
        core_out, host_out, sems = rest[:n_out], rest[n_out:n_out + h_out], rest[n_out + h_out:]
        ids = [pl.program_id(a) for a in range(len(grid))]
        first = functools.reduce(jnp.logical_and, [i == 0 for i in ids])
        last = functools.reduce(jnp.logical_and, [i == g - 1 for i, g in zip(ids, grid)])

        @pl.when(first)
        def _():
            hosted.start(host_in, host_out, sems)

        body(*core_in, *core_out)

        @pl.when(last)
        def _():
            hosted.finish(host_in, host_out, sems)

    outs = pl.pallas_call(
        wrapped, name=name, grid=grid, in_specs=list(in_specs) + [ANY] * h_in, out_specs=list(out_specs) + [ANY] * h_out,
        out_shape=list(out_shape) + hosted.out_shapes, scratch_shapes=hosted.sems,
        compiler_params=pltpu.CompilerParams(dimension_semantics=("arbitrary",) * len(grid),
                                             vmem_limit_bytes=VMEM_LIMIT_BYTES, has_side_effects=True))(
                                                 *args, *hosted.args)
    return outs[:n_out], outs[n_out:]


def _pair_weights(tq, band, transposed):
    d = np.arange(tq)[:, None] - np.arange(tq)[None, :]
    d = (d.T if transposed else d)[None] + (np.arange(band + 1) * tq)[:, None, None]
    w = sum(((d >= 0) & (d <= win) & (d % dil == 0)).astype(np.float32) for win, dil in DILATED_PAIRS)
    return jnp.asarray(w, F32)


def _att_geometry(t, mode):
    tq = _tile(t, ATT_TILE, LANES)
    nq = t // tq
    band = nq - 1 if mode == "fox" else min(DILATED_PAIRS[-1][0] // tq, nq - 1)
    return tq, nq, band


def _qkv_prep(z, *, off, name):
    t = z.shape[0]
    tb = _tile(t, 512, SUBLANES)
    cb = off // GROUP

    def body(z_ref, o_ref):
        scale = jnp.where(pl.program_id(1) == 0, ATT_SCALE, 1.0)
        o_ref[...] = (z_ref[...] * scale).astype(o_ref.dtype)

    return pl.pallas_call(
        body, name=name, grid=(t // tb, 3), in_specs=[pl.BlockSpec((tb, GROUP), lambda i, j: (i, j + cb))],
        out_specs=pl.BlockSpec((tb, GROUP), lambda i, j: (i, j)), out_shape=_sds((t, 3 * GROUP), MM_DTYPE),
        compiler_params=_params("parallel", "parallel"))(z)


def _block(ref, j, tq):
    return ref[pl.ds(pl.multiple_of(j * tq, tq), tq), :]


def _lane_block(ref, j, tq):
    return ref[0, :, pl.ds(pl.multiple_of(j * tq, tq), tq)]


def _att_tile(mode, q, kj, cq, ckj, causal, w):
    s = _dg(q, kj, 1, 1, False)
    if mode == "fox":
        s = s + (cq - ckj)
        return s if causal is None else jnp.where(causal, s, NEG)
    return jnp.where(w > 0.0, s, NEG)


def _att_fwd(qkv, cq, ck, *, mode, name, hosted=None):
    t = qkv.shape[0]
    tq, nq, band = _att_geometry(t, mode)
    weights = _pair_weights(tq, band if mode == "dil" else 0, False)

    def body(q_ref, k_ref, v_ref, cq_ref, ck_ref, w_ref, o_ref, lse_ref):
        i = pl.program_id(1)
        q, cqv = q_ref[...], cq_ref[0]
        causal = (lax.broadcasted_iota(jnp.int32, (tq, tq), 0) >= lax.broadcasted_iota(jnp.int32, (tq, tq), 1))

        def step(j, carry, masked):
            m_old, l_old, acc = carry
            w = w_ref[i - j] if mode == "dil" else None
            s = _att_tile(mode, q, _block(k_ref, j, tq), cqv, _lane_block(ck_ref, j, tq), causal if masked else None, w)
            m_new = jnp.maximum(m_old, jnp.max(s, axis=-1, keepdims=True))
            alpha = jnp.exp(m_old - m_new)
            p = jnp.exp(s - m_new)
            if w is not None:
                p = p * w
            l_new = alpha * l_old + jnp.sum(p, axis=-1, keepdims=True)
            return m_new, l_new, alpha * acc + _dg(p, _block(v_ref, j, tq), 1, 0, False)

        carry = (jnp.full((tq, 1), NEG, F32), jnp.zeros((tq, 1), F32), jnp.zeros((tq, HEAD), F32))
        if mode == "fox":
            carry = lax.fori_loop(0, i, lambda j, c: step(j, c, False), carry)
            carry = step(i, carry, True)
        else:
            carry = lax.fori_loop(jnp.maximum(i - band, 0), i + 1, lambda j, c: step(j, c, True), carry)
        m_fin, l_fin, acc = carry
        o_ref[...] = acc / l_fin
        lse_ref[0] = m_fin + jnp.log(l_fin)

    col = pl.BlockSpec((1, tq, 1), lambda h, i: (h, i, 0))
    in_specs = [
        pl.BlockSpec((tq, HEAD), lambda h, i: (i, h)),
        pl.BlockSpec((t, HEAD), lambda h, i: (0, N_HEADS + h)),
        pl.BlockSpec((t, HEAD), lambda h, i: (0, 2 * N_HEADS + h)),
        col,
        pl.BlockSpec((1, 1, t), lambda h, i: (h, 0, 0)),
        pl.BlockSpec(weights.shape, lambda h, i: (0, 0, 0)),
    ]
    return _grid_call(
        body, name=name, grid=(N_HEADS, nq), in_specs=in_specs,
        out_specs=[pl.BlockSpec((tq, HEAD), lambda h, i: (i, h)), col],
        out_shape=[_sds((t, GROUP)), _sds((N_HEADS, t, 1))], args=(qkv, qkv, qkv, cq, ck, weights), hosted=hosted)


def _att_bwd_q(qkv, cq, ck, do, o, lse, *, mode, name):
    t = qkv.shape[0]
    tq, nq, band = _att_geometry(t, mode)
    weights = _pair_weights(tq, band if mode == "dil" else 0, False)

    def body(q_ref, k_ref, v_ref, cq_ref, ck_ref, w_ref, do_ref, o_ref, lse_ref, dq_ref, dcq_ref, delta_ref):
        i = pl.program_id(1)
        q, cqv, lse = q_ref[...], cq_ref[0], lse_ref[0]
        dov = do_ref[...]
        delta = jnp.sum(dov * o_ref[...], axis=-1, keepdims=True)
        do16 = dov.astype(MM_DTYPE)
        causal = (lax.broadcasted_iota(jnp.int32, (tq, tq), 0) >= lax.broadcasted_iota(jnp.int32, (tq, tq), 1))

        def step(j, carry, masked):
            dq, dcq = carry
            kj = _block(k_ref, j, tq)
            w = w_ref[i - j] if mode == "dil" else None
            s = _att_tile(mode, q, kj, cqv, _lane_block(ck_ref, j, tq), causal if masked else None, w)
            p = jnp.exp(s - lse)
            if w is not None:
                p = p * w
            ds = p * (_dg(do16, _block(v_ref, j, tq), 1, 1, False) - delta)
            return dq + _dg(ds, kj, 1, 0, False), dcq + jnp.sum(ds, axis=-1, keepdims=True)

        carry = (jnp.zeros((tq, HEAD), F32), jnp.zeros((tq, 1), F32))
        if mode == "fox":
            carry = lax.fori_loop(0, i, lambda j, c: step(j, c, False), carry)
            carry = step(i, carry, True)
        else:
            carry = lax.fori_loop(jnp.maximum(i - band, 0), i + 1, lambda j, c: step(j, c, True), carry)
        dq_ref[...] = (carry[0] * ATT_SCALE).astype(dq_ref.dtype)
        dcq_ref[0] = carry[1]
        delta_ref[0] = delta

    col = pl.BlockSpec((1, tq, 1), lambda h, i: (h, i, 0))
    row = pl.BlockSpec((tq, HEAD), lambda h, i: (i, h))
    in_specs = [
        row,
        pl.BlockSpec((t, HEAD), lambda h, i: (0, N_HEADS + h)),
        pl.BlockSpec((t, HEAD), lambda h, i: (0, 2 * N_HEADS + h)),
        col,
        pl.BlockSpec((1, 1, t), lambda h, i: (h, 0, 0)),
        pl.BlockSpec(weights.shape, lambda h, i: (0, 0, 0)),
        row, row, col,
    ]
    return pl.pallas_call(
        body, name=name, grid=(N_HEADS, nq), in_specs=in_specs, out_specs=[row, col, col],
        out_shape=[_sds((t, GROUP), MM_DTYPE), _sds((N_HEADS, t, 1)), _sds((N_HEADS, t, 1))],
        compiler_params=_params("parallel", "parallel"))(qkv, qkv, qkv, cq, ck, weights, do, o, lse)


def _att_bwd_kv(qkv, cq, ck, do, lse_row, delta_row, *, mode, name, hosted=None):
    t = qkv.shape[0]
    tq, nq, band = _att_geometry(t, mode)
    weights = _pair_weights(tq, band if mode == "dil" else 0, True)

    def body(q_ref, k_ref, v_ref, cq_ref, ck_ref, w_ref, do_ref, lse_ref, delta_ref, dk_ref, dv_ref, dck_ref):
        jk = pl.program_id(1)
        kj, vj, ckv = k_ref[...], v_ref[...], cq_ref[0]
        causal = (lax.broadcasted_iota(jnp.int32, (tq, tq), 1) >= lax.broadcasted_iota(jnp.int32, (tq, tq), 0))

        def step(qi, carry, masked):
            dk, dv, dck = carry
            qb = _block(q_ref, qi, tq)
            s = _dg(kj, qb, 1, 1, False)
            w = None
            if mode == "fox":
                s = s + (_lane_block(ck_ref, qi, tq) - ckv)
                if masked:
                    s = jnp.where(causal, s, NEG)
            else:
                w = w_ref[qi - jk]
                s = jnp.where(w > 0.0, s, NEG)
            p = jnp.exp(s - _lane_block(lse_ref, qi, tq))
            if w is not None:
                p = p * w
            do16 = _block(do_ref, qi, tq).astype(MM_DTYPE)
            ds = p * (_dg(vj, do16, 1, 1, False) - _lane_block(delta_ref, qi, tq))
            return (dk + _dg(ds, qb, 1, 0, False), dv + _dg(p, do16, 1, 0, False),
                    dck - jnp.sum(ds, axis=-1, keepdims=True))

        carry = (jnp.zeros((tq, HEAD), F32), jnp.zeros((tq, HEAD), F32), jnp.zeros((tq, 1), F32))
        if mode == "fox":
            carry = step(jk, carry, True)
            carry = lax.fori_loop(jk + 1, nq, lambda qi, c: step(qi, c, False), carry)
        else:
            carry = lax.fori_loop(jk, jnp.minimum(jk + band, nq - 1) + 1, lambda qi, c: step(qi, c, True), carry)
        dk_ref[...] = carry[0].astype(dk_ref.dtype)
        dv_ref[...] = carry[1].astype(dv_ref.dtype)
        dck_ref[0] = carry[2]

    col = pl.BlockSpec((1, tq, 1), lambda h, j: (h, j, 0))
    lanes = pl.BlockSpec((1, 1, t), lambda h, j: (h, 0, 0))
    in_specs = [
        pl.BlockSpec((t, HEAD), lambda h, j: (0, h)),
        pl.BlockSpec((tq, HEAD), lambda h, j: (j, N_HEADS + h)),
        pl.BlockSpec((tq, HEAD), lambda h, j: (j, 2 * N_HEADS + h)),
        col, lanes,
        pl.BlockSpec(weights.shape, lambda h, j: (0, 0, 0)),
        pl.BlockSpec((t, HEAD), lambda h, j: (0, h)),
        lanes, lanes,
    ]
    out = pl.BlockSpec((tq, HEAD), lambda h, j: (j, h))
    return _grid_call(
        body, name=name, grid=(N_HEADS, nq), in_specs=in_specs, out_specs=[out, out, col],
        out_shape=[_sds((t, GROUP), MM_DTYPE), _sds((t, GROUP), MM_DTYPE), _sds((N_HEADS, t, 1))],
        args=(qkv, qkv, qkv, cq, ck, weights, do, lse_row, delta_row), hosted=hosted)


def _silu(x):
    return x * jax.nn.sigmoid(x)


def _l2n(x):
    return x * lax.rsqrt(jnp.sum(x * x, axis=-1, keepdims=True) + EPS)


def _gdn_pre(xqkv, gts):
    rows, c = xqkv.shape[0], GDN_CHUNK
    nb = rows // c
    ri = lax.broadcasted_iota(jnp.int32, (1, c, c), 1)
    ci = lax.broadcasted_iota(jnp.int32, (1, c, c), 2)
    tril, strict, eye = ri >= ci, ri > ci, ri == ci
    eyef = eye.astype(F32)
    last = lax.broadcasted_iota(jnp.int32, (1, c, 1), 1) == c - 1
    lane = lax.broadcasted_iota(jnp.int32, gts.shape, 1)
    to3 = lambda a: a.reshape(nb, c, a.shape[-1])
    to2 = lambda a: a.reshape(rows, a.shape[-1])
    bmm = lambda a, b: _mmf(a, b, "nn", False, True)
    bmm_nt = lambda a, b: _mmf(a, b, "nt", False, True)
    gcs = to2(_mmf(jnp.broadcast_to(tril.astype(F32), (nb, c, c)), to3(gts), "nn", True, True))
    us, ws, qgs, kds, qks, egls = [], [], [], [], [], []
    for h in range(N_HEADS):
        q = to3(_l2n(_silu(xqkv[:, h * HEAD:(h + 1) * HEAD])) * ATT_SCALE)
        k = to3(_l2n(_silu(xqkv[:, GROUP + h * HEAD:GROUP + (h + 1) * HEAD])))
        v = to3(_silu(xqkv[:, 2 * GROUP + h * HEAD:2 * GROUP + (h + 1) * HEAD]))
        beta = to3(jnp.sum(jnp.where(lane == G_BETA + h, gts, 0.0), axis=-1, keepdims=True))
        gc = to3(jnp.sum(jnp.where(lane == G_ALPHA + h, gcs, 0.0), axis=-1, keepdims=True))
        gr = jnp.sum(jnp.where(eye, jnp.broadcast_to(gc, (nb, c, c)), 0.0), axis=1, keepdims=True)
        decay = jnp.where(tril, jnp.exp(jnp.where(tril, gc - gr, 0.0)), 0.0)
        kbeta, vbeta = k * beta, v * beta
        low = jnp.where(strict, bmm_nt(kbeta, k) * decay, 0.0)
        inv, pw = eyef - low, bmm(low, low)
        for step in range(5):
            inv = inv + bmm(inv, pw)
            if step < 4:
                pw = bmm(pw, pw)
        eg = jnp.exp(gc)
        g_last = jnp.sum(jnp.where(last, gc, 0.0), axis=1, keepdims=True)
        us.append(to2(bmm(inv, vbeta)))
        ws.append(to2(bmm(inv, kbeta * eg)))
        qgs.append(to2(q * eg))
        kds.append(to2(k * jnp.exp(g_last - gc)))
        qks.append(to2(jnp.where(tril, bmm_nt(q, k) * decay, 0.0)))
        egls.append(jnp.broadcast_to(jnp.exp(g_last), (nb, 1, LANES)))
    cat = lambda parts: jnp.concatenate(parts, axis=1)
    return cat(us), cat(ws), cat(qgs), cat(kds), jnp.stack(qks, axis=0), jnp.stack(egls, axis=1)


def _gdn_seq(states, u, w, qg, kd, qk, egl):
    states = list(states)
    outs = []
    for r in range(u.shape[0] // GDN_CHUNK):
        rs = slice(r * GDN_CHUNK, (r + 1) * GDN_CHUNK)
        heads = []
        for h in range(N_HEADS):
            hs = slice(h * HEAD, (h + 1) * HEAD)
            s_in = states[h]
            v_new = u[rs, hs] - _mm(w[rs, hs], s_in)
            heads.append(_mm(qg[rs, hs], s_in) + _mm(qk[h, rs], v_new))
            states[h] = s_in * egl[r, h] + _mm_tn(kd[rs, hs], v_new)
        outs.append(jnp.concatenate(heads, axis=1))
    return jnp.concatenate(outs, axis=0), tuple(states)


def _gdn_out(o, zg, ng):
    ys = []
    for h in range(N_HEADS):
        oh = o[:, h * HEAD:(h + 1) * HEAD]
        ys.append(oh * lax.rsqrt(jnp.mean(oh * oh, axis=-1, keepdims=True) + EPS) * ng * _silu(zg[:, h * HEAD:(h + 1) * HEAD]))
    return jnp.concatenate(ys, axis=1)


GDN_PRE_ROWS = 8 * GDN_CHUNK
GDN_SEQ_ROWS = 4 * GDN_CHUNK


def _gdn_pre_specs(t):
    rows = _tile(t, GDN_PRE_ROWS, GDN_CHUNK)
    nb = rows // GDN_CHUNK
    wide = pl.BlockSpec((rows, GROUP), lambda i: (i, 0))
    ins = [pl.BlockSpec((rows, 3 * GROUP), lambda i: (i, 0)), pl.BlockSpec((rows, LANES), lambda i: (i, 0))]
    mids = [wide, wide, wide, wide, pl.BlockSpec((N_HEADS, rows, GDN_CHUNK), lambda i: (0, i, 0)),
            pl.BlockSpec((nb, N_HEADS, 1, LANES), lambda i: (i, 0, 0, 0))]
    shapes = [_sds((t, GROUP))] * 4 + [_sds((N_HEADS, t, GDN_CHUNK)), _sds((t // GDN_CHUNK, N_HEADS, 1, LANES))]
    return rows, ins, mids, shapes


def _gdn_pre_fwd(xqkv, gts, *, name):
    t = xqkv.shape[0]
    rows, ins, mids, shapes = _gdn_pre_specs(t)

    def body(x_ref, g_ref, *out_refs):
        for ref, val in zip(out_refs, _gdn_pre(x_ref[...], g_ref[...])):
            ref[...] = val

    return pl.pallas_call(body, name=name, grid=(t // rows,), in_specs=ins, out_specs=mids, out_shape=shapes,
                          compiler_params=_params("parallel"))(xqkv, gts)


def _gdn_pre_bwd(xqkv, gts, dmids, *, name):
    t = xqkv.shape[0]
    rows, ins, mids, _ = _gdn_pre_specs(t)

    def body(x_ref, g_ref, *refs):
        _, vjp = jax.vjp(_gdn_pre, x_ref[...], g_ref[...])
        dx, dg = vjp(tuple(r[...] for r in refs[:6]))
        refs[6][...] = dx
        refs[7][...] = dg

    return pl.pallas_call(body, name=name, grid=(t // rows,), in_specs=ins + mids, out_specs=ins,
                          out_shape=[_sds((t, 3 * GROUP)), _sds((t, LANES))],
                          compiler_params=_params("parallel"))(xqkv, gts, *dmids)


def _gdn_seq_specs(t, rev):
    rows = _tile(t, GDN_SEQ_ROWS, GDN_CHUNK)
    nb, n = rows // GDN_CHUNK, t // rows
    at = (lambda i: n - 1 - i) if rev else (lambda i: i)
    wide = pl.BlockSpec((rows, GROUP), lambda i: (at(i), 0))
    mids = [wide, wide, wide, wide, pl.BlockSpec((N_HEADS, rows, GDN_CHUNK), lambda i: (0, at(i), 0)),
            pl.BlockSpec((nb, N_HEADS, 1, LANES), lambda i: (at(i), 0, 0, 0))]
    state = pl.BlockSpec((1, N_HEADS, HEAD, HEAD), lambda i: (at(i), 0, 0, 0))
    return rows, n, wide, mids, state


def _gdn_seq_fwd(mids_in, *, name):
    t = mids_in[0].shape[0]
    rows, n, wide, mids, state = _gdn_seq_specs(t, False)

    def body(u_ref, w_ref, qg_ref, kd_ref, qk_ref, egl_ref, o_ref, sv_ref, s_ref):
        @pl.when(pl.program_id(0) == 0)
        def _():
            s_ref[...] = jnp.zeros_like(s_ref)

        sv_ref[0] = s_ref[...]
        o, new = _gdn_seq(tuple(s_ref[h] for h in range(N_HEADS)), u_ref[...], w_ref[...], qg_ref[...], kd_ref[...],
                          qk_ref[...], egl_ref[...])
        o_ref[...] = o
        for h in range(N_HEADS):
            s_ref[h] = new[h]

    return pl.pallas_call(
        body, name=name, grid=(n,), in_specs=mids, out_specs=[wide, state],
        out_shape=[_sds((t, GROUP)), _sds((n, N_HEADS, HEAD, HEAD))],
        scratch_shapes=[pltpu.VMEM((N_HEADS, HEAD, HEAD), F32)], compiler_params=_params("arbitrary"))(*mids_in)


def _gdn_seq_bwd(mids_in, states, do, *, name):
    t = mids_in[0].shape[0]
    rows, n, wide, mids, state = _gdn_seq_specs(t, True)

    def body(u_ref, w_ref, qg_ref, kd_ref, qk_ref, egl_ref, sv_ref, do_ref, *refs):
        d_refs, ds_ref = refs[:6], refs[6]

        @pl.when(pl.program_id(0) == 0)
        def _():
            ds_ref[...] = jnp.zeros_like(ds_ref)

        s_in = tuple(sv_ref[0, h] for h in range(N_HEADS))
        _, vjp = jax.vjp(_gdn_seq, s_in, u_ref[...], w_ref[...], qg_ref[...], kd_ref[...], qk_ref[...], egl_ref[...])
        grads = vjp((do_ref[...], tuple(ds_ref[h] for h in range(N_HEADS))))
        for ref, val in zip(d_refs, grads[1:]):
            ref[...] = val
        for h in range(N_HEADS):
            ds_ref[h] = grads[0][h]

    shapes = [_sds(m.shape) for m in mids_in]
    return pl.pallas_call(
        body, name=name, grid=(n,), in_specs=mids + [state, wide], out_specs=mids, out_shape=shapes,
        scratch_shapes=[pltpu.VMEM((N_HEADS, HEAD, HEAD), F32)],
        compiler_params=_params("arbitrary"))(*mids_in, states, do)


def _gdn_out_specs(t):
    rows = _tile(t, 512, SUBLANES)
    wide = pl.BlockSpec((rows, GROUP), lambda i: (i, 0))
    return rows, wide, pl.BlockSpec((rows, GROUP), lambda i: (i, Z_CZ // GROUP)), pl.BlockSpec((1, HEAD), lambda i: (0, 0))


def _gdn_out_fwd(o, z, ng, *, name):
    t = o.shape[0]
    rows, wide, zspec, vec = _gdn_out_specs(t)

    def body(o_ref, z_ref, ng_ref, y_ref):
        y_ref[...] = _gdn_out(o_ref[...], z_ref[...], ng_ref[...]).astype(y_ref.dtype)

    return pl.pallas_call(body, name=name, grid=(t // rows,), in_specs=[wide, zspec, vec], out_specs=wide,
                          out_shape=_sds((t, GROUP), MM_DTYPE), compiler_params=_params("parallel"))(o, z, ng)


def _gdn_out_bwd(o, z, ng, dy, *, name):
    t = o.shape[0]
    rows, wide, zspec, vec = _gdn_out_specs(t)

    def body(o_ref, z_ref, ng_ref, dy_ref, do_ref, dz_ref, dng_ref):
        _, vjp = jax.vjp(_gdn_out, o_ref[...], z_ref[...], ng_ref[...])
        do, dz, dng = vjp(dy_ref[...])
        do_ref[...] = do
        dz_ref[...] = dz.astype(dz_ref.dtype)

        @pl.when(pl.program_id(0) == 0)
        def _():
            dng_ref[...] = jnp.zeros_like(dng_ref)

        dng_ref[...] += dng

    do, dz, dng = pl.pallas_call(
        body, name=name, grid=(t // rows,), in_specs=[wide, zspec, vec, wide], out_specs=[wide, wide, vec],
        out_shape=[_sds((t, GROUP)), _sds((t, GROUP), MM_DTYPE), _sds((1, HEAD))],
        compiler_params=_params("arbitrary"))(o, z, ng, dy)
    return do, dz, dng.reshape(HEAD)


def _swiglu(up, gate):
    return _silu(gate) * up


def _conv_value(x_ref, p_ref, w_ref, b_ref, first):
    xv = x_ref[...]
    prev = jnp.where(first, 0.0, p_ref[...])
    acc = jnp.zeros_like(xv) + b_ref[...]
    kw = w_ref.shape[0]
    for k in range(kw):
        acc = acc + w_ref[k:k + 1, :] * _delayed(xv, prev, kw - 1 - k)
    return acc


def _ffn_act_specs(t, two_f, kw, col_of):
    dff = two_f // 2
    tb, tc = _tile(t, 512, SUBLANES), _tile(dff, 512, LANES)
    nh, n8 = dff // tc, tb // SUBLANES
    specs = []
    for half in (0, 1):
        col = lambda j, half=half: col_of(j, nh) + half * nh
        specs += [pl.BlockSpec((tb, tc), lambda j, i, col=col: (i, col(j))),
                  pl.BlockSpec((SUBLANES, tc), lambda j, i, col=col: (jnp.maximum(i * n8 - 1, 0), col(j))),
                  pl.BlockSpec((kw, tc), lambda j, i, col=col: (0, col(j))),
                  pl.BlockSpec((1, tc), lambda j, i, col=col: (0, col(j)))]
    return tb, tc, nh, specs


def _ffn_act_fwd(uu, w, b, *, name):
    t, two_f = uu.shape
    tb, tc, nh, specs = _ffn_act_specs(t, two_f, w.shape[0], lambda j, nh: j)

    def body(xu, pu, wu, bu, xg, pg, wg, bg, o_ref):
        first = pl.program_id(1) == 0
        o_ref[...] = _swiglu(_conv_value(xu, pu, wu, bu, first), _conv_value(xg, pg, wg, bg, first)).astype(o_ref.dtype)

    b2 = b.reshape(1, two_f)
    return pl.pallas_call(body, name=name, grid=(nh, t // tb), in_specs=specs,
                          out_specs=pl.BlockSpec((tb, tc), lambda j, i: (i, j)), out_shape=_sds((t, two_f // 2), MM_DTYPE),
                          compiler_params=_params("parallel", "parallel"))(uu, uu, w, b2, uu, uu, w, b2)


def _ffn_act_bwd(uu, w, b, dact, *, name):
    t, two_f = uu.shape
    tb, tc, nh, specs = _ffn_act_specs(t, two_f, w.shape[0], lambda j, nh: j % nh)

    def body(xu, pu, wu, bu, xg, pg, wg, bg, da_ref, o_ref):
        first = pl.program_id(1) == 0
        _, vjp = jax.vjp(_swiglu, _conv_value(xu, pu, wu, bu, first), _conv_value(xg, pg, wg, bg, first))
        dup, dgate = vjp(da_ref[...])
        o_ref[...] = jnp.where(pl.program_id(0) < nh, dup, dgate)

    b2 = b.reshape(1, two_f)
    return pl.pallas_call(body, name=name, grid=(2 * nh, t // tb),
                          in_specs=specs + [pl.BlockSpec((tb, tc), lambda j, i: (i, j % nh))],
                          out_specs=pl.BlockSpec((tb, tc), lambda j, i: (i, j)), out_shape=_sds((t, two_f)),
                          compiler_params=_params("parallel", "parallel"))(uu, uu, w, b2, uu, uu, w, b2, dact)


def _sum_slots(parts, *, out_dtype=F32, name):
    if not isinstance(parts, (list, tuple)):
        parts = [parts[s] for s in range(parts.shape[0])]
    r = parts[0].shape[0]
    rb = _tile(r, 2048, 2 * SUBLANES)
    spec = pl.BlockSpec((rb, LANES), lambda i: (i, 0))

    def body(*refs):
        acc = refs[0][...].astype(F32)
        for ref in refs[1:-1]:
            acc = acc + ref[...].astype(F32)
        refs[-1][...] = acc.astype(out_dtype)

    return pl.pallas_call(body, name=name, grid=(r // rb,), in_specs=[spec] * len(parts), out_specs=spec,
                          out_shape=_sds((r, LANES), out_dtype), compiler_params=_params("parallel"))(*parts)


def _adamw(w, g, m, v, *, name):
    r, c = w.shape
    rb = _tile(r, 128, SUBLANES)
    spec = pl.BlockSpec((rb, c), lambda i: (i, 0))

    def body(w_ref, g_ref, m_ref, v_ref, d_ref, nm_ref, nv_ref):
        gv = g_ref[...]
        mn = ADAM_B1 * m_ref[...] + (1.0 - ADAM_B1) * gv
        vn = ADAM_B2 * v_ref[...] + (1.0 - ADAM_B2) * (gv * gv)
        m_hat = mn / (1.0 - ADAM_B1 ** ADAM_STEP)
        v_hat = vn / (1.0 - ADAM_B2 ** ADAM_STEP)
        d_ref[...] = -ADAM_LR * (m_hat / (jnp.sqrt(v_hat) + ADAM_EPS) + ADAM_WD * w_ref[...])
        nm_ref[...] = mn
        nv_ref[...] = vn

    return pl.pallas_call(body, name=name, grid=(r // rb,), in_specs=[spec] * 4, out_specs=[spec] * 3,
                          out_shape=[_sds((r, c))] * 3, compiler_params=_params("parallel"))(w, g, m, v)


def _position():
    return lax.axis_index("x"), lax.axis_index("y"), lax.axis_index("c")


def _chip_gather(shards, *, name):
    n = len(shards)

    def body(*refs):
        ins, outs = refs[:n], refs[n:2 * n]
        send_sems, recv_sems, local_sems = refs[2 * n:]
        x, y, c = _position()
        mine = 2 * x + y
        chips = [(1 - x, y), (x, 1 - y), (1 - x, 1 - y)]
        local = [pltpu.make_async_copy(ins[a], outs[a].at[mine], local_sems.at[a]) for a in range(n)]
        for cp in local:
            cp.start()
        sends = []
        for a in range(n):
            for r, (px, py) in enumerate(chips):
                sends.append(pltpu.make_async_remote_copy(
                    src_ref=ins[a], dst_ref=outs[a].at[mine], send_sem=send_sems.at[3 * a + r],
                    recv_sem=recv_sems.at[3 * a + r], device_id=(px, py, c), device_id_type=MESH_ID))
        for cp in sends:
            cp.start()
        for a in range(n):
            for r, (px, py) in enumerate(chips):
                pltpu.make_async_remote_copy(
                    src_ref=ins[a], dst_ref=outs[a].at[2 * px + py], send_sem=send_sems.at[3 * a + r],
                    recv_sem=recv_sems.at[3 * a + r], device_id=(px, py, c), device_id_type=MESH_ID).wait_recv()
        for cp in sends:
            cp.wait_send()
        for cp in local:
            cp.wait()

    return pl.pallas_call(
        body, name=name, in_specs=[ANY] * n, out_specs=[ANY] * n,
        out_shape=[_sds((4,) + s.shape, s.dtype) for s in shards],
        scratch_shapes=[pltpu.SemaphoreType.DMA((3 * n,)), pltpu.SemaphoreType.DMA((3 * n,)),
                        pltpu.SemaphoreType.DMA((n,))],
        compiler_params=pltpu.CompilerParams(has_side_effects=True))(*shards)


def _gather_halves_step(shards):
    n = len(shards)

    def copy(ins, outs, send_sems, recv_sems, a, r, chip_of_block, half, to, second):
        k = (3 * n if second else 0) + 3 * a + r
        px, py = chip_of_block
        src = outs[a].at[2 * px + py, half] if second else ins[a].at[half]
        return pltpu.make_async_remote_copy(
            src_ref=src, dst_ref=outs[a].at[2 * px + py, half], send_sem=send_sems.at[k], recv_sem=recv_sems.at[k],
            device_id=to, device_id_type=MESH_ID)

    def first_copies(ins, outs, sems):
        x, y, c = _position()
        chips = [(1 - x, y), (x, 1 - y), (1 - x, 1 - y)]
        return [copy(ins, outs, *sems, a, r, (x, y), c, (px, py, c), False) for a in range(n)
                for r, (px, py) in enumerate(chips)]

    def start(ins, outs, sems):
        for cp in first_copies(ins, outs, sems):
            cp.start()

    def finish(ins, outs, sems):
        x, y, c = _position()
        chips = [(1 - x, y), (x, 1 - y), (1 - x, 1 - y)]
        passed = []
        for a in range(n):
            for r, chip in enumerate(chips):
                copy(ins, outs, *sems, a, r, chip, c, (x, y, c), False).wait_recv()
                passed.append(copy(ins, outs, *sems, a, r, chip, c, (x, y, 1 - c), True))
                passed[-1].start()
        for a in range(n):
            for r, chip in enumerate(chips):
                copy(ins, outs, *sems, a, r, chip, 1 - c, (x, y, c), True).wait_recv()
        for cp in first_copies(ins, outs, sems) + passed:
            cp.wait_send()

    return _Hosted(shards, [_sds((4,) + s.shape, s.dtype) for s in shards],
                   [pltpu.SemaphoreType.DMA((6 * n,)), pltpu.SemaphoreType.DMA((6 * n,))], start, finish)


def _exchange_step(v):
    def copies(ins, outs, sems):
        x, y, c = _position()
        mine = 2 * x + y
        return [pltpu.make_async_remote_copy(
            src_ref=ins[0].at[2 * px + py], dst_ref=outs[0].at[mine], send_sem=sems[0].at[r], recv_sem=sems[1].at[r],
            device_id=(px, py, c), device_id_type=MESH_ID) for r, (px, py) in enumerate([(1 - x, y), (x, 1 - y), (1 - x, 1 - y)])]

    def start(ins, outs, sems):
        for cp in copies(ins, outs, sems):
            cp.start()

    def finish(ins, outs, sems):
        x, y, c = _position()
        mine = 2 * x + y
        for r, (px, py) in enumerate([(1 - x, y), (x, 1 - y), (1 - x, 1 - y)]):
            pltpu.make_async_remote_copy(
                src_ref=ins[0].at[mine], dst_ref=outs[0].at[2 * px + py], send_sem=sems[0].at[r], recv_sem=sems[1].at[r],
                device_id=(px, py, c), device_id_type=MESH_ID).wait_recv()
        for cp in copies(ins, outs, sems):
            cp.wait_send()

    return _Hosted([v], [_sds(v.shape, v.dtype)], [pltpu.SemaphoreType.DMA((3,)), pltpu.SemaphoreType.DMA((3,))],
                   start, finish)


def _run_step(step, *, name):
    n = len(step.args)

    def body(*refs):
        ins, outs, sems = refs[:n], refs[n:n + len(step.out_shapes)], refs[n + len(step.out_shapes):]
        step.start(ins, outs, sems)
        step.finish(ins, outs, sems)

    return pl.pallas_call(body, name=name, in_specs=[ANY] * n, out_specs=[ANY] * len(step.out_shapes),
                          out_shape=step.out_shapes, scratch_shapes=step.sems,
                          compiler_params=pltpu.CompilerParams(has_side_effects=True))(*step.args)


def _place_own(results, own):
    x, y, _ = _position()
    return [lax.dynamic_update_index_in_dim(r, o, 2 * x + y, 0) for r, o in zip(results, own)]


def _sibling_send(v, *, name):
    def body(v_ref, got_ref, send_sem, recv_sem):
        x, y, c = _position()
        cp = pltpu.make_async_remote_copy(src_ref=v_ref, dst_ref=got_ref, send_sem=send_sem, recv_sem=recv_sem,
                                          device_id=(x, y, 1 - c), device_id_type=MESH_ID)
        cp.start()
        cp.wait()

    return pl.pallas_call(
        body, name=name, in_specs=[ANY], out_specs=ANY, out_shape=_sds(v.shape, v.dtype),
        scratch_shapes=[pltpu.SemaphoreType.DMA, pltpu.SemaphoreType.DMA],
        compiler_params=pltpu.CompilerParams(has_side_effects=True))(v)


def _all_sum(v, *, name):
    r = v.shape[0]
    masks = [(mx, my, mc) for mx in (0, 1) for my in (0, 1) for mc in (0, 1)][1:]

    def body(v_ref, out_ref, slots, send_sems, recv_sems, local_sem):
        x, y, c = _position()
        me = 4 * x + 2 * y + c

        def peer(mask):
            return tuple(1 - p if bit else p for p, bit in zip((x, y, c), mask))

        local = pltpu.make_async_copy(v_ref, slots.at[me], local_sem)
        local.start()
        sends = [pltpu.make_async_remote_copy(
            src_ref=v_ref, dst_ref=slots.at[me], send_sem=send_sems.at[k], recv_sem=recv_sems.at[k],
            device_id=peer(mask), device_id_type=MESH_ID) for k, mask in enumerate(masks)]
        for cp in sends:
            cp.start()
        for k, mask in enumerate(masks):
            px, py, pc = peer(mask)
            pltpu.make_async_remote_copy(
                src_ref=v_ref, dst_ref=slots.at[4 * px + 2 * py + pc], send_sem=send_sems.at[k],
                recv_sem=recv_sems.at[k], device_id=(px, py, pc), device_id_type=MESH_ID).wait_recv()
        for cp in sends:
            cp.wait_send()
        local.wait()
        acc = slots[0]
        for s in range(1, 8):
            acc = acc + slots[s]
        out_ref[...] = acc

    vm = pl.BlockSpec(memory_space=pltpu.VMEM)
    return pl.pallas_call(
        body, name=name, in_specs=[vm], out_specs=vm, out_shape=_sds((r, LANES)),
        scratch_shapes=[pltpu.VMEM((8, r, LANES), F32), pltpu.SemaphoreType.DMA((7,)), pltpu.SemaphoreType.DMA((7,)),
                        pltpu.SemaphoreType.DMA],
        compiler_params=pltpu.CompilerParams(vmem_limit_bytes=VMEM_LIMIT_BYTES, has_side_effects=True))(v)


def _pack_rows(arrays, align=SUBLANES * LANES):
    flat = jnp.concatenate([a.reshape(-1) for a in arrays])
    n = flat.shape[0]
    pad = (-n) % align
    if pad:
        flat = jnp.concatenate([flat, jnp.zeros((pad,), flat.dtype)])
    return flat.reshape(-1, LANES), [a.shape for a in arrays]


def _unpack_rows(rows, shapes):
    flat = rows.reshape(-1)
    out, off = [], 0
    for s in shapes:
        n = int(np.prod(s))
        out.append(flat[off:off + n].reshape(s))
        off += n
    return out


def _pad_w_in(w):
    d = w.shape[0]
    return jnp.concatenate([w[:, 0:2560], w[:, 2564:4612], w[:, 4620:6156], w[:, 2560:2564], w[:, 4612:4620],
                            jnp.zeros((d, Z_COLS - IN_COLS), w.dtype)], axis=1)


def _unpad_w_in(g):
    return jnp.concatenate([g[:, 0:2560], g[:, 6144:6148], g[:, 2560:4608], g[:, 6148:6156], g[:, 4608:6144]], axis=1)


def _gate_rows(p):
    return (_lane_row(p["fox_f_bias"], G_F), _lane_row(p["gdn_a_log"], G_ALPHA), _lane_row(p["gdn_dt_bias"], G_ALPHA))


def _head_cols(c_rows, t):
    ct = c_rows[:, :N_HEADS].T
    return ct.reshape(N_HEADS, t, 1), ct.reshape(N_HEADS, 1, t)


def _layer_fwd(x, p, hosted=None):
    t = x.shape[0]
    s = {"x": x}
    s["h"] = _rmsnorm_fwd(x, p["norm_mix"], name="mix_norm")
    z = s["z"] = _matmul(s["h"], p["w_in"], name="in_proj")
    s["xc"] = _conv_fwd(z, p["lru_conv_w"], p["lru_conv_b"], ncols=GROUP, coff=Z_AX, name="lru_conv")
    lru = s["lru"] = (_block_diag(p["lru_wa"]).astype(MM_DTYPE), p["lru_ba"].reshape(1, GROUP),
                      _block_diag(p["lru_wx"]).astype(MM_DTYPE), p["lru_bx"].reshape(1, GROUP),
                      p["lru_lambda"].reshape(1, GROUP))
    s["a"], u = _lru_gates_fwd(s["xc"], *lru, name="lru_gates")
    s["ha"] = _scan(s["a"], u, name="lru_scan")
    s["gate_a"] = z[:, Z_AG:Z_AG + GROUP].reshape(t * LRU_BLOCKS, LRU_BD)
    y_a = _post_fwd(s["ha"].reshape(t * LRU_BLOCKS, LRU_BD), p["norm_a"].reshape(LRU_BLOCKS, LRU_BD), s["gate_a"],
                    name="lru_post")
    s["gts"] = _gates_fwd(z, *_gate_rows(p), name="gates")
    s["cq"], s["ck"] = _head_cols(_scan(None, s["gts"], name="fox_cumsum"), t)
    s["qkv_b"] = _qkv_prep(z, off=Z_BQ, name="fox_prep")
    (s["ob"], s["lse_b"]), hosted_out = _att_fwd(s["qkv_b"], s["cq"], s["ck"], mode="fox", name="fox_att", hosted=hosted)
    gain_b = jnp.tile(p["norm_b"].reshape(N_HEADS, HEAD), (2, 1))
    y_b = _post_fwd(s["ob"].reshape(t * N_HEADS, HEAD), gain_b, None, name="fox_post")
    s["cconv"] = _conv_fwd(z, p["gdn_conv_w"], jnp.zeros((3 * GROUP,), F32), ncols=3 * GROUP, coff=Z_CQKV,
                           name="gdn_conv")
    s["gdn_mids"] = _gdn_pre_fwd(s["cconv"], s["gts"], name="gdn_pre")
    s["oc"], s["gdn_states"] = _gdn_seq_fwd(s["gdn_mids"], name="gdn_seq")
    y_c = _gdn_out_fwd(s["oc"], z, p["gdn_norm"].reshape(1, HEAD), name="gdn_out")
    s["qkv_d"] = _qkv_prep(z, off=Z_DQ, name="dil_prep")
    (s["od"], s["lse_d"]), _ = _att_fwd(s["qkv_d"], s["cq"], s["ck"], mode="dil", name="dil_att")
    gain_d = jnp.tile(p["norm_d"].reshape(N_HEADS, HEAD), (2, 1))
    y_d = _post_fwd(s["od"].reshape(t * N_HEADS, HEAD), gain_d, None, name="dil_post")
    y = s["y"] = jnp.concatenate([y_a.reshape(t, GROUP), y_b.reshape(t, GROUP), y_c, y_d.reshape(t, GROUP)], axis=1)
    x1 = s["x1"] = _matmul(y, p["w_out"], add=x, name="out_proj")
    s["h2"] = _rmsnorm_fwd(x1, p["norm_ffn"], name="ffn_norm")
    s["uu"] = _matmul(s["h2"], p["ffn_w_up"], name="ffn_up")
    s["act"] = _ffn_act_fwd(s["uu"], p["ffn_conv_w"], p["ffn_conv_b"], name="ffn_conv_swiglu")
    return _matmul(s["act"], p["ffn_w_down"], add=x1, name="ffn_down"), s, hosted_out


def _layer_bwd(dx2, p, s, hosted=None):
    t = dx2.shape[0]
    g = {}
    dact = _matmul(dx2, p["ffn_w_down"], form="nt", name="ffn_down_dx")
    g["ffn_w_down"] = _matmul(s["act"], dx2, form="tn", name="ffn_down_dw")
    du = _ffn_act_bwd(s["uu"], p["ffn_conv_w"], p["ffn_conv_b"], dact, name="ffn_conv_swiglu_bwd")
    duu, g["ffn_conv_w"], g["ffn_conv_b"] = _conv_bwd(s["uu"], p["ffn_conv_w"], du, ncols=du.shape[1],
                                                      name="ffn_conv_bwd")
    dh2 = _matmul(duu, p["ffn_w_up"], form="nt", name="ffn_up_dx")
    g["ffn_w_up"] = _matmul(s["h2"], duu, form="tn", name="ffn_up_dw")
    dx1, g["norm_ffn"] = _rmsnorm_bwd(s["x1"], p["norm_ffn"], dh2, dx2, name="ffn_norm_bwd")
    dy = _matmul(dx1, p["w_out"], form="nt", name="out_proj_dx")
    g["w_out"] = _matmul(s["y"], dx1, form="tn", name="out_proj_dw")
    z = s["z"]
    dha, dgn, dgate_a = _post_bwd(s["ha"].reshape(t * LRU_BLOCKS, LRU_BD), p["norm_a"].reshape(LRU_BLOCKS, LRU_BD),
                                  s["gate_a"], dy[:, 0:GROUP].reshape(t * LRU_BLOCKS, LRU_BD), name="lru_post_bwd")
    g["norm_a"] = dgn.reshape(GROUP)
    gsc = _scan(_shift_up(s["a"]), dha.reshape(t, GROUP), reverse=True, name="lru_scan_bwd")
    dxc, dwa, dba, dwx, dbx, dlam = _lru_gates_bwd(s["xc"], *s["lru"], gsc, _shift_down(s["ha"]), name="lru_gates_bwd")
    g["lru_wa"], g["lru_wx"] = _diag_blocks(dwa), _diag_blocks(dwx)
    g["lru_ba"], g["lru_bx"], g["lru_lambda"] = dba.reshape(GROUP), dbx.reshape(GROUP), dlam.reshape(GROUP)
    dax, g["lru_conv_w"], g["lru_conv_b"] = _conv_bwd(z, p["lru_conv_w"], dxc, ncols=GROUP, coff=Z_AX,
                                                      name="lru_conv_bwd")
    gain_b = jnp.tile(p["norm_b"].reshape(N_HEADS, HEAD), (2, 1))
    dob, dgn, _ = _post_bwd(s["ob"].reshape(t * N_HEADS, HEAD), gain_b, None,
                            dy[:, GROUP:2 * GROUP].reshape(t * N_HEADS, HEAD), name="fox_post_bwd")
    g["norm_b"] = (dgn[:N_HEADS] + dgn[N_HEADS:]).reshape(GROUP)
    dob = dob.reshape(t, GROUP)
    dbq, dcq, delta = _att_bwd_q(s["qkv_b"], s["cq"], s["ck"], dob, s["ob"], s["lse_b"], mode="fox", name="fox_att_dq")
    (dbk, dbv, dck), hosted_out = _att_bwd_kv(s["qkv_b"], s["cq"], s["ck"], dob, s["lse_b"].reshape(N_HEADS, 1, t),
                                              delta.reshape(N_HEADS, 1, t), mode="fox", name="fox_att_dkv", hosted=hosted)
    pad_lanes = ((0, 0), (0, LANES - N_HEADS))
    dc_rows = _sum_slots([jnp.pad(dcq.reshape(N_HEADS, t).T, pad_lanes), jnp.pad(dck.reshape(N_HEADS, t).T, pad_lanes)],
                         name="fox_dc_sum")
    dgts_fox = _scan(None, dc_rows, reverse=True, name="fox_cumsum_bwd")
    gain_d = jnp.tile(p["norm_d"].reshape(N_HEADS, HEAD), (2, 1))
    dod, dgn, _ = _post_bwd(s["od"].reshape(t * N_HEADS, HEAD), gain_d, None,
                            dy[:, 3 * GROUP:4 * GROUP].reshape(t * N_HEADS, HEAD), name="dil_post_bwd")
    g["norm_d"] = (dgn[:N_HEADS] + dgn[N_HEADS:]).reshape(GROUP)
    dod = dod.reshape(t, GROUP)
    ddq, _, delta = _att_bwd_q(s["qkv_d"], s["cq"], s["ck"], dod, s["od"], s["lse_d"], mode="dil", name="dil_att_dq")
    (ddk, ddv, _), _ = _att_bwd_kv(s["qkv_d"], s["cq"], s["ck"], dod, s["lse_d"].reshape(N_HEADS, 1, t),
                                   delta.reshape(N_HEADS, 1, t), mode="dil", name="dil_att_dkv")
    doc, dcz, g["gdn_norm"] = _gdn_out_bwd(s["oc"], z, p["gdn_norm"].reshape(1, HEAD), dy[:, 2 * GROUP:3 * GROUP],
                                           name="gdn_out_bwd")
    dmids = _gdn_seq_bwd(s["gdn_mids"], s["gdn_states"], doc, name="gdn_seq_bwd")
    dcconv, dgts_gdn = _gdn_pre_bwd(s["cconv"], s["gts"], dmids, name="gdn_pre_bwd")
    dcqkv, g["gdn_conv_w"], _ = _conv_bwd(z, p["gdn_conv_w"], dcconv, ncols=3 * GROUP, coff=Z_CQKV,
                                          name="gdn_conv_bwd")
    dgts = _sum_slots([dgts_fox, dgts_gdn], name="gates_dsum")
    dzg, dfb, dal, ddt = _gates_bwd(z, *_gate_rows(p), dgts, name="gates_bwd")
    g["fox_f_bias"] = dfb[0, G_F:G_F + N_HEADS]
    g["gdn_a_log"] = dal[0, G_ALPHA:G_ALPHA + N_HEADS]
    g["gdn_dt_bias"] = ddt[0, G_ALPHA:G_ALPHA + N_HEADS]
    dz = jnp.concatenate([dax, dgate_a.reshape(t, GROUP), dbq, dbk, dbv, dcqkv, dcz, ddq, ddk, ddv, dzg], axis=1)
    dh = _matmul(dz, p["w_in"], form="nt", name="in_proj_dx")
    g["w_in"] = _matmul(s["h"], dz, form="tn", name="in_proj_dw")
    dx, g["norm_mix"] = _rmsnorm_bwd(s["x"], p["norm_mix"], dh, dx1, name="mix_norm_bwd")
    return dx, g, hosted_out


def _local_step(x, tgt, layers, norm_final):
    saved = []
    for p in layers:
        x, s, _ = _layer_fwd(x, p)
        saved.append(s)
    loss, dx, dnf = _loss_head(x, norm_final, tgt, name="loss_head")
    grads = []
    for p, s in zip(reversed(layers), reversed(saved)):
        dx, g, _ = _layer_bwd(dx, p, s)
        grads.append(g)
    return loss, dx, grads[::-1], dnf


BIG = ("w_in", "w_out", "ffn_w_up", "ffn_w_down")
PACK_ROWS = 4096
SHARDED_SMALL = ("lru_conv_w", "gdn_conv_w", "ffn_conv_w")
NAMES = ("norm_mix", "w_in", "lru_conv_w", "lru_conv_b", "lru_wa", "lru_ba", "lru_wx", "lru_bx", "lru_lambda",
         "fox_f_bias", "gdn_conv_w", "gdn_a_log", "gdn_dt_bias", "gdn_norm", "norm_a", "norm_b", "norm_d", "w_out",
         "norm_ffn", "ffn_w_up", "ffn_conv_w", "ffn_conv_b", "ffn_w_down", "norm_final")
SMALL = tuple(n for n in NAMES if n not in BIG)


def _big_pieces(g, k_axis_cols):
    if k_axis_cols:
        d, n = g.shape
        return g.reshape(d, 4, n // 4).transpose(1, 0, 2).reshape(4, -1, LANES)
    return g.reshape(4, -1, LANES)


def _reduce_start(gl, *, c):
    whole = {"w_in": _unpad_w_in(gl["w_in"]), "w_out": gl["w_out"], "ffn_w_up": gl["ffn_w_up"],
             "ffn_w_down": gl["ffn_w_down"]}
    cols = {"w_in": True, "w_out": False, "ffn_w_up": True, "ffn_w_down": False}
    pieces = [_big_pieces(whole[n], cols[n]) for n in BIG]
    rows = [q.shape[1] for q in pieces]
    pad = (-sum(rows)) % PACK_ROWS
    packed = jnp.concatenate(pieces + [jnp.zeros((4, pad, LANES), F32)], axis=1)
    half = packed.shape[1] // 2
    halves = packed.reshape(4, 2, half, LANES)
    mine = lax.dynamic_index_in_dim(halves, c, axis=1, keepdims=False).reshape(4 * half, LANES)
    other = lax.dynamic_index_in_dim(halves, 1 - c, axis=1, keepdims=False).reshape(4 * half, LANES)
    got = _sibling_send(other, name="grad_sibling_send")
    chip_sum = _sum_slots([mine, got], out_dtype=MM_DTYPE, name="grad_sibling_sum").reshape(4, half, LANES)
    return chip_sum, rows


def _reduce_finish(chip_sum, from_chips, rows, *, c):
    x, y, _ = _position()
    (from_chips,) = _place_own([from_chips], [lax.dynamic_index_in_dim(chip_sum, 2 * x + y, 0, keepdims=False)])
    total_half = _sum_slots(from_chips, name="grad_chip_sum")
    other_half = _sibling_send(total_half, name="grad_sibling_swap")
    total = jnp.where(c == 0, jnp.concatenate([total_half, other_half]), jnp.concatenate([other_half, total_half]))
    out, off = {}, 0
    for n, r in zip(BIG, rows):
        out[n] = total[off:off + r]
        off += r
    return out


def kernel(x, norm_mix, w_in, lru_conv_w, lru_conv_b, lru_wa, lru_ba, lru_wx, lru_bx, lru_lambda, fox_f_bias, gdn_conv_w, gdn_a_log, gdn_dt_bias, gdn_norm, norm_a, norm_b, norm_d, w_out, norm_ffn, ffn_w_up, ffn_conv_w, ffn_conv_b, ffn_w_down, norm_final, loss_target, m_norm_mix, m_w_in, m_lru_conv_w, m_lru_conv_b, m_lru_wa, m_lru_ba, m_lru_wx, m_lru_bx, m_lru_lambda, m_fox_f_bias, m_gdn_conv_w, m_gdn_a_log, m_gdn_dt_bias, m_gdn_norm, m_norm_a, m_norm_b, m_norm_d, m_w_out, m_norm_ffn, m_ffn_w_up, m_ffn_conv_w, m_ffn_conv_b, m_ffn_w_down, m_norm_final, v_norm_mix, v_w_in, v_lru_conv_w, v_lru_conv_b, v_lru_wa, v_lru_ba, v_lru_wx, v_lru_bx, v_lru_lambda, v_fox_f_bias, v_gdn_conv_w, v_gdn_a_log, v_gdn_dt_bias, v_gdn_norm, v_norm_a, v_norm_b, v_norm_d, v_w_out, v_norm_ffn, v_ffn_w_up, v_ffn_conv_w, v_ffn_conv_b, v_ffn_w_down, v_norm_final):
    w = dict(zip(NAMES, (norm_mix, w_in, lru_conv_w, lru_conv_b, lru_wa, lru_ba, lru_wx, lru_bx, lru_lambda, fox_f_bias,
                         gdn_conv_w, gdn_a_log, gdn_dt_bias, gdn_norm, norm_a, norm_b, norm_d, w_out, norm_ffn, ffn_w_up,
                         ffn_conv_w, ffn_conv_b, ffn_w_down, norm_final)))
    m = dict(zip(NAMES, (m_norm_mix, m_w_in, m_lru_conv_w, m_lru_conv_b, m_lru_wa, m_lru_ba, m_lru_wx, m_lru_bx,
                         m_lru_lambda, m_fox_f_bias, m_gdn_conv_w, m_gdn_a_log, m_gdn_dt_bias, m_gdn_norm, m_norm_a,
                         m_norm_b, m_norm_d, m_w_out, m_norm_ffn, m_ffn_w_up, m_ffn_conv_w, m_ffn_conv_b, m_ffn_w_down,
                         m_norm_final)))
    v = dict(zip(NAMES, (v_norm_mix, v_w_in, v_lru_conv_w, v_lru_conv_b, v_lru_wa, v_lru_ba, v_lru_wx, v_lru_bx,
                         v_lru_lambda, v_fox_f_bias, v_gdn_conv_w, v_gdn_a_log, v_gdn_dt_bias, v_gdn_norm, v_norm_a,
                         v_norm_b, v_norm_d, v_w_out, v_norm_ffn, v_ffn_w_up, v_ffn_conv_w, v_ffn_conv_b, v_ffn_w_down,
                         v_norm_final)))
    depth = w_in.shape[0]
    xi, yi, ci = _position()
    chip = 2 * xi + yi

    conv_rows, conv_shapes = _pack_rows([w[n] for n in SHARDED_SMALL])
    (conv_all,) = _chip_gather([conv_rows], name="conv_taps_gather")
    conv_full = {}
    per_chip = [_unpack_rows(conv_all[k], conv_shapes) for k in range(4)]
    for i, n in enumerate(SHARDED_SMALL):
        conv_full[n] = jnp.concatenate([per_chip[k][i] for k in range(4)], axis=-1)
    halves = lambda l: [w[n][l].astype(MM_DTYPE).reshape(2, w[n].shape[1] // 2, w[n].shape[2]) for n in BIG]

    def layer_params(l, gathered):
        g_in, g_out, g_up, g_dn = (g.reshape((4,) + w[n].shape[1:]) for n, g in zip(BIG, gathered))
        p = {n: w[n][l] for n in SMALL if n != "norm_final" and n not in SHARDED_SMALL}
        for n in SHARDED_SMALL:
            p[n] = conv_full[n][l]
        p["w_in"] = _pad_w_in(jnp.concatenate([g_in[k] for k in range(4)], axis=1))
        p["w_out"] = g_out.reshape(-1, g_out.shape[-1])
        p["ffn_w_up"] = jnp.concatenate([g_up[k] for k in range(4)], axis=1)
        p["ffn_w_down"] = g_dn.reshape(-1, g_dn.shape[-1])
        return p

    gathered = _place_own(_run_step(_gather_halves_step(halves(0)), name="weights_gather"), halves(0))
    layers, saved, xl = [], [], x[0]
    for l in range(depth):
        p = layer_params(l, gathered)
        step = _gather_halves_step(halves(l + 1)) if l + 1 < depth else None
        xl, s, results = _layer_fwd(xl, p, hosted=step)
        if step is not None:
            gathered = _place_own(results, halves(l + 1))
        layers.append(p)
        saved.append(s)
    loss, dx, g_norm_final = _loss_head(xl, norm_final, loss_target[0], name="loss_head")
    loss = lax.psum(loss, ("x", "y", "c"))
    grads, big, pending = [None] * depth, [None] * depth, None
    for l in reversed(range(depth)):
        step = None if pending is None else _exchange_step(pending[1])
        dx, grads[l], results = _layer_bwd(dx, layers[l], saved[l], hosted=step)
        if pending is not None:
            big[pending[0]] = _reduce_finish(pending[1], results[0], pending[2], c=ci)
        pending = (l,) + _reduce_start(grads[l], c=ci)
    (last_exchange,) = _run_step(_exchange_step(pending[1]), name="grad_chip_exchange")
    big[pending[0]] = _reduce_finish(pending[1], last_exchange, pending[2], c=ci)
    grad_x = dx

    small_names = [n for n in SMALL if n != "norm_final"]
    small_rows, small_shapes = _pack_rows([jnp.stack([grads[l][n] for l in range(depth)]) for n in small_names]
                                          + [g_norm_final])
    small_sum = _unpack_rows(_all_sum(small_rows, name="small_grads_sum"), small_shapes)
    gsum = dict(zip(small_names + ["norm_final"], small_sum))
    for n in SHARDED_SMALL:
        width = w[n].shape[-1]
        gsum[n] = lax.dynamic_slice_in_dim(gsum[n], chip * width, width, axis=-1)
    for n in BIG:
        gsum[n] = jnp.stack([big[l][n].reshape(w[n].shape[1:]) for l in range(depth)])

    delta, new_m, new_v = {}, {}, {}
    for n in BIG:
        cols = w[n].shape[-1]
        d_, m_, v_ = _adamw(w[n].reshape(-1, cols), gsum[n].reshape(-1, cols), m[n].reshape(-1, cols),
                            v[n].reshape(-1, cols), name="adamw_" + n)
        delta[n], new_m[n], new_v[n] = (a.reshape(w[n].shape) for a in (d_, m_, v_))
    packs = [_pack_rows([src[n] for n in SMALL]) for src in (w, gsum, m, v)]
    outs = _adamw(*[pk[0] for pk in packs], name="adamw_small")
    for dst, rows_ in zip((delta, new_m, new_v), outs):
        dst.update(zip(SMALL, _unpack_rows(rows_, packs[0][1])))
    return (loss, grad_x[None], *[gsum[n] for n in NAMES], *[delta[n] for n in NAMES], *[new_m[n] for n in NAMES],
            *[new_v[n] for n in NAMES])
```

```python
import functools

import numpy as np
import jax
import jax.numpy as jnp
from jax import lax
from jax.experimental import pallas as pl
from jax.experimental.pallas import tpu as pltpu

F32 = jnp.float32
MM_DTYPE = jnp.bfloat16
VMEM_LIMIT_BYTES = 56 * 1024 * 1024
LANES = 128
SUBLANES = 8

GROUP = 512
HEAD = 128
N_HEADS = GROUP // HEAD
LRU_BLOCKS = 8
LRU_BD = GROUP // LRU_BLOCKS
LRU_C = 8.0
GDN_CHUNK = 64
DILATED_PAIRS = ((128, 1), (512, 4), (2048, 16))
EPS = 1e-6
NEG = -1e30
ATT_SCALE = HEAD ** -0.5
ATT_TILE = 512

ADAM_LR, ADAM_B1, ADAM_B2, ADAM_EPS, ADAM_WD, ADAM_STEP = 0.001, 0.9, 0.999, 1e-08, 0.01, 10

Z_AX, Z_AG, Z_BQ, Z_BK, Z_BV = 0, 512, 1024, 1536, 2048
Z_CQKV, Z_CZ, Z_DQ, Z_DK, Z_DV, Z_GATES, Z_COLS = 2560, 4096, 4608, 5120, 5632, 6144, 6272
IN_COLS = 6156
G_F, G_BETA, G_ALPHA = 0, 4, 8

MESH_ID = pl.DeviceIdType.MESH
ANY = pl.BlockSpec(memory_space=pl.ANY)


def _tile(n, target, align):
    t = min(n, target) // align * align
    while t >= align:
        if n % t == 0:
            return t
        t -= align
    return n


def _params(*sem):
    return pltpu.CompilerParams(dimension_semantics=sem, vmem_limit_bytes=VMEM_LIMIT_BYTES)


def _sds(shape, dtype=F32):
    return jax.ShapeDtypeStruct(tuple(shape), dtype)


def _dg(a, b, ca, cb, hi, batched=False):
    off = int(batched)
    dims = (((ca + off,), (cb + off,)), (((0,), (0,)) if batched else ((), ())))
    dot = lambda p, q: lax.dot_general(p, q, dims, preferred_element_type=F32)
    if hi:
        a_hi, b_hi = a.astype(MM_DTYPE), b.astype(MM_DTYPE)
        a_lo = (a - a_hi.astype(F32)).astype(MM_DTYPE)
        b_lo = (b - b_hi.astype(F32)).astype(MM_DTYPE)
        return dot(a_hi, b_hi) + (dot(a_hi, b_lo) + dot(a_lo, b_hi))
    return dot(a.astype(MM_DTYPE), b.astype(MM_DTYPE))


_FORMS = {"nn": (1, 0), "nt": (1, 1), "tn": (0, 0)}


@functools.partial(jax.custom_vjp, nondiff_argnums=(2, 3, 4))
def _mmf(a, b, form, hi, batched=False):
    ca, cb = _FORMS[form]
    return _dg(a, b, ca, cb, hi, batched)


def _mmf_fwd(a, b, form, hi, batched):
    return _mmf(a, b, form, hi, batched), (a, b)


def _mmf_bwd(form, hi, batched, res, g):
    a, b = res
    if form == "nn":
        return _dg(g, b, 1, 1, hi, batched), _dg(a, g, 0, 0, hi, batched)
    if form == "nt":
        return _dg(g, b, 1, 0, hi, batched), _dg(g, a, 0, 0, hi, batched)
    return _dg(b, g, 1, 1, hi, batched), _dg(a, g, 1, 0, hi, batched)


_mmf.defvjp(_mmf_fwd, _mmf_bwd)


def _mm(a, b):
    return _mmf(a, b, "nn", False)


def _mm_nt(a, b):
    return _mmf(a, b, "nt", False)


def _mm_tn(a, b):
    return _mmf(a, b, "tn", False)


MXU_DIM = 256
MATMUL_VMEM_BUDGET = 40 * 1024 * 1024


def _matmul_tiles(m, n, k, a_bytes, b_bytes, o_bytes, has_add):
    divisors = lambda d: [t for t in range(d, 0, -LANES) if d % t == 0 and t % LANES == 0] or [d]
    for tk in divisors(k):
        fits = []
        for tm in divisors(m):
            for tn in divisors(n):
                need = 2 * (tm * tk * a_bytes + tk * tn * b_bytes) + 2 * tm * tn * o_bytes
                need += (2 * tm * tn * 4 if has_add else 0) + (tm * tn * 4 if tk < k else 0)
                if tm <= 1024 and tn <= 1024 and need <= MATMUL_VMEM_BUDGET:
                    fits.append((min(tm, MXU_DIM) * min(tn, MXU_DIM), tm * tn, tm, tn))
        if fits:
            return max(fits)[2:] + (tk,)
    return _tile(m, 128, LANES), _tile(n, 128, LANES), _tile(k, 128, LANES)


def _matmul(a, b, *, form="nn", add=None, out_dtype=F32, tiles=None, name):
    ca, cb = _FORMS[form]
    m, k = (a.shape[1], a.shape[0]) if form == "tn" else a.shape
    n = b.shape[0] if form == "nt" else b.shape[1]
    tm, tn, tk = tiles or _matmul_tiles(m, n, k, a.dtype.itemsize, b.dtype.itemsize, jnp.dtype(out_dtype).itemsize,
                                        add is not None)
    nk = k // tk
    a_spec = (pl.BlockSpec((tk, tm), lambda i, j, kk: (kk, i)) if form == "tn"
              else pl.BlockSpec((tm, tk), lambda i, j, kk: (i, kk)))
    b_spec = (pl.BlockSpec((tn, tk), lambda i, j, kk: (j, kk)) if form == "nt"
              else pl.BlockSpec((tk, tn), lambda i, j, kk: (kk, j)))
    o_spec = pl.BlockSpec((tm, tn), lambda i, j, kk: (i, j))

    def body(*refs):
        a_ref, b_ref = refs[:2]
        add_ref = None if add is None else refs[2]
        o_ref = refs[2 + (add is not None)]
        part = _dg(a_ref[...], b_ref[...], ca, cb, False)
        if nk == 1:
            o_ref[...] = (part if add is None else part + add_ref[...]).astype(out_dtype)
            return
        acc_ref = refs[-1]
        kk = pl.program_id(2)

        @pl.when(kk == 0)
        def _():
            acc_ref[...] = jnp.zeros_like(acc_ref)

        acc_ref[...] += part

        @pl.when(kk == nk - 1)
        def _():
            r = acc_ref[...]
            if add is not None:
                r = r + add_ref[...]
            o_ref[...] = r.astype(out_dtype)

    args = (a, b) if add is None else (a, b, add)
    specs = [a_spec, b_spec] if add is None else [a_spec, b_spec, o_spec]
    return pl.pallas_call(
        body, name=name, grid=(m // tm, n // tn, nk), in_specs=specs, out_specs=o_spec,
        out_shape=_sds((m, n), out_dtype), scratch_shapes=[] if nk == 1 else [pltpu.VMEM((tm, tn), F32)],
        compiler_params=_params("parallel", "parallel", "arbitrary"))(*args)


def _rms(x, g):
    return x * lax.rsqrt(jnp.mean(x * x, axis=-1, keepdims=True) + EPS) * g


def _rmsnorm_fwd(x, g, *, name):
    t, d = x.shape
    tb = _tile(t, 512, SUBLANES)

    def body(x_ref, g_ref, o_ref):
        o_ref[...] = _rms(x_ref[...], g_ref[...]).astype(o_ref.dtype)

    row = pl.BlockSpec((tb, d), lambda i: (i, 0))
    vec = pl.BlockSpec((1, d), lambda i: (0, 0))
    return pl.pallas_call(body, name=name, grid=(t // tb,), in_specs=[row, vec], out_specs=row,
                          out_shape=_sds((t, d), MM_DTYPE), compiler_params=_params("parallel"))(x, g.reshape(1, d))


def _rmsnorm_bwd(x, g, dh, res, *, name):
    t, d = x.shape
    tb = _tile(t, 256, SUBLANES)

    def body(x_ref, g_ref, dh_ref, res_ref, dx_ref, dg_ref):
        _, vjp = jax.vjp(_rms, x_ref[...], g_ref[...])
        dx, dg = vjp(dh_ref[...])
        dx_ref[...] = dx + res_ref[...]

        @pl.when(pl.program_id(0) == 0)
        def _():
            dg_ref[...] = jnp.zeros_like(dg_ref)

        dg_ref[...] += dg

    row = pl.BlockSpec((tb, d), lambda i: (i, 0))
    vec = pl.BlockSpec((1, d), lambda i: (0, 0))
    dx, dg = pl.pallas_call(body, name=name, grid=(t // tb,), in_specs=[row, vec, row, row], out_specs=[row, vec],
                            out_shape=[_sds((t, d)), _sds((1, d))], compiler_params=_params("arbitrary"))(
                                x, g.reshape(1, d), dh, res)
    return dx, dg.reshape(d)


def _loss_head(x, g, tgt, *, name):
    t, d = x.shape
    tb = _tile(t, 256, SUBLANES)

    def body(x_ref, g_ref, t_ref, loss_ref, dx_ref, dg_ref):
        def f(xv, gv):
            e = _rms(xv, gv) - t_ref[...]
            return jnp.sum(jnp.sum(e * e, axis=-1, keepdims=True), axis=0, keepdims=True) * (0.5 / d)

        l, vjp = jax.vjp(f, x_ref[...], g_ref[...])
        dx, dg = vjp(jnp.ones((1, 1), F32))
        dx_ref[...] = dx

        @pl.when(pl.program_id(0) == 0)
        def _():
            dg_ref[...] = jnp.zeros_like(dg_ref)
            loss_ref[...] = jnp.zeros_like(loss_ref)

        dg_ref[...] += dg
        loss_ref[...] += jnp.zeros(loss_ref.shape, F32) + l

    row = pl.BlockSpec((tb, d), lambda i: (i, 0))
    vec = pl.BlockSpec((1, d), lambda i: (0, 0))
    lspec = pl.BlockSpec((SUBLANES, LANES), lambda i: (0, 0))
    loss, dx, dg = pl.pallas_call(
        body, name=name, grid=(t // tb,), in_specs=[row, vec, row], out_specs=[lspec, row, vec],
        out_shape=[_sds((SUBLANES, LANES)), _sds((t, d)), _sds((1, d))], compiler_params=_params("arbitrary"))(
            x, g.reshape(1, d), tgt)
    return loss[0, 0], dx, dg.reshape(d)


def _delayed(x, prev, j):
    if j == 0:
        return x
    sh = pltpu.roll(x, j, axis=0)
    row = lax.broadcasted_iota(jnp.int32, prev.shape, 0)
    top = jnp.where(row < j, pltpu.roll(prev, j, axis=0), sh[0:SUBLANES])
    return jnp.concatenate([top, sh[SUBLANES:]], axis=0)


def _advanced(x, nxt, j):
    if j == 0:
        return x
    tb = x.shape[0]
    sh = pltpu.roll(x, tb - j, axis=0)
    row = lax.broadcasted_iota(jnp.int32, nxt.shape, 0)
    bot = jnp.where(row + j < SUBLANES, sh[tb - SUBLANES:], pltpu.roll(nxt, SUBLANES - j, axis=0))
    return jnp.concatenate([sh[:tb - SUBLANES], bot], axis=0)


def _conv_tiles(t, ncols, coff):
    tc = _tile(ncols, 512, LANES)
    assert coff % tc == 0
    tb = _tile(t, 512, SUBLANES)
    return tc, tb, coff // tc


def _conv_fwd(x, w, b, *, ncols, coff=0, name):
    t = x.shape[0]
    kw = w.shape[0]
    tc, tb, cb = _conv_tiles(t, ncols, coff)
    n8 = tb // SUBLANES

    def body(x_ref, p_ref, w_ref, b_ref, o_ref):
        xv = x_ref[...]
        prev = jnp.where(pl.program_id(1) > 0, p_ref[...], 0.0)
        acc = jnp.zeros_like(xv) + b_ref[...]
        for k in range(kw):
            acc = acc + w_ref[k:k + 1, :] * _delayed(xv, prev, kw - 1 - k)
        o_ref[...] = acc

    in_specs = [
        pl.BlockSpec((tb, tc), lambda c, i: (i, c + cb)),
        pl.BlockSpec((SUBLANES, tc), lambda c, i: (jnp.maximum(i * n8 - 1, 0), c + cb)),
        pl.BlockSpec((kw, tc), lambda c, i: (0, c)),
        pl.BlockSpec((1, tc), lambda c, i: (0, c)),
    ]
    return pl.pallas_call(
        body, name=name, grid=(ncols // tc, t // tb), in_specs=in_specs,
        out_specs=pl.BlockSpec((tb, tc), lambda c, i: (i, c)), out_shape=_sds((t, ncols)),
        compiler_params=_params("parallel", "parallel"))(x, x, w, b.reshape(1, ncols))


def _conv_bwd(x, w, dy, *, ncols, coff=0, name):
    t = x.shape[0]
    kw = w.shape[0]
    tc, tb, cb = _conv_tiles(t, ncols, coff)
    n8 = tb // SUBLANES
    nt = t // tb

    split = dy.ndim == 3
    nhalf = ncols // tc // 2
    half_of = lambda c: (c >= nhalf).astype(jnp.int32)

    def body(x_ref, p_ref, dy_ref, n_ref, w_ref, dx_ref, dw_ref, db_ref):
        i = pl.program_id(1)
        xv, dyv = x_ref[...], (dy_ref[0] if split else dy_ref[...])
        prev = jnp.where(i > 0, p_ref[...], 0.0)
        nxt = jnp.where(i < nt - 1, n_ref[0] if split else n_ref[...], 0.0)

        @pl.when(i == 0)
        def _():
            dw_ref[...] = jnp.zeros_like(dw_ref)
            db_ref[...] = jnp.zeros_like(db_ref)

        dx = jnp.zeros_like(dyv)
        for k in range(kw):
            j = kw - 1 - k
            dx = dx + w_ref[k:k + 1, :] * _advanced(dyv, nxt, j)
            dw_ref[k:k + 1, :] += jnp.sum(dyv * _delayed(xv, prev, j), axis=0, keepdims=True)
        dx_ref[...] = dx.astype(dx_ref.dtype)
        db_ref[...] += jnp.sum(dyv, axis=0, keepdims=True)

    in_specs = [
        pl.BlockSpec((tb, tc), lambda c, i: (i, c + cb)),
        pl.BlockSpec((SUBLANES, tc), lambda c, i: (jnp.maximum(i * n8 - 1, 0), c + cb)),
        (pl.BlockSpec((1, tb, tc), lambda c, i: (half_of(c), i, c - half_of(c) * nhalf)) if split
         else pl.BlockSpec((tb, tc), lambda c, i: (i, c))),
        (pl.BlockSpec((1, SUBLANES, tc),
                      lambda c, i: (half_of(c), jnp.minimum((i + 1) * n8, nt * n8 - 1), c - half_of(c) * nhalf))
         if split else pl.BlockSpec((SUBLANES, tc), lambda c, i: (jnp.minimum((i + 1) * n8, nt * n8 - 1), c))),
        pl.BlockSpec((kw, tc), lambda c, i: (0, c)),
    ]
    out_specs = [
        pl.BlockSpec((tb, tc), lambda c, i: (i, c)),
        pl.BlockSpec((kw, tc), lambda c, i: (0, c)),
        pl.BlockSpec((1, tc), lambda c, i: (0, c)),
    ]
    dx, dw, db = pl.pallas_call(
        body, name=name, grid=(ncols // tc, nt), in_specs=in_specs, out_specs=out_specs,
        out_shape=[_sds((t, ncols), MM_DTYPE), _sds((kw, ncols)), _sds((1, ncols))],
        compiler_params=_params("parallel", "arbitrary"))(x, x, dy, dy, w)
    return dx, dw, db.reshape(ncols)


def _scan(a, u, *, reverse=False, name):
    t, c = u.shape
    tc = _tile(c, 512, LANES)
    tb = _tile(t, 256, SUBLANES)
    nt = t // tb

    def body(*refs):
        if a is None:
            u_ref, o_ref, carry_ref = refs
        else:
            a_ref, u_ref, o_ref, carry_ref = refs

        @pl.when(pl.program_id(1) == 0)
        def _():
            carry_ref[...] = jnp.zeros_like(carry_ref)

        hv = u_ref[...]
        av = None if a is None else a_ref[...]
        row = lax.broadcasted_iota(jnp.int32, hv.shape, 0)
        s = 1
        while s < tb:
            live = row < tb - s if reverse else row >= s
            shift = tb - s if reverse else s
            h_sh = jnp.where(live, pltpu.roll(hv, shift, axis=0), 0.0)
            if av is None:
                hv = hv + h_sh
            else:
                hv = av * h_sh + hv
                av = av * jnp.where(live, pltpu.roll(av, shift, axis=0), 1.0)
            s *= 2
        hv = hv + carry_ref[...] if av is None else hv + av * carry_ref[...]
        o_ref[...] = hv
        carry_ref[...] = o_ref[pl.ds(0 if reverse else tb - 1, 1), :]

    spec = pl.BlockSpec((tb, tc), (lambda cc, i: (nt - 1 - i, cc)) if reverse else (lambda cc, i: (i, cc)))
    args, specs = ((u,), [spec]) if a is None else ((a, u), [spec, spec])
    return pl.pallas_call(
        body, name=name, grid=(c // tc, t // tb), in_specs=specs, out_specs=spec, out_shape=_sds((t, c)),
        scratch_shapes=[pltpu.VMEM((1, tc), F32)], compiler_params=_params("parallel", "arbitrary"))(*args)


def _shift_down(x):
    return lax.pad(x, jnp.zeros((), x.dtype), ((1, -1, 0), (0, 0, 0)))


def _shift_up(x):
    return lax.pad(x, jnp.zeros((), x.dtype), ((-1, 1, 0), (0, 0, 0)))


def _neg_expm1(y):
    small = -(y * (1.0 + y * (0.5 + y * (1.0 / 6.0 + y * (1.0 / 24.0 + y * (1.0 / 120.0))))))
    return jnp.where(y > -0.05, small, 1.0 - jnp.exp(y))


def _lru_gates(xc, wa, ba, wx, bx, lam):
    r = jax.nn.sigmoid(_mm(xc, wa) + ba)
    i = jax.nn.sigmoid(_mm(xc, wx) + bx)
    log_a = -LRU_C * r * jax.nn.softplus(-lam)
    a = jnp.exp(log_a)
    u = jnp.sqrt(_neg_expm1(2.0 * log_a)) * (i * xc)
    return a, u


def _lru_specs(t):
    tb = _tile(t, 256, SUBLANES)
    row = pl.BlockSpec((tb, GROUP), lambda i: (i, 0))
    mat = pl.BlockSpec((GROUP, GROUP), lambda i: (0, 0))
    vec = pl.BlockSpec((1, GROUP), lambda i: (0, 0))
    return tb, row, mat, vec


def _lru_gates_fwd(xc, wa, ba, wx, bx, lam, *, name):
    t = xc.shape[0]
    tb, row, mat, vec = _lru_specs(t)

    def body(xc_ref, wa_ref, ba_ref, wx_ref, bx_ref, lam_ref, a_ref, u_ref):
        a, u = _lru_gates(xc_ref[...], wa_ref[...], ba_ref[...], wx_ref[...], bx_ref[...], lam_ref[...])
        a_ref[...] = a
        u_ref[...] = u

    return pl.pallas_call(
        body, name=name, grid=(t // tb,), in_specs=[row, mat, vec, mat, vec, vec], out_specs=[row, row],
        out_shape=[_sds((t, GROUP)), _sds((t, GROUP))], compiler_params=_params("parallel"))(xc, wa, ba, wx, bx, lam)


def _lru_gates_bwd(xc, wa, ba, wx, bx, lam, g, h_prev, *, name):
    t = xc.shape[0]
    tb, row, mat, vec = _lru_specs(t)

    def body(xc_ref, wa_ref, ba_ref, wx_ref, bx_ref, lam_ref, g_ref, hp_ref,
             dxc_ref, dwa_ref, dba_ref, dwx_ref, dbx_ref, dlam_ref):
        _, vjp = jax.vjp(_lru_gates, xc_ref[...], wa_ref[...], ba_ref[...], wx_ref[...], bx_ref[...], lam_ref[...])
        gv = g_ref[...]
        dxc, dwa, dba, dwx, dbx, dlam = vjp((gv * hp_ref[...], gv))
        dxc_ref[...] = dxc
        accs = (dwa_ref, dba_ref, dwx_ref, dbx_ref, dlam_ref)

        @pl.when(pl.program_id(0) == 0)
        def _():
            for r in accs:
                r[...] = jnp.zeros_like(r)

        for r, v in zip(accs, (dwa, dba, dwx, dbx, dlam)):
            r[...] += v

    return pl.pallas_call(
        body, name=name, grid=(t // tb,), in_specs=[row, mat, vec, mat, vec, vec, row, row],
        out_specs=[row, mat, vec, mat, vec, vec],
        out_shape=[_sds((t, GROUP)), _sds((GROUP, GROUP)), _sds((1, GROUP)), _sds((GROUP, GROUP)), _sds((1, GROUP)),
                   _sds((1, GROUP))],
        compiler_params=_params("arbitrary"))(xc, wa, ba, wx, bx, lam, g, h_prev)


def _block_diag(w):
    eye = jnp.eye(LRU_BLOCKS, dtype=w.dtype)
    return (eye[:, None, :, None] * w[:, :, None, :]).reshape(GROUP, GROUP)


def _diag_blocks(m):
    m4 = m.reshape(LRU_BLOCKS, LRU_BD, LRU_BLOCKS, LRU_BD)
    return jnp.stack([m4[n, :, n, :] for n in range(LRU_BLOCKS)])


def _group_mean_matrix(group):
    idx = np.arange(GROUP) // group
    return jnp.asarray((idx[:, None] == idx[None, :]).astype(np.float32) / group, F32)


def _post(h, gain, means, gate):
    y = h * lax.rsqrt(_mmf(h * h, means, "nn", True) + EPS) * gain
    return y if gate is None else y * jax.nn.gelu(gate)


def _post_specs(t):
    tb = _tile(t, 512, SUBLANES)
    col = lambda c: pl.BlockSpec((tb, GROUP), lambda i: (i, c))
    return tb, col, pl.BlockSpec((1, GROUP), lambda i: (0, 0)), pl.BlockSpec((GROUP, GROUP), lambda i: (0, 0))


def _post_fwd(h, gain, z, *, group, gate_col=None, name):
    t = h.shape[0]
    tb, col, vec, mat = _post_specs(t)
    gated = gate_col is not None

    def body(*refs):
        gate = refs[3][...] if gated else None
        refs[-1][...] = _post(refs[0][...], refs[1][...], refs[2][...], gate).astype(refs[-1].dtype)

    args, specs = [h, gain.reshape(1, GROUP), _group_mean_matrix(group)], [col(0), vec, mat]
    if gated:
        args, specs = args + [z], specs + [col(gate_col)]
    return pl.pallas_call(body, name=name, grid=(t // tb,), in_specs=specs, out_specs=col(0),
                          out_shape=_sds((t, GROUP), MM_DTYPE), compiler_params=_params("parallel"))(*args)


def _post_bwd(h, gain, z, dy, *, group, dy_col, gate_col=None, name):
    t = h.shape[0]
    tb, col, vec, mat = _post_specs(t)
    gated = gate_col is not None

    def body(*refs):
        h_ref, gn_ref, m_ref, dy_ref = refs[:4]
        if gated:
            z_ref, dh_ref, dgn_ref, dgt_ref = refs[4:]
            _, vjp = jax.vjp(lambda a, b, c: _post(a, b, m_ref[...], c), h_ref[...], gn_ref[...], z_ref[...])
            dh, dgn, dgt = vjp(dy_ref[...])
            dgt_ref[...] = dgt.astype(dgt_ref.dtype)
        else:
            dh_ref, dgn_ref = refs[4:]
            _, vjp = jax.vjp(lambda a, b: _post(a, b, m_ref[...], None), h_ref[...], gn_ref[...])
            dh, dgn = vjp(dy_ref[...])
        dh_ref[...] = dh

        @pl.when(pl.program_id(0) == 0)
        def _():
            dgn_ref[...] = jnp.zeros_like(dgn_ref)

        dgn_ref[...] += dgn

    args, specs = [h, gain.reshape(1, GROUP), _group_mean_matrix(group), dy], [col(0), vec, mat, col(dy_col)]
    out_specs, out_shape = [col(0), vec], [_sds((t, GROUP)), _sds((1, GROUP))]
    if gated:
        args, specs = args + [z], specs + [col(gate_col)]
        out_specs, out_shape = out_specs + [col(0)], out_shape + [_sds((t, GROUP), MM_DTYPE)]
    outs = pl.pallas_call(body, name=name, grid=(t // tb,), in_specs=specs, out_specs=out_specs, out_shape=out_shape,
                          compiler_params=_params("arbitrary"))(*args)
    return outs[0], outs[1].reshape(GROUP), (outs[2] if gated else None)


def _gates(zg, fb, alog, dtb):
    lane = lax.broadcasted_iota(jnp.int32, zg.shape, 1)
    logf = jax.nn.log_sigmoid(zg + fb)
    beta = jax.nn.sigmoid(zg)
    gdec = -jnp.exp(alog) * jax.nn.softplus(zg + dtb)
    return jnp.where(lane < G_BETA, logf, jnp.where(lane < G_ALPHA, beta, jnp.where(lane < G_ALPHA + 4, gdec, 0.0)))


def _lane_row(v, off):
    return jnp.pad(v.reshape(1, N_HEADS), ((0, 0), (off, LANES - N_HEADS - off)))


def _gates_fwd(z, fb, alog, dtb, *, name):
    t = z.shape[0]
    tb = _tile(t, 1024, SUBLANES)
    zspec = pl.BlockSpec((tb, LANES), lambda i: (i, Z_GATES // LANES))
    row = pl.BlockSpec((tb, LANES), lambda i: (i, 0))
    vec = pl.BlockSpec((1, LANES), lambda i: (0, 0))

    def body(z_ref, fb_ref, al_ref, dt_ref, o_ref):
        o_ref[...] = _gates(z_ref[...], fb_ref[...], al_ref[...], dt_ref[...])

    return pl.pallas_call(body, name=name, grid=(t // tb,), in_specs=[zspec, vec, vec, vec], out_specs=row,
                          out_shape=_sds((t, LANES)), compiler_params=_params("parallel"))(z, fb, alog, dtb)


def _gates_bwd(z, fb, alog, dtb, dg, *, name):
    t = z.shape[0]
    tb = _tile(t, 1024, SUBLANES)
    zspec = pl.BlockSpec((tb, LANES), lambda i: (i, Z_GATES // LANES))
    row = pl.BlockSpec((tb, LANES), lambda i: (i, 0))
    vec = pl.BlockSpec((1, LANES), lambda i: (0, 0))

    def body(z_ref, fb_ref, al_ref, dt_ref, dg_ref, dz_ref, dfb_ref, dal_ref, ddt_ref):
        _, vjp = jax.vjp(_gates, z_ref[...], fb_ref[...], al_ref[...], dt_ref[...])
        dz, dfb, dal, ddt = vjp(dg_ref[...])
        dz_ref[...] = dz.astype(dz_ref.dtype)
        accs = (dfb_ref, dal_ref, ddt_ref)

        @pl.when(pl.program_id(0) == 0)
        def _():
            for r in accs:
                r[...] = jnp.zeros_like(r)

        for r, v in zip(accs, (dfb, dal, ddt)):
            r[...] += v

    return pl.pallas_call(
        body, name=name, grid=(t // tb,), in_specs=[zspec, vec, vec, vec, row], out_specs=[row, vec, vec, vec],
        out_shape=[_sds((t, LANES), MM_DTYPE), _sds((1, LANES)), _sds((1, LANES)), _sds((1, LANES))],
        compiler_params=_params("arbitrary"))(z, fb, alog, dtb, dg)


class _Hosted:
    def __init__(self, args, out_shapes, sems, start, finish):
        self.args, self.out_shapes, self.sems, self.start, self.finish = list(args), list(out_shapes), list(sems), start, finish


def _grid_call(body, *, name, grid, in_specs, out_specs, out_shape, args, hosted=None):
    if hosted is None:
        outs = pl.pallas_call(body, name=name, grid=grid, in_specs=in_specs, out_specs=out_specs, out_shape=out_shape,
                              compiler_params=_params(*(["parallel"] * len(grid))))(*args)
        return outs, None
    n_in, n_out, h_in, h_out = len(in_specs), len(out_specs), len(hosted.args), len(hosted.out_shapes)

    def wrapped(*refs):
        core_in, host_in = refs[:n_in], refs[n_in:n_in + h_in]
        rest = refs[n_in + h_in:]
        core_out, host_out, sems = rest[:n_out], rest[n_out:n_out + h_out], rest[n_out + h_out:]
        ids = [pl.program_id(a) for a in range(len(grid))]
        first = functools.reduce(jnp.logical_and, [i == 0 for i in ids])
        last = functools.reduce(jnp.logical_and, [i == g - 1 for i, g in zip(ids, grid)])

        @pl.when(first)
        def _():
            hosted.start(host_in, host_out, sems)

        body(*core_in, *core_out)

        @pl.when(last)
        def _():
            hosted.finish(host_in, host_out, sems)

    outs = pl.pallas_call(
        wrapped, name=name, grid=grid, in_specs=list(in_specs) + [ANY] * h_in, out_specs=list(out_specs) + [ANY] * h_out,
        out_shape=list(out_shape) + hosted.out_shapes, scratch_shapes=hosted.sems,
        compiler_params=pltpu.CompilerParams(dimension_semantics=("arbitrary",) * len(grid),
                                             vmem_limit_bytes=VMEM_LIMIT_BYTES, has_side_effects=True))(
                                                 *args, *hosted.args)
    return outs[:n_out], outs[n_out:]


def _pair_weights(tq, band, transposed):
    d = np.arange(tq)[:, None] - np.arange(tq)[None, :]
    d = (d.T if transposed else d)[None] + (np.arange(band + 1) * tq)[:, None, None]
    w = sum(((d >= 0) & (d <= win) & (d % dil == 0)).astype(np.float32) for win, dil in DILATED_PAIRS)
    return jnp.asarray(w, F32)


def _att_geometry(t, mode):
    tq = _tile(t, ATT_TILE, LANES)
    nq = t // tq
    band = nq - 1 if mode == "fox" else min(DILATED_PAIRS[-1][0] // tq, nq - 1)
    return tq, nq, band


def _qkv_prep(z, *, off, name):
    t = z.shape[0]
    tb = _tile(t, 512, SUBLANES)
    cb = off // GROUP

    def body(z_ref, o_ref):
        scale = jnp.where(pl.program_id(1) == 0, ATT_SCALE, 1.0)
        o_ref[...] = (z_ref[...] * scale).astype(o_ref.dtype)

    return pl.pallas_call(
        body, name=name, grid=(t // tb, 3), in_specs=[pl.BlockSpec((tb, GROUP), lambda i, j: (i, j + cb))],
        out_specs=pl.BlockSpec((tb, GROUP), lambda i, j: (i, j)), out_shape=_sds((t, 3 * GROUP), MM_DTYPE),
        compiler_params=_params("parallel", "parallel"))(z)


def _block(ref, j, tq):
    return ref[pl.ds(pl.multiple_of(j * tq, tq), tq), :]


def _lane_block(ref, j, tq):
    return ref[0, :, pl.ds(pl.multiple_of(j * tq, tq), tq)]


def _att_tile(mode, q, kj, cq, ckj, causal, w):
    s = _dg(q, kj, 1, 1, False)
    if mode == "fox":
        s = s + (cq - ckj)
        return s if causal is None else jnp.where(causal, s, NEG)
    return jnp.where(w > 0.0, s, NEG)


def _att_fwd(qkv, cq, ck, *, mode, name, hosted=None):
    t = qkv.shape[0]
    tq, nq, band = _att_geometry(t, mode)
    weights = _pair_weights(tq, band if mode == "dil" else 0, False)

    def body(q_ref, k_ref, v_ref, cq_ref, ck_ref, w_ref, o_ref, lse_ref):
        i = pl.program_id(1)
        q, cqv = q_ref[...], cq_ref[0]
        causal = (lax.broadcasted_iota(jnp.int32, (tq, tq), 0) >= lax.broadcasted_iota(jnp.int32, (tq, tq), 1))

        def step(j, carry, masked):
            m_old, l_old, acc = carry
            w = w_ref[i - j] if mode == "dil" else None
            s = _att_tile(mode, q, _block(k_ref, j, tq), cqv, _lane_block(ck_ref, j, tq), causal if masked else None, w)
            m_new = jnp.maximum(m_old, jnp.max(s, axis=-1, keepdims=True))
            alpha = jnp.exp(m_old - m_new)
            p = jnp.exp(s - m_new)
            if w is not None:
                p = p * w
            l_new = alpha * l_old + jnp.sum(p, axis=-1, keepdims=True)
            return m_new, l_new, alpha * acc + _dg(p, _block(v_ref, j, tq), 1, 0, False)

        carry = (jnp.full((tq, 1), NEG, F32), jnp.zeros((tq, 1), F32), jnp.zeros((tq, HEAD), F32))
        if mode == "fox":
            carry = lax.fori_loop(0, i, lambda j, c: step(j, c, False), carry)
            carry = step(i, carry, True)
        else:
            carry = lax.fori_loop(jnp.maximum(i - band, 0), i + 1, lambda j, c: step(j, c, True), carry)
        m_fin, l_fin, acc = carry
        o_ref[...] = acc / l_fin
        lse_ref[0] = m_fin + jnp.log(l_fin)

    col = pl.BlockSpec((1, tq, 1), lambda h, i: (h, i, 0))
    in_specs = [
        pl.BlockSpec((tq, HEAD), lambda h, i: (i, h)),
        pl.BlockSpec((t, HEAD), lambda h, i: (0, N_HEADS + h)),
        pl.BlockSpec((t, HEAD), lambda h, i: (0, 2 * N_HEADS + h)),
        col,
        pl.BlockSpec((1, 1, t), lambda h, i: (h, 0, 0)),
        pl.BlockSpec(weights.shape, lambda h, i: (0, 0, 0)),
    ]
    return _grid_call(
        body, name=name, grid=(N_HEADS, nq), in_specs=in_specs,
        out_specs=[pl.BlockSpec((tq, HEAD), lambda h, i: (i, h)), col],
        out_shape=[_sds((t, GROUP)), _sds((N_HEADS, t, 1))], args=(qkv, qkv, qkv, cq, ck, weights), hosted=hosted)


def _att_bwd_q(qkv, cq, ck, do, o, lse, *, mode, name):
    t = qkv.shape[0]
    tq, nq, band = _att_geometry(t, mode)
    weights = _pair_weights(tq, band if mode == "dil" else 0, False)

    def body(q_ref, k_ref, v_ref, cq_ref, ck_ref, w_ref, do_ref, o_ref, lse_ref, dq_ref, dcq_ref, delta_ref):
        i = pl.program_id(1)
        q, cqv, lse = q_ref[...], cq_ref[0], lse_ref[0]
        dov = do_ref[...]
        delta = jnp.sum(dov * o_ref[...], axis=-1, keepdims=True)
        do16 = dov.astype(MM_DTYPE)
        causal = (lax.broadcasted_iota(jnp.int32, (tq, tq), 0) >= lax.broadcasted_iota(jnp.int32, (tq, tq), 1))

        def step(j, carry, masked):
            dq, dcq = carry
            kj = _block(k_ref, j, tq)
            w = w_ref[i - j] if mode == "dil" else None
            s = _att_tile(mode, q, kj, cqv, _lane_block(ck_ref, j, tq), causal if masked else None, w)
            p = jnp.exp(s - lse)
            if w is not None:
                p = p * w
            ds = p * (_dg(do16, _block(v_ref, j, tq), 1, 1, False) - delta)
            return dq + _dg(ds, kj, 1, 0, False), dcq + jnp.sum(ds, axis=-1, keepdims=True)

        carry = (jnp.zeros((tq, HEAD), F32), jnp.zeros((tq, 1), F32))
        if mode == "fox":
            carry = lax.fori_loop(0, i, lambda j, c: step(j, c, False), carry)
            carry = step(i, carry, True)
        else:
            carry = lax.fori_loop(jnp.maximum(i - band, 0), i + 1, lambda j, c: step(j, c, True), carry)
        dq_ref[...] = (carry[0] * ATT_SCALE).astype(dq_ref.dtype)
        dcq_ref[0] = carry[1]
        delta_ref[0] = delta

    col = pl.BlockSpec((1, tq, 1), lambda h, i: (h, i, 0))
    row = pl.BlockSpec((tq, HEAD), lambda h, i: (i, h))
    in_specs = [
        row,
        pl.BlockSpec((t, HEAD), lambda h, i: (0, N_HEADS + h)),
        pl.BlockSpec((t, HEAD), lambda h, i: (0, 2 * N_HEADS + h)),
        col,
        pl.BlockSpec((1, 1, t), lambda h, i: (h, 0, 0)),
        pl.BlockSpec(weights.shape, lambda h, i: (0, 0, 0)),
        row, row, col,
    ]
    return pl.pallas_call(
        body, name=name, grid=(N_HEADS, nq), in_specs=in_specs, out_specs=[row, col, col],
        out_shape=[_sds((t, GROUP), MM_DTYPE), _sds((N_HEADS, t, 1)), _sds((N_HEADS, t, 1))],
        compiler_params=_params("parallel", "parallel"))(qkv, qkv, qkv, cq, ck, weights, do, o, lse)


def _att_bwd_kv(qkv, cq, ck, do, lse_row, delta_row, *, mode, name, hosted=None):
    t = qkv.shape[0]
    tq, nq, band = _att_geometry(t, mode)
    weights = _pair_weights(tq, band if mode == "dil" else 0, True)

    def body(q_ref, k_ref, v_ref, cq_ref, ck_ref, w_ref, do_ref, lse_ref, delta_ref, dk_ref, dv_ref, dck_ref):
        jk = pl.program_id(1)
        kj, vj, ckv = k_ref[...], v_ref[...], cq_ref[0]
        causal = (lax.broadcasted_iota(jnp.int32, (tq, tq), 1) >= lax.broadcasted_iota(jnp.int32, (tq, tq), 0))

        def step(qi, carry, masked):
            dk, dv, dck = carry
            qb = _block(q_ref, qi, tq)
            s = _dg(kj, qb, 1, 1, False)
            w = None
            if mode == "fox":
                s = s + (_lane_block(ck_ref, qi, tq) - ckv)
                if masked:
                    s = jnp.where(causal, s, NEG)
            else:
                w = w_ref[qi - jk]
                s = jnp.where(w > 0.0, s, NEG)
            p = jnp.exp(s - _lane_block(lse_ref, qi, tq))
            if w is not None:
                p = p * w
            do16 = _block(do_ref, qi, tq).astype(MM_DTYPE)
            ds = p * (_dg(vj, do16, 1, 1, False) - _lane_block(delta_ref, qi, tq))
            return (dk + _dg(ds, qb, 1, 0, False), dv + _dg(p, do16, 1, 0, False),
                    dck - jnp.sum(ds, axis=-1, keepdims=True))

        carry = (jnp.zeros((tq, HEAD), F32), jnp.zeros((tq, HEAD), F32), jnp.zeros((tq, 1), F32))
        if mode == "fox":
            carry = step(jk, carry, True)
            carry = lax.fori_loop(jk + 1, nq, lambda qi, c: step(qi, c, False), carry)
        else:
            carry = lax.fori_loop(jk, jnp.minimum(jk + band, nq - 1) + 1, lambda qi, c: step(qi, c, True), carry)
        dk_ref[...] = carry[0].astype(dk_ref.dtype)
        dv_ref[...] = carry[1].astype(dv_ref.dtype)
        dck_ref[0] = carry[2]

    col = pl.BlockSpec((1, tq, 1), lambda h, j: (h, j, 0))
    lanes = pl.BlockSpec((1, 1, t), lambda h, j: (h, 0, 0))
    in_specs = [
        pl.BlockSpec((t, HEAD), lambda h, j: (0, h)),
        pl.BlockSpec((tq, HEAD), lambda h, j: (j, N_HEADS + h)),
        pl.BlockSpec((tq, HEAD), lambda h, j: (j, 2 * N_HEADS + h)),
        col, lanes,
        pl.BlockSpec(weights.shape, lambda h, j: (0, 0, 0)),
        pl.BlockSpec((t, HEAD), lambda h, j: (0, h)),
        lanes, lanes,
    ]
    out = pl.BlockSpec((tq, HEAD), lambda h, j: (j, h))
    return _grid_call(
        body, name=name, grid=(N_HEADS, nq), in_specs=in_specs, out_specs=[out, out, col],
        out_shape=[_sds((t, GROUP), MM_DTYPE), _sds((t, GROUP), MM_DTYPE), _sds((N_HEADS, t, 1))],
        args=(qkv, qkv, qkv, cq, ck, weights, do, lse_row, delta_row), hosted=hosted)


def _silu(x):
    return x * jax.nn.sigmoid(x)


def _l2n(x):
    return x * lax.rsqrt(jnp.sum(x * x, axis=-1, keepdims=True) + EPS)


def _gdn_pre(xqkv, gts):
    rows, c = xqkv.shape[0], GDN_CHUNK
    nb = rows // c
    ri = lax.broadcasted_iota(jnp.int32, (1, c, c), 1)
    ci = lax.broadcasted_iota(jnp.int32, (1, c, c), 2)
    tril, strict, eye = ri >= ci, ri > ci, ri == ci
    eyef = eye.astype(F32)
    last = lax.broadcasted_iota(jnp.int32, (1, c, 1), 1) == c - 1
    lane = lax.broadcasted_iota(jnp.int32, gts.shape, 1)
    to3 = lambda a: a.reshape(nb, c, a.shape[-1])
    to2 = lambda a: a.reshape(rows, a.shape[-1])
    bmm = lambda a, b: _mmf(a, b, "nn", False, True)
    bmm_nt = lambda a, b: _mmf(a, b, "nt", False, True)
    gcs = to2(_mmf(jnp.broadcast_to(tril.astype(F32), (nb, c, c)), to3(gts), "nn", True, True))
    us, ws, qgs, kds, qks, egls = [], [], [], [], [], []
    for h in range(N_HEADS):
        q = to3(_l2n(_silu(xqkv[:, h * HEAD:(h + 1) * HEAD])) * ATT_SCALE)
        k = to3(_l2n(_silu(xqkv[:, GROUP + h * HEAD:GROUP + (h + 1) * HEAD])))
        v = to3(_silu(xqkv[:, 2 * GROUP + h * HEAD:2 * GROUP + (h + 1) * HEAD]))
        beta = to3(jnp.sum(jnp.where(lane == G_BETA + h, gts, 0.0), axis=-1, keepdims=True))
        gc = to3(jnp.sum(jnp.where(lane == G_ALPHA + h, gcs, 0.0), axis=-1, keepdims=True))
        gr = jnp.sum(jnp.where(eye, jnp.broadcast_to(gc, (nb, c, c)), 0.0), axis=1, keepdims=True)
        decay = jnp.where(tril, jnp.exp(jnp.where(tril, gc - gr, 0.0)), 0.0)
        kbeta, vbeta = k * beta, v * beta
        low = jnp.where(strict, bmm_nt(kbeta, k) * decay, 0.0)
        inv, pw = eyef - low, bmm(low, low)
        for step in range(5):
            inv = inv + bmm(inv, pw)
            if step < 4:
                pw = bmm(pw, pw)
        eg = jnp.exp(gc)
        g_last = jnp.sum(jnp.where(last, gc, 0.0), axis=1, keepdims=True)
        us.append(to2(bmm(inv, vbeta)))
        ws.append(to2(bmm(inv, kbeta * eg)))
        qgs.append(to2(q * eg))
        kds.append(to2(k * jnp.exp(g_last - gc)))
        qks.append(to2(jnp.where(tril, bmm_nt(q, k) * decay, 0.0)))
        egls.append(jnp.broadcast_to(jnp.exp(g_last), (nb, 1, LANES)))
    cat = lambda parts: jnp.concatenate(parts, axis=1)
    return cat(us), cat(ws), cat(qgs), cat(kds), jnp.stack(qks, axis=0), jnp.stack(egls, axis=1)


def _gdn_seq(states, u, w, qg, kd, qk, egl):
    states = list(states)
    outs = []
    for r in range(u.shape[0] // GDN_CHUNK):
        rs = slice(r * GDN_CHUNK, (r + 1) * GDN_CHUNK)
        heads = []
        for h in range(N_HEADS):
            hs = slice(h * HEAD, (h + 1) * HEAD)
            s_in = states[h]
            v_new = u[rs, hs] - _mm(w[rs, hs], s_in)
            heads.append(_mm(qg[rs, hs], s_in) + _mm(qk[h, rs], v_new))
            states[h] = s_in * egl[r, h] + _mm_tn(kd[rs, hs], v_new)
        outs.append(jnp.concatenate(heads, axis=1))
    return jnp.concatenate(outs, axis=0), tuple(states)


def _gdn_out(o, zg, ng):
    ys = []
    for h in range(N_HEADS):
        oh = o[:, h * HEAD:(h + 1) * HEAD]
        ys.append(oh * lax.rsqrt(jnp.mean(oh * oh, axis=-1, keepdims=True) + EPS) * ng * _silu(zg[:, h * HEAD:(h + 1) * HEAD]))
    return jnp.concatenate(ys, axis=1)


GDN_PRE_ROWS = 8 * GDN_CHUNK
GDN_SEQ_ROWS = 4 * GDN_CHUNK


def _gdn_pre_specs(t):
    rows = _tile(t, GDN_PRE_ROWS, GDN_CHUNK)
    nb = rows // GDN_CHUNK
    wide = pl.BlockSpec((rows, GROUP), lambda i: (i, 0))
    ins = [pl.BlockSpec((rows, 3 * GROUP), lambda i: (i, 0)), pl.BlockSpec((rows, LANES), lambda i: (i, 0))]
    mids = [wide, wide, wide, wide, pl.BlockSpec((N_HEADS, rows, GDN_CHUNK), lambda i: (0, i, 0)),
            pl.BlockSpec((nb, N_HEADS, 1, LANES), lambda i: (i, 0, 0, 0))]
    shapes = [_sds((t, GROUP))] * 4 + [_sds((N_HEADS, t, GDN_CHUNK)), _sds((t // GDN_CHUNK, N_HEADS, 1, LANES))]
    return rows, ins, mids, shapes


def _gdn_pre_fwd(xqkv, gts, *, name):
    t = xqkv.shape[0]
    rows, ins, mids, shapes = _gdn_pre_specs(t)

    def body(x_ref, g_ref, *out_refs):
        for ref, val in zip(out_refs, _gdn_pre(x_ref[...], g_ref[...])):
            ref[...] = val

    return pl.pallas_call(body, name=name, grid=(t // rows,), in_specs=ins, out_specs=mids, out_shape=shapes,
                          compiler_params=_params("parallel"))(xqkv, gts)


def _gdn_pre_bwd(xqkv, gts, dmids, *, name):
    t = xqkv.shape[0]
    rows, ins, mids, _ = _gdn_pre_specs(t)

    def body(x_ref, g_ref, *refs):
        _, vjp = jax.vjp(_gdn_pre, x_ref[...], g_ref[...])
        dx, dg = vjp(tuple(r[...] for r in refs[:6]))
        refs[6][...] = dx
        refs[7][...] = dg

    return pl.pallas_call(body, name=name, grid=(t // rows,), in_specs=ins + mids, out_specs=ins,
                          out_shape=[_sds((t, 3 * GROUP)), _sds((t, LANES))],
                          compiler_params=_params("parallel"))(xqkv, gts, *dmids)


def _gdn_seq_specs(t, rev):
    rows = _tile(t, GDN_SEQ_ROWS, GDN_CHUNK)
    nb, n = rows // GDN_CHUNK, t // rows
    at = (lambda i: n - 1 - i) if rev else (lambda i: i)
    wide = pl.BlockSpec((rows, GROUP), lambda i: (at(i), 0))
    mids = [wide, wide, wide, wide, pl.BlockSpec((N_HEADS, rows, GDN_CHUNK), lambda i: (0, at(i), 0)),
            pl.BlockSpec((nb, N_HEADS, 1, LANES), lambda i: (at(i), 0, 0, 0))]
    state = pl.BlockSpec((1, N_HEADS, HEAD, HEAD), lambda i: (at(i), 0, 0, 0))
    return rows, n, wide, mids, state


def _gdn_seq_fwd(mids_in, *, name):
    t = mids_in[0].shape[0]
    rows, n, wide, mids, state = _gdn_seq_specs(t, False)

    def body(u_ref, w_ref, qg_ref, kd_ref, qk_ref, egl_ref, o_ref, sv_ref, s_ref):
        @pl.when(pl.program_id(0) == 0)
        def _():
            s_ref[...] = jnp.zeros_like(s_ref)

        sv_ref[0] = s_ref[...]
        o, new = _gdn_seq(tuple(s_ref[h] for h in range(N_HEADS)), u_ref[...], w_ref[...], qg_ref[...], kd_ref[...],
                          qk_ref[...], egl_ref[...])
        o_ref[...] = o
        for h in range(N_HEADS):
            s_ref[h] = new[h]

    return pl.pallas_call(
        body, name=name, grid=(n,), in_specs=mids, out_specs=[wide, state],
        out_shape=[_sds((t, GROUP)), _sds((n, N_HEADS, HEAD, HEAD))],
        scratch_shapes=[pltpu.VMEM((N_HEADS, HEAD, HEAD), F32)], compiler_params=_params("arbitrary"))(*mids_in)


def _gdn_seq_bwd(mids_in, states, do, *, name):
    t = mids_in[0].shape[0]
    rows, n, wide, mids, state = _gdn_seq_specs(t, True)

    def body(u_ref, w_ref, qg_ref, kd_ref, qk_ref, egl_ref, sv_ref, do_ref, *refs):
        d_refs, ds_ref = refs[:6], refs[6]

        @pl.when(pl.program_id(0) == 0)
        def _():
            ds_ref[...] = jnp.zeros_like(ds_ref)

        s_in = tuple(sv_ref[0, h] for h in range(N_HEADS))
        _, vjp = jax.vjp(_gdn_seq, s_in, u_ref[...], w_ref[...], qg_ref[...], kd_ref[...], qk_ref[...], egl_ref[...])
        grads = vjp((do_ref[...], tuple(ds_ref[h] for h in range(N_HEADS))))
        for ref, val in zip(d_refs, grads[1:]):
            ref[...] = val
        for h in range(N_HEADS):
            ds_ref[h] = grads[0][h]

    shapes = [_sds(m.shape) for m in mids_in]
    return pl.pallas_call(
        body, name=name, grid=(n,), in_specs=mids + [state, wide], out_specs=mids, out_shape=shapes,
        scratch_shapes=[pltpu.VMEM((N_HEADS, HEAD, HEAD), F32)],
        compiler_params=_params("arbitrary"))(*mids_in, states, do)


def _gdn_out_specs(t):
    rows = _tile(t, 512, SUBLANES)
    wide = pl.BlockSpec((rows, GROUP), lambda i: (i, 0))
    return rows, wide, pl.BlockSpec((rows, GROUP), lambda i: (i, Z_CZ // GROUP)), pl.BlockSpec((1, HEAD), lambda i: (0, 0))


def _gdn_out_fwd(o, z, ng, *, name):
    t = o.shape[0]
    rows, wide, zspec, vec = _gdn_out_specs(t)

    def body(o_ref, z_ref, ng_ref, y_ref):
        y_ref[...] = _gdn_out(o_ref[...], z_ref[...], ng_ref[...]).astype(y_ref.dtype)

    return pl.pallas_call(body, name=name, grid=(t // rows,), in_specs=[wide, zspec, vec], out_specs=wide,
                          out_shape=_sds((t, GROUP), MM_DTYPE), compiler_params=_params("parallel"))(o, z, ng)


def _gdn_out_bwd(o, z, ng, dy, *, dy_col, name):
    t = o.shape[0]
    rows, wide, zspec, vec = _gdn_out_specs(t)
    dyspec = pl.BlockSpec((rows, GROUP), lambda i: (i, dy_col))

    def body(o_ref, z_ref, ng_ref, dy_ref, do_ref, dz_ref, dng_ref):
        _, vjp = jax.vjp(_gdn_out, o_ref[...], z_ref[...], ng_ref[...])
        do, dz, dng = vjp(dy_ref[...])
        do_ref[...] = do
        dz_ref[...] = dz.astype(dz_ref.dtype)

        @pl.when(pl.program_id(0) == 0)
        def _():
            dng_ref[...] = jnp.zeros_like(dng_ref)

        dng_ref[...] += dng

    do, dz, dng = pl.pallas_call(
        body, name=name, grid=(t // rows,), in_specs=[wide, zspec, vec, dyspec], out_specs=[wide, wide, vec],
        out_shape=[_sds((t, GROUP)), _sds((t, GROUP), MM_DTYPE), _sds((1, HEAD))],
        compiler_params=_params("arbitrary"))(o, z, ng, dy)
    return do, dz, dng.reshape(HEAD)


def _swiglu(up, gate):
    return _silu(gate) * up


def _conv_value(x_ref, p_ref, w_ref, b_ref, first):
    xv = x_ref[...]
    prev = jnp.where(first, 0.0, p_ref[...])
    acc = jnp.zeros_like(xv) + b_ref[...]
    kw = w_ref.shape[0]
    for k in range(kw):
        acc = acc + w_ref[k:k + 1, :] * _delayed(xv, prev, kw - 1 - k)
    return acc


def _ffn_act_specs(t, two_f, kw, col_of):
    dff = two_f // 2
    tb, tc = _tile(t, 512, SUBLANES), _tile(dff, 512, LANES)
    nh, n8 = dff // tc, tb // SUBLANES
    specs = []
    for half in (0, 1):
        col = lambda j, half=half: col_of(j, nh) + half * nh
        specs += [pl.BlockSpec((tb, tc), lambda j, i, col=col: (i, col(j))),
                  pl.BlockSpec((SUBLANES, tc), lambda j, i, col=col: (jnp.maximum(i * n8 - 1, 0), col(j))),
                  pl.BlockSpec((kw, tc), lambda j, i, col=col: (0, col(j))),
                  pl.BlockSpec((1, tc), lambda j, i, col=col: (0, col(j)))]
    return tb, tc, nh, specs


def _ffn_act_fwd(uu, w, b, *, name):
    t, two_f = uu.shape
    tb, tc, nh, specs = _ffn_act_specs(t, two_f, w.shape[0], lambda j, nh: j)

    def body(xu, pu, wu, bu, xg, pg, wg, bg, o_ref):
        first = pl.program_id(1) == 0
        o_ref[...] = _swiglu(_conv_value(xu, pu, wu, bu, first), _conv_value(xg, pg, wg, bg, first)).astype(o_ref.dtype)

    b2 = b.reshape(1, two_f)
    return pl.pallas_call(body, name=name, grid=(nh, t // tb), in_specs=specs,
                          out_specs=pl.BlockSpec((tb, tc), lambda j, i: (i, j)), out_shape=_sds((t, two_f // 2), MM_DTYPE),
                          compiler_params=_params("parallel", "parallel"))(uu, uu, w, b2, uu, uu, w, b2)


def _ffn_act_bwd(uu, w, b, dact, *, name):
    t, two_f = uu.shape
    tb, tc, nh, specs = _ffn_act_specs(t, two_f, w.shape[0], lambda j, nh: j)

    def body(xu, pu, wu, bu, xg, pg, wg, bg, da_ref, o_ref):
        first = pl.program_id(1) == 0
        _, vjp = jax.vjp(_swiglu, _conv_value(xu, pu, wu, bu, first), _conv_value(xg, pg, wg, bg, first))
        o_ref[0], o_ref[1] = vjp(da_ref[...])

    b2 = b.reshape(1, two_f)
    return pl.pallas_call(body, name=name, grid=(nh, t // tb),
                          in_specs=specs + [pl.BlockSpec((tb, tc), lambda j, i: (i, j))],
                          out_specs=pl.BlockSpec((2, tb, tc), lambda j, i: (0, i, j)), out_shape=_sds((2, t, two_f // 2)),
                          compiler_params=_params("parallel", "parallel"))(uu, uu, w, b2, uu, uu, w, b2, dact)


def _sum_slots(parts, *, out_dtype=F32, name):
    if not isinstance(parts, (list, tuple)):
        parts = [parts[s] for s in range(parts.shape[0])]
    r = parts[0].shape[0]
    rb = _tile(r, 2048, 2 * SUBLANES)
    spec = pl.BlockSpec((rb, LANES), lambda i: (i, 0))

    def body(*refs):
        acc = refs[0][...].astype(F32)
        for ref in refs[1:-1]:
            acc = acc + ref[...].astype(F32)
        refs[-1][...] = acc.astype(out_dtype)

    return pl.pallas_call(body, name=name, grid=(r // rb,), in_specs=[spec] * len(parts), out_specs=spec,
                          out_shape=_sds((r, LANES), out_dtype), compiler_params=_params("parallel"))(*parts)


def _adamw(w, g, m, v, *, name):
    r, c = w.shape
    rb = _tile(r, 128, SUBLANES)
    spec = pl.BlockSpec((rb, c), lambda i: (i, 0))

    def body(w_ref, g_ref, m_ref, v_ref, d_ref, nm_ref, nv_ref):
        gv = g_ref[...]
        mn = ADAM_B1 * m_ref[...] + (1.0 - ADAM_B1) * gv
        vn = ADAM_B2 * v_ref[...] + (1.0 - ADAM_B2) * (gv * gv)
        m_hat = mn / (1.0 - ADAM_B1 ** ADAM_STEP)
        v_hat = vn / (1.0 - ADAM_B2 ** ADAM_STEP)
        d_ref[...] = -ADAM_LR * (m_hat / (jnp.sqrt(v_hat) + ADAM_EPS) + ADAM_WD * w_ref[...])
        nm_ref[...] = mn
        nv_ref[...] = vn

    return pl.pallas_call(body, name=name, grid=(r // rb,), in_specs=[spec] * 4, out_specs=[spec] * 3,
                          out_shape=[_sds((r, c))] * 3, compiler_params=_params("parallel"))(w, g, m, v)


def _position():
    return lax.axis_index("x"), lax.axis_index("y"), lax.axis_index("c")


def _chip_gather(shards, *, name):
    n = len(shards)

    def body(*refs):
        ins, outs = refs[:n], refs[n:2 * n]
        send_sems, recv_sems, local_sems = refs[2 * n:]
        x, y, c = _position()
        mine = 2 * x + y
        chips = [(1 - x, y), (x, 1 - y), (1 - x, 1 - y)]
        local = [pltpu.make_async_copy(ins[a], outs[a].at[mine], local_sems.at[a]) for a in range(n)]
        for cp in local:
            cp.start()
        sends = []
        for a in range(n):
            for r, (px, py) in enumerate(chips):
                sends.append(pltpu.make_async_remote_copy(
                    src_ref=ins[a], dst_ref=outs[a].at[mine], send_sem=send_sems.at[3 * a + r],
                    recv_sem=recv_sems.at[3 * a + r], device_id=(px, py, c), device_id_type=MESH_ID))
        for cp in sends:
            cp.start()
        for a in range(n):
            for r, (px, py) in enumerate(chips):
                pltpu.make_async_remote_copy(
                    src_ref=ins[a], dst_ref=outs[a].at[2 * px + py], send_sem=send_sems.at[3 * a + r],
                    recv_sem=recv_sems.at[3 * a + r], device_id=(px, py, c), device_id_type=MESH_ID).wait_recv()
        for cp in sends:
            cp.wait_send()
        for cp in local:
            cp.wait()

    return pl.pallas_call(
        body, name=name, in_specs=[ANY] * n, out_specs=[ANY] * n,
        out_shape=[_sds((4,) + s.shape, s.dtype) for s in shards],
        scratch_shapes=[pltpu.SemaphoreType.DMA((3 * n,)), pltpu.SemaphoreType.DMA((3 * n,)),
                        pltpu.SemaphoreType.DMA((n,))],
        compiler_params=pltpu.CompilerParams(has_side_effects=True))(*shards)


def _gather_halves_step(shards):
    n = len(shards)

    def copy(ins, outs, send_sems, recv_sems, a, r, chip_of_block, half, to, second):
        k = (3 * n if second else 0) + 3 * a + r
        px, py = chip_of_block
        src = outs[a].at[2 * px + py, half] if second else ins[a].at[half]
        return pltpu.make_async_remote_copy(
            src_ref=src, dst_ref=outs[a].at[2 * px + py, half], send_sem=send_sems.at[k], recv_sem=recv_sems.at[k],
            device_id=to, device_id_type=MESH_ID)

    def first_copies(ins, outs, sems):
        x, y, c = _position()
        chips = [(1 - x, y), (x, 1 - y), (1 - x, 1 - y)]
        return [copy(ins, outs, *sems, a, r, (x, y), c, (px, py, c), False) for a in range(n)
                for r, (px, py) in enumerate(chips)]

    def start(ins, outs, sems):
        for cp in first_copies(ins, outs, sems):
            cp.start()

    def finish(ins, outs, sems):
        x, y, c = _position()
        chips = [(1 - x, y), (x, 1 - y), (1 - x, 1 - y)]
        passed = []
        for a in range(n):
            for r, chip in enumerate(chips):
                copy(ins, outs, *sems, a, r, chip, c, (x, y, c), False).wait_recv()
                passed.append(copy(ins, outs, *sems, a, r, chip, c, (x, y, 1 - c), True))
                passed[-1].start()
        for a in range(n):
            for r, chip in enumerate(chips):
                copy(ins, outs, *sems, a, r, chip, 1 - c, (x, y, c), True).wait_recv()
        for cp in first_copies(ins, outs, sems) + passed:
            cp.wait_send()

    return _Hosted(shards, [_sds((4,) + s.shape, s.dtype) for s in shards],
                   [pltpu.SemaphoreType.DMA((6 * n,)), pltpu.SemaphoreType.DMA((6 * n,))], start, finish)


def _exchange_step(v):
    def copies(ins, outs, sems):
        x, y, c = _position()
        mine = 2 * x + y
        return [pltpu.make_async_remote_copy(
            src_ref=ins[0].at[2 * px + py], dst_ref=outs[0].at[mine], send_sem=sems[0].at[r], recv_sem=sems[1].at[r],
            device_id=(px, py, c), device_id_type=MESH_ID) for r, (px, py) in enumerate([(1 - x, y), (x, 1 - y), (1 - x, 1 - y)])]

    def start(ins, outs, sems):
        for cp in copies(ins, outs, sems):
            cp.start()

    def finish(ins, outs, sems):
        x, y, c = _position()
        mine = 2 * x + y
        for r, (px, py) in enumerate([(1 - x, y), (x, 1 - y), (1 - x, 1 - y)]):
            pltpu.make_async_remote_copy(
                src_ref=ins[0].at[mine], dst_ref=outs[0].at[2 * px + py], send_sem=sems[0].at[r], recv_sem=sems[1].at[r],
                device_id=(px, py, c), device_id_type=MESH_ID).wait_recv()
        for cp in copies(ins, outs, sems):
            cp.wait_send()

    return _Hosted([v], [_sds(v.shape, v.dtype)], [pltpu.SemaphoreType.DMA((3,)), pltpu.SemaphoreType.DMA((3,))],
                   start, finish)


def _run_step(step, *, name):
    n = len(step.args)

    def body(*refs):
        ins, outs, sems = refs[:n], refs[n:n + len(step.out_shapes)], refs[n + len(step.out_shapes):]
        step.start(ins, outs, sems)
        step.finish(ins, outs, sems)

    return pl.pallas_call(body, name=name, in_specs=[ANY] * n, out_specs=[ANY] * len(step.out_shapes),
                          out_shape=step.out_shapes, scratch_shapes=step.sems,
                          compiler_params=pltpu.CompilerParams(has_side_effects=True))(*step.args)


def _place_own(results, own):
    x, y, _ = _position()
    return [lax.dynamic_update_index_in_dim(r, o, 2 * x + y, 0) for r, o in zip(results, own)]


def _sibling_send(v, *, name):
    def body(v_ref, got_ref, send_sem, recv_sem):
        x, y, c = _position()
        cp = pltpu.make_async_remote_copy(src_ref=v_ref, dst_ref=got_ref, send_sem=send_sem, recv_sem=recv_sem,
                                          device_id=(x, y, 1 - c), device_id_type=MESH_ID)
        cp.start()
        cp.wait()

    return pl.pallas_call(
        body, name=name, in_specs=[ANY], out_specs=ANY, out_shape=_sds(v.shape, v.dtype),
        scratch_shapes=[pltpu.SemaphoreType.DMA, pltpu.SemaphoreType.DMA],
        compiler_params=pltpu.CompilerParams(has_side_effects=True))(v)


def _all_sum(v, *, name):
    r = v.shape[0]
    masks = [(mx, my, mc) for mx in (0, 1) for my in (0, 1) for mc in (0, 1)][1:]

    def body(v_ref, out_ref, slots, send_sems, recv_sems, local_sem):
        x, y, c = _position()
        me = 4 * x + 2 * y + c

        def peer(mask):
            return tuple(1 - p if bit else p for p, bit in zip((x, y, c), mask))

        local = pltpu.make_async_copy(v_ref, slots.at[me], local_sem)
        local.start()
        sends = [pltpu.make_async_remote_copy(
            src_ref=v_ref, dst_ref=slots.at[me], send_sem=send_sems.at[k], recv_sem=recv_sems.at[k],
            device_id=peer(mask), device_id_type=MESH_ID) for k, mask in enumerate(masks)]
        for cp in sends:
            cp.start()
        for k, mask in enumerate(masks):
            px, py, pc = peer(mask)
            pltpu.make_async_remote_copy(
                src_ref=v_ref, dst_ref=slots.at[4 * px + 2 * py + pc], send_sem=send_sems.at[k],
                recv_sem=recv_sems.at[k], device_id=(px, py, pc), device_id_type=MESH_ID).wait_recv()
        for cp in sends:
            cp.wait_send()
        local.wait()
        acc = slots[0]
        for s in range(1, 8):
            acc = acc + slots[s]
        out_ref[...] = acc

    vm = pl.BlockSpec(memory_space=pltpu.VMEM)
    return pl.pallas_call(
        body, name=name, in_specs=[vm], out_specs=vm, out_shape=_sds((r, LANES)),
        scratch_shapes=[pltpu.VMEM((8, r, LANES), F32), pltpu.SemaphoreType.DMA((7,)), pltpu.SemaphoreType.DMA((7,)),
                        pltpu.SemaphoreType.DMA],
        compiler_params=pltpu.CompilerParams(vmem_limit_bytes=VMEM_LIMIT_BYTES, has_side_effects=True))(v)


def _pack_rows(arrays, align=SUBLANES * LANES):
    flat = jnp.concatenate([a.reshape(-1) for a in arrays])
    n = flat.shape[0]
    pad = (-n) % align
    if pad:
        flat = jnp.concatenate([flat, jnp.zeros((pad,), flat.dtype)])
    return flat.reshape(-1, LANES), [a.shape for a in arrays]


def _unpack_rows(rows, shapes):
    flat = rows.reshape(-1)
    out, off = [], 0
    for s in shapes:
        n = int(np.prod(s))
        out.append(flat[off:off + n].reshape(s))
        off += n
    return out


def _pad_w_in(w):
    d = w.shape[0]
    return jnp.concatenate([w[:, 0:2560], w[:, 2564:4612], w[:, 4620:6156], w[:, 2560:2564], w[:, 4612:4620],
                            jnp.zeros((d, Z_COLS - IN_COLS), w.dtype)], axis=1)


def _unpad_w_in(g):
    return jnp.concatenate([g[:, 0:2560], g[:, 6144:6148], g[:, 2560:4608], g[:, 6148:6156], g[:, 4608:6144]], axis=1)


def _gate_rows(p):
    return (_lane_row(p["fox_f_bias"], G_F), _lane_row(p["gdn_a_log"], G_ALPHA), _lane_row(p["gdn_dt_bias"], G_ALPHA))


def _head_cols(c_rows, t):
    ct = c_rows[:, :N_HEADS].T
    return ct.reshape(N_HEADS, t, 1), ct.reshape(N_HEADS, 1, t)


def _layer_fwd(x, p, hosted=None):
    t = x.shape[0]
    s = {"x": x}
    s["h"] = _rmsnorm_fwd(x, p["norm_mix"], name="mix_norm")
    z = s["z"] = _matmul(s["h"], p["w_in"], name="in_proj")
    s["xc"] = _conv_fwd(z, p["lru_conv_w"], p["lru_conv_b"], ncols=GROUP, coff=Z_AX, name="lru_conv")
    lru = s["lru"] = (_block_diag(p["lru_wa"]).astype(MM_DTYPE), p["lru_ba"].reshape(1, GROUP),
                      _block_diag(p["lru_wx"]).astype(MM_DTYPE), p["lru_bx"].reshape(1, GROUP),
                      p["lru_lambda"].reshape(1, GROUP))
    s["a"], u = _lru_gates_fwd(s["xc"], *lru, name="lru_gates")
    s["ha"] = _scan(s["a"], u, name="lru_scan")
    y_a = _post_fwd(s["ha"], p["norm_a"], z, group=LRU_BD, gate_col=Z_AG // GROUP, name="lru_post")
    s["gts"] = _gates_fwd(z, *_gate_rows(p), name="gates")
    s["cq"], s["ck"] = _head_cols(_scan(None, s["gts"], name="fox_cumsum"), t)
    s["qkv_b"] = _qkv_prep(z, off=Z_BQ, name="fox_prep")
    (s["ob"], s["lse_b"]), hosted_out = _att_fwd(s["qkv_b"], s["cq"], s["ck"], mode="fox", name="fox_att", hosted=hosted)
    y_b = _post_fwd(s["ob"], p["norm_b"], z, group=HEAD, name="fox_post")
    s["cconv"] = _conv_fwd(z, p["gdn_conv_w"], jnp.zeros((3 * GROUP,), F32), ncols=3 * GROUP, coff=Z_CQKV,
                           name="gdn_conv")
    s["gdn_mids"] = _gdn_pre_fwd(s["cconv"], s["gts"], name="gdn_pre")
    s["oc"], s["gdn_states"] = _gdn_seq_fwd(s["gdn_mids"], name="gdn_seq")
    y_c = _gdn_out_fwd(s["oc"], z, p["gdn_norm"].reshape(1, HEAD), name="gdn_out")
    s["qkv_d"] = _qkv_prep(z, off=Z_DQ, name="dil_prep")
    (s["od"], s["lse_d"]), _ = _att_fwd(s["qkv_d"], s["cq"], s["ck"], mode="dil", name="dil_att")
    y_d = _post_fwd(s["od"], p["norm_d"], z, group=HEAD, name="dil_post")
    y = s["y"] = jnp.concatenate([y_a, y_b, y_c, y_d], axis=1)
    x1 = s["x1"] = _matmul(y, p["w_out"], add=x, name="out_proj")
    s["h2"] = _rmsnorm_fwd(x1, p["norm_ffn"], name="ffn_norm")
    s["uu"] = _matmul(s["h2"], p["ffn_w_up"], name="ffn_up")
    s["act"] = _ffn_act_fwd(s["uu"], p["ffn_conv_w"], p["ffn_conv_b"], name="ffn_conv_swiglu")
    return _matmul(s["act"], p["ffn_w_down"], add=x1, name="ffn_down"), s, hosted_out


def _layer_bwd(dx2, p, s, hosted=None):
    t = dx2.shape[0]
    g = {}
    dact = _matmul(dx2, p["ffn_w_down"], form="nt", name="ffn_down_dx")
    g["ffn_w_down"] = _matmul(s["act"], dx2, form="tn", name="ffn_down_dw")
    du = _ffn_act_bwd(s["uu"], p["ffn_conv_w"], p["ffn_conv_b"], dact, name="ffn_conv_swiglu_bwd")
    duu, g["ffn_conv_w"], g["ffn_conv_b"] = _conv_bwd(s["uu"], p["ffn_conv_w"], du, ncols=s["uu"].shape[1],
                                                      name="ffn_conv_bwd")
    dh2 = _matmul(duu, p["ffn_w_up"], form="nt", name="ffn_up_dx")
    g["ffn_w_up"] = _matmul(s["h2"], duu, form="tn", name="ffn_up_dw")
    dx1, g["norm_ffn"] = _rmsnorm_bwd(s["x1"], p["norm_ffn"], dh2, dx2, name="ffn_norm_bwd")
    dy = _matmul(dx1, p["w_out"], form="nt", name="out_proj_dx")
    g["w_out"] = _matmul(s["y"], dx1, form="tn", name="out_proj_dw")
    z = s["z"]
    dha, g["norm_a"], dgate_a = _post_bwd(s["ha"], p["norm_a"], z, dy, group=LRU_BD, dy_col=0, gate_col=Z_AG // GROUP,
                                          name="lru_post_bwd")
    gsc = _scan(_shift_up(s["a"]), dha, reverse=True, name="lru_scan_bwd")
    dxc, dwa, dba, dwx, dbx, dlam = _lru_gates_bwd(s["xc"], *s["lru"], gsc, _shift_down(s["ha"]), name="lru_gates_bwd")
    g["lru_wa"], g["lru_wx"] = _diag_blocks(dwa), _diag_blocks(dwx)
    g["lru_ba"], g["lru_bx"], g["lru_lambda"] = dba.reshape(GROUP), dbx.reshape(GROUP), dlam.reshape(GROUP)
    dax, g["lru_conv_w"], g["lru_conv_b"] = _conv_bwd(z, p["lru_conv_w"], dxc, ncols=GROUP, coff=Z_AX,
                                                      name="lru_conv_bwd")
    dob, g["norm_b"], _ = _post_bwd(s["ob"], p["norm_b"], z, dy, group=HEAD, dy_col=1, name="fox_post_bwd")
    dbq, dcq, delta = _att_bwd_q(s["qkv_b"], s["cq"], s["ck"], dob, s["ob"], s["lse_b"], mode="fox", name="fox_att_dq")
    (dbk, dbv, dck), hosted_out = _att_bwd_kv(s["qkv_b"], s["cq"], s["ck"], dob, s["lse_b"].reshape(N_HEADS, 1, t),
                                              delta.reshape(N_HEADS, 1, t), mode="fox", name="fox_att_dkv", hosted=hosted)
    pad_lanes = ((0, 0), (0, LANES - N_HEADS))
    dc_rows = _sum_slots([jnp.pad(dcq.reshape(N_HEADS, t).T, pad_lanes), jnp.pad(dck.reshape(N_HEADS, t).T, pad_lanes)],
                         name="fox_dc_sum")
    dgts_fox = _scan(None, dc_rows, reverse=True, name="fox_cumsum_bwd")
    dod, g["norm_d"], _ = _post_bwd(s["od"], p["norm_d"], z, dy, group=HEAD, dy_col=3, name="dil_post_bwd")
    ddq, _, delta = _att_bwd_q(s["qkv_d"], s["cq"], s["ck"], dod, s["od"], s["lse_d"], mode="dil", name="dil_att_dq")
    (ddk, ddv, _), _ = _att_bwd_kv(s["qkv_d"], s["cq"], s["ck"], dod, s["lse_d"].reshape(N_HEADS, 1, t),
                                   delta.reshape(N_HEADS, 1, t), mode="dil", name="dil_att_dkv")
    doc, dcz, g["gdn_norm"] = _gdn_out_bwd(s["oc"], z, p["gdn_norm"].reshape(1, HEAD), dy, dy_col=2, name="gdn_out_bwd")
    dmids = _gdn_seq_bwd(s["gdn_mids"], s["gdn_states"], doc, name="gdn_seq_bwd")
    dcconv, dgts_gdn = _gdn_pre_bwd(s["cconv"], s["gts"], dmids, name="gdn_pre_bwd")
    dcqkv, g["gdn_conv_w"], _ = _conv_bwd(z, p["gdn_conv_w"], dcconv, ncols=3 * GROUP, coff=Z_CQKV,
                                          name="gdn_conv_bwd")
    dgts = _sum_slots([dgts_fox, dgts_gdn], name="gates_dsum")
    dzg, dfb, dal, ddt = _gates_bwd(z, *_gate_rows(p), dgts, name="gates_bwd")
    g["fox_f_bias"] = dfb[0, G_F:G_F + N_HEADS]
    g["gdn_a_log"] = dal[0, G_ALPHA:G_ALPHA + N_HEADS]
    g["gdn_dt_bias"] = ddt[0, G_ALPHA:G_ALPHA + N_HEADS]
    dz = jnp.concatenate([dax, dgate_a, dbq, dbk, dbv, dcqkv, dcz, ddq, ddk, ddv, dzg], axis=1)
    dh = _matmul(dz, p["w_in"], form="nt", name="in_proj_dx")
    g["w_in"] = _matmul(s["h"], dz, form="tn", name="in_proj_dw")
    dx, g["norm_mix"] = _rmsnorm_bwd(s["x"], p["norm_mix"], dh, dx1, name="mix_norm_bwd")
    return dx, g, hosted_out


def _local_step(x, tgt, layers, norm_final):
    saved = []
    for p in layers:
        x, s, _ = _layer_fwd(x, p)
        saved.append(s)
    loss, dx, dnf = _loss_head(x, norm_final, tgt, name="loss_head")
    grads = []
    for p, s in zip(reversed(layers), reversed(saved)):
        dx, g, _ = _layer_bwd(dx, p, s)
        grads.append(g)
    return loss, dx, grads[::-1], dnf


BIG = ("w_in", "w_out", "ffn_w_up", "ffn_w_down")
PACK_ROWS = 4096
SHARDED_SMALL = ("lru_conv_w", "gdn_conv_w", "ffn_conv_w")
NAMES = ("norm_mix", "w_in", "lru_conv_w", "lru_conv_b", "lru_wa", "lru_ba", "lru_wx", "lru_bx", "lru_lambda",
         "fox_f_bias", "gdn_conv_w", "gdn_a_log", "gdn_dt_bias", "gdn_norm", "norm_a", "norm_b", "norm_d", "w_out",
         "norm_ffn", "ffn_w_up", "ffn_conv_w", "ffn_conv_b", "ffn_w_down", "norm_final")
SMALL = tuple(n for n in NAMES if n not in BIG)


def _big_pieces(g, k_axis_cols):
    if k_axis_cols:
        d, n = g.shape
        return g.reshape(d, 4, n // 4).transpose(1, 0, 2).reshape(4, -1, LANES)
    return g.reshape(4, -1, LANES)


def _reduce_start(gl, *, c):
    whole = {"w_in": _unpad_w_in(gl["w_in"]), "w_out": gl["w_out"], "ffn_w_up": gl["ffn_w_up"],
             "ffn_w_down": gl["ffn_w_down"]}
    cols = {"w_in": True, "w_out": False, "ffn_w_up": True, "ffn_w_down": False}
    pieces = [_big_pieces(whole[n], cols[n]) for n in BIG]
    rows = [q.shape[1] for q in pieces]
    pad = (-sum(rows)) % PACK_ROWS
    packed = jnp.concatenate(pieces + [jnp.zeros((4, pad, LANES), F32)], axis=1)
    half = packed.shape[1] // 2
    halves = packed.reshape(4, 2, half, LANES)
    mine = lax.dynamic_index_in_dim(halves, c, axis=1, keepdims=False).reshape(4 * half, LANES)
    other = lax.dynamic_index_in_dim(halves, 1 - c, axis=1, keepdims=False).reshape(4 * half, LANES)
    got = _sibling_send(other, name="grad_sibling_send")
    chip_sum = _sum_slots([mine, got], out_dtype=MM_DTYPE, name="grad_sibling_sum").reshape(4, half, LANES)
    return chip_sum, rows


def _reduce_finish(chip_sum, from_chips, rows, *, c):
    x, y, _ = _position()
    (from_chips,) = _place_own([from_chips], [lax.dynamic_index_in_dim(chip_sum, 2 * x + y, 0, keepdims=False)])
    total_half = _sum_slots(from_chips, name="grad_chip_sum")
    other_half = _sibling_send(total_half, name="grad_sibling_swap")
    total = jnp.where(c == 0, jnp.concatenate([total_half, other_half]), jnp.concatenate([other_half, total_half]))
    out, off = {}, 0
    for n, r in zip(BIG, rows):
        out[n] = total[off:off + r]
        off += r
    return out


def kernel(x, norm_mix, w_in, lru_conv_w, lru_conv_b, lru_wa, lru_ba, lru_wx, lru_bx, lru_lambda, fox_f_bias, gdn_conv_w, gdn_a_log, gdn_dt_bias, gdn_norm, norm_a, norm_b, norm_d, w_out, norm_ffn, ffn_w_up, ffn_conv_w, ffn_conv_b, ffn_w_down, norm_final, loss_target, m_norm_mix, m_w_in, m_lru_conv_w, m_lru_conv_b, m_lru_wa, m_lru_ba, m_lru_wx, m_lru_bx, m_lru_lambda, m_fox_f_bias, m_gdn_conv_w, m_gdn_a_log, m_gdn_dt_bias, m_gdn_norm, m_norm_a, m_norm_b, m_norm_d, m_w_out, m_norm_ffn, m_ffn_w_up, m_ffn_conv_w, m_ffn_conv_b, m_ffn_w_down, m_norm_final, v_norm_mix, v_w_in, v_lru_conv_w, v_lru_conv_b, v_lru_wa, v_lru_ba, v_lru_wx, v_lru_bx, v_lru_lambda, v_fox_f_bias, v_gdn_conv_w, v_gdn_a_log, v_gdn_dt_bias, v_gdn_norm, v_norm_a, v_norm_b, v_norm_d, v_w_out, v_norm_ffn, v_ffn_w_up, v_ffn_conv_w, v_ffn_conv_b, v_ffn_w_down, v_norm_final):
    w = dict(zip(NAMES, (norm_mix, w_in, lru_conv_w, lru_conv_b, lru_wa, lru_ba, lru_wx, lru_bx, lru_lambda, fox_f_bias,
                         gdn_conv_w, gdn_a_log, gdn_dt_bias, gdn_norm, norm_a, norm_b, norm_d, w_out, norm_ffn, ffn_w_up,
                         ffn_conv_w, ffn_conv_b, ffn_w_down, norm_final)))
    m = dict(zip(NAMES, (m_norm_mix, m_w_in, m_lru_conv_w, m_lru_conv_b, m_lru_wa, m_lru_ba, m_lru_wx, m_lru_bx,
                         m_lru_lambda, m_fox_f_bias, m_gdn_conv_w, m_gdn_a_log, m_gdn_dt_bias, m_gdn_norm, m_norm_a,
                         m_norm_b, m_norm_d, m_w_out, m_norm_ffn, m_ffn_w_up, m_ffn_conv_w, m_ffn_conv_b, m_ffn_w_down,
                         m_norm_final)))
    v = dict(zip(NAMES, (v_norm_mix, v_w_in, v_lru_conv_w, v_lru_conv_b, v_lru_wa, v_lru_ba, v_lru_wx, v_lru_bx,
                         v_lru_lambda, v_fox_f_bias, v_gdn_conv_w, v_gdn_a_log, v_gdn_dt_bias, v_gdn_norm, v_norm_a,
                         v_norm_b, v_norm_d, v_w_out, v_norm_ffn, v_ffn_w_up, v_ffn_conv_w, v_ffn_conv_b, v_ffn_w_down,
                         v_norm_final)))
    depth = w_in.shape[0]
    xi, yi, ci = _position()
    chip = 2 * xi + yi

    conv_rows, conv_shapes = _pack_rows([w[n] for n in SHARDED_SMALL])
    (conv_all,) = _chip_gather([conv_rows], name="conv_taps_gather")
    conv_full = {}
    per_chip = [_unpack_rows(conv_all[k], conv_shapes) for k in range(4)]
    for i, n in enumerate(SHARDED_SMALL):
        conv_full[n] = jnp.concatenate([per_chip[k][i] for k in range(4)], axis=-1)
    halves = lambda l: [w[n][l].astype(MM_DTYPE).reshape(2, w[n].shape[1] // 2, w[n].shape[2]) for n in BIG]

    def layer_params(l, gathered):
        g_in, g_out, g_up, g_dn = (g.reshape((4,) + w[n].shape[1:]) for n, g in zip(BIG, gathered))
        p = {n: w[n][l] for n in SMALL if n != "norm_final" and n not in SHARDED_SMALL}
        for n in SHARDED_SMALL:
            p[n] = conv_full[n][l]
        p["w_in"] = _pad_w_in(jnp.concatenate([g_in[k] for k in range(4)], axis=1))
        p["w_out"] = g_out.reshape(-1, g_out.shape[-1])
        p["ffn_w_up"] = jnp.concatenate([g_up[k] for k in range(4)], axis=1)
        p["ffn_w_down"] = g_dn.reshape(-1, g_dn.shape[-1])
        return p

    gathered = _place_own(_run_step(_gather_halves_step(halves(0)), name="weights_gather"), halves(0))
    layers, saved, xl = [], [], x[0]
    for l in range(depth):
        p = layer_params(l, gathered)
        step = _gather_halves_step(halves(l + 1)) if l + 1 < depth else None
        xl, s, results = _layer_fwd(xl, p, hosted=step)
        if step is not None:
            gathered = _place_own(results, halves(l + 1))
        layers.append(p)
        saved.append(s)
    loss, dx, g_norm_final = _loss_head(xl, norm_final, loss_target[0], name="loss_head")
    loss = lax.psum(loss, ("x", "y", "c"))
    grads, big, pending = [None] * depth, [None] * depth, None
    for l in reversed(range(depth)):
        step = None if pending is None else _exchange_step(pending[1])
        dx, grads[l], results = _layer_bwd(dx, layers[l], saved[l], hosted=step)
        if pending is not None:
            big[pending[0]] = _reduce_finish(pending[1], results[0], pending[2], c=ci)
        pending = (l,) + _reduce_start(grads[l], c=ci)
    (last_exchange,) = _run_step(_exchange_step(pending[1]), name="grad_chip_exchange")
    big[pending[0]] = _reduce_finish(pending[1], last_exchange, pending[2], c=ci)
    grad_x = dx

    small_names = [n for n in SMALL if n != "norm_final"]
    small_rows, small_shapes = _pack_rows([jnp.stack([grads[l][n] for l in range(depth)]) for n in small_names]
                                          + [g_norm_final])
    small_sum = _unpack_rows(_all_sum(small_rows, name="small_grads_sum"), small_shapes)
    gsum = dict(zip(small_names + ["norm_final"], small_sum))
    for n in SHARDED_SMALL:
        width = w[n].shape[-1]
        gsum[n] = lax.dynamic_slice_in_dim(gsum[n], chip * width, width, axis=-1)
    for n in BIG:
        gsum[n] = jnp.stack([big[l][n].reshape(w[n].shape[1:]) for l in range(depth)])

    delta, new_m, new_v = {}, {}, {}
    for n in BIG:
        cols = w[n].shape[-1]
        d_, m_, v_ = _adamw(w[n].reshape(-1, cols), gsum[n].reshape(-1, cols), m[n].reshape(-1, cols),
                            v[n].reshape(-1, cols), name="adamw_" + n)
        delta[n], new_m[n], new_v[n] = (a.reshape(w[n].shape) for a in (d_, m_, v_))
    packs = [_pack_rows([src[n] for n in SMALL]) for src in (w, gsum, m, v)]
    outs = _adamw(*[pk[0] for pk in packs], name="adamw_small")
    for dst, rows_ in zip((delta, new_m, new_v), outs):
        dst.update(zip(SMALL, _unpack_rows(rows_, packs[0][1])))
    return (loss, grad_x[None], *[gsum[n] for n in NAMES], *[delta[n] for n in NAMES], *[new_m[n] for n in NAMES],
            *[new_v[n] for n in NAMES])
```

```python
import functools

import numpy as np
import jax
import jax.numpy as jnp
from jax import lax
from jax.experimental import pallas as pl
from jax.experimental.pallas import tpu as pltpu

F32 = jnp.float32
MM_DTYPE = jnp.bfloat16
VMEM_LIMIT_BYTES = 56 * 1024 * 1024
LANES = 128
SUBLANES = 8

GROUP = 512
HEAD = 128
N_HEADS = GROUP // HEAD
LRU_BLOCKS = 8
LRU_BD = GROUP // LRU_BLOCKS
LRU_C = 8.0
GDN_CHUNK = 64
DILATED_PAIRS = ((128, 1), (512, 4), (2048, 16))
EPS = 1e-6
NEG = -1e30
ATT_SCALE = HEAD ** -0.5
ATT_TILE = 512

ADAM_LR, ADAM_B1, ADAM_B2, ADAM_EPS, ADAM_WD, ADAM_STEP = 0.001, 0.9, 0.999, 1e-08, 0.01, 10

Z_AX, Z_AG, Z_BQ, Z_BK, Z_BV = 0, 512, 1024, 1536, 2048
Z_CQKV, Z_CZ, Z_DQ, Z_DK, Z_DV, Z_GATES, Z_COLS = 2560, 4096, 4608, 5120, 5632, 6144, 6272
IN_COLS = 6156
G_F, G_BETA, G_ALPHA = 0, 4, 8

MESH_ID = pl.DeviceIdType.MESH
ANY = pl.BlockSpec(memory_space=pl.ANY)


def _tile(n, target, align):
    t = min(n, target) // align * align
    while t >= align:
        if n % t == 0:
            return t
        t -= align
    return n


def _params(*sem):
    return pltpu.CompilerParams(dimension_semantics=sem, vmem_limit_bytes=VMEM_LIMIT_BYTES)


def _sds(shape, dtype=F32):
    return jax.ShapeDtypeStruct(tuple(shape), dtype)


def _dg(a, b, ca, cb, hi, batched=False):
    off = int(batched)
    dims = (((ca + off,), (cb + off,)), (((0,), (0,)) if batched else ((), ())))
    dot = lambda p, q: lax.dot_general(p, q, dims, preferred_element_type=F32)
    if hi:
        a_hi, b_hi = a.astype(MM_DTYPE), b.astype(MM_DTYPE)
        a_lo = (a - a_hi.astype(F32)).astype(MM_DTYPE)
        b_lo = (b - b_hi.astype(F32)).astype(MM_DTYPE)
        return dot(a_hi, b_hi) + (dot(a_hi, b_lo) + dot(a_lo, b_hi))
    return dot(a.astype(MM_DTYPE), b.astype(MM_DTYPE))


_FORMS = {"nn": (1, 0), "nt": (1, 1), "tn": (0, 0)}


@functools.partial(jax.custom_vjp, nondiff_argnums=(2, 3, 4))
def _mmf(a, b, form, hi, batched=False):
    ca, cb = _FORMS[form]
    return _dg(a, b, ca, cb, hi, batched)


def _mmf_fwd(a, b, form, hi, batched):
    return _mmf(a, b, form, hi, batched), (a, b)


def _mmf_bwd(form, hi, batched, res, g):
    a, b = res
    if form == "nn":
        return _dg(g, b, 1, 1, hi, batched), _dg(a, g, 0, 0, hi, batched)
    if form == "nt":
        return _dg(g, b, 1, 0, hi, batched), _dg(g, a, 0, 0, hi, batched)
    return _dg(b, g, 1, 1, hi, batched), _dg(a, g, 1, 0, hi, batched)


_mmf.defvjp(_mmf_fwd, _mmf_bwd)


def _mm(a, b):
    return _mmf(a, b, "nn", False)


def _mm_nt(a, b):
    return _mmf(a, b, "nt", False)


def _mm_tn(a, b):
    return _mmf(a, b, "tn", False)


MXU_DIM = 256
MATMUL_VMEM_BUDGET = 40 * 1024 * 1024


def _matmul_tiles(m, n, k, a_bytes, b_bytes, o_bytes, has_add):
    divisors = lambda d: [t for t in range(d, 0, -LANES) if d % t == 0 and t % LANES == 0] or [d]
    for tk in divisors(k):
        fits = []
        for tm in divisors(m):
            for tn in divisors(n):
                need = 2 * (tm * tk * a_bytes + tk * tn * b_bytes) + 2 * tm * tn * o_bytes
                need += (2 * tm * tn * 4 if has_add else 0) + (tm * tn * 4 if tk < k else 0)
                if tm <= 1024 and tn <= 1024 and need <= MATMUL_VMEM_BUDGET:
                    fits.append((min(tm, MXU_DIM) * min(tn, MXU_DIM), tm * tn, tm, tn))
        if fits:
            return max(fits)[2:] + (tk,)
    return _tile(m, 128, LANES), _tile(n, 128, LANES), _tile(k, 128, LANES)


def _matmul(a, b, *, form="nn", add=None, out_dtype=F32, tiles=None, name):
    ca, cb = _FORMS[form]
    m, k = (a.shape[1], a.shape[0]) if form == "tn" else a.shape
    n = b.shape[0] if form == "nt" else b.shape[1]
    tm, tn, tk = tiles or _matmul_tiles(m, n, k, a.dtype.itemsize, b.dtype.itemsize, jnp.dtype(out_dtype).itemsize,
                                        add is not None)
    nk = k // tk
    a_spec = (pl.BlockSpec((tk, tm), lambda i, j, kk: (kk, i)) if form == "tn"
              else pl.BlockSpec((tm, tk), lambda i, j, kk: (i, kk)))
    b_spec = (pl.BlockSpec((tn, tk), lambda i, j, kk: (j, kk)) if form == "nt"
              else pl.BlockSpec((tk, tn), lambda i, j, kk: (kk, j)))
    o_spec = pl.BlockSpec((tm, tn), lambda i, j, kk: (i, j))

    def body(*refs):
        a_ref, b_ref = refs[:2]
        add_ref = None if add is None else refs[2]
        o_ref = refs[2 + (add is not None)]
        part = _dg(a_ref[...], b_ref[...], ca, cb, False)
        if nk == 1:
            o_ref[...] = (part if add is None else part + add_ref[...]).astype(out_dtype)
            return
        acc_ref = refs[-1]
        kk = pl.program_id(2)

        @pl.when(kk == 0)
        def _():
            acc_ref[...] = jnp.zeros_like(acc_ref)

        acc_ref[...] += part

        @pl.when(kk == nk - 1)
        def _():
            r = acc_ref[...]
            if add is not None:
                r = r + add_ref[...]
            o_ref[...] = r.astype(out_dtype)

    args = (a, b) if add is None else (a, b, add)
    specs = [a_spec, b_spec] if add is None else [a_spec, b_spec, o_spec]
    return pl.pallas_call(
        body, name=name, grid=(m // tm, n // tn, nk), in_specs=specs, out_specs=o_spec,
        out_shape=_sds((m, n), out_dtype), scratch_shapes=[] if nk == 1 else [pltpu.VMEM((tm, tn), F32)],
        compiler_params=_params("parallel", "parallel", "arbitrary"))(*args)


def _rms(x, g):
    return x * lax.rsqrt(jnp.mean(x * x, axis=-1, keepdims=True) + EPS) * g


def _rmsnorm_fwd(x, g, *, name):
    t, d = x.shape
    tb = _tile(t, 512, SUBLANES)

    def body(x_ref, g_ref, o_ref):
        o_ref[...] = _rms(x_ref[...], g_ref[...]).astype(o_ref.dtype)

    row = pl.BlockSpec((tb, d), lambda i: (i, 0))
    vec = pl.BlockSpec((1, d), lambda i: (0, 0))
    return pl.pallas_call(body, name=name, grid=(t // tb,), in_specs=[row, vec], out_specs=row,
                          out_shape=_sds((t, d), MM_DTYPE), compiler_params=_params("parallel"))(x, g.reshape(1, d))


def _rmsnorm_bwd(x, g, dh, res, *, name):
    t, d = x.shape
    tb = _tile(t, 256, SUBLANES)

    def body(x_ref, g_ref, dh_ref, res_ref, dx_ref, dg_ref):
        _, vjp = jax.vjp(_rms, x_ref[...], g_ref[...])
        dx, dg = vjp(dh_ref[...])
        dx_ref[...] = dx + res_ref[...]

        @pl.when(pl.program_id(0) == 0)
        def _():
            dg_ref[...] = jnp.zeros_like(dg_ref)

        dg_ref[...] += dg

    row = pl.BlockSpec((tb, d), lambda i: (i, 0))
    vec = pl.BlockSpec((1, d), lambda i: (0, 0))
    dx, dg = pl.pallas_call(body, name=name, grid=(t // tb,), in_specs=[row, vec, row, row], out_specs=[row, vec],
                            out_shape=[_sds((t, d)), _sds((1, d))], compiler_params=_params("arbitrary"))(
                                x, g.reshape(1, d), dh, res)
    return dx, dg.reshape(d)


def _loss_head(x, g, tgt, *, name):
    t, d = x.shape
    tb = _tile(t, 256, SUBLANES)

    def body(x_ref, g_ref, t_ref, loss_ref, dx_ref, dg_ref):
        def f(xv, gv):
            e = _rms(xv, gv) - t_ref[...]
            return jnp.sum(jnp.sum(e * e, axis=-1, keepdims=True), axis=0, keepdims=True) * (0.5 / d)

        l, vjp = jax.vjp(f, x_ref[...], g_ref[...])
        dx, dg = vjp(jnp.ones((1, 1), F32))
        dx_ref[...] = dx

        @pl.when(pl.program_id(0) == 0)
        def _():
            dg_ref[...] = jnp.zeros_like(dg_ref)
            loss_ref[...] = jnp.zeros_like(loss_ref)

        dg_ref[...] += dg
        loss_ref[...] += jnp.zeros(loss_ref.shape, F32) + l

    row = pl.BlockSpec((tb, d), lambda i: (i, 0))
    vec = pl.BlockSpec((1, d), lambda i: (0, 0))
    lspec = pl.BlockSpec((SUBLANES, LANES), lambda i: (0, 0))
    loss, dx, dg = pl.pallas_call(
        body, name=name, grid=(t // tb,), in_specs=[row, vec, row], out_specs=[lspec, row, vec],
        out_shape=[_sds((SUBLANES, LANES)), _sds((t, d)), _sds((1, d))], compiler_params=_params("arbitrary"))(
            x, g.reshape(1, d), tgt)
    return loss[0, 0], dx, dg.reshape(d)


def _delayed(x, prev, j):
    if j == 0:
        return x
    sh = pltpu.roll(x, j, axis=0)
    row = lax.broadcasted_iota(jnp.int32, prev.shape, 0)
    top = jnp.where(row < j, pltpu.roll(prev, j, axis=0), sh[0:SUBLANES])
    return jnp.concatenate([top, sh[SUBLANES:]], axis=0)


def _advanced(x, nxt, j):
    if j == 0:
        return x
    tb = x.shape[0]
    sh = pltpu.roll(x, tb - j, axis=0)
    row = lax.broadcasted_iota(jnp.int32, nxt.shape, 0)
    bot = jnp.where(row + j < SUBLANES, sh[tb - SUBLANES:], pltpu.roll(nxt, SUBLANES - j, axis=0))
    return jnp.concatenate([sh[:tb - SUBLANES], bot], axis=0)


def _conv_tiles(t, ncols, coff):
    tc = _tile(ncols, 512, LANES)
    assert coff % tc == 0
    tb = _tile(t, 512, SUBLANES)
    return tc, tb, coff // tc


def _conv_fwd(x, w, b, *, ncols, coff=0, name):
    t = x.shape[0]
    kw = w.shape[0]
    tc, tb, cb = _conv_tiles(t, ncols, coff)
    n8 = tb // SUBLANES

    def body(x_ref, p_ref, w_ref, b_ref, o_ref):
        xv = x_ref[...]
        prev = jnp.where(pl.program_id(1) > 0, p_ref[...], 0.0)
        acc = jnp.zeros_like(xv) + b_ref[...]
        for k in range(kw):
            acc = acc + w_ref[k:k + 1, :] * _delayed(xv, prev, kw - 1 - k)
        o_ref[...] = acc

    in_specs = [
        pl.BlockSpec((tb, tc), lambda c, i: (i, c + cb)),
        pl.BlockSpec((SUBLANES, tc), lambda c, i: (jnp.maximum(i * n8 - 1, 0), c + cb)),
        pl.BlockSpec((kw, tc), lambda c, i: (0, c)),
        pl.BlockSpec((1, tc), lambda c, i: (0, c)),
    ]
    return pl.pallas_call(
        body, name=name, grid=(ncols // tc, t // tb), in_specs=in_specs,
        out_specs=pl.BlockSpec((tb, tc), lambda c, i: (i, c)), out_shape=_sds((t, ncols)),
        compiler_params=_params("parallel", "parallel"))(x, x, w, b.reshape(1, ncols))


def _conv_bwd(x, w, dy, *, ncols, coff=0, name):
    t = x.shape[0]
    kw = w.shape[0]
    tc, tb, cb = _conv_tiles(t, ncols, coff)
    n8 = tb // SUBLANES
    nt = t // tb

    split = dy.ndim == 3
    nhalf = ncols // tc // 2
    half_of = lambda c: (c >= nhalf).astype(jnp.int32)

    def body(x_ref, p_ref, dy_ref, n_ref, w_ref, dx_ref, dw_ref, db_ref):
        i = pl.program_id(1)
        xv, dyv = x_ref[...], (dy_ref[0] if split else dy_ref[...])
        prev = jnp.where(i > 0, p_ref[...], 0.0)
        nxt = jnp.where(i < nt - 1, n_ref[0] if split else n_ref[...], 0.0)

        @pl.when(i == 0)
        def _():
            dw_ref[...] = jnp.zeros_like(dw_ref)
            db_ref[...] = jnp.zeros_like(db_ref)

        dx = jnp.zeros_like(dyv)
        for k in range(kw):
            j = kw - 1 - k
            dx = dx + w_ref[k:k + 1, :] * _advanced(dyv, nxt, j)
            dw_ref[k:k + 1, :] += jnp.sum(dyv * _delayed(xv, prev, j), axis=0, keepdims=True)
        dx_ref[...] = dx.astype(dx_ref.dtype)
        db_ref[...] += jnp.sum(dyv, axis=0, keepdims=True)

    in_specs = [
        pl.BlockSpec((tb, tc), lambda c, i: (i, c + cb)),
        pl.BlockSpec((SUBLANES, tc), lambda c, i: (jnp.maximum(i * n8 - 1, 0), c + cb)),
        (pl.BlockSpec((1, tb, tc), lambda c, i: (half_of(c), i, c - half_of(c) * nhalf)) if split
         else pl.BlockSpec((tb, tc), lambda c, i: (i, c))),
        (pl.BlockSpec((1, SUBLANES, tc),
                      lambda c, i: (half_of(c), jnp.minimum((i + 1) * n8, nt * n8 - 1), c - half_of(c) * nhalf))
         if split else pl.BlockSpec((SUBLANES, tc), lambda c, i: (jnp.minimum((i + 1) * n8, nt * n8 - 1), c))),
        pl.BlockSpec((kw, tc), lambda c, i: (0, c)),
    ]
    out_specs = [
        pl.BlockSpec((tb, tc), lambda c, i: (i, c)),
        pl.BlockSpec((kw, tc), lambda c, i: (0, c)),
        pl.BlockSpec((1, tc), lambda c, i: (0, c)),
    ]
    dx, dw, db = pl.pallas_call(
        body, name=name, grid=(ncols // tc, nt), in_specs=in_specs, out_specs=out_specs,
        out_shape=[_sds((t, ncols), MM_DTYPE), _sds((kw, ncols)), _sds((1, ncols))],
        compiler_params=_params("parallel", "arbitrary"))(x, x, dy, dy, w)
    return dx, dw, db.reshape(ncols)


def _scan(a, u, *, reverse=False, name):
    t, c = u.shape
    tc = _tile(c, 512, LANES)
    tb = _tile(t, 256, SUBLANES)
    nt = t // tb

    def body(*refs):
        if a is None:
            u_ref, o_ref, carry_ref = refs
        else:
            a_ref, u_ref, o_ref, carry_ref = refs

        @pl.when(pl.program_id(1) == 0)
        def _():
            carry_ref[...] = jnp.zeros_like(carry_ref)

        hv = u_ref[...]
        av = None if a is None else a_ref[...]
        row = lax.broadcasted_iota(jnp.int32, hv.shape, 0)
        s = 1
        while s < tb:
            live = row < tb - s if reverse else row >= s
            shift = tb - s if reverse else s
            h_sh = jnp.where(live, pltpu.roll(hv, shift, axis=0), 0.0)
            if av is None:
                hv = hv + h_sh
            else:
                hv = av * h_sh + hv
                av = av * jnp.where(live, pltpu.roll(av, shift, axis=0), 1.0)
            s *= 2
        hv = hv + carry_ref[...] if av is None else hv + av * carry_ref[...]
        o_ref[...] = hv
        carry_ref[...] = o_ref[pl.ds(0 if reverse else tb - 1, 1), :]

    spec = pl.BlockSpec((tb, tc), (lambda cc, i: (nt - 1 - i, cc)) if reverse else (lambda cc, i: (i, cc)))
    args, specs = ((u,), [spec]) if a is None else ((a, u), [spec, spec])
    return pl.pallas_call(
        body, name=name, grid=(c // tc, t // tb), in_specs=specs, out_specs=spec, out_shape=_sds((t, c)),
        scratch_shapes=[pltpu.VMEM((1, tc), F32)], compiler_params=_params("parallel", "arbitrary"))(*args)


def _shift_down(x):
    return lax.pad(x, jnp.zeros((), x.dtype), ((1, -1, 0), (0, 0, 0)))


def _shift_up(x):
    return lax.pad(x, jnp.zeros((), x.dtype), ((-1, 1, 0), (0, 0, 0)))


def _neg_expm1(y):
    small = -(y * (1.0 + y * (0.5 + y * (1.0 / 6.0 + y * (1.0 / 24.0 + y * (1.0 / 120.0))))))
    return jnp.where(y > -0.05, small, 1.0 - jnp.exp(y))


def _lru_gates(xc, wa, ba, wx, bx, lam):
    r = jax.nn.sigmoid(_mm(xc, wa) + ba)
    i = jax.nn.sigmoid(_mm(xc, wx) + bx)
    log_a = -LRU_C * r * jax.nn.softplus(-lam)
    a = jnp.exp(log_a)
    u = jnp.sqrt(_neg_expm1(2.0 * log_a)) * (i * xc)
    return a, u


def _lru_specs(t):
    tb = _tile(t, 256, SUBLANES)
    row = pl.BlockSpec((tb, GROUP), lambda i: (i, 0))
    mat = pl.BlockSpec((GROUP, GROUP), lambda i: (0, 0))
    vec = pl.BlockSpec((1, GROUP), lambda i: (0, 0))
    return tb, row, mat, vec


def _lru_gates_fwd(xc, wa, ba, wx, bx, lam, *, name):
    t = xc.shape[0]
    tb, row, mat, vec = _lru_specs(t)

    def body(xc_ref, wa_ref, ba_ref, wx_ref, bx_ref, lam_ref, a_ref, u_ref):
        a, u = _lru_gates(xc_ref[...], wa_ref[...], ba_ref[...], wx_ref[...], bx_ref[...], lam_ref[...])
        a_ref[...] = a
        u_ref[...] = u

    return pl.pallas_call(
        body, name=name, grid=(t // tb,), in_specs=[row, mat, vec, mat, vec, vec], out_specs=[row, row],
        out_shape=[_sds((t, GROUP)), _sds((t, GROUP))], compiler_params=_params("parallel"))(xc, wa, ba, wx, bx, lam)


def _lru_gates_bwd(xc, wa, ba, wx, bx, lam, g, h_prev, *, name):
    t = xc.shape[0]
    tb, row, mat, vec = _lru_specs(t)

    def body(xc_ref, wa_ref, ba_ref, wx_ref, bx_ref, lam_ref, g_ref, hp_ref,
             dxc_ref, dwa_ref, dba_ref, dwx_ref, dbx_ref, dlam_ref):
        _, vjp = jax.vjp(_lru_gates, xc_ref[...], wa_ref[...], ba_ref[...], wx_ref[...], bx_ref[...], lam_ref[...])
        gv = g_ref[...]
        dxc, dwa, dba, dwx, dbx, dlam = vjp((gv * hp_ref[...], gv))
        dxc_ref[...] = dxc
        accs = (dwa_ref, dba_ref, dwx_ref, dbx_ref, dlam_ref)

        @pl.when(pl.program_id(0) == 0)
        def _():
            for r in accs:
                r[...] = jnp.zeros_like(r)

        for r, v in zip(accs, (dwa, dba, dwx, dbx, dlam)):
            r[...] += v

    return pl.pallas_call(
        body, name=name, grid=(t // tb,), in_specs=[row, mat, vec, mat, vec, vec, row, row],
        out_specs=[row, mat, vec, mat, vec, vec],
        out_shape=[_sds((t, GROUP)), _sds((GROUP, GROUP)), _sds((1, GROUP)), _sds((GROUP, GROUP)), _sds((1, GROUP)),
                   _sds((1, GROUP))],
        compiler_params=_params("arbitrary"))(xc, wa, ba, wx, bx, lam, g, h_prev)


def _block_diag(w):
    eye = jnp.eye(LRU_BLOCKS, dtype=w.dtype)
    return (eye[:, None, :, None] * w[:, :, None, :]).reshape(GROUP, GROUP)


def _diag_blocks(m):
    m4 = m.reshape(LRU_BLOCKS, LRU_BD, LRU_BLOCKS, LRU_BD)
    return jnp.stack([m4[n, :, n, :] for n in range(LRU_BLOCKS)])


def _group_mean_matrix(group):
    idx = np.arange(GROUP) // group
    return jnp.asarray((idx[:, None] == idx[None, :]).astype(np.float32) / group, F32)


def _post(h, gain, means, gate):
    y = h * lax.rsqrt(_mmf(h * h, means, "nn", True) + EPS) * gain
    return y if gate is None else y * jax.nn.gelu(gate)


def _post_specs(t):
    tb = _tile(t, 512, SUBLANES)
    col = lambda c: pl.BlockSpec((tb, GROUP), lambda i: (i, c))
    return tb, col, pl.BlockSpec((1, GROUP), lambda i: (0, 0)), pl.BlockSpec((GROUP, GROUP), lambda i: (0, 0))


def _post_fwd(h, gain, z, *, group, gate_col=None, name):
    t = h.shape[0]
    tb, col, vec, mat = _post_specs(t)
    gated = gate_col is not None

    def body(*refs):
        gate = refs[3][...] if gated else None
        refs[-1][...] = _post(refs[0][...], refs[1][...], refs[2][...], gate).astype(refs[-1].dtype)

    args, specs = [h, gain.reshape(1, GROUP), _group_mean_matrix(group)], [col(0), vec, mat]
    if gated:
        args, specs = args + [z], specs + [col(gate_col)]
    return pl.pallas_call(body, name=name, grid=(t // tb,), in_specs=specs, out_specs=col(0),
                          out_shape=_sds((t, GROUP), MM_DTYPE), compiler_params=_params("parallel"))(*args)


def _post_bwd(h, gain, z, dy, *, group, dy_col, gate_col=None, name):
    t = h.shape[0]
    tb, col, vec, mat = _post_specs(t)
    gated = gate_col is not None

    def body(*refs):
        h_ref, gn_ref, m_ref, dy_ref = refs[:4]
        if gated:
            z_ref, dh_ref, dgn_ref, dgt_ref = refs[4:]
            _, vjp = jax.vjp(lambda a, b, c: _post(a, b, m_ref[...], c), h_ref[...], gn_ref[...], z_ref[...])
            dh, dgn, dgt = vjp(dy_ref[...])
            dgt_ref[...] = dgt.astype(dgt_ref.dtype)
        else:
            dh_ref, dgn_ref = refs[4:]
            _, vjp = jax.vjp(lambda a, b: _post(a, b, m_ref[...], None), h_ref[...], gn_ref[...])
            dh, dgn = vjp(dy_ref[...])
        dh_ref[...] = dh

        @pl.when(pl.program_id(0) == 0)
        def _():
            dgn_ref[...] = jnp.zeros_like(dgn_ref)

        dgn_ref[...] += dgn

    args, specs = [h, gain.reshape(1, GROUP), _group_mean_matrix(group), dy], [col(0), vec, mat, col(dy_col)]
    out_specs, out_shape = [col(0), vec], [_sds((t, GROUP)), _sds((1, GROUP))]
    if gated:
        args, specs = args + [z], specs + [col(gate_col)]
        out_specs, out_shape = out_specs + [col(0)], out_shape + [_sds((t, GROUP), MM_DTYPE)]
    outs = pl.pallas_call(body, name=name, grid=(t // tb,), in_specs=specs, out_specs=out_specs, out_shape=out_shape,
                          compiler_params=_params("arbitrary"))(*args)
    return outs[0], outs[1].reshape(GROUP), (outs[2] if gated else None)


def _gates(zg, fb, alog, dtb):
    lane = lax.broadcasted_iota(jnp.int32, zg.shape, 1)
    logf = jax.nn.log_sigmoid(zg + fb)
    beta = jax.nn.sigmoid(zg)
    gdec = -jnp.exp(alog) * jax.nn.softplus(zg + dtb)
    return jnp.where(lane < G_BETA, logf, jnp.where(lane < G_ALPHA, beta, jnp.where(lane < G_ALPHA + 4, gdec, 0.0)))


def _lane_row(v, off):
    return jnp.pad(v.reshape(1, N_HEADS), ((0, 0), (off, LANES - N_HEADS - off)))


def _gates_fwd(z, fb, alog, dtb, *, name):
    t = z.shape[0]
    tb = _tile(t, 1024, SUBLANES)
    zspec = pl.BlockSpec((tb, LANES), lambda i: (i, Z_GATES // LANES))
    row = pl.BlockSpec((tb, LANES), lambda i: (i, 0))
    vec = pl.BlockSpec((1, LANES), lambda i: (0, 0))

    def body(z_ref, fb_ref, al_ref, dt_ref, o_ref):
        o_ref[...] = _gates(z_ref[...], fb_ref[...], al_ref[...], dt_ref[...])

    return pl.pallas_call(body, name=name, grid=(t // tb,), in_specs=[zspec, vec, vec, vec], out_specs=row,
                          out_shape=_sds((t, LANES)), compiler_params=_params("parallel"))(z, fb, alog, dtb)


def _gates_bwd(z, fb, alog, dtb, dg, *, name):
    t = z.shape[0]
    tb = _tile(t, 1024, SUBLANES)
    zspec = pl.BlockSpec((tb, LANES), lambda i: (i, Z_GATES // LANES))
    row = pl.BlockSpec((tb, LANES), lambda i: (i, 0))
    vec = pl.BlockSpec((1, LANES), lambda i: (0, 0))

    def body(z_ref, fb_ref, al_ref, dt_ref, dg_ref, dz_ref, dfb_ref, dal_ref, ddt_ref):
        _, vjp = jax.vjp(_gates, z_ref[...], fb_ref[...], al_ref[...], dt_ref[...])
        dz, dfb, dal, ddt = vjp(dg_ref[...])
        dz_ref[...] = dz.astype(dz_ref.dtype)
        accs = (dfb_ref, dal_ref, ddt_ref)

        @pl.when(pl.program_id(0) == 0)
        def _():
            for r in accs:
                r[...] = jnp.zeros_like(r)

        for r, v in zip(accs, (dfb, dal, ddt)):
            r[...] += v

    return pl.pallas_call(
        body, name=name, grid=(t // tb,), in_specs=[zspec, vec, vec, vec, row], out_specs=[row, vec, vec, vec],
        out_shape=[_sds((t, LANES), MM_DTYPE), _sds((1, LANES)), _sds((1, LANES)), _sds((1, LANES))],
        compiler_params=_params("arbitrary"))(z, fb, alog, dtb, dg)


class _Hosted:
    def __init__(self, args, out_shapes, sems, start, finish):
        self.args, self.out_shapes, self.sems, self.start, self.finish = list(args), list(out_shapes), list(sems), start, finish


def _grid_call(body, *, name, grid, in_specs, out_specs, out_shape, args, hosted=None):
    if hosted is None:
        outs = pl.pallas_call(body, name=name, grid=grid, in_specs=in_specs, out_specs=out_specs, out_shape=out_shape,
                              compiler_params=_params(*(["parallel"] * len(grid))))(*args)
        return outs, None
    n_in, n_out, h_in, h_out = len(in_specs), len(out_specs), len(hosted.args), len(hosted.out_shapes)

    def wrapped(*refs):
        core_in, host_in = refs[:n_in], refs[n_in:n_in + h_in]
        rest = refs[n_in + h_in:]
        core_out, host_out, sems = rest[:n_out], rest[n_out:n_out + h_out], rest[n_out + h_out:]
        ids = [pl.program_id(a) for a in range(len(grid))]
        first = functools.reduce(jnp.logical_and, [i == 0 for i in ids])
        last = functools.reduce(jnp.logical_and, [i == g - 1 for i, g in zip(ids, grid)])

        @pl.when(first)
        def _():
            hosted.start(host_in, host_out, sems)

        body(*core_in, *core_out)

        @pl.when(last)
        def _():
            hosted.finish(host_in, host_out, sems)

    outs = pl.pallas_call(
        wrapped, name=name, grid=grid, in_specs=list(in_specs) + [ANY] * h_in, out_specs=list(out_specs) + [ANY] * h_out,
        out_shape=list(out_shape) + hosted.out_shapes, scratch_shapes=hosted.sems,
        compiler_params=pltpu.CompilerParams(dimension_semantics=("arbitrary",) * len(grid),
                                             vmem_limit_bytes=VMEM_LIMIT_BYTES, has_side_effects=True))(
                                                 *args, *hosted.args)
    return outs[:n_out], outs[n_out:]


def _pair_weights(tq, band, transposed):
    d = np.arange(tq)[:, None] - np.arange(tq)[None, :]
    d = (d.T if transposed else d)[None] + (np.arange(band + 1) * tq)[:, None, None]
    w = sum(((d >= 0) & (d <= win) & (d % dil == 0)).astype(np.float32) for win, dil in DILATED_PAIRS)
    return jnp.asarray(w, F32)


def _att_geometry(t, mode):
    tq = _tile(t, ATT_TILE, LANES)
    nq = t // tq
    band = nq - 1 if mode == "fox" else min(DILATED_PAIRS[-1][0] // tq, nq - 1)
    return tq, nq, band


def _qkv_prep(z, *, off, name):
    t = z.shape[0]
    tb = _tile(t, 512, SUBLANES)
    cb = off // GROUP

    def body(z_ref, o_ref):
        scale = jnp.where(pl.program_id(1) == 0, ATT_SCALE, 1.0)
        o_ref[...] = (z_ref[...] * scale).astype(o_ref.dtype)

    return pl.pallas_call(
        body, name=name, grid=(t // tb, 3), in_specs=[pl.BlockSpec((tb, GROUP), lambda i, j: (i, j + cb))],
        out_specs=pl.BlockSpec((tb, GROUP), lambda i, j: (i, j)), out_shape=_sds((t, 3 * GROUP), MM_DTYPE),
        compiler_params=_params("parallel", "parallel"))(z)


def _block(ref, j, tq):
    return ref[pl.ds(pl.multiple_of(j * tq, tq), tq), :]


def _lane_block(ref, j, tq):
    return ref[0, :, pl.ds(pl.multiple_of(j * tq, tq), tq)]


def _att_tile(mode, q, kj, cq, ckj, causal, w):
    s = _dg(q, kj, 1, 1, False)
    if mode == "fox":
        s = s + (cq - ckj)
        return s if causal is None else jnp.where(causal, s, NEG)
    return jnp.where(w > 0.0, s, NEG)


def _att_fwd(qkv, cq, ck, *, mode, name, hosted=None):
    t = qkv.shape[0]
    tq, nq, band = _att_geometry(t, mode)
    weights = _pair_weights(tq, band if mode == "dil" else 0, False)

    def body(q_ref, k_ref, v_ref, cq_ref, ck_ref, w_ref, o_ref, lse_ref):
        i = pl.program_id(1)
        q, cqv = q_ref[...], cq_ref[0]
        causal = (lax.broadcasted_iota(jnp.int32, (tq, tq), 0) >= lax.broadcasted_iota(jnp.int32, (tq, tq), 1))

        def step(j, carry, masked):
            m_old, l_old, acc = carry
            w = w_ref[i - j] if mode == "dil" else None
            s = _att_tile(mode, q, _block(k_ref, j, tq), cqv, _lane_block(ck_ref, j, tq), causal if masked else None, w)
            m_new = jnp.maximum(m_old, jnp.max(s, axis=-1, keepdims=True))
            alpha = jnp.exp(m_old - m_new)
            p = jnp.exp(s - m_new)
            if w is not None:
                p = p * w
            l_new = alpha * l_old + jnp.sum(p, axis=-1, keepdims=True)
            return m_new, l_new, alpha * acc + _dg(p, _block(v_ref, j, tq), 1, 0, False)

        carry = (jnp.full((tq, 1), NEG, F32), jnp.zeros((tq, 1), F32), jnp.zeros((tq, HEAD), F32))
        if mode == "fox":
            carry = lax.fori_loop(0, i, lambda j, c: step(j, c, False), carry)
            carry = step(i, carry, True)
        else:
            carry = lax.fori_loop(jnp.maximum(i - band, 0), i + 1, lambda j, c: step(j, c, True), carry)
        m_fin, l_fin, acc = carry
        o_ref[...] = acc / l_fin
        lse_ref[0] = m_fin + jnp.log(l_fin)

    col = pl.BlockSpec((1, tq, 1), lambda h, i: (h, i, 0))
    in_specs = [
        pl.BlockSpec((tq, HEAD), lambda h, i: (i, h)),
        pl.BlockSpec((t, HEAD), lambda h, i: (0, N_HEADS + h)),
        pl.BlockSpec((t, HEAD), lambda h, i: (0, 2 * N_HEADS + h)),
        col,
        pl.BlockSpec((1, 1, t), lambda h, i: (h, 0, 0)),
        pl.BlockSpec(weights.shape, lambda h, i: (0, 0, 0)),
    ]
    return _grid_call(
        body, name=name, grid=(N_HEADS, nq), in_specs=in_specs,
        out_specs=[pl.BlockSpec((tq, HEAD), lambda h, i: (i, h)), col],
        out_shape=[_sds((t, GROUP)), _sds((N_HEADS, t, 1))], args=(qkv, qkv, qkv, cq, ck, weights), hosted=hosted)


def _att_bwd_q(qkv, cq, ck, do, o, lse, *, mode, name, hosted=None):
    t = qkv.shape[0]
    tq, nq, band = _att_geometry(t, mode)
    weights = _pair_weights(tq, band if mode == "dil" else 0, False)

    def body(q_ref, k_ref, v_ref, cq_ref, ck_ref, w_ref, do_ref, o_ref, lse_ref, dq_ref, dcq_ref, delta_ref):
        i = pl.program_id(1)
        q, cqv, lse = q_ref[...], cq_ref[0], lse_ref[0]
        dov = do_ref[...]
        delta = jnp.sum(dov * o_ref[...], axis=-1, keepdims=True)
        do16 = dov.astype(MM_DTYPE)
        causal = (lax.broadcasted_iota(jnp.int32, (tq, tq), 0) >= lax.broadcasted_iota(jnp.int32, (tq, tq), 1))

        def step(j, carry, masked):
            dq, dcq = carry
            kj = _block(k_ref, j, tq)
            w = w_ref[i - j] if mode == "dil" else None
            s = _att_tile(mode, q, kj, cqv, _lane_block(ck_ref, j, tq), causal if masked else None, w)
            p = jnp.exp(s - lse)
            if w is not None:
                p = p * w
            ds = p * (_dg(do16, _block(v_ref, j, tq), 1, 1, False) - delta)
            return dq + _dg(ds, kj, 1, 0, False), dcq + jnp.sum(ds, axis=-1, keepdims=True)

        carry = (jnp.zeros((tq, HEAD), F32), jnp.zeros((tq, 1), F32))
        if mode == "fox":
            carry = lax.fori_loop(0, i, lambda j, c: step(j, c, False), carry)
            carry = step(i, carry, True)
        else:
            carry = lax.fori_loop(jnp.maximum(i - band, 0), i + 1, lambda j, c: step(j, c, True), carry)
        dq_ref[...] = (carry[0] * ATT_SCALE).astype(dq_ref.dtype)
        dcq_ref[0] = carry[1]
        delta_ref[0] = delta

    col = pl.BlockSpec((1, tq, 1), lambda h, i: (h, i, 0))
    row = pl.BlockSpec((tq, HEAD), lambda h, i: (i, h))
    in_specs = [
        row,
        pl.BlockSpec((t, HEAD), lambda h, i: (0, N_HEADS + h)),
        pl.BlockSpec((t, HEAD), lambda h, i: (0, 2 * N_HEADS + h)),
        col,
        pl.BlockSpec((1, 1, t), lambda h, i: (h, 0, 0)),
        pl.BlockSpec(weights.shape, lambda h, i: (0, 0, 0)),
        row, row, col,
    ]
    return _grid_call(
        body, name=name, grid=(N_HEADS, nq), in_specs=in_specs, out_specs=[row, col, col],
        out_shape=[_sds((t, GROUP), MM_DTYPE), _sds((N_HEADS, t, 1)), _sds((N_HEADS, t, 1))],
        args=(qkv, qkv, qkv, cq, ck, weights, do, o, lse), hosted=hosted)


def _att_bwd_kv(qkv, cq, ck, do, lse_row, delta_row, *, mode, name, hosted=None):
    t = qkv.shape[0]
    tq, nq, band = _att_geometry(t, mode)
    weights = _pair_weights(tq, band if mode == "dil" else 0, True)

    def body(q_ref, k_ref, v_ref, cq_ref, ck_ref, w_ref, do_ref, lse_ref, delta_ref, dk_ref, dv_ref, dck_ref):
        jk = pl.program_id(1)
        kj, vj, ckv = k_ref[...], v_ref[...], cq_ref[0]
        causal = (lax.broadcasted_iota(jnp.int32, (tq, tq), 1) >= lax.broadcasted_iota(jnp.int32, (tq, tq), 0))

        def step(qi, carry, masked):
            dk, dv, dck = carry
            qb = _block(q_ref, qi, tq)
            s = _dg(kj, qb, 1, 1, False)
            w = None
            if mode == "fox":
                s = s + (_lane_block(ck_ref, qi, tq) - ckv)
                if masked:
                    s = jnp.where(causal, s, NEG)
            else:
                w = w_ref[qi - jk]
                s = jnp.where(w > 0.0, s, NEG)
            p = jnp.exp(s - _lane_block(lse_ref, qi, tq))
            if w is not None:
                p = p * w
            do16 = _block(do_ref, qi, tq).astype(MM_DTYPE)
            ds = p * (_dg(vj, do16, 1, 1, False) - _lane_block(delta_ref, qi, tq))
            return (dk + _dg(ds, qb, 1, 0, False), dv + _dg(p, do16, 1, 0, False),
                    dck - jnp.sum(ds, axis=-1, keepdims=True))

        carry = (jnp.zeros((tq, HEAD), F32), jnp.zeros((tq, HEAD), F32), jnp.zeros((tq, 1), F32))
        if mode == "fox":
            carry = step(jk, carry, True)
            carry = lax.fori_loop(jk + 1, nq, lambda qi, c: step(qi, c, False), carry)
        else:
            carry = lax.fori_loop(jk, jnp.minimum(jk + band, nq - 1) + 1, lambda qi, c: step(qi, c, True), carry)
        dk_ref[...] = carry[0].astype(dk_ref.dtype)
        dv_ref[...] = carry[1].astype(dv_ref.dtype)
        dck_ref[0] = carry[2]

    col = pl.BlockSpec((1, tq, 1), lambda h, j: (h, j, 0))
    lanes = pl.BlockSpec((1, 1, t), lambda h, j: (h, 0, 0))
    in_specs = [
        pl.BlockSpec((t, HEAD), lambda h, j: (0, h)),
        pl.BlockSpec((tq, HEAD), lambda h, j: (j, N_HEADS + h)),
        pl.BlockSpec((tq, HEAD), lambda h, j: (j, 2 * N_HEADS + h)),
        col, lanes,
        pl.BlockSpec(weights.shape, lambda h, j: (0, 0, 0)),
        pl.BlockSpec((t, HEAD), lambda h, j: (0, h)),
        lanes, lanes,
    ]
    out = pl.BlockSpec((tq, HEAD), lambda h, j: (j, h))
    return _grid_call(
        body, name=name, grid=(N_HEADS, nq), in_specs=in_specs, out_specs=[out, out, col],
        out_shape=[_sds((t, GROUP), MM_DTYPE), _sds((t, GROUP), MM_DTYPE), _sds((N_HEADS, t, 1))],
        args=(qkv, qkv, qkv, cq, ck, weights, do, lse_row, delta_row), hosted=hosted)


def _silu(x):
    return x * jax.nn.sigmoid(x)


def _l2n(x):
    return x * lax.rsqrt(jnp.sum(x * x, axis=-1, keepdims=True) + EPS)


def _gdn_pre(xqkv, gts):
    rows, c = xqkv.shape[0], GDN_CHUNK
    nb = rows // c
    ri = lax.broadcasted_iota(jnp.int32, (1, c, c), 1)
    ci = lax.broadcasted_iota(jnp.int32, (1, c, c), 2)
    tril, strict, eye = ri >= ci, ri > ci, ri == ci
    eyef = eye.astype(F32)
    last = lax.broadcasted_iota(jnp.int32, (1, c, 1), 1) == c - 1
    lane = lax.broadcasted_iota(jnp.int32, gts.shape, 1)
    to3 = lambda a: a.reshape(nb, c, a.shape[-1])
    to2 = lambda a: a.reshape(rows, a.shape[-1])
    bmm = lambda a, b: _mmf(a, b, "nn", False, True)
    bmm_nt = lambda a, b: _mmf(a, b, "nt", False, True)
    gcs = to2(_mmf(jnp.broadcast_to(tril.astype(F32), (nb, c, c)), to3(gts), "nn", True, True))
    us, ws, qgs, kds, qks, egls = [], [], [], [], [], []
    for h in range(N_HEADS):
        q = to3(_l2n(_silu(xqkv[:, h * HEAD:(h + 1) * HEAD])) * ATT_SCALE)
        k = to3(_l2n(_silu(xqkv[:, GROUP + h * HEAD:GROUP + (h + 1) * HEAD])))
        v = to3(_silu(xqkv[:, 2 * GROUP + h * HEAD:2 * GROUP + (h + 1) * HEAD]))
        beta = to3(jnp.sum(jnp.where(lane == G_BETA + h, gts, 0.0), axis=-1, keepdims=True))
        gc = to3(jnp.sum(jnp.where(lane == G_ALPHA + h, gcs, 0.0), axis=-1, keepdims=True))
        gr = jnp.sum(jnp.where(eye, jnp.broadcast_to(gc, (nb, c, c)), 0.0), axis=1, keepdims=True)
        decay = jnp.where(tril, jnp.exp(jnp.where(tril, gc - gr, 0.0)), 0.0)
        kbeta, vbeta = k * beta, v * beta
        low = jnp.where(strict, bmm_nt(kbeta, k) * decay, 0.0)
        inv, pw = eyef - low, bmm(low, low)
        for step in range(5):
            inv = inv + bmm(inv, pw)
            if step < 4:
                pw = bmm(pw, pw)
        eg = jnp.exp(gc)
        g_last = jnp.sum(jnp.where(last, gc, 0.0), axis=1, keepdims=True)
        us.append(to2(bmm(inv, vbeta)))
        ws.append(to2(bmm(inv, kbeta * eg)))
        qgs.append(to2(q * eg))
        kds.append(to2(k * jnp.exp(g_last - gc)))
        qks.append(to2(jnp.where(tril, bmm_nt(q, k) * decay, 0.0)))
        egls.append(jnp.broadcast_to(jnp.exp(g_last), (nb, 1, LANES)))
    cat = lambda parts: jnp.concatenate(parts, axis=1)
    return cat(us), cat(ws), cat(qgs), cat(kds), jnp.stack(qks, axis=0), jnp.stack(egls, axis=1)


def _gdn_seq(states, u, w, qg, kd, qk, egl):
    states = list(states)
    outs = []
    for r in range(u.shape[0] // GDN_CHUNK):
        rs = slice(r * GDN_CHUNK, (r + 1) * GDN_CHUNK)
        heads = []
        for h in range(N_HEADS):
            hs = slice(h * HEAD, (h + 1) * HEAD)
            s_in = states[h]
            v_new = u[rs, hs] - _mm(w[rs, hs], s_in)
            heads.append(_mm(qg[rs, hs], s_in) + _mm(qk[h, rs], v_new))
            states[h] = s_in * egl[r, h] + _mm_tn(kd[rs, hs], v_new)
        outs.append(jnp.concatenate(heads, axis=1))
    return jnp.concatenate(outs, axis=0), tuple(states)


def _gdn_out(o, zg, ng):
    ys = []
    for h in range(N_HEADS):
        oh = o[:, h * HEAD:(h + 1) * HEAD]
        ys.append(oh * lax.rsqrt(jnp.mean(oh * oh, axis=-1, keepdims=True) + EPS) * ng * _silu(zg[:, h * HEAD:(h + 1) * HEAD]))
    return jnp.concatenate(ys, axis=1)


GDN_PRE_ROWS = 8 * GDN_CHUNK
GDN_SEQ_ROWS = 4 * GDN_CHUNK


def _gdn_pre_specs(t):
    rows = _tile(t, GDN_PRE_ROWS, GDN_CHUNK)
    nb = rows // GDN_CHUNK
    wide = pl.BlockSpec((rows, GROUP), lambda i: (i, 0))
    ins = [pl.BlockSpec((rows, 3 * GROUP), lambda i: (i, 0)), pl.BlockSpec((rows, LANES), lambda i: (i, 0))]
    mids = [wide, wide, wide, wide, pl.BlockSpec((N_HEADS, rows, GDN_CHUNK), lambda i: (0, i, 0)),
            pl.BlockSpec((nb, N_HEADS, 1, LANES), lambda i: (i, 0, 0, 0))]
    shapes = [_sds((t, GROUP))] * 4 + [_sds((N_HEADS, t, GDN_CHUNK)), _sds((t // GDN_CHUNK, N_HEADS, 1, LANES))]
    return rows, ins, mids, shapes


def _gdn_pre_fwd(xqkv, gts, *, name):
    t = xqkv.shape[0]
    rows, ins, mids, shapes = _gdn_pre_specs(t)

    def body(x_ref, g_ref, *out_refs):
        for ref, val in zip(out_refs, _gdn_pre(x_ref[...], g_ref[...])):
            ref[...] = val

    return pl.pallas_call(body, name=name, grid=(t // rows,), in_specs=ins, out_specs=mids, out_shape=shapes,
                          compiler_params=_params("parallel"))(xqkv, gts)


def _gdn_pre_bwd(xqkv, gts, dmids, *, name):
    t = xqkv.shape[0]
    rows, ins, mids, _ = _gdn_pre_specs(t)

    def body(x_ref, g_ref, *refs):
        _, vjp = jax.vjp(_gdn_pre, x_ref[...], g_ref[...])
        dx, dg = vjp(tuple(r[...] for r in refs[:6]))
        refs[6][...] = dx
        refs[7][...] = dg

    return pl.pallas_call(body, name=name, grid=(t // rows,), in_specs=ins + mids, out_specs=ins,
                          out_shape=[_sds((t, 3 * GROUP)), _sds((t, LANES))],
                          compiler_params=_params("parallel"))(xqkv, gts, *dmids)


def _gdn_seq_specs(t, rev):
    rows = _tile(t, GDN_SEQ_ROWS, GDN_CHUNK)
    nb, n = rows // GDN_CHUNK, t // rows
    at = (lambda i: n - 1 - i) if rev else (lambda i: i)
    wide = pl.BlockSpec((rows, GROUP), lambda i: (at(i), 0))
    mids = [wide, wide, wide, wide, pl.BlockSpec((N_HEADS, rows, GDN_CHUNK), lambda i: (0, at(i), 0)),
            pl.BlockSpec((nb, N_HEADS, 1, LANES), lambda i: (at(i), 0, 0, 0))]
    state = pl.BlockSpec((1, N_HEADS, HEAD, HEAD), lambda i: (at(i), 0, 0, 0))
    return rows, n, wide, mids, state


def _gdn_seq_fwd(mids_in, *, name):
    t = mids_in[0].shape[0]
    rows, n, wide, mids, state = _gdn_seq_specs(t, False)

    def body(u_ref, w_ref, qg_ref, kd_ref, qk_ref, egl_ref, o_ref, sv_ref, s_ref):
        @pl.when(pl.program_id(0) == 0)
        def _():
            s_ref[...] = jnp.zeros_like(s_ref)

        sv_ref[0] = s_ref[...]
        o, new = _gdn_seq(tuple(s_ref[h] for h in range(N_HEADS)), u_ref[...], w_ref[...], qg_ref[...], kd_ref[...],
                          qk_ref[...], egl_ref[...])
        o_ref[...] = o
        for h in range(N_HEADS):
            s_ref[h] = new[h]

    return pl.pallas_call(
        body, name=name, grid=(n,), in_specs=mids, out_specs=[wide, state],
        out_shape=[_sds((t, GROUP)), _sds((n, N_HEADS, HEAD, HEAD))],
        scratch_shapes=[pltpu.VMEM((N_HEADS, HEAD, HEAD), F32)], compiler_params=_params("arbitrary"))(*mids_in)


def _gdn_seq_bwd(mids_in, states, do, *, name):
    t = mids_in[0].shape[0]
    rows, n, wide, mids, state = _gdn_seq_specs(t, True)

    def body(u_ref, w_ref, qg_ref, kd_ref, qk_ref, egl_ref, sv_ref, do_ref, *refs):
        d_refs, ds_ref = refs[:6], refs[6]

        @pl.when(pl.program_id(0) == 0)
        def _():
            ds_ref[...] = jnp.zeros_like(ds_ref)

        s_in = tuple(sv_ref[0, h] for h in range(N_HEADS))
        _, vjp = jax.vjp(_gdn_seq, s_in, u_ref[...], w_ref[...], qg_ref[...], kd_ref[...], qk_ref[...], egl_ref[...])
        grads = vjp((do_ref[...], tuple(ds_ref[h] for h in range(N_HEADS))))
        for ref, val in zip(d_refs, grads[1:]):
            ref[...] = val
        for h in range(N_HEADS):
            ds_ref[h] = grads[0][h]

    shapes = [_sds(m.shape) for m in mids_in]
    return pl.pallas_call(
        body, name=name, grid=(n,), in_specs=mids + [state, wide], out_specs=mids, out_shape=shapes,
        scratch_shapes=[pltpu.VMEM((N_HEADS, HEAD, HEAD), F32)],
        compiler_params=_params("arbitrary"))(*mids_in, states, do)


def _gdn_out_specs(t):
    rows = _tile(t, 512, SUBLANES)
    wide = pl.BlockSpec((rows, GROUP), lambda i: (i, 0))
    return rows, wide, pl.BlockSpec((rows, GROUP), lambda i: (i, Z_CZ // GROUP)), pl.BlockSpec((1, HEAD), lambda i: (0, 0))


def _gdn_out_fwd(o, z, ng, *, name):
    t = o.shape[0]
    rows, wide, zspec, vec = _gdn_out_specs(t)

    def body(o_ref, z_ref, ng_ref, y_ref):
        y_ref[...] = _gdn_out(o_ref[...], z_ref[...], ng_ref[...]).astype(y_ref.dtype)

    return pl.pallas_call(body, name=name, grid=(t // rows,), in_specs=[wide, zspec, vec], out_specs=wide,
                          out_shape=_sds((t, GROUP), MM_DTYPE), compiler_params=_params("parallel"))(o, z, ng)


def _gdn_out_bwd(o, z, ng, dy, *, dy_col, name):
    t = o.shape[0]
    rows, wide, zspec, vec = _gdn_out_specs(t)
    dyspec = pl.BlockSpec((rows, GROUP), lambda i: (i, dy_col))

    def body(o_ref, z_ref, ng_ref, dy_ref, do_ref, dz_ref, dng_ref):
        _, vjp = jax.vjp(_gdn_out, o_ref[...], z_ref[...], ng_ref[...])
        do, dz, dng = vjp(dy_ref[...])
        do_ref[...] = do
        dz_ref[...] = dz.astype(dz_ref.dtype)

        @pl.when(pl.program_id(0) == 0)
        def _():
            dng_ref[...] = jnp.zeros_like(dng_ref)

        dng_ref[...] += dng

    do, dz, dng = pl.pallas_call(
        body, name=name, grid=(t // rows,), in_specs=[wide, zspec, vec, dyspec], out_specs=[wide, wide, vec],
        out_shape=[_sds((t, GROUP)), _sds((t, GROUP), MM_DTYPE), _sds((1, HEAD))],
        compiler_params=_params("arbitrary"))(o, z, ng, dy)
    return do, dz, dng.reshape(HEAD)


def _swiglu(up, gate):
    return _silu(gate) * up


def _conv_value(x_ref, p_ref, w_ref, b_ref, first):
    xv = x_ref[...]
    prev = jnp.where(first, 0.0, p_ref[...])
    acc = jnp.zeros_like(xv) + b_ref[...]
    kw = w_ref.shape[0]
    for k in range(kw):
        acc = acc + w_ref[k:k + 1, :] * _delayed(xv, prev, kw - 1 - k)
    return acc


def _ffn_act_specs(t, two_f, kw, col_of):
    dff = two_f // 2
    tb, tc = _tile(t, 512, SUBLANES), _tile(dff, 512, LANES)
    nh, n8 = dff // tc, tb // SUBLANES
    specs = []
    for half in (0, 1):
        col = lambda j, half=half: col_of(j, nh) + half * nh
        specs += [pl.BlockSpec((tb, tc), lambda j, i, col=col: (i, col(j))),
                  pl.BlockSpec((SUBLANES, tc), lambda j, i, col=col: (jnp.maximum(i * n8 - 1, 0), col(j))),
                  pl.BlockSpec((kw, tc), lambda j, i, col=col: (0, col(j))),
                  pl.BlockSpec((1, tc), lambda j, i, col=col: (0, col(j)))]
    return tb, tc, nh, specs


def _ffn_act_fwd(uu, w, b, *, name):
    t, two_f = uu.shape
    tb, tc, nh, specs = _ffn_act_specs(t, two_f, w.shape[0], lambda j, nh: j)

    def body(xu, pu, wu, bu, xg, pg, wg, bg, o_ref):
        first = pl.program_id(1) == 0
        o_ref[...] = _swiglu(_conv_value(xu, pu, wu, bu, first), _conv_value(xg, pg, wg, bg, first)).astype(o_ref.dtype)

    b2 = b.reshape(1, two_f)
    return pl.pallas_call(body, name=name, grid=(nh, t // tb), in_specs=specs,
                          out_specs=pl.BlockSpec((tb, tc), lambda j, i: (i, j)), out_shape=_sds((t, two_f // 2), MM_DTYPE),
                          compiler_params=_params("parallel", "parallel"))(uu, uu, w, b2, uu, uu, w, b2)


def _ffn_act_bwd(uu, w, b, dact, *, name):
    t, two_f = uu.shape
    tb, tc, nh, specs = _ffn_act_specs(t, two_f, w.shape[0], lambda j, nh: j)

    def body(xu, pu, wu, bu, xg, pg, wg, bg, da_ref, o_ref):
        first = pl.program_id(1) == 0
        _, vjp = jax.vjp(_swiglu, _conv_value(xu, pu, wu, bu, first), _conv_value(xg, pg, wg, bg, first))
        o_ref[0], o_ref[1] = vjp(da_ref[...])

    b2 = b.reshape(1, two_f)
    return pl.pallas_call(body, name=name, grid=(nh, t // tb),
                          in_specs=specs + [pl.BlockSpec((tb, tc), lambda j, i: (i, j))],
                          out_specs=pl.BlockSpec((2, tb, tc), lambda j, i: (0, i, j)), out_shape=_sds((2, t, two_f // 2)),
                          compiler_params=_params("parallel", "parallel"))(uu, uu, w, b2, uu, uu, w, b2, dact)


def _sum_slots(parts, *, out_dtype=F32, name):
    if not isinstance(parts, (list, tuple)):
        parts = [parts[s] for s in range(parts.shape[0])]
    r = parts[0].shape[0]
    rb = _tile(r, 2048, 2 * SUBLANES)
    spec = pl.BlockSpec((rb, LANES), lambda i: (i, 0))

    def body(*refs):
        acc = refs[0][...].astype(F32)
        for ref in refs[1:-1]:
            acc = acc + ref[...].astype(F32)
        refs[-1][...] = acc.astype(out_dtype)

    return pl.pallas_call(body, name=name, grid=(r // rb,), in_specs=[spec] * len(parts), out_specs=spec,
                          out_shape=_sds((r, LANES), out_dtype), compiler_params=_params("parallel"))(*parts)


def _adamw(w, g, m, v, *, name):
    r, c = w.shape
    rb = _tile(r, 128, SUBLANES)
    spec = pl.BlockSpec((rb, c), lambda i: (i, 0))

    def body(w_ref, g_ref, m_ref, v_ref, d_ref, nm_ref, nv_ref):
        gv = g_ref[...]
        mn = ADAM_B1 * m_ref[...] + (1.0 - ADAM_B1) * gv
        vn = ADAM_B2 * v_ref[...] + (1.0 - ADAM_B2) * (gv * gv)
        m_hat = mn / (1.0 - ADAM_B1 ** ADAM_STEP)
        v_hat = vn / (1.0 - ADAM_B2 ** ADAM_STEP)
        d_ref[...] = -ADAM_LR * (m_hat / (jnp.sqrt(v_hat) + ADAM_EPS) + ADAM_WD * w_ref[...])
        nm_ref[...] = mn
        nv_ref[...] = vn

    return pl.pallas_call(body, name=name, grid=(r // rb,), in_specs=[spec] * 4, out_specs=[spec] * 3,
                          out_shape=[_sds((r, c))] * 3, compiler_params=_params("parallel"))(w, g, m, v)


def _position():
    return lax.axis_index("x"), lax.axis_index("y"), lax.axis_index("c")


def _chip_gather(shards, *, name):
    n = len(shards)

    def body(*refs):
        ins, outs = refs[:n], refs[n:2 * n]
        send_sems, recv_sems, local_sems = refs[2 * n:]
        x, y, c = _position()
        mine = 2 * x + y
        chips = [(1 - x, y), (x, 1 - y), (1 - x, 1 - y)]
        local = [pltpu.make_async_copy(ins[a], outs[a].at[mine], local_sems.at[a]) for a in range(n)]
        for cp in local:
            cp.start()
        sends = []
        for a in range(n):
            for r, (px, py) in enumerate(chips):
                sends.append(pltpu.make_async_remote_copy(
                    src_ref=ins[a], dst_ref=outs[a].at[mine], send_sem=send_sems.at[3 * a + r],
                    recv_sem=recv_sems.at[3 * a + r], device_id=(px, py, c), device_id_type=MESH_ID))
        for cp in sends:
            cp.start()
        for a in range(n):
            for r, (px, py) in enumerate(chips):
                pltpu.make_async_remote_copy(
                    src_ref=ins[a], dst_ref=outs[a].at[2 * px + py], send_sem=send_sems.at[3 * a + r],
                    recv_sem=recv_sems.at[3 * a + r], device_id=(px, py, c), device_id_type=MESH_ID).wait_recv()
        for cp in sends:
            cp.wait_send()
        for cp in local:
            cp.wait()

    return pl.pallas_call(
        body, name=name, in_specs=[ANY] * n, out_specs=[ANY] * n,
        out_shape=[_sds((4,) + s.shape, s.dtype) for s in shards],
        scratch_shapes=[pltpu.SemaphoreType.DMA((3 * n,)), pltpu.SemaphoreType.DMA((3 * n,)),
                        pltpu.SemaphoreType.DMA((n,))],
        compiler_params=pltpu.CompilerParams(has_side_effects=True))(*shards)


def _gather_halves_step(shards):
    n = len(shards)

    def copy(ins, outs, send_sems, recv_sems, a, r, chip_of_block, half, to, second):
        k = (3 * n if second else 0) + 3 * a + r
        px, py = chip_of_block
        src = outs[a].at[2 * px + py, half] if second else ins[a].at[half]
        return pltpu.make_async_remote_copy(
            src_ref=src, dst_ref=outs[a].at[2 * px + py, half], send_sem=send_sems.at[k], recv_sem=recv_sems.at[k],
            device_id=to, device_id_type=MESH_ID)

    def first_copies(ins, outs, sems):
        x, y, c = _position()
        chips = [(1 - x, y), (x, 1 - y), (1 - x, 1 - y)]
        return [copy(ins, outs, *sems, a, r, (x, y), c, (px, py, c), False) for a in range(n)
                for r, (px, py) in enumerate(chips)]

    def start(ins, outs, sems):
        for cp in first_copies(ins, outs, sems):
            cp.start()

    def finish(ins, outs, sems):
        x, y, c = _position()
        chips = [(1 - x, y), (x, 1 - y), (1 - x, 1 - y)]
        passed = []
        for a in range(n):
            for r, chip in enumerate(chips):
                copy(ins, outs, *sems, a, r, chip, c, (x, y, c), False).wait_recv()
                passed.append(copy(ins, outs, *sems, a, r, chip, c, (x, y, 1 - c), True))
                passed[-1].start()
        for a in range(n):
            for r, chip in enumerate(chips):
                copy(ins, outs, *sems, a, r, chip, 1 - c, (x, y, c), True).wait_recv()
        for cp in first_copies(ins, outs, sems) + passed:
            cp.wait_send()

    return _Hosted(shards, [_sds((4,) + s.shape, s.dtype) for s in shards],
                   [pltpu.SemaphoreType.DMA((6 * n,)), pltpu.SemaphoreType.DMA((6 * n,))], start, finish)


def _exchange_step(v):
    def copies(ins, outs, sems):
        x, y, c = _position()
        mine = 2 * x + y
        return [pltpu.make_async_remote_copy(
            src_ref=ins[0].at[2 * px + py], dst_ref=outs[0].at[mine], send_sem=sems[0].at[r], recv_sem=sems[1].at[r],
            device_id=(px, py, c), device_id_type=MESH_ID) for r, (px, py) in enumerate([(1 - x, y), (x, 1 - y), (1 - x, 1 - y)])]

    def start(ins, outs, sems):
        for cp in copies(ins, outs, sems):
            cp.start()

    def finish(ins, outs, sems):
        x, y, c = _position()
        mine = 2 * x + y
        for r, (px, py) in enumerate([(1 - x, y), (x, 1 - y), (1 - x, 1 - y)]):
            pltpu.make_async_remote_copy(
                src_ref=ins[0].at[mine], dst_ref=outs[0].at[2 * px + py], send_sem=sems[0].at[r], recv_sem=sems[1].at[r],
                device_id=(px, py, c), device_id_type=MESH_ID).wait_recv()
        for cp in copies(ins, outs, sems):
            cp.wait_send()

    return _Hosted([v], [_sds(v.shape, v.dtype)], [pltpu.SemaphoreType.DMA((3,)), pltpu.SemaphoreType.DMA((3,))],
                   start, finish)


def _run_step(step, *, name):
    n = len(step.args)

    def body(*refs):
        ins, outs, sems = refs[:n], refs[n:n + len(step.out_shapes)], refs[n + len(step.out_shapes):]
        step.start(ins, outs, sems)
        step.finish(ins, outs, sems)

    return pl.pallas_call(body, name=name, in_specs=[ANY] * n, out_specs=[ANY] * len(step.out_shapes),
                          out_shape=step.out_shapes, scratch_shapes=step.sems,
                          compiler_params=pltpu.CompilerParams(has_side_effects=True))(*step.args)


def _place_own(results, own):
    x, y, _ = _position()
    return [lax.dynamic_update_index_in_dim(r, o, 2 * x + y, 0) for r, o in zip(results, own)]


def _sibling_step(v, other_half=False):
    shape = v.shape[:1] + v.shape[2:] if other_half else v.shape

    def copy(ins, outs, sems):
        x, y, c = _position()
        src = ins[0].at[pl.ds(0, v.shape[0]), 1 - c] if other_half else ins[0]
        return pltpu.make_async_remote_copy(src_ref=src, dst_ref=outs[0], send_sem=sems[0], recv_sem=sems[1],
                                            device_id=(x, y, 1 - c), device_id_type=MESH_ID)

    return _Hosted([v], [_sds(shape, v.dtype)], [pltpu.SemaphoreType.DMA, pltpu.SemaphoreType.DMA],
                   lambda ins, outs, sems: copy(ins, outs, sems).start(), lambda ins, outs, sems: copy(ins, outs, sems).wait())


def _sibling_send(v, *, name):
    return _run_step(_sibling_step(v), name=name)[0]


def _all_sum(v, *, name):
    r = v.shape[0]
    masks = [(mx, my, mc) for mx in (0, 1) for my in (0, 1) for mc in (0, 1)][1:]

    def body(v_ref, out_ref, slots, send_sems, recv_sems, local_sem):
        x, y, c = _position()
        me = 4 * x + 2 * y + c

        def peer(mask):
            return tuple(1 - p if bit else p for p, bit in zip((x, y, c), mask))

        local = pltpu.make_async_copy(v_ref, slots.at[me], local_sem)
        local.start()
        sends = [pltpu.make_async_remote_copy(
            src_ref=v_ref, dst_ref=slots.at[me], send_sem=send_sems.at[k], recv_sem=recv_sems.at[k],
            device_id=peer(mask), device_id_type=MESH_ID) for k, mask in enumerate(masks)]
        for cp in sends:
            cp.start()
        for k, mask in enumerate(masks):
            px, py, pc = peer(mask)
            pltpu.make_async_remote_copy(
                src_ref=v_ref, dst_ref=slots.at[4 * px + 2 * py + pc], send_sem=send_sems.at[k],
                recv_sem=recv_sems.at[k], device_id=(px, py, pc), device_id_type=MESH_ID).wait_recv()
        for cp in sends:
            cp.wait_send()
        local.wait()
        acc = slots[0]
        for s in range(1, 8):
            acc = acc + slots[s]
        out_ref[...] = acc

    vm = pl.BlockSpec(memory_space=pltpu.VMEM)
    return pl.pallas_call(
        body, name=name, in_specs=[vm], out_specs=vm, out_shape=_sds((r, LANES)),
        scratch_shapes=[pltpu.VMEM((8, r, LANES), F32), pltpu.SemaphoreType.DMA((7,)), pltpu.SemaphoreType.DMA((7,)),
                        pltpu.SemaphoreType.DMA],
        compiler_params=pltpu.CompilerParams(vmem_limit_bytes=VMEM_LIMIT_BYTES, has_side_effects=True))(v)


def _pack_rows(arrays, align=SUBLANES * LANES):
    flat = jnp.concatenate([a.reshape(-1) for a in arrays])
    n = flat.shape[0]
    pad = (-n) % align
    if pad:
        flat = jnp.concatenate([flat, jnp.zeros((pad,), flat.dtype)])
    return flat.reshape(-1, LANES), [a.shape for a in arrays]


def _unpack_rows(rows, shapes):
    flat = rows.reshape(-1)
    out, off = [], 0
    for s in shapes:
        n = int(np.prod(s))
        out.append(flat[off:off + n].reshape(s))
        off += n
    return out


def _pad_w_in(w):
    d = w.shape[0]
    return jnp.concatenate([w[:, 0:2560], w[:, 2564:4612], w[:, 4620:6156], w[:, 2560:2564], w[:, 4612:4620],
                            jnp.zeros((d, Z_COLS - IN_COLS), w.dtype)], axis=1)


def _unpad_w_in(g):
    return jnp.concatenate([g[:, 0:2560], g[:, 6144:6148], g[:, 2560:4608], g[:, 6148:6156], g[:, 4608:6144]], axis=1)


def _gate_rows(p):
    return (_lane_row(p["fox_f_bias"], G_F), _lane_row(p["gdn_a_log"], G_ALPHA), _lane_row(p["gdn_dt_bias"], G_ALPHA))


def _head_cols(c_rows, t):
    ct = c_rows[:, :N_HEADS].T
    return ct.reshape(N_HEADS, t, 1), ct.reshape(N_HEADS, 1, t)


def _layer_fwd(x, p, hosted=None):
    t = x.shape[0]
    s = {"x": x}
    s["h"] = _rmsnorm_fwd(x, p["norm_mix"], name="mix_norm")
    z = s["z"] = _matmul(s["h"], p["w_in"], name="in_proj")
    s["xc"] = _conv_fwd(z, p["lru_conv_w"], p["lru_conv_b"], ncols=GROUP, coff=Z_AX, name="lru_conv")
    lru = s["lru"] = (_block_diag(p["lru_wa"]).astype(MM_DTYPE), p["lru_ba"].reshape(1, GROUP),
                      _block_diag(p["lru_wx"]).astype(MM_DTYPE), p["lru_bx"].reshape(1, GROUP),
                      p["lru_lambda"].reshape(1, GROUP))
    s["a"], u = _lru_gates_fwd(s["xc"], *lru, name="lru_gates")
    s["ha"] = _scan(s["a"], u, name="lru_scan")
    y_a = _post_fwd(s["ha"], p["norm_a"], z, group=LRU_BD, gate_col=Z_AG // GROUP, name="lru_post")
    s["gts"] = _gates_fwd(z, *_gate_rows(p), name="gates")
    s["cq"], s["ck"] = _head_cols(_scan(None, s["gts"], name="fox_cumsum"), t)
    s["qkv_b"] = _qkv_prep(z, off=Z_BQ, name="fox_prep")
    (s["ob"], s["lse_b"]), hosted_out = _att_fwd(s["qkv_b"], s["cq"], s["ck"], mode="fox", name="fox_att", hosted=hosted)
    y_b = _post_fwd(s["ob"], p["norm_b"], z, group=HEAD, name="fox_post")
    s["cconv"] = _conv_fwd(z, p["gdn_conv_w"], jnp.zeros((3 * GROUP,), F32), ncols=3 * GROUP, coff=Z_CQKV,
                           name="gdn_conv")
    s["gdn_mids"] = _gdn_pre_fwd(s["cconv"], s["gts"], name="gdn_pre")
    s["oc"], s["gdn_states"] = _gdn_seq_fwd(s["gdn_mids"], name="gdn_seq")
    y_c = _gdn_out_fwd(s["oc"], z, p["gdn_norm"].reshape(1, HEAD), name="gdn_out")
    s["qkv_d"] = _qkv_prep(z, off=Z_DQ, name="dil_prep")
    (s["od"], s["lse_d"]), _ = _att_fwd(s["qkv_d"], s["cq"], s["ck"], mode="dil", name="dil_att")
    y_d = _post_fwd(s["od"], p["norm_d"], z, group=HEAD, name="dil_post")
    y = s["y"] = jnp.concatenate([y_a, y_b, y_c, y_d], axis=1)
    x1 = s["x1"] = _matmul(y, p["w_out"], add=x, name="out_proj")
    s["h2"] = _rmsnorm_fwd(x1, p["norm_ffn"], name="ffn_norm")
    s["uu"] = _matmul(s["h2"], p["ffn_w_up"], name="ffn_up")
    s["act"] = _ffn_act_fwd(s["uu"], p["ffn_conv_w"], p["ffn_conv_b"], name="ffn_conv_swiglu")
    return _matmul(s["act"], p["ffn_w_down"], add=x1, name="ffn_down"), s, hosted_out


def _layer_bwd(dx2, p, s, comm=None):
    t = dx2.shape[0]
    g = {}
    dact = _matmul(dx2, p["ffn_w_down"], form="nt", name="ffn_down_dx")
    g["ffn_w_down"] = _matmul(s["act"], dx2, form="tn", name="ffn_down_dw")
    du = _ffn_act_bwd(s["uu"], p["ffn_conv_w"], p["ffn_conv_b"], dact, name="ffn_conv_swiglu_bwd")
    duu, g["ffn_conv_w"], g["ffn_conv_b"] = _conv_bwd(s["uu"], p["ffn_conv_w"], du, ncols=s["uu"].shape[1],
                                                      name="ffn_conv_bwd")
    dh2 = _matmul(duu, p["ffn_w_up"], form="nt", name="ffn_up_dx")
    g["ffn_w_up"] = _matmul(s["h2"], duu, form="tn", name="ffn_up_dw")
    dx1, g["norm_ffn"] = _rmsnorm_bwd(s["x1"], p["norm_ffn"], dh2, dx2, name="ffn_norm_bwd")
    dy = _matmul(dx1, p["w_out"], form="nt", name="out_proj_dx")
    g["w_out"] = _matmul(s["y"], dx1, form="tn", name="out_proj_dw")
    z = s["z"]
    dha, g["norm_a"], dgate_a = _post_bwd(s["ha"], p["norm_a"], z, dy, group=LRU_BD, dy_col=0, gate_col=Z_AG // GROUP,
                                          name="lru_post_bwd")
    gsc = _scan(_shift_up(s["a"]), dha, reverse=True, name="lru_scan_bwd")
    dxc, dwa, dba, dwx, dbx, dlam = _lru_gates_bwd(s["xc"], *s["lru"], gsc, _shift_down(s["ha"]), name="lru_gates_bwd")
    g["lru_wa"], g["lru_wx"] = _diag_blocks(dwa), _diag_blocks(dwx)
    g["lru_ba"], g["lru_bx"], g["lru_lambda"] = dba.reshape(GROUP), dbx.reshape(GROUP), dlam.reshape(GROUP)
    dax, g["lru_conv_w"], g["lru_conv_b"] = _conv_bwd(z, p["lru_conv_w"], dxc, ncols=GROUP, coff=Z_AX,
                                                      name="lru_conv_bwd")
    dob, g["norm_b"], _ = _post_bwd(s["ob"], p["norm_b"], z, dy, group=HEAD, dy_col=1, name="fox_post_bwd")
    (dbq, dcq, delta), sent = _att_bwd_q(s["qkv_b"], s["cq"], s["ck"], dob, s["ob"], s["lse_b"], mode="fox",
                                         name="fox_att_dq", hosted=None if comm is None else comm[0])
    operand = None if comm is None else comm[1](sent[0])
    (dbk, dbv, dck), got = _att_bwd_kv(s["qkv_b"], s["cq"], s["ck"], dob, s["lse_b"].reshape(N_HEADS, 1, t),
                                       delta.reshape(N_HEADS, 1, t), mode="fox", name="fox_att_dkv",
                                       hosted=None if comm is None else _exchange_step(operand))
    hosted_out = None if comm is None else (operand, got[0])
    pad_lanes = ((0, 0), (0, LANES - N_HEADS))
    dc_rows = _sum_slots([jnp.pad(dcq.reshape(N_HEADS, t).T, pad_lanes), jnp.pad(dck.reshape(N_HEADS, t).T, pad_lanes)],
                         name="fox_dc_sum")
    dgts_fox = _scan(None, dc_rows, reverse=True, name="fox_cumsum_bwd")
    dod, g["norm_d"], _ = _post_bwd(s["od"], p["norm_d"], z, dy, group=HEAD, dy_col=3, name="dil_post_bwd")
    (ddq, _, delta), _ = _att_bwd_q(s["qkv_d"], s["cq"], s["ck"], dod, s["od"], s["lse_d"], mode="dil", name="dil_att_dq")
    (ddk, ddv, _), _ = _att_bwd_kv(s["qkv_d"], s["cq"], s["ck"], dod, s["lse_d"].reshape(N_HEADS, 1, t),
                                   delta.reshape(N_HEADS, 1, t), mode="dil", name="dil_att_dkv")
    doc, dcz, g["gdn_norm"] = _gdn_out_bwd(s["oc"], z, p["gdn_norm"].reshape(1, HEAD), dy, dy_col=2, name="gdn_out_bwd")
    dmids = _gdn_seq_bwd(s["gdn_mids"], s["gdn_states"], doc, name="gdn_seq_bwd")
    dcconv, dgts_gdn = _gdn_pre_bwd(s["cconv"], s["gts"], dmids, name="gdn_pre_bwd")
    dcqkv, g["gdn_conv_w"], _ = _conv_bwd(z, p["gdn_conv_w"], dcconv, ncols=3 * GROUP, coff=Z_CQKV,
                                          name="gdn_conv_bwd")
    dgts = _sum_slots([dgts_fox, dgts_gdn], name="gates_dsum")
    dzg, dfb, dal, ddt = _gates_bwd(z, *_gate_rows(p), dgts, name="gates_bwd")
    g["fox_f_bias"] = dfb[0, G_F:G_F + N_HEADS]
    g["gdn_a_log"] = dal[0, G_ALPHA:G_ALPHA + N_HEADS]
    g["gdn_dt_bias"] = ddt[0, G_ALPHA:G_ALPHA + N_HEADS]
    dz = jnp.concatenate([dax, dgate_a, dbq, dbk, dbv, dcqkv, dcz, ddq, ddk, ddv, dzg], axis=1)
    dh = _matmul(dz, p["w_in"], form="nt", name="in_proj_dx")
    g["w_in"] = _matmul(s["h"], dz, form="tn", name="in_proj_dw")
    dx, g["norm_mix"] = _rmsnorm_bwd(s["x"], p["norm_mix"], dh, dx1, name="mix_norm_bwd")
    return dx, g, hosted_out


def _local_step(x, tgt, layers, norm_final):
    saved = []
    for p in layers:
        x, s, _ = _layer_fwd(x, p)
        saved.append(s)
    loss, dx, dnf = _loss_head(x, norm_final, tgt, name="loss_head")
    grads = []
    for p, s in zip(reversed(layers), reversed(saved)):
        dx, g, _ = _layer_bwd(dx, p, s)
        grads.append(g)
    return loss, dx, grads[::-1], dnf


BIG = ("w_in", "w_out", "ffn_w_up", "ffn_w_down")
PACK_ROWS = 4096
SHARDED_SMALL = ("lru_conv_w", "gdn_conv_w", "ffn_conv_w")
NAMES = ("norm_mix", "w_in", "lru_conv_w", "lru_conv_b", "lru_wa", "lru_ba", "lru_wx", "lru_bx", "lru_lambda",
         "fox_f_bias", "gdn_conv_w", "gdn_a_log", "gdn_dt_bias", "gdn_norm", "norm_a", "norm_b", "norm_d", "w_out",
         "norm_ffn", "ffn_w_up", "ffn_conv_w", "ffn_conv_b", "ffn_w_down", "norm_final")
SMALL = tuple(n for n in NAMES if n not in BIG)


def _big_pieces(g, k_axis_cols):
    if k_axis_cols:
        d, n = g.shape
        return g.reshape(d, 4, n // 4).transpose(1, 0, 2).reshape(4, -1, LANES)
    return g.reshape(4, -1, LANES)


def _reduce_pack(gl):
    whole = {"w_in": _unpad_w_in(gl["w_in"]), "w_out": gl["w_out"], "ffn_w_up": gl["ffn_w_up"],
             "ffn_w_down": gl["ffn_w_down"]}
    cols = {"w_in": True, "w_out": False, "ffn_w_up": True, "ffn_w_down": False}
    pieces = [_big_pieces(whole[n], cols[n]) for n in BIG]
    rows = [q.shape[1] for q in pieces]
    pad = (-sum(rows)) % PACK_ROWS
    packed = jnp.concatenate(pieces + [jnp.zeros((4, pad, LANES), F32)], axis=1)
    return packed.reshape(4, 2, packed.shape[1] // 2, LANES), rows


def _chip_sum(halves, c, got):
    half = halves.shape[2]
    rb = _tile(half, 2048, 2 * SUBLANES)

    def body(c_ref, mine_ref, got_ref, o_ref):
        o_ref[...] = (mine_ref[0] + got_ref[...]).astype(o_ref.dtype)

    slot = pl.BlockSpec((1, rb, LANES), lambda k, i, c_ref: (k, i, 0))
    grid_spec = pltpu.PrefetchScalarGridSpec(
        num_scalar_prefetch=1, grid=(4, half // rb),
        in_specs=[pl.BlockSpec((1, 1, rb, LANES), lambda k, i, c_ref: (k, c_ref[0], i, 0)), slot], out_specs=slot)
    return pl.pallas_call(body, name="grad_sibling_sum", grid_spec=grid_spec, out_shape=_sds((4, half, LANES), MM_DTYPE),
                          compiler_params=_params("parallel", "parallel"))(c.reshape(1).astype(jnp.int32), halves, got)


def _reduce_finish(chip_sum, from_chips, rows, *, c):
    x, y, _ = _position()
    (from_chips,) = _place_own([from_chips], [lax.dynamic_index_in_dim(chip_sum, 2 * x + y, 0, keepdims=False)])
    total_half = _sum_slots(from_chips, name="grad_chip_sum")
    other_half = _sibling_send(total_half, name="grad_sibling_swap")
    total = jnp.where(c == 0, jnp.concatenate([total_half, other_half]), jnp.concatenate([other_half, total_half]))
    out, off = {}, 0
    for n, r in zip(BIG, rows):
        out[n] = total[off:off + r]
        off += r
    return out


def kernel(x, norm_mix, w_in, lru_conv_w, lru_conv_b, lru_wa, lru_ba, lru_wx, lru_bx, lru_lambda, fox_f_bias, gdn_conv_w, gdn_a_log, gdn_dt_bias, gdn_norm, norm_a, norm_b, norm_d, w_out, norm_ffn, ffn_w_up, ffn_conv_w, ffn_conv_b, ffn_w_down, norm_final, loss_target, m_norm_mix, m_w_in, m_lru_conv_w, m_lru_conv_b, m_lru_wa, m_lru_ba, m_lru_wx, m_lru_bx, m_lru_lambda, m_fox_f_bias, m_gdn_conv_w, m_gdn_a_log, m_gdn_dt_bias, m_gdn_norm, m_norm_a, m_norm_b, m_norm_d, m_w_out, m_norm_ffn, m_ffn_w_up, m_ffn_conv_w, m_ffn_conv_b, m_ffn_w_down, m_norm_final, v_norm_mix, v_w_in, v_lru_conv_w, v_lru_conv_b, v_lru_wa, v_lru_ba, v_lru_wx, v_lru_bx, v_lru_lambda, v_fox_f_bias, v_gdn_conv_w, v_gdn_a_log, v_gdn_dt_bias, v_gdn_norm, v_norm_a, v_norm_b, v_norm_d, v_w_out, v_norm_ffn, v_ffn_w_up, v_ffn_conv_w, v_ffn_conv_b, v_ffn_w_down, v_norm_final):
    w = dict(zip(NAMES, (norm_mix, w_in, lru_conv_w, lru_conv_b, lru_wa, lru_ba, lru_wx, lru_bx, lru_lambda, fox_f_bias,
                         gdn_conv_w, gdn_a_log, gdn_dt_bias, gdn_norm, norm_a, norm_b, norm_d, w_out, norm_ffn, ffn_w_up,
                         ffn_conv_w, ffn_conv_b, ffn_w_down, norm_final)))
    m = dict(zip(NAMES, (m_norm_mix, m_w_in, m_lru_conv_w, m_lru_conv_b, m_lru_wa, m_lru_ba, m_lru_wx, m_lru_bx,
                         m_lru_lambda, m_fox_f_bias, m_gdn_conv_w, m_gdn_a_log, m_gdn_dt_bias, m_gdn_norm, m_norm_a,
                         m_norm_b, m_norm_d, m_w_out, m_norm_ffn, m_ffn_w_up, m_ffn_conv_w, m_ffn_conv_b, m_ffn_w_down,
                         m_norm_final)))
    v = dict(zip(NAMES, (v_norm_mix, v_w_in, v_lru_conv_w, v_lru_conv_b, v_lru_wa, v_lru_ba, v_lru_wx, v_lru_bx,
                         v_lru_lambda, v_fox_f_bias, v_gdn_conv_w, v_gdn_a_log, v_gdn_dt_bias, v_gdn_norm, v_norm_a,
                         v_norm_b, v_norm_d, v_w_out, v_norm_ffn, v_ffn_w_up, v_ffn_conv_w, v_ffn_conv_b, v_ffn_w_down,
                         v_norm_final)))
    depth = w_in.shape[0]
    xi, yi, ci = _position()
    chip = 2 * xi + yi

    conv_rows, conv_shapes = _pack_rows([w[n] for n in SHARDED_SMALL])
    (conv_all,) = _chip_gather([conv_rows], name="conv_taps_gather")
    conv_full = {}
    per_chip = [_unpack_rows(conv_all[k], conv_shapes) for k in range(4)]
    for i, n in enumerate(SHARDED_SMALL):
        conv_full[n] = jnp.concatenate([per_chip[k][i] for k in range(4)], axis=-1)
    halves = lambda l: [w[n][l].astype(MM_DTYPE).reshape(2, w[n].shape[1] // 2, w[n].shape[2]) for n in BIG]

    def layer_params(l, gathered):
        g_in, g_out, g_up, g_dn = (g.reshape((4,) + w[n].shape[1:]) for n, g in zip(BIG, gathered))
        p = {n: w[n][l] for n in SMALL if n != "norm_final" and n not in SHARDED_SMALL}
        for n in SHARDED_SMALL:
            p[n] = conv_full[n][l]
        p["w_in"] = _pad_w_in(jnp.concatenate([g_in[k] for k in range(4)], axis=1))
        p["w_out"] = g_out.reshape(-1, g_out.shape[-1])
        p["ffn_w_up"] = jnp.concatenate([g_up[k] for k in range(4)], axis=1)
        p["ffn_w_down"] = g_dn.reshape(-1, g_dn.shape[-1])
        return p

    gathered = _place_own(_run_step(_gather_halves_step(halves(0)), name="weights_gather"), halves(0))
    layers, saved, xl = [], [], x[0]
    for l in range(depth):
        p = layer_params(l, gathered)
        step = _gather_halves_step(halves(l + 1)) if l + 1 < depth else None
        xl, s, results = _layer_fwd(xl, p, hosted=step)
        if step is not None:
            gathered = _place_own(results, halves(l + 1))
        layers.append(p)
        saved.append(s)
    loss, dx, g_norm_final = _loss_head(xl, norm_final, loss_target[0], name="loss_head")
    loss = lax.psum(loss, ("x", "y", "c"))
    grads, big, pending = [None] * depth, [None] * depth, None
    for l in reversed(range(depth)):
        if pending is None:
            dx, grads[l], _ = _layer_bwd(dx, layers[l], saved[l])
        else:
            before, halves_, rows = pending
            dx, grads[l], (operand, from_chips) = _layer_bwd(
                dx, layers[l], saved[l], comm=(_sibling_step(halves_, True), functools.partial(_chip_sum, halves_, ci)))
            big[before] = _reduce_finish(operand, from_chips, rows, c=ci)
        pending = (l,) + _reduce_pack(grads[l])
    before, halves_, rows = pending
    (got,) = _run_step(_sibling_step(halves_, True), name="grad_sibling_send")
    operand = _chip_sum(halves_, ci, got)
    (from_chips,) = _run_step(_exchange_step(operand), name="grad_chip_exchange")
    big[before] = _reduce_finish(operand, from_chips, rows, c=ci)
    grad_x = dx

    small_names = [n for n in SMALL if n != "norm_final"]
    small_rows, small_shapes = _pack_rows([jnp.stack([grads[l][n] for l in range(depth)]) for n in small_names]
                                          + [g_norm_final])
    small_sum = _unpack_rows(_all_sum(small_rows, name="small_grads_sum"), small_shapes)
    gsum = dict(zip(small_names + ["norm_final"], small_sum))
    for n in SHARDED_SMALL:
        width = w[n].shape[-1]
        gsum[n] = lax.dynamic_slice_in_dim(gsum[n], chip * width, width, axis=-1)
    for n in BIG:
        gsum[n] = jnp.stack([big[l][n].reshape(w[n].shape[1:]) for l in range(depth)])

    delta, new_m, new_v = {}, {}, {}
    for n in BIG:
        cols = w[n].shape[-1]
        d_, m_, v_ = _adamw(w[n].reshape(-1, cols), gsum[n].reshape(-1, cols), m[n].reshape(-1, cols),
                            v[n].reshape(-1, cols), name="adamw_" + n)
        delta[n], new_m[n], new_v[n] = (a.reshape(w[n].shape) for a in (d_, m_, v_))
    packs = [_pack_rows([src[n] for n in SMALL]) for src in (w, gsum, m, v)]
    outs = _adamw(*[pk[0] for pk in packs], name="adamw_small")
    for dst, rows_ in zip((delta, new_m, new_v), outs):
        dst.update(zip(SMALL, _unpack_rows(rows_, packs[0][1])))
    return (loss, grad_x[None], *[gsum[n] for n in NAMES], *[delta[n] for n in NAMES], *[new_m[n] for n in NAMES],
            *[new_v[n] for n in NAMES])
```

```python
import functools

import numpy as np
import jax
import jax.numpy as jnp
from jax import lax
from jax.experimental import pallas as pl
from jax.experimental.pallas import tpu as pltpu

F32 = jnp.float32
MM_DTYPE = jnp.bfloat16
VMEM_LIMIT_BYTES = 56 * 1024 * 1024
LANES = 128
SUBLANES = 8

GROUP = 512
HEAD = 128
N_HEADS = GROUP // HEAD
LRU_BLOCKS = 8
LRU_BD = GROUP // LRU_BLOCKS
LRU_C = 8.0
GDN_CHUNK = 64
DILATED_PAIRS = ((128, 1), (512, 4), (2048, 16))
EPS = 1e-6
NEG = -1e30
ATT_SCALE = HEAD ** -0.5
ATT_TILE = 512

ADAM_LR, ADAM_B1, ADAM_B2, ADAM_EPS, ADAM_WD, ADAM_STEP = 0.001, 0.9, 0.999, 1e-08, 0.01, 10

Z_AX, Z_AG, Z_BQ, Z_BK, Z_BV = 0, 512, 1024, 1536, 2048
Z_CQKV, Z_CZ, Z_DQ, Z_DK, Z_DV, Z_GATES, Z_COLS = 2560, 4096, 4608, 5120, 5632, 6144, 6272
IN_COLS = 6156
G_F, G_BETA, G_ALPHA = 0, 4, 8

MESH_ID = pl.DeviceIdType.MESH
ANY = pl.BlockSpec(memory_space=pl.ANY)


def _tile(n, target, align):
    t = min(n, target) // align * align
    while t >= align:
        if n % t == 0:
            return t
        t -= align
    return n


def _params(*sem):
    return pltpu.CompilerParams(dimension_semantics=sem, vmem_limit_bytes=VMEM_LIMIT_BYTES)


def _sds(shape, dtype=F32):
    return jax.ShapeDtypeStruct(tuple(shape), dtype)


def _dg(a, b, ca, cb, hi, batched=False):
    off = int(batched)
    dims = (((ca + off,), (cb + off,)), (((0,), (0,)) if batched else ((), ())))
    dot = lambda p, q: lax.dot_general(p, q, dims, preferred_element_type=F32)
    if hi:
        a_hi, b_hi = a.astype(MM_DTYPE), b.astype(MM_DTYPE)
        a_lo = (a - a_hi.astype(F32)).astype(MM_DTYPE)
        b_lo = (b - b_hi.astype(F32)).astype(MM_DTYPE)
        return dot(a_hi, b_hi) + (dot(a_hi, b_lo) + dot(a_lo, b_hi))
    return dot(a.astype(MM_DTYPE), b.astype(MM_DTYPE))


_FORMS = {"nn": (1, 0), "nt": (1, 1), "tn": (0, 0)}


@functools.partial(jax.custom_vjp, nondiff_argnums=(2, 3, 4))
def _mmf(a, b, form, hi, batched=False):
    ca, cb = _FORMS[form]
    return _dg(a, b, ca, cb, hi, batched)


def _mmf_fwd(a, b, form, hi, batched):
    return _mmf(a, b, form, hi, batched), (a, b)


def _mmf_bwd(form, hi, batched, res, g):
    a, b = res
    if form == "nn":
        return _dg(g, b, 1, 1, hi, batched), _dg(a, g, 0, 0, hi, batched)
    if form == "nt":
        return _dg(g, b, 1, 0, hi, batched), _dg(g, a, 0, 0, hi, batched)
    return _dg(b, g, 1, 1, hi, batched), _dg(a, g, 1, 0, hi, batched)


_mmf.defvjp(_mmf_fwd, _mmf_bwd)


def _mm(a, b):
    return _mmf(a, b, "nn", False)


def _mm_nt(a, b):
    return _mmf(a, b, "nt", False)


def _mm_tn(a, b):
    return _mmf(a, b, "tn", False)


MXU_DIM = 256
MATMUL_VMEM_BUDGET = 40 * 1024 * 1024


def _matmul_tiles(m, n, k, a_bytes, b_bytes, o_bytes, has_add):
    divisors = lambda d: [t for t in range(d, 0, -LANES) if d % t == 0 and t % LANES == 0] or [d]
    for tk in divisors(k):
        fits = []
        for tm in divisors(m):
            for tn in divisors(n):
                need = 2 * (tm * tk * a_bytes + tk * tn * b_bytes) + 2 * tm * tn * o_bytes
                need += (2 * tm * tn * 4 if has_add else 0) + (tm * tn * 4 if tk < k else 0)
                if tm <= 1024 and tn <= 1024 and need <= MATMUL_VMEM_BUDGET:
                    fits.append((min(tm, MXU_DIM) * min(tn, MXU_DIM), tm * tn, tm, tn))
        if fits:
            return max(fits)[2:] + (tk,)
    return _tile(m, 128, LANES), _tile(n, 128, LANES), _tile(k, 128, LANES)


def _matmul(a, b, *, form="nn", add=None, out_dtype=F32, tiles=None, name):
    ca, cb = _FORMS[form]
    m, k = (a.shape[1], a.shape[0]) if form == "tn" else a.shape
    n = b.shape[0] if form == "nt" else b.shape[1]
    tm, tn, tk = tiles or _matmul_tiles(m, n, k, a.dtype.itemsize, b.dtype.itemsize, jnp.dtype(out_dtype).itemsize,
                                        add is not None)
    nk = k // tk
    a_spec = (pl.BlockSpec((tk, tm), lambda i, j, kk: (kk, i)) if form == "tn"
              else pl.BlockSpec((tm, tk), lambda i, j, kk: (i, kk)))
    b_spec = (pl.BlockSpec((tn, tk), lambda i, j, kk: (j, kk)) if form == "nt"
              else pl.BlockSpec((tk, tn), lambda i, j, kk: (kk, j)))
    o_spec = pl.BlockSpec((tm, tn), lambda i, j, kk: (i, j))

    def body(*refs):
        a_ref, b_ref = refs[:2]
        add_ref = None if add is None else refs[2]
        o_ref = refs[2 + (add is not None)]
        part = _dg(a_ref[...], b_ref[...], ca, cb, False)
        if nk == 1:
            o_ref[...] = (part if add is None else part + add_ref[...]).astype(out_dtype)
            return
        acc_ref = refs[-1]
        kk = pl.program_id(2)

        @pl.when(kk == 0)
        def _():
            acc_ref[...] = jnp.zeros_like(acc_ref)

        acc_ref[...] += part

        @pl.when(kk == nk - 1)
        def _():
            r = acc_ref[...]
            if add is not None:
                r = r + add_ref[...]
            o_ref[...] = r.astype(out_dtype)

    args = (a, b) if add is None else (a, b, add)
    specs = [a_spec, b_spec] if add is None else [a_spec, b_spec, o_spec]
    return pl.pallas_call(
        body, name=name, grid=(m // tm, n // tn, nk), in_specs=specs, out_specs=o_spec,
        out_shape=_sds((m, n), out_dtype), scratch_shapes=[] if nk == 1 else [pltpu.VMEM((tm, tn), F32)],
        compiler_params=_params("parallel", "parallel", "arbitrary"))(*args)


def _rms(x, g):
    return x * lax.rsqrt(jnp.mean(x * x, axis=-1, keepdims=True) + EPS) * g


def _rmsnorm_fwd(x, g, *, name):
    t, d = x.shape
    tb = _tile(t, 512, SUBLANES)

    def body(x_ref, g_ref, o_ref):
        o_ref[...] = _rms(x_ref[...], g_ref[...]).astype(o_ref.dtype)

    row = pl.BlockSpec((tb, d), lambda i: (i, 0))
    vec = pl.BlockSpec((1, d), lambda i: (0, 0))
    return pl.pallas_call(body, name=name, grid=(t // tb,), in_specs=[row, vec], out_specs=row,
                          out_shape=_sds((t, d), MM_DTYPE), compiler_params=_params("parallel"))(x, g.reshape(1, d))


def _rmsnorm_bwd(x, g, dh, res, *, name):
    t, d = x.shape
    tb = _tile(t, 256, SUBLANES)

    def body(x_ref, g_ref, dh_ref, res_ref, dx_ref, dg_ref):
        _, vjp = jax.vjp(_rms, x_ref[...], g_ref[...])
        dx, dg = vjp(dh_ref[...])
        dx_ref[...] = dx + res_ref[...]

        @pl.when(pl.program_id(0) == 0)
        def _():
            dg_ref[...] = jnp.zeros_like(dg_ref)

        dg_ref[...] += dg

    row = pl.BlockSpec((tb, d), lambda i: (i, 0))
    vec = pl.BlockSpec((1, d), lambda i: (0, 0))
    dx, dg = pl.pallas_call(body, name=name, grid=(t // tb,), in_specs=[row, vec, row, row], out_specs=[row, vec],
                            out_shape=[_sds((t, d)), _sds((1, d))], compiler_params=_params("arbitrary"))(
                                x, g.reshape(1, d), dh, res)
    return dx, dg.reshape(d)


def _loss_head(x, g, tgt, *, name):
    t, d = x.shape
    tb = _tile(t, 256, SUBLANES)

    def body(x_ref, g_ref, t_ref, loss_ref, dx_ref, dg_ref):
        def f(xv, gv):
            e = _rms(xv, gv) - t_ref[...]
            return jnp.sum(jnp.sum(e * e, axis=-1, keepdims=True), axis=0, keepdims=True) * (0.5 / d)

        l, vjp = jax.vjp(f, x_ref[...], g_ref[...])
        dx, dg = vjp(jnp.ones((1, 1), F32))
        dx_ref[...] = dx

        @pl.when(pl.program_id(0) == 0)
        def _():
            dg_ref[...] = jnp.zeros_like(dg_ref)
            loss_ref[...] = jnp.zeros_like(loss_ref)

        dg_ref[...] += dg
        loss_ref[...] += jnp.zeros(loss_ref.shape, F32) + l

    row = pl.BlockSpec((tb, d), lambda i: (i, 0))
    vec = pl.BlockSpec((1, d), lambda i: (0, 0))
    lspec = pl.BlockSpec((SUBLANES, LANES), lambda i: (0, 0))
    loss, dx, dg = pl.pallas_call(
        body, name=name, grid=(t // tb,), in_specs=[row, vec, row], out_specs=[lspec, row, vec],
        out_shape=[_sds((SUBLANES, LANES)), _sds((t, d)), _sds((1, d))], compiler_params=_params("arbitrary"))(
            x, g.reshape(1, d), tgt)
    return loss[0, 0], dx, dg.reshape(d)


def _delayed(x, prev, j):
    if j == 0:
        return x
    sh = pltpu.roll(x, j, axis=0)
    row = lax.broadcasted_iota(jnp.int32, prev.shape, 0)
    top = jnp.where(row < j, pltpu.roll(prev, j, axis=0), sh[0:SUBLANES])
    return jnp.concatenate([top, sh[SUBLANES:]], axis=0)


def _advanced(x, nxt, j):
    if j == 0:
        return x
    tb = x.shape[0]
    sh = pltpu.roll(x, tb - j, axis=0)
    row = lax.broadcasted_iota(jnp.int32, nxt.shape, 0)
    bot = jnp.where(row + j < SUBLANES, sh[tb - SUBLANES:], pltpu.roll(nxt, SUBLANES - j, axis=0))
    return jnp.concatenate([sh[:tb - SUBLANES], bot], axis=0)


def _conv_tiles(t, ncols, coff):
    tc = _tile(ncols, 512, LANES)
    assert coff % tc == 0
    tb = _tile(t, 512, SUBLANES)
    return tc, tb, coff // tc


def _conv_fwd(x, w, b, *, ncols, coff=0, name):
    t = x.shape[0]
    kw = w.shape[0]
    tc, tb, cb = _conv_tiles(t, ncols, coff)
    n8 = tb // SUBLANES

    def body(x_ref, p_ref, w_ref, b_ref, o_ref):
        xv = x_ref[...]
        prev = jnp.where(pl.program_id(1) > 0, p_ref[...], 0.0)
        acc = jnp.zeros_like(xv) + b_ref[...]
        for k in range(kw):
            acc = acc + w_ref[k:k + 1, :] * _delayed(xv, prev, kw - 1 - k)
        o_ref[...] = acc

    in_specs = [
        pl.BlockSpec((tb, tc), lambda c, i: (i, c + cb)),
        pl.BlockSpec((SUBLANES, tc), lambda c, i: (jnp.maximum(i * n8 - 1, 0), c + cb)),
        pl.BlockSpec((kw, tc), lambda c, i: (0, c)),
        pl.BlockSpec((1, tc), lambda c, i: (0, c)),
    ]
    return pl.pallas_call(
        body, name=name, grid=(ncols // tc, t // tb), in_specs=in_specs,
        out_specs=pl.BlockSpec((tb, tc), lambda c, i: (i, c)), out_shape=_sds((t, ncols)),
        compiler_params=_params("parallel", "parallel"))(x, x, w, b.reshape(1, ncols))


def _conv_bwd(x, w, dy, *, ncols, coff=0, name):
    t = x.shape[0]
    kw = w.shape[0]
    tc, tb, cb = _conv_tiles(t, ncols, coff)
    n8 = tb // SUBLANES
    nt = t // tb

    split = dy.ndim == 3
    nhalf = ncols // tc // 2
    half_of = lambda c: (c >= nhalf).astype(jnp.int32)

    def body(x_ref, p_ref, dy_ref, n_ref, w_ref, dx_ref, dw_ref, db_ref):
        i = pl.program_id(1)
        xv, dyv = x_ref[...], (dy_ref[0] if split else dy_ref[...])
        prev = jnp.where(i > 0, p_ref[...], 0.0)
        nxt = jnp.where(i < nt - 1, n_ref[0] if split else n_ref[...], 0.0)

        @pl.when(i == 0)
        def _():
            dw_ref[...] = jnp.zeros_like(dw_ref)
            db_ref[...] = jnp.zeros_like(db_ref)

        dx = jnp.zeros_like(dyv)
        for k in range(kw):
            j = kw - 1 - k
            dx = dx + w_ref[k:k + 1, :] * _advanced(dyv, nxt, j)
            dw_ref[k:k + 1, :] += jnp.sum(dyv * _delayed(xv, prev, j), axis=0, keepdims=True)
        dx_ref[...] = dx.astype(dx_ref.dtype)
        db_ref[...] += jnp.sum(dyv, axis=0, keepdims=True)

    in_specs = [
        pl.BlockSpec((tb, tc), lambda c, i: (i, c + cb)),
        pl.BlockSpec((SUBLANES, tc), lambda c, i: (jnp.maximum(i * n8 - 1, 0), c + cb)),
        (pl.BlockSpec((1, tb, tc), lambda c, i: (half_of(c), i, c - half_of(c) * nhalf)) if split
         else pl.BlockSpec((tb, tc), lambda c, i: (i, c))),
        (pl.BlockSpec((1, SUBLANES, tc),
                      lambda c, i: (half_of(c), jnp.minimum((i + 1) * n8, nt * n8 - 1), c - half_of(c) * nhalf))
         if split else pl.BlockSpec((SUBLANES, tc), lambda c, i: (jnp.minimum((i + 1) * n8, nt * n8 - 1), c))),
        pl.BlockSpec((kw, tc), lambda c, i: (0, c)),
    ]
    out_specs = [
        pl.BlockSpec((tb, tc), lambda c, i: (i, c)),
        pl.BlockSpec((kw, tc), lambda c, i: (0, c)),
        pl.BlockSpec((1, tc), lambda c, i: (0, c)),
    ]
    dx, dw, db = pl.pallas_call(
        body, name=name, grid=(ncols // tc, nt), in_specs=in_specs, out_specs=out_specs,
        out_shape=[_sds((t, ncols), MM_DTYPE), _sds((kw, ncols)), _sds((1, ncols))],
        compiler_params=_params("parallel", "arbitrary"))(x, x, dy, dy, w)
    return dx, dw, db.reshape(ncols)


def _scan(a, u, *, reverse=False, name):
    t, c = u.shape
    tc = _tile(c, 512, LANES)
    tb = _tile(t, 256, SUBLANES)
    nt = t // tb

    def body(*refs):
        if a is None:
            u_ref, o_ref, carry_ref = refs
        else:
            a_ref, u_ref, o_ref, carry_ref = refs

        @pl.when(pl.program_id(1) == 0)
        def _():
            carry_ref[...] = jnp.zeros_like(carry_ref)

        hv = u_ref[...]
        av = None if a is None else a_ref[...]
        row = lax.broadcasted_iota(jnp.int32, hv.shape, 0)
        s = 1
        while s < tb:
            live = row < tb - s if reverse else row >= s
            shift = tb - s if reverse else s
            h_sh = jnp.where(live, pltpu.roll(hv, shift, axis=0), 0.0)
            if av is None:
                hv = hv + h_sh
            else:
                hv = av * h_sh + hv
                av = av * jnp.where(live, pltpu.roll(av, shift, axis=0), 1.0)
            s *= 2
        hv = hv + carry_ref[...] if av is None else hv + av * carry_ref[...]
        o_ref[...] = hv
        carry_ref[...] = o_ref[pl.ds(0 if reverse else tb - 1, 1), :]

    spec = pl.BlockSpec((tb, tc), (lambda cc, i: (nt - 1 - i, cc)) if reverse else (lambda cc, i: (i, cc)))
    args, specs = ((u,), [spec]) if a is None else ((a, u), [spec, spec])
    return pl.pallas_call(
        body, name=name, grid=(c // tc, t // tb), in_specs=specs, out_specs=spec, out_shape=_sds((t, c)),
        scratch_shapes=[pltpu.VMEM((1, tc), F32)], compiler_params=_params("parallel", "arbitrary"))(*args)


def _shift_down(x):
    return lax.pad(x, jnp.zeros((), x.dtype), ((1, -1, 0), (0, 0, 0)))


def _shift_up(x):
    return lax.pad(x, jnp.zeros((), x.dtype), ((-1, 1, 0), (0, 0, 0)))


def _neg_expm1(y):
    small = -(y * (1.0 + y * (0.5 + y * (1.0 / 6.0 + y * (1.0 / 24.0 + y * (1.0 / 120.0))))))
    return jnp.where(y > -0.05, small, 1.0 - jnp.exp(y))


def _lru_gates(xc, wa, ba, wx, bx, lam):
    r = jax.nn.sigmoid(_mm(xc, wa) + ba)
    i = jax.nn.sigmoid(_mm(xc, wx) + bx)
    log_a = -LRU_C * r * jax.nn.softplus(-lam)
    a = jnp.exp(log_a)
    u = jnp.sqrt(_neg_expm1(2.0 * log_a)) * (i * xc)
    return a, u


def _lru_specs(t):
    tb = _tile(t, 256, SUBLANES)
    row = pl.BlockSpec((tb, GROUP), lambda i: (i, 0))
    mat = pl.BlockSpec((GROUP, GROUP), lambda i: (0, 0))
    vec = pl.BlockSpec((1, GROUP), lambda i: (0, 0))
    return tb, row, mat, vec


def _lru_gates_fwd(xc, wa, ba, wx, bx, lam, *, name):
    t = xc.shape[0]
    tb, row, mat, vec = _lru_specs(t)

    def body(xc_ref, wa_ref, ba_ref, wx_ref, bx_ref, lam_ref, a_ref, u_ref):
        a, u = _lru_gates(xc_ref[...], wa_ref[...], ba_ref[...], wx_ref[...], bx_ref[...], lam_ref[...])
        a_ref[...] = a
        u_ref[...] = u

    return pl.pallas_call(
        body, name=name, grid=(t // tb,), in_specs=[row, mat, vec, mat, vec, vec], out_specs=[row, row],
        out_shape=[_sds((t, GROUP)), _sds((t, GROUP))], compiler_params=_params("parallel"))(xc, wa, ba, wx, bx, lam)


def _lru_gates_bwd(xc, wa, ba, wx, bx, lam, g, h_prev, *, name):
    t = xc.shape[0]
    tb, row, mat, vec = _lru_specs(t)

    def body(xc_ref, wa_ref, ba_ref, wx_ref, bx_ref, lam_ref, g_ref, hp_ref,
             dxc_ref, dwa_ref, dba_ref, dwx_ref, dbx_ref, dlam_ref):
        _, vjp = jax.vjp(_lru_gates, xc_ref[...], wa_ref[...], ba_ref[...], wx_ref[...], bx_ref[...], lam_ref[...])
        gv = g_ref[...]
        dxc, dwa, dba, dwx, dbx, dlam = vjp((gv * hp_ref[...], gv))
        dxc_ref[...] = dxc
        accs = (dwa_ref, dba_ref, dwx_ref, dbx_ref, dlam_ref)

        @pl.when(pl.program_id(0) == 0)
        def _():
            for r in accs:
                r[...] = jnp.zeros_like(r)

        for r, v in zip(accs, (dwa, dba, dwx, dbx, dlam)):
            r[...] += v

    return pl.pallas_call(
        body, name=name, grid=(t // tb,), in_specs=[row, mat, vec, mat, vec, vec, row, row],
        out_specs=[row, mat, vec, mat, vec, vec],
        out_shape=[_sds((t, GROUP)), _sds((GROUP, GROUP)), _sds((1, GROUP)), _sds((GROUP, GROUP)), _sds((1, GROUP)),
                   _sds((1, GROUP))],
        compiler_params=_params("arbitrary"))(xc, wa, ba, wx, bx, lam, g, h_prev)


def _block_diag(w):
    eye = jnp.eye(LRU_BLOCKS, dtype=w.dtype)
    return (eye[:, None, :, None] * w[:, :, None, :]).reshape(GROUP, GROUP)


def _diag_blocks(m):
    m4 = m.reshape(LRU_BLOCKS, LRU_BD, LRU_BLOCKS, LRU_BD)
    return jnp.stack([m4[n, :, n, :] for n in range(LRU_BLOCKS)])


def _group_mean_matrix(group):
    idx = np.arange(GROUP) // group
    return jnp.asarray((idx[:, None] == idx[None, :]).astype(np.float32) / group, F32)


def _post(h, gain, means, gate):
    y = h * lax.rsqrt(_mmf(h * h, means, "nn", True) + EPS) * gain
    return y if gate is None else y * jax.nn.gelu(gate)


def _post_specs(t):
    tb = _tile(t, 512, SUBLANES)
    col = lambda c: pl.BlockSpec((tb, GROUP), lambda i: (i, c))
    return tb, col, pl.BlockSpec((1, GROUP), lambda i: (0, 0)), pl.BlockSpec((GROUP, GROUP), lambda i: (0, 0))


def _post_fwd(h, gain, z, *, group, gate_col=None, name):
    t = h.shape[0]
    tb, col, vec, mat = _post_specs(t)
    gated = gate_col is not None

    def body(*refs):
        gate = refs[3][...] if gated else None
        refs[-1][...] = _post(refs[0][...], refs[1][...], refs[2][...], gate).astype(refs[-1].dtype)

    args, specs = [h, gain.reshape(1, GROUP), _group_mean_matrix(group)], [col(0), vec, mat]
    if gated:
        args, specs = args + [z], specs + [col(gate_col)]
    return pl.pallas_call(body, name=name, grid=(t // tb,), in_specs=specs, out_specs=col(0),
                          out_shape=_sds((t, GROUP), MM_DTYPE), compiler_params=_params("parallel"))(*args)


def _post_bwd(h, gain, z, dy, *, group, dy_col, gate_col=None, name):
    t = h.shape[0]
    tb, col, vec, mat = _post_specs(t)
    gated = gate_col is not None

    def body(*refs):
        h_ref, gn_ref, m_ref, dy_ref = refs[:4]
        if gated:
            z_ref, dh_ref, dgn_ref, dgt_ref = refs[4:]
            _, vjp = jax.vjp(lambda a, b, c: _post(a, b, m_ref[...], c), h_ref[...], gn_ref[...], z_ref[...])
            dh, dgn, dgt = vjp(dy_ref[...])
            dgt_ref[...] = dgt.astype(dgt_ref.dtype)
        else:
            dh_ref, dgn_ref = refs[4:]
            _, vjp = jax.vjp(lambda a, b: _post(a, b, m_ref[...], None), h_ref[...], gn_ref[...])
            dh, dgn = vjp(dy_ref[...])
        dh_ref[...] = dh

        @pl.when(pl.program_id(0) == 0)
        def _():
            dgn_ref[...] = jnp.zeros_like(dgn_ref)

        dgn_ref[...] += dgn

    args, specs = [h, gain.reshape(1, GROUP), _group_mean_matrix(group), dy], [col(0), vec, mat, col(dy_col)]
    out_specs, out_shape = [col(0), vec], [_sds((t, GROUP)), _sds((1, GROUP))]
    if gated:
        args, specs = args + [z], specs + [col(gate_col)]
        out_specs, out_shape = out_specs + [col(0)], out_shape + [_sds((t, GROUP), MM_DTYPE)]
    outs = pl.pallas_call(body, name=name, grid=(t // tb,), in_specs=specs, out_specs=out_specs, out_shape=out_shape,
                          compiler_params=_params("arbitrary"))(*args)
    return outs[0], outs[1].reshape(GROUP), (outs[2] if gated else None)


def _gates(zg, fb, alog, dtb):
    lane = lax.broadcasted_iota(jnp.int32, zg.shape, 1)
    logf = jax.nn.log_sigmoid(zg + fb)
    beta = jax.nn.sigmoid(zg)
    gdec = -jnp.exp(alog) * jax.nn.softplus(zg + dtb)
    return jnp.where(lane < G_BETA, logf, jnp.where(lane < G_ALPHA, beta, jnp.where(lane < G_ALPHA + 4, gdec, 0.0)))


def _lane_row(v, off):
    return jnp.pad(v.reshape(1, N_HEADS), ((0, 0), (off, LANES - N_HEADS - off)))


def _gates_fwd(z, fb, alog, dtb, *, name):
    t = z.shape[0]
    tb = _tile(t, 1024, SUBLANES)
    zspec = pl.BlockSpec((tb, LANES), lambda i: (i, Z_GATES // LANES))
    row = pl.BlockSpec((tb, LANES), lambda i: (i, 0))
    vec = pl.BlockSpec((1, LANES), lambda i: (0, 0))

    def body(z_ref, fb_ref, al_ref, dt_ref, o_ref):
        o_ref[...] = _gates(z_ref[...], fb_ref[...], al_ref[...], dt_ref[...])

    return pl.pallas_call(body, name=name, grid=(t // tb,), in_specs=[zspec, vec, vec, vec], out_specs=row,
                          out_shape=_sds((t, LANES)), compiler_params=_params("parallel"))(z, fb, alog, dtb)


def _gates_bwd(z, fb, alog, dtb, dg, *, name):
    t = z.shape[0]
    tb = _tile(t, 1024, SUBLANES)
    zspec = pl.BlockSpec((tb, LANES), lambda i: (i, Z_GATES // LANES))
    row = pl.BlockSpec((tb, LANES), lambda i: (i, 0))
    vec = pl.BlockSpec((1, LANES), lambda i: (0, 0))

    def body(z_ref, fb_ref, al_ref, dt_ref, dg_ref, dz_ref, dfb_ref, dal_ref, ddt_ref):
        _, vjp = jax.vjp(_gates, z_ref[...], fb_ref[...], al_ref[...], dt_ref[...])
        dz, dfb, dal, ddt = vjp(dg_ref[...])
        dz_ref[...] = dz.astype(dz_ref.dtype)
        accs = (dfb_ref, dal_ref, ddt_ref)

        @pl.when(pl.program_id(0) == 0)
        def _():
            for r in accs:
                r[...] = jnp.zeros_like(r)

        for r, v in zip(accs, (dfb, dal, ddt)):
            r[...] += v

    return pl.pallas_call(
        body, name=name, grid=(t // tb,), in_specs=[zspec, vec, vec, vec, row], out_specs=[row, vec, vec, vec],
        out_shape=[_sds((t, LANES), MM_DTYPE), _sds((1, LANES)), _sds((1, LANES)), _sds((1, LANES))],
        compiler_params=_params("arbitrary"))(z, fb, alog, dtb, dg)


class _Hosted:
    def __init__(self, args, out_shapes, sems, start, finish):
        self.args, self.out_shapes, self.sems, self.start, self.finish = list(args), list(out_shapes), list(sems), start, finish


def _grid_call(body, *, name, grid, in_specs, out_specs, out_shape, args, hosted=None):
    if hosted is None:
        outs = pl.pallas_call(body, name=name, grid=grid, in_specs=in_specs, out_specs=out_specs, out_shape=out_shape,
                              compiler_params=_params(*(["parallel"] * len(grid))))(*args)
        return outs, None
    n_in, n_out, h_in, h_out = len(in_specs), len(out_specs), len(hosted.args), len(hosted.out_shapes)

    def wrapped(*refs):
        core_in, host_in = refs[:n_in], refs[n_in:n_in + h_in]
        rest = refs[n_in + h_in:]
        core_out, host_out, sems = rest[:n_out], rest[n_out:n_out + h_out], rest[n_out + h_out:]
        ids = [pl.program_id(a) for a in range(len(grid))]
        first = functools.reduce(jnp.logical_and, [i == 0 for i in ids])
        last = functools.reduce(jnp.logical_and, [i == g - 1 for i, g in zip(ids, grid)])

        @pl.when(first)
        def _():
            hosted.start(host_in, host_out, sems)

        body(*core_in, *core_out)

        @pl.when(last)
        def _():
            hosted.finish(host_in, host_out, sems)

    outs = pl.pallas_call(
        wrapped, name=name, grid=grid, in_specs=list(in_specs) + [ANY] * h_in, out_specs=list(out_specs) + [ANY] * h_out,
        out_shape=list(out_shape) + hosted.out_shapes, scratch_shapes=hosted.sems,
        compiler_params=pltpu.CompilerParams(dimension_semantics=("arbitrary",) * len(grid),
                                             vmem_limit_bytes=VMEM_LIMIT_BYTES, has_side_effects=True))(
                                                 *args, *hosted.args)
    return outs[:n_out], outs[n_out:]


def _pair_weights(tq, band, transposed):
    d = np.arange(tq)[:, None] - np.arange(tq)[None, :]
    d = (d.T if transposed else d)[None] + (np.arange(band + 1) * tq)[:, None, None]
    w = sum(((d >= 0) & (d <= win) & (d % dil == 0)).astype(np.float32) for win, dil in DILATED_PAIRS)
    return jnp.asarray(w, F32)


def _att_geometry(t, mode):
    tq = _tile(t, ATT_TILE, LANES)
    nq = t // tq
    band = nq - 1 if mode == "fox" else min(DILATED_PAIRS[-1][0] // tq, nq - 1)
    return tq, nq, band


def _qkv_prep(z, *, off, name):
    t = z.shape[0]
    tb = _tile(t, 512, SUBLANES)
    cb = off // GROUP

    def body(z_ref, o_ref):
        scale = jnp.where(pl.program_id(1) == 0, ATT_SCALE, 1.0)
        o_ref[...] = (z_ref[...] * scale).astype(o_ref.dtype)

    return pl.pallas_call(
        body, name=name, grid=(t // tb, 3), in_specs=[pl.BlockSpec((tb, GROUP), lambda i, j: (i, j + cb))],
        out_specs=pl.BlockSpec((tb, GROUP), lambda i, j: (i, j)), out_shape=_sds((t, 3 * GROUP), MM_DTYPE),
        compiler_params=_params("parallel", "parallel"))(z)


def _block(ref, j, tq):
    return ref[pl.ds(pl.multiple_of(j * tq, tq), tq), :]


def _lane_block(ref, j, tq):
    return ref[0, :, pl.ds(pl.multiple_of(j * tq, tq), tq)]


def _att_tile(mode, q, kj, cq, ckj, causal, w):
    s = _dg(q, kj, 1, 1, False)
    if mode == "fox":
        s = s + (cq - ckj)
        return s if causal is None else jnp.where(causal, s, NEG)
    return jnp.where(w > 0.0, s, NEG)


def _att_fwd(qkv, cq, ck, *, mode, name, hosted=None):
    t = qkv.shape[0]
    tq, nq, band = _att_geometry(t, mode)
    weights = _pair_weights(tq, band if mode == "dil" else 0, False)

    def body(q_ref, k_ref, v_ref, cq_ref, ck_ref, w_ref, o_ref, lse_ref):
        i = pl.program_id(1)
        q, cqv = q_ref[...], cq_ref[0]
        causal = (lax.broadcasted_iota(jnp.int32, (tq, tq), 0) >= lax.broadcasted_iota(jnp.int32, (tq, tq), 1))

        def step(j, carry, masked):
            m_old, l_old, acc = carry
            w = w_ref[i - j] if mode == "dil" else None
            s = _att_tile(mode, q, _block(k_ref, j, tq), cqv, _lane_block(ck_ref, j, tq), causal if masked else None, w)
            m_new = jnp.maximum(m_old, jnp.max(s, axis=-1, keepdims=True))
            alpha = jnp.exp(m_old - m_new)
            p = jnp.exp(s - m_new)
            if w is not None:
                p = p * w
            l_new = alpha * l_old + jnp.sum(p, axis=-1, keepdims=True)
            return m_new, l_new, alpha * acc + _dg(p, _block(v_ref, j, tq), 1, 0, False)

        carry = (jnp.full((tq, 1), NEG, F32), jnp.zeros((tq, 1), F32), jnp.zeros((tq, HEAD), F32))
        if mode == "fox":
            carry = lax.fori_loop(0, i, lambda j, c: step(j, c, False), carry)
            carry = step(i, carry, True)
        else:
            carry = lax.fori_loop(jnp.maximum(i - band, 0), i + 1, lambda j, c: step(j, c, True), carry)
        m_fin, l_fin, acc = carry
        o_ref[...] = acc / l_fin
        lse_ref[0] = m_fin + jnp.log(l_fin)

    col = pl.BlockSpec((1, tq, 1), lambda h, i: (h, i, 0))
    in_specs = [
        pl.BlockSpec((tq, HEAD), lambda h, i: (i, h)),
        pl.BlockSpec((t, HEAD), lambda h, i: (0, N_HEADS + h)),
        pl.BlockSpec((t, HEAD), lambda h, i: (0, 2 * N_HEADS + h)),
        col,
        pl.BlockSpec((1, 1, t), lambda h, i: (h, 0, 0)),
        pl.BlockSpec(weights.shape, lambda h, i: (0, 0, 0)),
    ]
    return _grid_call(
        body, name=name, grid=(N_HEADS, nq), in_specs=in_specs,
        out_specs=[pl.BlockSpec((tq, HEAD), lambda h, i: (i, h)), col],
        out_shape=[_sds((t, GROUP)), _sds((N_HEADS, t, 1))], args=(qkv, qkv, qkv, cq, ck, weights), hosted=hosted)


def _att_bwd_q(qkv, cq, ck, do, o, lse, *, mode, name, hosted=None):
    t = qkv.shape[0]
    tq, nq, band = _att_geometry(t, mode)
    weights = _pair_weights(tq, band if mode == "dil" else 0, False)

    def body(q_ref, k_ref, v_ref, cq_ref, ck_ref, w_ref, do_ref, o_ref, lse_ref, dq_ref, dcq_ref, delta_ref):
        i = pl.program_id(1)
        q, cqv, lse = q_ref[...], cq_ref[0], lse_ref[0]
        dov = do_ref[...]
        delta = jnp.sum(dov * o_ref[...], axis=-1, keepdims=True)
        do16 = dov.astype(MM_DTYPE)
        causal = (lax.broadcasted_iota(jnp.int32, (tq, tq), 0) >= lax.broadcasted_iota(jnp.int32, (tq, tq), 1))

        def step(j, carry, masked):
            dq, dcq = carry
            kj = _block(k_ref, j, tq)
            w = w_ref[i - j] if mode == "dil" else None
            s = _att_tile(mode, q, kj, cqv, _lane_block(ck_ref, j, tq), causal if masked else None, w)
            p = jnp.exp(s - lse)
            if w is not None:
                p = p * w
            ds = p * (_dg(do16, _block(v_ref, j, tq), 1, 1, False) - delta)
            return dq + _dg(ds, kj, 1, 0, False), dcq + jnp.sum(ds, axis=-1, keepdims=True)

        carry = (jnp.zeros((tq, HEAD), F32), jnp.zeros((tq, 1), F32))
        if mode == "fox":
            carry = lax.fori_loop(0, i, lambda j, c: step(j, c, False), carry)
            carry = step(i, carry, True)
        else:
            carry = lax.fori_loop(jnp.maximum(i - band, 0), i + 1, lambda j, c: step(j, c, True), carry)
        dq_ref[...] = (carry[0] * ATT_SCALE).astype(dq_ref.dtype)
        dcq_ref[0] = carry[1]
        delta_ref[0] = delta

    col = pl.BlockSpec((1, tq, 1), lambda h, i: (h, i, 0))
    row = pl.BlockSpec((tq, HEAD), lambda h, i: (i, h))
    in_specs = [
        row,
        pl.BlockSpec((t, HEAD), lambda h, i: (0, N_HEADS + h)),
        pl.BlockSpec((t, HEAD), lambda h, i: (0, 2 * N_HEADS + h)),
        col,
        pl.BlockSpec((1, 1, t), lambda h, i: (h, 0, 0)),
        pl.BlockSpec(weights.shape, lambda h, i: (0, 0, 0)),
        row, row, col,
    ]
    return _grid_call(
        body, name=name, grid=(N_HEADS, nq), in_specs=in_specs, out_specs=[row, col, col],
        out_shape=[_sds((t, GROUP), MM_DTYPE), _sds((N_HEADS, t, 1)), _sds((N_HEADS, t, 1))],
        args=(qkv, qkv, qkv, cq, ck, weights, do, o, lse), hosted=hosted)


def _att_bwd_kv(qkv, cq, ck, do, lse_row, delta_row, *, mode, name, hosted=None):
    t = qkv.shape[0]
    tq, nq, band = _att_geometry(t, mode)
    weights = _pair_weights(tq, band if mode == "dil" else 0, True)

    def body(q_ref, k_ref, v_ref, cq_ref, ck_ref, w_ref, do_ref, lse_ref, delta_ref, dk_ref, dv_ref, dck_ref):
        jk = pl.program_id(1)
        kj, vj, ckv = k_ref[...], v_ref[...], cq_ref[0]
        causal = (lax.broadcasted_iota(jnp.int32, (tq, tq), 1) >= lax.broadcasted_iota(jnp.int32, (tq, tq), 0))

        def step(qi, carry, masked):
            dk, dv, dck = carry
            qb = _block(q_ref, qi, tq)
            s = _dg(kj, qb, 1, 1, False)
            w = None
            if mode == "fox":
                s = s + (_lane_block(ck_ref, qi, tq) - ckv)
                if masked:
                    s = jnp.where(causal, s, NEG)
            else:
                w = w_ref[qi - jk]
                s = jnp.where(w > 0.0, s, NEG)
            p = jnp.exp(s - _lane_block(lse_ref, qi, tq))
            if w is not None:
                p = p * w
            do16 = _block(do_ref, qi, tq).astype(MM_DTYPE)
            ds = p * (_dg(vj, do16, 1, 1, False) - _lane_block(delta_ref, qi, tq))
            return (dk + _dg(ds, qb, 1, 0, False), dv + _dg(p, do16, 1, 0, False),
                    dck - jnp.sum(ds, axis=-1, keepdims=True))

        carry = (jnp.zeros((tq, HEAD), F32), jnp.zeros((tq, HEAD), F32), jnp.zeros((tq, 1), F32))
        if mode == "fox":
            carry = step(jk, carry, True)
            carry = lax.fori_loop(jk + 1, nq, lambda qi, c: step(qi, c, False), carry)
        else:
            carry = lax.fori_loop(jk, jnp.minimum(jk + band, nq - 1) + 1, lambda qi, c: step(qi, c, True), carry)
        dk_ref[...] = carry[0].astype(dk_ref.dtype)
        dv_ref[...] = carry[1].astype(dv_ref.dtype)
        dck_ref[0] = carry[2]

    col = pl.BlockSpec((1, tq, 1), lambda h, j: (h, j, 0))
    lanes = pl.BlockSpec((1, 1, t), lambda h, j: (h, 0, 0))
    in_specs = [
        pl.BlockSpec((t, HEAD), lambda h, j: (0, h)),
        pl.BlockSpec((tq, HEAD), lambda h, j: (j, N_HEADS + h)),
        pl.BlockSpec((tq, HEAD), lambda h, j: (j, 2 * N_HEADS + h)),
        col, lanes,
        pl.BlockSpec(weights.shape, lambda h, j: (0, 0, 0)),
        pl.BlockSpec((t, HEAD), lambda h, j: (0, h)),
        lanes, lanes,
    ]
    out = pl.BlockSpec((tq, HEAD), lambda h, j: (j, h))
    return _grid_call(
        body, name=name, grid=(N_HEADS, nq), in_specs=in_specs, out_specs=[out, out, col],
        out_shape=[_sds((t, GROUP), MM_DTYPE), _sds((t, GROUP), MM_DTYPE), _sds((N_HEADS, t, 1))],
        args=(qkv, qkv, qkv, cq, ck, weights, do, lse_row, delta_row), hosted=hosted)


def _silu(x):
    return x * jax.nn.sigmoid(x)


def _l2n(x):
    return x * lax.rsqrt(jnp.sum(x * x, axis=-1, keepdims=True) + EPS)


def _gdn_pre(xqkv, gts):
    rows, c = xqkv.shape[0], GDN_CHUNK
    nb = rows // c
    ri = lax.broadcasted_iota(jnp.int32, (1, c, c), 1)
    ci = lax.broadcasted_iota(jnp.int32, (1, c, c), 2)
    tril, strict, eye = ri >= ci, ri > ci, ri == ci
    eyef = eye.astype(F32)
    last = lax.broadcasted_iota(jnp.int32, (1, c, 1), 1) == c - 1
    lane = lax.broadcasted_iota(jnp.int32, gts.shape, 1)
    to3 = lambda a: a.reshape(nb, c, a.shape[-1])
    to2 = lambda a: a.reshape(rows, a.shape[-1])
    bmm = lambda a, b: _mmf(a, b, "nn", False, True)
    bmm_nt = lambda a, b: _mmf(a, b, "nt", False, True)
    gcs = to2(_mmf(jnp.broadcast_to(tril.astype(F32), (nb, c, c)), to3(gts), "nn", True, True))
    us, ws, qgs, kds, qks, egls = [], [], [], [], [], []
    for h in range(N_HEADS):
        q = to3(_l2n(_silu(xqkv[:, h * HEAD:(h + 1) * HEAD])) * ATT_SCALE)
        k = to3(_l2n(_silu(xqkv[:, GROUP + h * HEAD:GROUP + (h + 1) * HEAD])))
        v = to3(_silu(xqkv[:, 2 * GROUP + h * HEAD:2 * GROUP + (h + 1) * HEAD]))
        beta = to3(jnp.sum(jnp.where(lane == G_BETA + h, gts, 0.0), axis=-1, keepdims=True))
        gc = to3(jnp.sum(jnp.where(lane == G_ALPHA + h, gcs, 0.0), axis=-1, keepdims=True))
        gr = jnp.sum(jnp.where(eye, jnp.broadcast_to(gc, (nb, c, c)), 0.0), axis=1, keepdims=True)
        decay = jnp.where(tril, jnp.exp(jnp.where(tril, gc - gr, 0.0)), 0.0)
        kbeta, vbeta = k * beta, v * beta
        low = jnp.where(strict, bmm_nt(kbeta, k) * decay, 0.0)
        inv, pw = eyef - low, bmm(low, low)
        for step in range(5):
            inv = inv + bmm(inv, pw)
            if step < 4:
                pw = bmm(pw, pw)
        eg = jnp.exp(gc)
        g_last = jnp.sum(jnp.where(last, gc, 0.0), axis=1, keepdims=True)
        us.append(to2(bmm(inv, vbeta)))
        ws.append(to2(bmm(inv, kbeta * eg)))
        qgs.append(to2(q * eg))
        kds.append(to2(k * jnp.exp(g_last - gc)))
        qks.append(to2(jnp.where(tril, bmm_nt(q, k) * decay, 0.0)))
        egls.append(jnp.broadcast_to(jnp.exp(g_last), (nb, 1, LANES)))
    cat = lambda parts: jnp.concatenate(parts, axis=1)
    return cat(us), cat(ws), cat(qgs), cat(kds), jnp.stack(qks, axis=0), jnp.stack(egls, axis=1)


def _gdn_seq(states, u, w, qg, kd, qk, egl):
    states = list(states)
    outs = []
    for r in range(u.shape[0] // GDN_CHUNK):
        rs = slice(r * GDN_CHUNK, (r + 1) * GDN_CHUNK)
        heads = []
        for h in range(N_HEADS):
            hs = slice(h * HEAD, (h + 1) * HEAD)
            s_in = states[h]
            v_new = u[rs, hs] - _mm(w[rs, hs], s_in)
            heads.append(_mm(qg[rs, hs], s_in) + _mm(qk[h, rs], v_new))
            states[h] = s_in * egl[r, h] + _mm_tn(kd[rs, hs], v_new)
        outs.append(jnp.concatenate(heads, axis=1))
    return jnp.concatenate(outs, axis=0), tuple(states)


def _gdn_out(o, zg, ng):
    ys = []
    for h in range(N_HEADS):
        oh = o[:, h * HEAD:(h + 1) * HEAD]
        ys.append(oh * lax.rsqrt(jnp.mean(oh * oh, axis=-1, keepdims=True) + EPS) * ng * _silu(zg[:, h * HEAD:(h + 1) * HEAD]))
    return jnp.concatenate(ys, axis=1)


GDN_PRE_ROWS = 8 * GDN_CHUNK
GDN_SEQ_ROWS = 4 * GDN_CHUNK


def _gdn_pre_specs(t):
    rows = _tile(t, GDN_PRE_ROWS, GDN_CHUNK)
    nb = rows // GDN_CHUNK
    wide = pl.BlockSpec((rows, GROUP), lambda i: (i, 0))
    ins = [pl.BlockSpec((rows, 3 * GROUP), lambda i: (i, 0)), pl.BlockSpec((rows, LANES), lambda i: (i, 0))]
    mids = [wide, wide, wide, wide, pl.BlockSpec((N_HEADS, rows, GDN_CHUNK), lambda i: (0, i, 0)),
            pl.BlockSpec((nb, N_HEADS, 1, LANES), lambda i: (i, 0, 0, 0))]
    shapes = [_sds((t, GROUP))] * 4 + [_sds((N_HEADS, t, GDN_CHUNK)), _sds((t // GDN_CHUNK, N_HEADS, 1, LANES))]
    return rows, ins, mids, shapes


def _gdn_pre_fwd(xqkv, gts, *, name):
    t = xqkv.shape[0]
    rows, ins, mids, shapes = _gdn_pre_specs(t)

    def body(x_ref, g_ref, *out_refs):
        for ref, val in zip(out_refs, _gdn_pre(x_ref[...], g_ref[...])):
            ref[...] = val

    return pl.pallas_call(body, name=name, grid=(t // rows,), in_specs=ins, out_specs=mids, out_shape=shapes,
                          compiler_params=_params("parallel"))(xqkv, gts)


def _gdn_pre_bwd(xqkv, gts, dmids, *, name):
    t = xqkv.shape[0]
    rows, ins, mids, _ = _gdn_pre_specs(t)

    def body(x_ref, g_ref, *refs):
        _, vjp = jax.vjp(_gdn_pre, x_ref[...], g_ref[...])
        dx, dg = vjp(tuple(r[...] for r in refs[:6]))
        refs[6][...] = dx
        refs[7][...] = dg

    return pl.pallas_call(body, name=name, grid=(t // rows,), in_specs=ins + mids, out_specs=ins,
                          out_shape=[_sds((t, 3 * GROUP)), _sds((t, LANES))],
                          compiler_params=_params("parallel"))(xqkv, gts, *dmids)


def _gdn_seq_specs(t, rev):
    rows = _tile(t, GDN_SEQ_ROWS, GDN_CHUNK)
    nb, n = rows // GDN_CHUNK, t // rows
    at = (lambda i: n - 1 - i) if rev else (lambda i: i)
    wide = pl.BlockSpec((rows, GROUP), lambda i: (at(i), 0))
    mids = [wide, wide, wide, wide, pl.BlockSpec((N_HEADS, rows, GDN_CHUNK), lambda i: (0, at(i), 0)),
            pl.BlockSpec((nb, N_HEADS, 1, LANES), lambda i: (at(i), 0, 0, 0))]
    state = pl.BlockSpec((1, N_HEADS, HEAD, HEAD), lambda i: (at(i), 0, 0, 0))
    return rows, n, wide, mids, state


def _gdn_seq_fwd(mids_in, *, name):
    t = mids_in[0].shape[0]
    rows, n, wide, mids, state = _gdn_seq_specs(t, False)

    def body(u_ref, w_ref, qg_ref, kd_ref, qk_ref, egl_ref, o_ref, sv_ref, s_ref):
        @pl.when(pl.program_id(0) == 0)
        def _():
            s_ref[...] = jnp.zeros_like(s_ref)

        sv_ref[0] = s_ref[...]
        o, new = _gdn_seq(tuple(s_ref[h] for h in range(N_HEADS)), u_ref[...], w_ref[...], qg_ref[...], kd_ref[...],
                          qk_ref[...], egl_ref[...])
        o_ref[...] = o
        for h in range(N_HEADS):
            s_ref[h] = new[h]

    return pl.pallas_call(
        body, name=name, grid=(n,), in_specs=mids, out_specs=[wide, state],
        out_shape=[_sds((t, GROUP)), _sds((n, N_HEADS, HEAD, HEAD))],
        scratch_shapes=[pltpu.VMEM((N_HEADS, HEAD, HEAD), F32)], compiler_params=_params("arbitrary"))(*mids_in)


def _gdn_seq_bwd(mids_in, states, do, *, name):
    t = mids_in[0].shape[0]
    rows, n, wide, mids, state = _gdn_seq_specs(t, True)

    def body(u_ref, w_ref, qg_ref, kd_ref, qk_ref, egl_ref, sv_ref, do_ref, *refs):
        d_refs, ds_ref = refs[:6], refs[6]

        @pl.when(pl.program_id(0) == 0)
        def _():
            ds_ref[...] = jnp.zeros_like(ds_ref)

        s_in = tuple(sv_ref[0, h] for h in range(N_HEADS))
        _, vjp = jax.vjp(_gdn_seq, s_in, u_ref[...], w_ref[...], qg_ref[...], kd_ref[...], qk_ref[...], egl_ref[...])
        grads = vjp((do_ref[...], tuple(ds_ref[h] for h in range(N_HEADS))))
        for ref, val in zip(d_refs, grads[1:]):
            ref[...] = val
        for h in range(N_HEADS):
            ds_ref[h] = grads[0][h]

    shapes = [_sds(m.shape) for m in mids_in]
    return pl.pallas_call(
        body, name=name, grid=(n,), in_specs=mids + [state, wide], out_specs=mids, out_shape=shapes,
        scratch_shapes=[pltpu.VMEM((N_HEADS, HEAD, HEAD), F32)],
        compiler_params=_params("arbitrary"))(*mids_in, states, do)


def _gdn_out_specs(t):
    rows = _tile(t, 512, SUBLANES)
    wide = pl.BlockSpec((rows, GROUP), lambda i: (i, 0))
    return rows, wide, pl.BlockSpec((rows, GROUP), lambda i: (i, Z_CZ // GROUP)), pl.BlockSpec((1, HEAD), lambda i: (0, 0))


def _gdn_out_fwd(o, z, ng, *, name):
    t = o.shape[0]
    rows, wide, zspec, vec = _gdn_out_specs(t)

    def body(o_ref, z_ref, ng_ref, y_ref):
        y_ref[...] = _gdn_out(o_ref[...], z_ref[...], ng_ref[...]).astype(y_ref.dtype)

    return pl.pallas_call(body, name=name, grid=(t // rows,), in_specs=[wide, zspec, vec], out_specs=wide,
                          out_shape=_sds((t, GROUP), MM_DTYPE), compiler_params=_params("parallel"))(o, z, ng)


def _gdn_out_bwd(o, z, ng, dy, *, dy_col, name):
    t = o.shape[0]
    rows, wide, zspec, vec = _gdn_out_specs(t)
    dyspec = pl.BlockSpec((rows, GROUP), lambda i: (i, dy_col))

    def body(o_ref, z_ref, ng_ref, dy_ref, do_ref, dz_ref, dng_ref):
        _, vjp = jax.vjp(_gdn_out, o_ref[...], z_ref[...], ng_ref[...])
        do, dz, dng = vjp(dy_ref[...])
        do_ref[...] = do
        dz_ref[...] = dz.astype(dz_ref.dtype)

        @pl.when(pl.program_id(0) == 0)
        def _():
            dng_ref[...] = jnp.zeros_like(dng_ref)

        dng_ref[...] += dng

    do, dz, dng = pl.pallas_call(
        body, name=name, grid=(t // rows,), in_specs=[wide, zspec, vec, dyspec], out_specs=[wide, wide, vec],
        out_shape=[_sds((t, GROUP)), _sds((t, GROUP), MM_DTYPE), _sds((1, HEAD))],
        compiler_params=_params("arbitrary"))(o, z, ng, dy)
    return do, dz, dng.reshape(HEAD)


def _swiglu(up, gate):
    return _silu(gate) * up


def _conv_value(x_ref, p_ref, w_ref, b_ref, first):
    xv = x_ref[...]
    prev = jnp.where(first, 0.0, p_ref[...])
    acc = jnp.zeros_like(xv) + b_ref[...]
    kw = w_ref.shape[0]
    for k in range(kw):
        acc = acc + w_ref[k:k + 1, :] * _delayed(xv, prev, kw - 1 - k)
    return acc


def _ffn_act_specs(t, two_f, kw, col_of):
    dff = two_f // 2
    tb, tc = _tile(t, 512, SUBLANES), _tile(dff, 512, LANES)
    nh, n8 = dff // tc, tb // SUBLANES
    specs = []
    for half in (0, 1):
        col = lambda j, half=half: col_of(j, nh) + half * nh
        specs += [pl.BlockSpec((tb, tc), lambda j, i, col=col: (i, col(j))),
                  pl.BlockSpec((SUBLANES, tc), lambda j, i, col=col: (jnp.maximum(i * n8 - 1, 0), col(j))),
                  pl.BlockSpec((kw, tc), lambda j, i, col=col: (0, col(j))),
                  pl.BlockSpec((1, tc), lambda j, i, col=col: (0, col(j)))]
    return tb, tc, nh, specs


def _ffn_act_fwd(uu, w, b, *, name):
    t, two_f = uu.shape
    tb, tc, nh, specs = _ffn_act_specs(t, two_f, w.shape[0], lambda j, nh: j)

    def body(xu, pu, wu, bu, xg, pg, wg, bg, o_ref):
        first = pl.program_id(1) == 0
        o_ref[...] = _swiglu(_conv_value(xu, pu, wu, bu, first), _conv_value(xg, pg, wg, bg, first)).astype(o_ref.dtype)

    b2 = b.reshape(1, two_f)
    return pl.pallas_call(body, name=name, grid=(nh, t // tb), in_specs=specs,
                          out_specs=pl.BlockSpec((tb, tc), lambda j, i: (i, j)), out_shape=_sds((t, two_f // 2), MM_DTYPE),
                          compiler_params=_params("parallel", "parallel"))(uu, uu, w, b2, uu, uu, w, b2)


def _ffn_act_bwd(uu, w, b, dact, *, name):
    t, two_f = uu.shape
    tb, tc, nh, specs = _ffn_act_specs(t, two_f, w.shape[0], lambda j, nh: j)

    def body(xu, pu, wu, bu, xg, pg, wg, bg, da_ref, o_ref):
        first = pl.program_id(1) == 0
        _, vjp = jax.vjp(_swiglu, _conv_value(xu, pu, wu, bu, first), _conv_value(xg, pg, wg, bg, first))
        o_ref[0], o_ref[1] = vjp(da_ref[...])

    b2 = b.reshape(1, two_f)
    return pl.pallas_call(body, name=name, grid=(nh, t // tb),
                          in_specs=specs + [pl.BlockSpec((tb, tc), lambda j, i: (i, j))],
                          out_specs=pl.BlockSpec((2, tb, tc), lambda j, i: (0, i, j)), out_shape=_sds((2, t, two_f // 2)),
                          compiler_params=_params("parallel", "parallel"))(uu, uu, w, b2, uu, uu, w, b2, dact)


def _sum_slots(parts, *, out_dtype=F32, name):
    if not isinstance(parts, (list, tuple)):
        parts = [parts[s] for s in range(parts.shape[0])]
    r = parts[0].shape[0]
    rb = _tile(r, 2048, 2 * SUBLANES)
    spec = pl.BlockSpec((rb, LANES), lambda i: (i, 0))

    def body(*refs):
        acc = refs[0][...].astype(F32)
        for ref in refs[1:-1]:
            acc = acc + ref[...].astype(F32)
        refs[-1][...] = acc.astype(out_dtype)

    return pl.pallas_call(body, name=name, grid=(r // rb,), in_specs=[spec] * len(parts), out_specs=spec,
                          out_shape=_sds((r, LANES), out_dtype), compiler_params=_params("parallel"))(*parts)


def _adamw(w, g, m, v, *, name):
    r, c = w.shape
    rb = _tile(r, 128, SUBLANES)
    spec = pl.BlockSpec((rb, c), lambda i: (i, 0))

    def body(w_ref, g_ref, m_ref, v_ref, d_ref, nm_ref, nv_ref):
        gv = g_ref[...]
        mn = ADAM_B1 * m_ref[...] + (1.0 - ADAM_B1) * gv
        vn = ADAM_B2 * v_ref[...] + (1.0 - ADAM_B2) * (gv * gv)
        m_hat = mn / (1.0 - ADAM_B1 ** ADAM_STEP)
        v_hat = vn / (1.0 - ADAM_B2 ** ADAM_STEP)
        d_ref[...] = -ADAM_LR * (m_hat / (jnp.sqrt(v_hat) + ADAM_EPS) + ADAM_WD * w_ref[...])
        nm_ref[...] = mn
        nv_ref[...] = vn

    return pl.pallas_call(body, name=name, grid=(r // rb,), in_specs=[spec] * 4, out_specs=[spec] * 3,
                          out_shape=[_sds((r, c))] * 3, compiler_params=_params("parallel"))(w, g, m, v)


def _position():
    return lax.axis_index("x"), lax.axis_index("y"), lax.axis_index("c")


def _chip_gather(shards, *, name):
    n = len(shards)

    def body(*refs):
        ins, outs = refs[:n], refs[n:2 * n]
        send_sems, recv_sems, local_sems = refs[2 * n:]
        x, y, c = _position()
        mine = 2 * x + y
        chips = [(1 - x, y), (x, 1 - y), (1 - x, 1 - y)]
        local = [pltpu.make_async_copy(ins[a], outs[a].at[mine], local_sems.at[a]) for a in range(n)]
        for cp in local:
            cp.start()
        sends = []
        for a in range(n):
            for r, (px, py) in enumerate(chips):
                sends.append(pltpu.make_async_remote_copy(
                    src_ref=ins[a], dst_ref=outs[a].at[mine], send_sem=send_sems.at[3 * a + r],
                    recv_sem=recv_sems.at[3 * a + r], device_id=(px, py, c), device_id_type=MESH_ID))
        for cp in sends:
            cp.start()
        for a in range(n):
            for r, (px, py) in enumerate(chips):
                pltpu.make_async_remote_copy(
                    src_ref=ins[a], dst_ref=outs[a].at[2 * px + py], send_sem=send_sems.at[3 * a + r],
                    recv_sem=recv_sems.at[3 * a + r], device_id=(px, py, c), device_id_type=MESH_ID).wait_recv()
        for cp in sends:
            cp.wait_send()
        for cp in local:
            cp.wait()

    return pl.pallas_call(
        body, name=name, in_specs=[ANY] * n, out_specs=[ANY] * n,
        out_shape=[_sds((4,) + s.shape, s.dtype) for s in shards],
        scratch_shapes=[pltpu.SemaphoreType.DMA((3 * n,)), pltpu.SemaphoreType.DMA((3 * n,)),
                        pltpu.SemaphoreType.DMA((n,))],
        compiler_params=pltpu.CompilerParams(has_side_effects=True))(*shards)


def _gather_halves_step(shards):
    n = len(shards)

    def copy(ins, outs, send_sems, recv_sems, a, r, chip_of_block, half, to, second):
        k = (3 * n if second else 0) + 3 * a + r
        px, py = chip_of_block
        src = outs[a].at[2 * px + py, half] if second else ins[a].at[half]
        return pltpu.make_async_remote_copy(
            src_ref=src, dst_ref=outs[a].at[2 * px + py, half], send_sem=send_sems.at[k], recv_sem=recv_sems.at[k],
            device_id=to, device_id_type=MESH_ID)

    def first_copies(ins, outs, sems):
        x, y, c = _position()
        chips = [(1 - x, y), (x, 1 - y), (1 - x, 1 - y)]
        return [copy(ins, outs, *sems, a, r, (x, y), c, (px, py, c), False) for a in range(n)
                for r, (px, py) in enumerate(chips)]

    def start(ins, outs, sems):
        for cp in first_copies(ins, outs, sems):
            cp.start()

    def finish(ins, outs, sems):
        x, y, c = _position()
        chips = [(1 - x, y), (x, 1 - y), (1 - x, 1 - y)]
        passed = []
        for a in range(n):
            for r, chip in enumerate(chips):
                copy(ins, outs, *sems, a, r, chip, c, (x, y, c), False).wait_recv()
                passed.append(copy(ins, outs, *sems, a, r, chip, c, (x, y, 1 - c), True))
                passed[-1].start()
        for a in range(n):
            for r, chip in enumerate(chips):
                copy(ins, outs, *sems, a, r, chip, 1 - c, (x, y, c), True).wait_recv()
        for cp in first_copies(ins, outs, sems) + passed:
            cp.wait_send()

    return _Hosted(shards, [_sds((4,) + s.shape, s.dtype) for s in shards],
                   [pltpu.SemaphoreType.DMA((6 * n,)), pltpu.SemaphoreType.DMA((6 * n,))], start, finish)


def _exchange_step(v):
    def copies(ins, outs, sems):
        x, y, c = _position()
        mine = 2 * x + y
        return [pltpu.make_async_remote_copy(
            src_ref=ins[0].at[2 * px + py], dst_ref=outs[0].at[mine], send_sem=sems[0].at[r], recv_sem=sems[1].at[r],
            device_id=(px, py, c), device_id_type=MESH_ID) for r, (px, py) in enumerate([(1 - x, y), (x, 1 - y), (1 - x, 1 - y)])]

    def start(ins, outs, sems):
        for cp in copies(ins, outs, sems):
            cp.start()

    def finish(ins, outs, sems):
        x, y, c = _position()
        mine = 2 * x + y
        for r, (px, py) in enumerate([(1 - x, y), (x, 1 - y), (1 - x, 1 - y)]):
            pltpu.make_async_remote_copy(
                src_ref=ins[0].at[mine], dst_ref=outs[0].at[2 * px + py], send_sem=sems[0].at[r], recv_sem=sems[1].at[r],
                device_id=(px, py, c), device_id_type=MESH_ID).wait_recv()
        for cp in copies(ins, outs, sems):
            cp.wait_send()

    return _Hosted([v], [_sds(v.shape, v.dtype)], [pltpu.SemaphoreType.DMA((3,)), pltpu.SemaphoreType.DMA((3,))],
                   start, finish)


def _run_step(step, *, name):
    n = len(step.args)

    def body(*refs):
        ins, outs, sems = refs[:n], refs[n:n + len(step.out_shapes)], refs[n + len(step.out_shapes):]
        step.start(ins, outs, sems)
        step.finish(ins, outs, sems)

    return pl.pallas_call(body, name=name, in_specs=[ANY] * n, out_specs=[ANY] * len(step.out_shapes),
                          out_shape=step.out_shapes, scratch_shapes=step.sems,
                          compiler_params=pltpu.CompilerParams(has_side_effects=True))(*step.args)


def _place_own(results, own):
    x, y, _ = _position()
    return [lax.dynamic_update_index_in_dim(r, o, 2 * x + y, 0) for r, o in zip(results, own)]


def _sibling_step(v, other_half=False):
    half = v.shape[1] // 2
    shape = (v.shape[0], half) + v.shape[2:] if other_half else v.shape

    def copy(ins, outs, sems):
        x, y, c = _position()
        src = ins[0].at[pl.ds(0, v.shape[0]), pl.ds(pl.multiple_of((1 - c) * half, SUBLANES), half)] if other_half else ins[0]
        return pltpu.make_async_remote_copy(src_ref=src, dst_ref=outs[0], send_sem=sems[0], recv_sem=sems[1],
                                            device_id=(x, y, 1 - c), device_id_type=MESH_ID)

    return _Hosted([v], [_sds(shape, v.dtype)], [pltpu.SemaphoreType.DMA, pltpu.SemaphoreType.DMA],
                   lambda ins, outs, sems: copy(ins, outs, sems).start(), lambda ins, outs, sems: copy(ins, outs, sems).wait())


def _sibling_send(v, *, name):
    return _run_step(_sibling_step(v), name=name)[0]


def _all_sum(v, *, name):
    r = v.shape[0]
    masks = [(mx, my, mc) for mx in (0, 1) for my in (0, 1) for mc in (0, 1)][1:]

    def body(v_ref, out_ref, slots, send_sems, recv_sems, local_sem):
        x, y, c = _position()
        me = 4 * x + 2 * y + c

        def peer(mask):
            return tuple(1 - p if bit else p for p, bit in zip((x, y, c), mask))

        local = pltpu.make_async_copy(v_ref, slots.at[me], local_sem)
        local.start()
        sends = [pltpu.make_async_remote_copy(
            src_ref=v_ref, dst_ref=slots.at[me], send_sem=send_sems.at[k], recv_sem=recv_sems.at[k],
            device_id=peer(mask), device_id_type=MESH_ID) for k, mask in enumerate(masks)]
        for cp in sends:
            cp.start()
        for k, mask in enumerate(masks):
            px, py, pc = peer(mask)
            pltpu.make_async_remote_copy(
                src_ref=v_ref, dst_ref=slots.at[4 * px + 2 * py + pc], send_sem=send_sems.at[k],
                recv_sem=recv_sems.at[k], device_id=(px, py, pc), device_id_type=MESH_ID).wait_recv()
        for cp in sends:
            cp.wait_send()
        local.wait()
        acc = slots[0]
        for s in range(1, 8):
            acc = acc + slots[s]
        out_ref[...] = acc

    vm = pl.BlockSpec(memory_space=pltpu.VMEM)
    return pl.pallas_call(
        body, name=name, in_specs=[vm], out_specs=vm, out_shape=_sds((r, LANES)),
        scratch_shapes=[pltpu.VMEM((8, r, LANES), F32), pltpu.SemaphoreType.DMA((7,)), pltpu.SemaphoreType.DMA((7,)),
                        pltpu.SemaphoreType.DMA],
        compiler_params=pltpu.CompilerParams(vmem_limit_bytes=VMEM_LIMIT_BYTES, has_side_effects=True))(v)


def _pack_rows(arrays, align=SUBLANES * LANES):
    flat = jnp.concatenate([a.reshape(-1) for a in arrays])
    n = flat.shape[0]
    pad = (-n) % align
    if pad:
        flat = jnp.concatenate([flat, jnp.zeros((pad,), flat.dtype)])
    return flat.reshape(-1, LANES), [a.shape for a in arrays]


def _unpack_rows(rows, shapes):
    flat = rows.reshape(-1)
    out, off = [], 0
    for s in shapes:
        n = int(np.prod(s))
        out.append(flat[off:off + n].reshape(s))
        off += n
    return out


def _pad_w_in(w):
    d = w.shape[0]
    return jnp.concatenate([w[:, 0:2560], w[:, 2564:4612], w[:, 4620:6156], w[:, 2560:2564], w[:, 4612:4620],
                            jnp.zeros((d, Z_COLS - IN_COLS), w.dtype)], axis=1)


def _unpad_w_in(g):
    return jnp.concatenate([g[:, 0:2560], g[:, 6144:6148], g[:, 2560:4608], g[:, 6148:6156], g[:, 4608:6144]], axis=1)


def _gate_rows(p):
    return (_lane_row(p["fox_f_bias"], G_F), _lane_row(p["gdn_a_log"], G_ALPHA), _lane_row(p["gdn_dt_bias"], G_ALPHA))


def _head_cols(c_rows, t):
    ct = c_rows[:, :N_HEADS].T
    return ct.reshape(N_HEADS, t, 1), ct.reshape(N_HEADS, 1, t)


def _layer_fwd(x, p, hosted=None):
    t = x.shape[0]
    s = {"x": x}
    s["h"] = _rmsnorm_fwd(x, p["norm_mix"], name="mix_norm")
    z = s["z"] = _matmul(s["h"], p["w_in"], name="in_proj")
    s["xc"] = _conv_fwd(z, p["lru_conv_w"], p["lru_conv_b"], ncols=GROUP, coff=Z_AX, name="lru_conv")
    lru = s["lru"] = (_block_diag(p["lru_wa"]).astype(MM_DTYPE), p["lru_ba"].reshape(1, GROUP),
                      _block_diag(p["lru_wx"]).astype(MM_DTYPE), p["lru_bx"].reshape(1, GROUP),
                      p["lru_lambda"].reshape(1, GROUP))
    s["a"], u = _lru_gates_fwd(s["xc"], *lru, name="lru_gates")
    s["ha"] = _scan(s["a"], u, name="lru_scan")
    y_a = _post_fwd(s["ha"], p["norm_a"], z, group=LRU_BD, gate_col=Z_AG // GROUP, name="lru_post")
    s["gts"] = _gates_fwd(z, *_gate_rows(p), name="gates")
    s["cq"], s["ck"] = _head_cols(_scan(None, s["gts"], name="fox_cumsum"), t)
    s["qkv_b"] = _qkv_prep(z, off=Z_BQ, name="fox_prep")
    (s["ob"], s["lse_b"]), hosted_out = _att_fwd(s["qkv_b"], s["cq"], s["ck"], mode="fox", name="fox_att", hosted=hosted)
    y_b = _post_fwd(s["ob"], p["norm_b"], z, group=HEAD, name="fox_post")
    s["cconv"] = _conv_fwd(z, p["gdn_conv_w"], jnp.zeros((3 * GROUP,), F32), ncols=3 * GROUP, coff=Z_CQKV,
                           name="gdn_conv")
    s["gdn_mids"] = _gdn_pre_fwd(s["cconv"], s["gts"], name="gdn_pre")
    s["oc"], s["gdn_states"] = _gdn_seq_fwd(s["gdn_mids"], name="gdn_seq")
    y_c = _gdn_out_fwd(s["oc"], z, p["gdn_norm"].reshape(1, HEAD), name="gdn_out")
    s["qkv_d"] = _qkv_prep(z, off=Z_DQ, name="dil_prep")
    (s["od"], s["lse_d"]), _ = _att_fwd(s["qkv_d"], s["cq"], s["ck"], mode="dil", name="dil_att")
    y_d = _post_fwd(s["od"], p["norm_d"], z, group=HEAD, name="dil_post")
    y = s["y"] = jnp.concatenate([y_a, y_b, y_c, y_d], axis=1)
    x1 = s["x1"] = _matmul(y, p["w_out"], add=x, name="out_proj")
    s["h2"] = _rmsnorm_fwd(x1, p["norm_ffn"], name="ffn_norm")
    s["uu"] = _matmul(s["h2"], p["ffn_w_up"], name="ffn_up")
    s["act"] = _ffn_act_fwd(s["uu"], p["ffn_conv_w"], p["ffn_conv_b"], name="ffn_conv_swiglu")
    return _matmul(s["act"], p["ffn_w_down"], add=x1, name="ffn_down"), s, hosted_out


def _layer_bwd(dx2, p, s, comm=None):
    t = dx2.shape[0]
    g = {}
    dact = _matmul(dx2, p["ffn_w_down"], form="nt", name="ffn_down_dx")
    g["ffn_w_down"] = _matmul(s["act"], dx2, form="tn", name="ffn_down_dw")
    du = _ffn_act_bwd(s["uu"], p["ffn_conv_w"], p["ffn_conv_b"], dact, name="ffn_conv_swiglu_bwd")
    duu, g["ffn_conv_w"], g["ffn_conv_b"] = _conv_bwd(s["uu"], p["ffn_conv_w"], du, ncols=s["uu"].shape[1],
                                                      name="ffn_conv_bwd")
    dh2 = _matmul(duu, p["ffn_w_up"], form="nt", name="ffn_up_dx")
    g["ffn_w_up"] = _matmul(s["h2"], duu, form="tn", name="ffn_up_dw")
    dx1, g["norm_ffn"] = _rmsnorm_bwd(s["x1"], p["norm_ffn"], dh2, dx2, name="ffn_norm_bwd")
    dy = _matmul(dx1, p["w_out"], form="nt", name="out_proj_dx")
    g["w_out"] = _matmul(s["y"], dx1, form="tn", name="out_proj_dw")
    z = s["z"]
    dha, g["norm_a"], dgate_a = _post_bwd(s["ha"], p["norm_a"], z, dy, group=LRU_BD, dy_col=0, gate_col=Z_AG // GROUP,
                                          name="lru_post_bwd")
    gsc = _scan(_shift_up(s["a"]), dha, reverse=True, name="lru_scan_bwd")
    dxc, dwa, dba, dwx, dbx, dlam = _lru_gates_bwd(s["xc"], *s["lru"], gsc, _shift_down(s["ha"]), name="lru_gates_bwd")
    g["lru_wa"], g["lru_wx"] = _diag_blocks(dwa), _diag_blocks(dwx)
    g["lru_ba"], g["lru_bx"], g["lru_lambda"] = dba.reshape(GROUP), dbx.reshape(GROUP), dlam.reshape(GROUP)
    dax, g["lru_conv_w"], g["lru_conv_b"] = _conv_bwd(z, p["lru_conv_w"], dxc, ncols=GROUP, coff=Z_AX,
                                                      name="lru_conv_bwd")
    dob, g["norm_b"], _ = _post_bwd(s["ob"], p["norm_b"], z, dy, group=HEAD, dy_col=1, name="fox_post_bwd")
    (dbq, dcq, delta), sent = _att_bwd_q(s["qkv_b"], s["cq"], s["ck"], dob, s["ob"], s["lse_b"], mode="fox",
                                         name="fox_att_dq", hosted=None if comm is None else comm[0])
    operand = None if comm is None else comm[1](sent[0])
    (dbk, dbv, dck), got = _att_bwd_kv(s["qkv_b"], s["cq"], s["ck"], dob, s["lse_b"].reshape(N_HEADS, 1, t),
                                       delta.reshape(N_HEADS, 1, t), mode="fox", name="fox_att_dkv",
                                       hosted=None if comm is None else _exchange_step(operand))
    hosted_out = None if comm is None else (operand, got[0])
    pad_lanes = ((0, 0), (0, LANES - N_HEADS))
    dc_rows = _sum_slots([jnp.pad(dcq.reshape(N_HEADS, t).T, pad_lanes), jnp.pad(dck.reshape(N_HEADS, t).T, pad_lanes)],
                         name="fox_dc_sum")
    dgts_fox = _scan(None, dc_rows, reverse=True, name="fox_cumsum_bwd")
    dod, g["norm_d"], _ = _post_bwd(s["od"], p["norm_d"], z, dy, group=HEAD, dy_col=3, name="dil_post_bwd")
    (ddq, _, delta), _ = _att_bwd_q(s["qkv_d"], s["cq"], s["ck"], dod, s["od"], s["lse_d"], mode="dil", name="dil_att_dq")
    (ddk, ddv, _), _ = _att_bwd_kv(s["qkv_d"], s["cq"], s["ck"], dod, s["lse_d"].reshape(N_HEADS, 1, t),
                                   delta.reshape(N_HEADS, 1, t), mode="dil", name="dil_att_dkv")
    doc, dcz, g["gdn_norm"] = _gdn_out_bwd(s["oc"], z, p["gdn_norm"].reshape(1, HEAD), dy, dy_col=2, name="gdn_out_bwd")
    dmids = _gdn_seq_bwd(s["gdn_mids"], s["gdn_states"], doc, name="gdn_seq_bwd")
    dcconv, dgts_gdn = _gdn_pre_bwd(s["cconv"], s["gts"], dmids, name="gdn_pre_bwd")
    dcqkv, g["gdn_conv_w"], _ = _conv_bwd(z, p["gdn_conv_w"], dcconv, ncols=3 * GROUP, coff=Z_CQKV,
                                          name="gdn_conv_bwd")
    dgts = _sum_slots([dgts_fox, dgts_gdn], name="gates_dsum")
    dzg, dfb, dal, ddt = _gates_bwd(z, *_gate_rows(p), dgts, name="gates_bwd")
    g["fox_f_bias"] = dfb[0, G_F:G_F + N_HEADS]
    g["gdn_a_log"] = dal[0, G_ALPHA:G_ALPHA + N_HEADS]
    g["gdn_dt_bias"] = ddt[0, G_ALPHA:G_ALPHA + N_HEADS]
    dz = jnp.concatenate([dax, dgate_a, dbq, dbk, dbv, dcqkv, dcz, ddq, ddk, ddv, dzg], axis=1)
    dh = _matmul(dz, p["w_in"], form="nt", name="in_proj_dx")
    g["w_in"] = _matmul(s["h"], dz, form="tn", name="in_proj_dw")
    dx, g["norm_mix"] = _rmsnorm_bwd(s["x"], p["norm_mix"], dh, dx1, name="mix_norm_bwd")
    return dx, g, hosted_out


def _local_step(x, tgt, layers, norm_final):
    saved = []
    for p in layers:
        x, s, _ = _layer_fwd(x, p)
        saved.append(s)
    loss, dx, dnf = _loss_head(x, norm_final, tgt, name="loss_head")
    grads = []
    for p, s in zip(reversed(layers), reversed(saved)):
        dx, g, _ = _layer_bwd(dx, p, s)
        grads.append(g)
    return loss, dx, grads[::-1], dnf


BIG = ("w_in", "w_out", "ffn_w_up", "ffn_w_down")
PACK_ROWS = 4096
SHARDED_SMALL = ("lru_conv_w", "gdn_conv_w", "ffn_conv_w")
NAMES = ("norm_mix", "w_in", "lru_conv_w", "lru_conv_b", "lru_wa", "lru_ba", "lru_wx", "lru_bx", "lru_lambda",
         "fox_f_bias", "gdn_conv_w", "gdn_a_log", "gdn_dt_bias", "gdn_norm", "norm_a", "norm_b", "norm_d", "w_out",
         "norm_ffn", "ffn_w_up", "ffn_conv_w", "ffn_conv_b", "ffn_w_down", "norm_final")
SMALL = tuple(n for n in NAMES if n not in BIG)


def _big_pieces(g, k_axis_cols):
    if k_axis_cols:
        d, n = g.shape
        return g.reshape(d, 4, n // 4).transpose(1, 0, 2).reshape(4, -1, LANES)
    return g.reshape(4, -1, LANES)


def _reduce_pack(gl):
    whole = {"w_in": _unpad_w_in(gl["w_in"]), "w_out": gl["w_out"], "ffn_w_up": gl["ffn_w_up"],
             "ffn_w_down": gl["ffn_w_down"]}
    cols = {"w_in": True, "w_out": False, "ffn_w_up": True, "ffn_w_down": False}
    pieces = [_big_pieces(whole[n], cols[n]) for n in BIG]
    rows = [q.shape[1] for q in pieces]
    pad = (-sum(rows)) % PACK_ROWS
    packed = jnp.concatenate(pieces + [jnp.zeros((4, pad, LANES), F32)], axis=1)
    return packed, rows


def _chip_sum(packed, c, got):
    half = packed.shape[1] // 2
    rb = _tile(half, 2048, 2 * SUBLANES)
    nb = half // rb

    def body(c_ref, mine_ref, got_ref, o_ref):
        o_ref[...] = (mine_ref[...] + got_ref[...]).astype(o_ref.dtype)

    slot = pl.BlockSpec((1, rb, LANES), lambda k, i, c_ref: (k, i, 0))
    grid_spec = pltpu.PrefetchScalarGridSpec(
        num_scalar_prefetch=1, grid=(4, nb),
        in_specs=[pl.BlockSpec((1, rb, LANES), lambda k, i, c_ref: (k, c_ref[0] * nb + i, 0)), slot], out_specs=slot)
    return pl.pallas_call(body, name="grad_sibling_sum", grid_spec=grid_spec, out_shape=_sds((4, half, LANES), MM_DTYPE),
                          compiler_params=_params("parallel", "parallel"))(c.reshape(1).astype(jnp.int32), packed, got)


def _reduce_finish(chip_sum, from_chips, rows, *, c):
    x, y, _ = _position()
    (from_chips,) = _place_own([from_chips], [lax.dynamic_index_in_dim(chip_sum, 2 * x + y, 0, keepdims=False)])
    total_half = _sum_slots(from_chips, name="grad_chip_sum")
    other_half = _sibling_send(total_half, name="grad_sibling_swap")
    total = jnp.where(c == 0, jnp.concatenate([total_half, other_half]), jnp.concatenate([other_half, total_half]))
    out, off = {}, 0
    for n, r in zip(BIG, rows):
        out[n] = total[off:off + r]
        off += r
    return out


def kernel(x, norm_mix, w_in, lru_conv_w, lru_conv_b, lru_wa, lru_ba, lru_wx, lru_bx, lru_lambda, fox_f_bias, gdn_conv_w, gdn_a_log, gdn_dt_bias, gdn_norm, norm_a, norm_b, norm_d, w_out, norm_ffn, ffn_w_up, ffn_conv_w, ffn_conv_b, ffn_w_down, norm_final, loss_target, m_norm_mix, m_w_in, m_lru_conv_w, m_lru_conv_b, m_lru_wa, m_lru_ba, m_lru_wx, m_lru_bx, m_lru_lambda, m_fox_f_bias, m_gdn_conv_w, m_gdn_a_log, m_gdn_dt_bias, m_gdn_norm, m_norm_a, m_norm_b, m_norm_d, m_w_out, m_norm_ffn, m_ffn_w_up, m_ffn_conv_w, m_ffn_conv_b, m_ffn_w_down, m_norm_final, v_norm_mix, v_w_in, v_lru_conv_w, v_lru_conv_b, v_lru_wa, v_lru_ba, v_lru_wx, v_lru_bx, v_lru_lambda, v_fox_f_bias, v_gdn_conv_w, v_gdn_a_log, v_gdn_dt_bias, v_gdn_norm, v_norm_a, v_norm_b, v_norm_d, v_w_out, v_norm_ffn, v_ffn_w_up, v_ffn_conv_w, v_ffn_conv_b, v_ffn_w_down, v_norm_final):
    w = dict(zip(NAMES, (norm_mix, w_in, lru_conv_w, lru_conv_b, lru_wa, lru_ba, lru_wx, lru_bx, lru_lambda, fox_f_bias,
                         gdn_conv_w, gdn_a_log, gdn_dt_bias, gdn_norm, norm_a, norm_b, norm_d, w_out, norm_ffn, ffn_w_up,
                         ffn_conv_w, ffn_conv_b, ffn_w_down, norm_final)))
    m = dict(zip(NAMES, (m_norm_mix, m_w_in, m_lru_conv_w, m_lru_conv_b, m_lru_wa, m_lru_ba, m_lru_wx, m_lru_bx,
                         m_lru_lambda, m_fox_f_bias, m_gdn_conv_w, m_gdn_a_log, m_gdn_dt_bias, m_gdn_norm, m_norm_a,
                         m_norm_b, m_norm_d, m_w_out, m_norm_ffn, m_ffn_w_up, m_ffn_conv_w, m_ffn_conv_b, m_ffn_w_down,
                         m_norm_final)))
    v = dict(zip(NAMES, (v_norm_mix, v_w_in, v_lru_conv_w, v_lru_conv_b, v_lru_wa, v_lru_ba, v_lru_wx, v_lru_bx,
                         v_lru_lambda, v_fox_f_bias, v_gdn_conv_w, v_gdn_a_log, v_gdn_dt_bias, v_gdn_norm, v_norm_a,
                         v_norm_b, v_norm_d, v_w_out, v_norm_ffn, v_ffn_w_up, v_ffn_conv_w, v_ffn_conv_b, v_ffn_w_down,
                         v_norm_final)))
    depth = w_in.shape[0]
    xi, yi, ci = _position()
    chip = 2 * xi + yi

    conv_rows, conv_shapes = _pack_rows([w[n] for n in SHARDED_SMALL])
    (conv_all,) = _chip_gather([conv_rows], name="conv_taps_gather")
    conv_full = {}
    per_chip = [_unpack_rows(conv_all[k], conv_shapes) for k in range(4)]
    for i, n in enumerate(SHARDED_SMALL):
        conv_full[n] = jnp.concatenate([per_chip[k][i] for k in range(4)], axis=-1)
    halves = lambda l: [w[n][l].astype(MM_DTYPE).reshape(2, w[n].shape[1] // 2, w[n].shape[2]) for n in BIG]

    def layer_params(l, gathered):
        g_in, g_out, g_up, g_dn = (g.reshape((4,) + w[n].shape[1:]) for n, g in zip(BIG, gathered))
        p = {n: w[n][l] for n in SMALL if n != "norm_final" and n not in SHARDED_SMALL}
        for n in SHARDED_SMALL:
            p[n] = conv_full[n][l]
        p["w_in"] = _pad_w_in(jnp.concatenate([g_in[k] for k in range(4)], axis=1))
        p["w_out"] = g_out.reshape(-1, g_out.shape[-1])
        p["ffn_w_up"] = jnp.concatenate([g_up[k] for k in range(4)], axis=1)
        p["ffn_w_down"] = g_dn.reshape(-1, g_dn.shape[-1])
        return p

    gathered = _place_own(_run_step(_gather_halves_step(halves(0)), name="weights_gather"), halves(0))
    layers, saved, xl = [], [], x[0]
    for l in range(depth):
        p = layer_params(l, gathered)
        step = _gather_halves_step(halves(l + 1)) if l + 1 < depth else None
        xl, s, results = _layer_fwd(xl, p, hosted=step)
        if step is not None:
            gathered = _place_own(results, halves(l + 1))
        layers.append(p)
        saved.append(s)
    loss, dx, g_norm_final = _loss_head(xl, norm_final, loss_target[0], name="loss_head")
    loss = lax.psum(loss, ("x", "y", "c"))
    grads, big, pending = [None] * depth, [None] * depth, None
    for l in reversed(range(depth)):
        if pending is None:
            dx, grads[l], _ = _layer_bwd(dx, layers[l], saved[l])
        else:
            before, halves_, rows = pending
            dx, grads[l], (operand, from_chips) = _layer_bwd(
                dx, layers[l], saved[l], comm=(_sibling_step(halves_, True), functools.partial(_chip_sum, halves_, ci)))
            big[before] = _reduce_finish(operand, from_chips, rows, c=ci)
        pending = (l,) + _reduce_pack(grads[l])
    before, halves_, rows = pending
    (got,) = _run_step(_sibling_step(halves_, True), name="grad_sibling_send")
    operand = _chip_sum(halves_, ci, got)
    (from_chips,) = _run_step(_exchange_step(operand), name="grad_chip_exchange")
    big[before] = _reduce_finish(operand, from_chips, rows, c=ci)
    grad_x = dx

    small_names = [n for n in SMALL if n != "norm_final"]
    small_rows, small_shapes = _pack_rows([jnp.stack([grads[l][n] for l in range(depth)]) for n in small_names]
                                          + [g_norm_final])
    small_sum = _unpack_rows(_all_sum(small_rows, name="small_grads_sum"), small_shapes)
    gsum = dict(zip(small_names + ["norm_final"], small_sum))
    for n in SHARDED_SMALL:
        width = w[n].shape[-1]
        gsum[n] = lax.dynamic_slice_in_dim(gsum[n], chip * width, width, axis=-1)
    for n in BIG:
        gsum[n] = jnp.stack([big[l][n].reshape(w[n].shape[1:]) for l in range(depth)])

    delta, new_m, new_v = {}, {}, {}
    for n in BIG:
        cols = w[n].shape[-1]
        d_, m_, v_ = _adamw(w[n].reshape(-1, cols), gsum[n].reshape(-1, cols), m[n].reshape(-1, cols),
                            v[n].reshape(-1, cols), name="adamw_" + n)
        delta[n], new_m[n], new_v[n] = (a.reshape(w[n].shape) for a in (d_, m_, v_))
    packs = [_pack_rows([src[n] for n in SMALL]) for src in (w, gsum, m, v)]
    outs = _adamw(*[pk[0] for pk in packs], name="adamw_small")
    for dst, rows_ in zip((delta, new_m, new_v), outs):
        dst.update(zip(SMALL, _unpack_rows(rows_, packs[0][1])))
    return (loss, grad_x[None], *[gsum[n] for n in NAMES], *[delta[n] for n in NAMES], *[new_m[n] for n in NAMES],
            *[new_v[n] for n in NAMES])
```
